```python
import math
import numpy as np
import jax
import jax.numpy as jnp
from jax import lax

D_MODEL = 1024
BATCH = 8
SEQ = 4096
DEPTH = 4

GRID_W = 64
CTX_LEN = 256
HEAD_DIM = 64
ROPE_THETA = 10000.0
NORM_EPS = 1e-6
F32 = jnp.float32

A_HEADS = 4
A_QK = HEAD_DIM
A_V = 2 * HEAD_DIM
A_QBLOCK = 128
B_HEADS = 4
B_DK = HEAD_DIM // 2
B_DV = HEAD_DIM
B_GATE_RANK = 16
B_GATE_TAU = 16.0
B_CHUNK = 64
C_HEADS = 4
C_DH = HEAD_DIM
NA_ROWS = 8
NA_COLS = 16

A_WIDTH = A_HEADS * A_V
B_WIDTH = B_HEADS * B_DV
C_WIDTH = C_HEADS * C_DH
MIX_WIDTH = A_WIDTH + B_WIDTH + C_WIDTH

IN_SIZES = (A_HEADS * 2 * A_QK, A_HEADS * 2 * A_QK, A_WIDTH,
            B_HEADS * B_DK, B_HEADS * B_DK, B_WIDTH, B_WIDTH, 2 * B_GATE_RANK,
            C_WIDTH, C_WIDTH, C_WIDTH)
IN_WIDTH = sum(IN_SIZES)

N_GROUPS = 4
EXPERTS_PER_GROUP = 8
N_EXPERTS = N_GROUPS * EXPERTS_PER_GROUP
TOP_K = 2
EXPERT_HIDDEN = 512
MOE_BLOCK = 128

kernel_name = 'hybrid_prefix_diffusion_trunk'


def rms_norm(x, g):
    xf = x.astype(F32)
    y = xf * lax.rsqrt(jnp.mean(xf * xf, axis=-1, keepdims=True) + NORM_EPS)
    return (y * g.astype(F32)).astype(x.dtype)


def split_in(p):
    return jnp.split(p, np.cumsum(IN_SIZES)[:-1].tolist(), axis=-1)


def axial_rope_tables(n_tokens, dtype):
    t = jnp.arange(n_tokens)
    row = (t // GRID_W).astype(F32)
    col = (t % GRID_W).astype(F32)
    n_freq = HEAD_DIM // 4
    inv = ROPE_THETA ** (-jnp.arange(n_freq, dtype=F32) / n_freq)
    ang_r = row[:, None] * inv
    ang_c = col[:, None] * inv
    return (jnp.cos(ang_r).astype(dtype), jnp.sin(ang_r).astype(dtype),
            jnp.cos(ang_c).astype(dtype), jnp.sin(ang_c).astype(dtype))


def apply_axial_rope(x, tabs):
    cr, sr, cc, sc = (tb.reshape((tb.shape[0],) + (1,) * (x.ndim - 3) + (tb.shape[1],)) for tb in tabs)
    x1, x2, x3, x4 = jnp.split(x, 4, axis=-1)
    return jnp.concatenate([x1 * cr - x2 * sr, x2 * cr + x1 * sr,
                            x3 * cc - x4 * sc, x4 * cc + x3 * sc], axis=-1)


def diff_attention(q, k, v, qc, kc, vc, lam, lam_init, g_sub, need_ctx):
    B_, S = q.shape[:2]
    L = qc.shape[1]
    scale = A_QK ** -0.5
    k_all = jnp.concatenate([kc, k], axis=1)
    v_all = jnp.concatenate([vc, v], axis=1)

    def attend(qb, kb, vb):
        s = jnp.einsum('bqhmd,bkhmd->bhmqk', qb, kb).astype(F32) * scale
        p = jax.nn.softmax(s, axis=-1)
        d = p[:, :, 0] - lam * p[:, :, 1]
        return jnp.einsum('bhqk,bkhv->bqhv', d.astype(vb.dtype), vb)

    nb = S // A_QBLOCK
    qblocks = jnp.moveaxis(q.reshape((B_, nb, A_QBLOCK) + q.shape[2:]), 1, 0)
    o = lax.map(lambda qb: attend(qb, k_all, v_all), qblocks)
    o = jnp.moveaxis(o, 0, 1).reshape(B_, S, A_HEADS, A_V)
    out = (rms_norm(o, g_sub) * (1.0 - lam_init)).reshape(B_, S, A_WIDTH)
    out_c = None
    if need_ctx:
        oc = attend(qc, kc, vc)
        out_c = (rms_norm(oc, g_sub) * (1.0 - lam_init)).reshape(B_, L, A_WIDTH)
    return out, out_c


def gla_scan(q, k, v, log_a, s0):
    B_, H, T, dk = q.shape
    n = T // B_CHUNK

    def chunks(t):
        return jnp.moveaxis(t.reshape(B_, H, n, B_CHUNK, t.shape[-1]), 2, 0)

    causal = jnp.tril(jnp.ones((B_CHUNK, B_CHUNK), dtype=bool))

    def step(s, inp):
        qc, kc, vc, ac = inp
        b = jnp.cumsum(ac, axis=2)
        rel = jnp.where(causal[:, :, None], b[:, :, :, None, :] - b[:, :, None, :, :], -jnp.inf)
        a_intra = jnp.einsum('bhtsd,bhsd->bhts', qc[:, :, :, None, :] * jnp.exp(rel), kc)
        o = (jnp.einsum('bhts,bhsv->bhtv', a_intra, vc)
             + jnp.einsum('bhtd,bhdv->bhtv', qc * jnp.exp(b), s))
        b_end = b[:, :, -1, :]
        s = (jnp.exp(b_end)[..., None] * s
             + jnp.einsum('bhsd,bhsv->bhdv', kc * jnp.exp(b_end[:, :, None, :] - b), vc))
        return s, o

    s_final, o = lax.scan(step, s0, (chunks(q), chunks(k), chunks(v), chunks(log_a)))
    o = jnp.moveaxis(o, 0, 2).reshape(B_, H, T, v.shape[-1])
    return o, s_final


def gla_mixer(q, k, v, r, glow, qc, kc, vc, rc, glowc, w_dec, b_dec, g_out, need_ctx):
    B_ = q.shape[0]

    def heads(t, d):
        return jnp.transpose(t.reshape(t.shape[0], t.shape[1], -1, d), (0, 2, 1, 3)).astype(F32)

    def decay(gl, direction):
        z = gl[..., direction * B_GATE_RANK:(direction + 1) * B_GATE_RANK] @ w_dec[direction] + b_dec[direction]
        return heads(jax.nn.log_sigmoid(z.astype(F32)) / B_GATE_TAU, B_DK)

    def flip(t):
        return jnp.flip(t, axis=2)

    scale = B_DK ** -0.5
    ql, kl, vl = heads(q, B_DK) * scale, heads(k, B_DK), heads(v, B_DV)
    qx, kx, vx = heads(qc, B_DK) * scale, heads(kc, B_DK), heads(vc, B_DV)
    s0 = jnp.zeros((B_, B_HEADS, B_DK, B_DV), F32)
    ox_f, sx_f = gla_scan(qx, kx, vx, decay(glowc, 0), s0)
    ol_f, _ = gla_scan(ql, kl, vl, decay(glow, 0), sx_f)
    ox_b, sx_b = gla_scan(flip(qx), flip(kx), flip(vx), flip(decay(glowc, 1)), s0)
    ol_b, _ = gla_scan(flip(ql), flip(kl), flip(vl), flip(decay(glow, 1)), sx_b)

    def readout(o, gate):
        o = jnp.transpose(rms_norm(o, g_out), (0, 2, 1, 3)).reshape(gate.shape[0], gate.shape[1], B_WIDTH)
        return (o * jax.nn.silu(gate.astype(F32))).astype(gate.dtype)

    out = readout(ol_f + flip(ol_b), r)
    out_c = readout(ox_f + flip(ox_b), rc) if need_ctx else None
    return out, out_c


def neighborhood_attention(q, k, v, qc, kc, vc, rpb, need_ctx):
    B_, S, H, dh = q.shape
    L = qc.shape[1]
    rows = S // GRID_W
    wr = min(NA_ROWS, rows)
    scale = dh ** -0.5

    def grid(t):
        return jnp.transpose(t.reshape(B_, rows, GRID_W, H, dh), (0, 3, 1, 2, 4))

    qg, kg, vg = grid(q), grid(k), grid(v)
    r = jnp.arange(rows)
    row_idx = jnp.clip(r - wr // 2, 0, rows - wr)[:, None] + jnp.arange(wr)
    cq = jnp.arange(GRID_W)
    col_start = jnp.clip(cq - NA_COLS // 2, 0, GRID_W - NA_COLS)
    col_mask = (cq[None, :] >= col_start[:, None]) & (cq[None, :] < col_start[:, None] + NA_COLS)
    k_rows = kg[:, :, row_idx]
    v_rows = vg[:, :, row_idx]
    s_nb = jnp.einsum('bhrqd,bhrwkd->bhrqwk', qg, k_rows).astype(F32) * scale
    dr = row_idx - r[:, None] + (NA_ROWS - 1)
    dc = jnp.clip(cq[None, :] - cq[:, None], -(NA_COLS - 1), NA_COLS - 1) + (NA_COLS - 1)
    bias = rpb[:, dr[:, None, :, None], dc[None, :, None, :]].astype(F32)
    s_nb = jnp.where(col_mask[:, None, :], s_nb + bias, -jnp.inf)
    s_c = jnp.einsum('bhrqd,bchd->bhrqc', qg, kc).astype(F32) * scale
    n_nb = wr * GRID_W
    p = jax.nn.softmax(jnp.concatenate([s_nb.reshape(B_, H, rows, GRID_W, n_nb), s_c], axis=-1), axis=-1)
    p_nb = p[..., :n_nb].reshape(s_nb.shape).astype(v.dtype)
    p_c = p[..., n_nb:].astype(v.dtype)
    o = (jnp.einsum('bhrqwk,bhrwkd->bhrqd', p_nb, v_rows)
         + jnp.einsum('bhrqc,bchd->bhrqd', p_c, vc))
    out = jnp.transpose(o, (0, 2, 3, 1, 4)).reshape(B_, S, H * dh)
    out_c = None
    if need_ctx:
        sc = jnp.einsum('bqhd,bkhd->bhqk', qc, kc).astype(F32) * scale
        out_c = jnp.einsum('bhqk,bkhd->bqhd', jax.nn.softmax(sc, axis=-1).astype(vc.dtype), vc).reshape(B_, L, H * dh)
    return out, out_c


def hier_moe(h, w_rg, b_rg, w_re, b_re, w_up, w_down):
    n_tok, D = h.shape
    n_assign = n_tok * TOP_K
    n_blocks = -(-(n_assign + N_EXPERTS * (MOE_BLOCK - 1)) // MOE_BLOCK)
    logit_g = (h @ w_rg + b_rg).astype(F32)
    grp = jnp.argmax(logit_g, axis=-1)
    p_grp = jnp.take_along_axis(jax.nn.softmax(logit_g, axis=-1), grp[:, None], axis=1)
    logit_e = (h @ w_re + b_re).astype(F32).reshape(n_tok, N_GROUPS, EXPERTS_PER_GROUP)
    logit_e = jnp.take_along_axis(logit_e, grp[:, None, None], axis=1)[:, 0]
    top_v, top_i = lax.top_k(logit_e, TOP_K)
    gate = jax.nn.softmax(top_v, axis=-1) * p_grp
    expert = (grp[:, None] * EXPERTS_PER_GROUP + top_i).reshape(-1)
    order = jnp.argsort(expert)
    e_sorted = expert[order]
    tok_sorted = order // TOP_K
    counts = jnp.bincount(expert, length=N_EXPERTS)
    padded = (counts + MOE_BLOCK - 1) // MOE_BLOCK * MOE_BLOCK
    start = jnp.cumsum(counts) - counts
    pad_end = jnp.cumsum(padded)
    pad_start = pad_end - padded
    dest = pad_start[e_sorted] + jnp.arange(n_assign) - start[e_sorted]
    buf = jnp.zeros((n_blocks * MOE_BLOCK, D), h.dtype).at[dest].set(h[tok_sorted])
    blk_expert = jnp.minimum(jnp.searchsorted(pad_end, jnp.arange(n_blocks) * MOE_BLOCK, side='right'), N_EXPERTS - 1)

    def expert_block(args):
        xb, e = args
        g, u = jnp.split(xb @ w_up[e], 2, axis=-1)
        return (jax.nn.silu(g) * u) @ w_down[e]

    y = lax.map(expert_block, (buf.reshape(n_blocks, MOE_BLOCK, D), blk_expert)).reshape(-1, D)
    y = y[dest] * gate.reshape(-1)[order][:, None].astype(h.dtype)
    return jax.ops.segment_sum(y, tok_sorted, num_segments=n_tok)


def hybrid_layer(x, xc, c_act, cc_act, rope, layer_idx, need_ctx,
                 w_mod, b_mod, g1, g2, w_in, w_out, lam, g_sub,
                 w_dec, b_dec, g_gla, rpb, w_rg, b_rg, w_re, b_re, w_up, w_down):
    B_, S, D = x.shape
    L = xc.shape[1]
    mod = (c_act @ w_mod + b_mod).reshape(B_, 6, 1, D)
    modc = (cc_act @ w_mod + b_mod).reshape(6, D)
    h = rms_norm(x, g1) * (1 + mod[:, 1]) + mod[:, 0]
    hc = rms_norm(xc, g1) * (1 + modc[1]) + modc[0]
    qa, ka, va, qb, kb, vb, rb, gb, qn, kn, vn = split_in(h @ w_in)
    qa_c, ka_c, va_c, qb_c, kb_c, vb_c, rb_c, gb_c, qn_c, kn_c, vn_c = split_in(hc @ w_in)

    lam_init = 0.8 - 0.6 * math.exp(-0.3 * layer_idx)
    lam_f = lam.astype(F32)
    lam_val = jnp.exp(jnp.sum(lam_f[0] * lam_f[1])) - jnp.exp(jnp.sum(lam_f[2] * lam_f[3])) + lam_init

    def a_qk(t):
        return t.reshape(t.shape[0], t.shape[1], A_HEADS, 2, A_QK)

    ya, ya_c = diff_attention(apply_axial_rope(a_qk(qa), rope), apply_axial_rope(a_qk(ka), rope),
                              va.reshape(B_, S, A_HEADS, A_V), a_qk(qa_c), a_qk(ka_c),
                              va_c.reshape(B_, L, A_HEADS, A_V), lam_val, lam_init, g_sub, need_ctx)
    yb, yb_c = gla_mixer(qb, kb, vb, rb, gb, qb_c, kb_c, vb_c, rb_c, gb_c, w_dec, b_dec, g_gla, need_ctx)

    def c_heads(t):
        return t.reshape(t.shape[0], t.shape[1], C_HEADS, C_DH)

    yn, yn_c = neighborhood_attention(c_heads(qn), c_heads(kn), c_heads(vn),
                                      c_heads(qn_c), c_heads(kn_c), c_heads(vn_c), rpb, need_ctx)
    x = x + mod[:, 2] * (jnp.concatenate([ya, yb, yn], axis=-1) @ w_out)
    h2 = rms_norm(x, g2) * (1 + mod[:, 4]) + mod[:, 3]
    if need_ctx:
        xc = xc + modc[2] * (jnp.concatenate([ya_c, yb_c, yn_c], axis=-1) @ w_out)
        h2c = rms_norm(xc, g2) * (1 + modc[4]) + modc[3]
        f = hier_moe(jnp.concatenate([h2.reshape(-1, D), h2c.reshape(-1, D)], axis=0),
                     w_rg, b_rg, w_re, b_re, w_up, w_down)
        x = x + mod[:, 5] * f[:B_ * S].reshape(B_, S, D)
        xc = xc + modc[5] * f[B_ * S:].reshape(B_, L, D)
    else:
        f = hier_moe(h2.reshape(-1, D), w_rg, b_rg, w_re, b_re, w_up, w_down)
        x = x + mod[:, 5] * f.reshape(B_, S, D)
    return x, xc


def setup_inputs(seed: int = 0) -> dict:
    key = jax.random.key(seed)
    ks = jax.random.split(key, 23)
    D = D_MODEL

    def nrm(k, shape, s):
        return jax.random.normal(k, shape, F32) * s

    return {
        'x': nrm(ks[0], (BATCH, SEQ, D), 1.0),
        'c': nrm(ks[1], (BATCH, D), 1.0),
        'ctx': nrm(ks[2], (BATCH, CTX_LEN, D), 1.0),
        'c_ctx': nrm(ks[3], (D,), 1.0),
        'w_mod': nrm(ks[4], (DEPTH, D, 6 * D), 0.5 * D ** -0.5),
        'b_mod': nrm(ks[5], (DEPTH, 6 * D), 0.02),
        'norm1_g': 1.0 + nrm(ks[6], (DEPTH, D), 0.02),
        'norm2_g': 1.0 + nrm(ks[7], (DEPTH, D), 0.02),
        'w_in': nrm(ks[8], (DEPTH, D, IN_WIDTH), D ** -0.5),
        'w_out': nrm(ks[9], (DEPTH, MIX_WIDTH, D), MIX_WIDTH ** -0.5),
        'diff_lambda': nrm(ks[10], (DEPTH, 4, A_QK), 0.1),
        'diff_sub_g': 1.0 + nrm(ks[11], (DEPTH, A_V), 0.02),
        'gla_w_decay': nrm(ks[12], (DEPTH, 2, B_GATE_RANK, B_HEADS * B_DK), B_GATE_RANK ** -0.5),
        'gla_b_decay': nrm(ks[13], (DEPTH, 2, B_HEADS * B_DK), 0.1),
        'gla_norm_g': 1.0 + nrm(ks[14], (DEPTH, B_DV), 0.02),
        'na_rel_bias': nrm(ks[15], (DEPTH, C_HEADS, 2 * NA_ROWS - 1, 2 * NA_COLS - 1), 0.02),
        'w_router_group': nrm(ks[16], (DEPTH, D, N_GROUPS), D ** -0.5),
        'b_router_group': nrm(ks[17], (DEPTH, N_GROUPS), 0.01),
        'w_router_expert': nrm(ks[18], (DEPTH, D, N_EXPERTS), D ** -0.5),
        'b_router_expert': nrm(ks[19], (DEPTH, N_EXPERTS), 0.01),
        'w_expert_up': nrm(ks[20], (DEPTH, N_EXPERTS, D, 2 * EXPERT_HIDDEN), D ** -0.5),
        'w_expert_down': nrm(ks[21], (DEPTH, N_EXPERTS, EXPERT_HIDDEN, D), EXPERT_HIDDEN ** -0.5),
        'final_g': 1.0 + nrm(ks[22], (D,), 0.02),
    }


def reference(x, c, ctx, c_ctx, w_mod, b_mod, norm1_g, norm2_g, w_in, w_out,
              diff_lambda, diff_sub_g, gla_w_decay, gla_b_decay, gla_norm_g, na_rel_bias,
              w_router_group, b_router_group, w_router_expert, b_router_expert,
              w_expert_up, w_expert_down, final_g):
    rope = axial_rope_tables(x.shape[1], x.dtype)
    c_act = jax.nn.silu(c)
    cc_act = jax.nn.silu(c_ctx)
    xc = ctx
    for l in range(DEPTH):
        x, xc = hybrid_layer(x, xc, c_act, cc_act, rope, l, l < DEPTH - 1,
                             w_mod[l], b_mod[l], norm1_g[l], norm2_g[l], w_in[l], w_out[l],
                             diff_lambda[l], diff_sub_g[l], gla_w_decay[l], gla_b_decay[l],
                             gla_norm_g[l], na_rel_bias[l], w_router_group[l], b_router_group[l],
                             w_router_expert[l], b_router_expert[l], w_expert_up[l], w_expert_down[l])
    return rms_norm(x, final_g)
```

```python
import functools
import math

import numpy as np
import jax
import jax.numpy as jnp
from jax import lax
from jax.experimental import pallas as pl
from jax.experimental.pallas import tpu as pltpu

F32 = jnp.float32
MXU_DTYPE = jnp.bfloat16
HI = lax.Precision.HIGHEST

GRID_W = 64
HEAD_DIM = 64
ROPE_THETA = 10000.0
NORM_EPS = 1e-6

A_HEADS = 4
A_QK = HEAD_DIM
A_V = 2 * HEAD_DIM
B_HEADS = 4
B_DK = HEAD_DIM // 2
B_DV = HEAD_DIM
B_GATE_RANK = 16
B_GATE_TAU = 16.0
B_CHUNK = 64
C_HEADS = 4
C_DH = HEAD_DIM
NA_ROWS = 8
NA_COLS = 16

A_WIDTH = A_HEADS * A_V
B_WIDTH = B_HEADS * B_DV
C_WIDTH = C_HEADS * C_DH
B_QK = B_HEADS * B_DK
IN_SIZES = (A_HEADS * 2 * A_QK, A_HEADS * 2 * A_QK, A_WIDTH,
            B_QK, B_QK, B_WIDTH, B_WIDTH, 2 * B_GATE_RANK,
            C_WIDTH, C_WIDTH, C_WIDTH)

N_GROUPS = 4
EXPERTS_PER_GROUP = 8
N_EXPERTS = N_GROUPS * EXPERTS_PER_GROUP
TOP_K = 2
EXPERT_HIDDEN = 512

LANES = 128
TOK_TILE = 256
NA_TILE_ROWS = 4
NA_KEY_TILES = 3
EXPERT_TILE = 256
GATE_PAD = LANES
BG_WIDTH = 2 * B_QK + 2 * B_WIDTH + GATE_PAD
IN_PAD_WIDTH = 3 * A_WIDTH + BG_WIDTH + 3 * C_WIDTH
ROUTER_PAD = LANES
VMEM_LIMIT = 48 * 1024 * 1024


def _silu(x):
    return x * (1.0 / (1.0 + jnp.exp(-x)))


def _cparams(sem):
    return pltpu.CompilerParams(dimension_semantics=sem, vmem_limit_bytes=VMEM_LIMIT)


def _mod_body(c_ref, w_ref, b_ref, o_ref):
    a = _silu(c_ref[...])
    o_ref[...] = jnp.dot(a, w_ref[...], precision=HI, preferred_element_type=F32) + b_ref[...]


def _modulation(cvec, w_mod, b_mod):
    depth, d, d6 = w_mod.shape
    rows = cvec.shape[0]
    return pl.pallas_call(
        _mod_body,
        grid=(depth, d6 // d),
        in_specs=[pl.BlockSpec((rows, d), lambda l, j: (0, 0)),
                  pl.BlockSpec((None, d, d), lambda l, j: (l, 0, j)),
                  pl.BlockSpec((None, 1, d), lambda l, j: (l, 0, j))],
        out_specs=pl.BlockSpec((None, rows, d), lambda l, j: (l, 0, j)),
        out_shape=jax.ShapeDtypeStruct((depth, rows, d6), F32),
        compiler_params=_cparams(("arbitrary", "arbitrary")),
        name="modulation",
    )(cvec, w_mod, b_mod.reshape(depth, 1, d6))


def _rms(x, g):
    return x * lax.rsqrt(jnp.mean(x * x, axis=-1, keepdims=True) + NORM_EPS) * g


def _rope(x, cos, sa, sb):
    return x * cos + pltpu.roll(x, LANES - 16, 1) * sa + pltpu.roll(x, 16, 1) * sb


def _in_proj_body(combine, *refs):
    if combine:
        (x_ref, y0_ref, y1_ref, pmod_ref, mod_ref, g_ref, w_ref, cos_ref, sa_ref, sb_ref,
         qa_ref, ka_ref, va_ref, bg_ref, qn_ref, kn_ref, vn_ref, xo_ref) = refs
        x = x_ref[0] + pmod_ref[5:6, :] * (y0_ref[0] + y1_ref[0])
        xo_ref[0] = x
    else:
        (x_ref, mod_ref, g_ref, w_ref, cos_ref, sa_ref, sb_ref,
         qa_ref, ka_ref, va_ref, bg_ref, qn_ref, kn_ref, vn_ref) = refs
        x = x_ref[0]
    h = _rms(x, g_ref[...]) * (1.0 + mod_ref[1:2, :]) + mod_ref[0:1, :]
    hb = h.astype(MXU_DTYPE)

    def proj(lo, hi):
        return jnp.dot(hb, w_ref[:, lo:hi], preferred_element_type=F32)

    cos, sa, sb = cos_ref[...], sa_ref[...], sb_ref[...]
    for hh in range(A_HEADS):
        lo = hh * LANES
        q = proj(lo, lo + LANES)
        qa_ref[0, :, lo:lo + LANES] = (_rope(q, cos, sa, sb) * (A_QK ** -0.5)).astype(qa_ref.dtype)
        k = proj(A_WIDTH + lo, A_WIDTH + lo + LANES)
        ka_ref[0, :, lo:lo + LANES] = _rope(k, cos, sa, sb).astype(ka_ref.dtype)
    o = 2 * A_WIDTH
    va_ref[0] = proj(o, o + A_WIDTH).astype(va_ref.dtype)
    o += A_WIDTH
    bg_ref[0] = proj(o, o + BG_WIDTH)
    o += BG_WIDTH
    qn_ref[0] = (proj(o, o + C_WIDTH) * (C_DH ** -0.5)).astype(qn_ref.dtype)
    o += C_WIDTH
    kn_ref[0] = proj(o, o + C_WIDTH).astype(kn_ref.dtype)
    o += C_WIDTH
    vn_ref[0] = proj(o, o + C_WIDTH).astype(vn_ref.dtype)


def _in_proj(layer, x, y0, y1, modsel, g1, w_in_p, rope_tabs):
    bsz, t, d = x.shape
    tm = TOK_TILE
    combine = y0 is not None
    tok = lambda b, i: (b, i, 0)
    x_spec = pl.BlockSpec((1, tm, d), tok)

    def mod_spec(l):
        return pl.BlockSpec((None, None, None, 6, d), lambda b, i: (l, b, jnp.minimum(i, 1), 0, 0))

    tab_spec = pl.BlockSpec((tm, LANES), lambda b, i: (i, 0))
    in_specs = [x_spec]
    args = [x]
    if combine:
        in_specs += [x_spec, x_spec, mod_spec(layer - 1)]
        args += [y0, y1, modsel]
    in_specs += [mod_spec(layer), pl.BlockSpec((1, d), lambda b, i: (0, 0)),
                 pl.BlockSpec((d, IN_PAD_WIDTH), lambda b, i: (0, 0)), tab_spec, tab_spec, tab_spec]
    args += [modsel, g1.reshape(1, d), w_in_p, *rope_tabs]

    def o(width, dtype):
        return pl.BlockSpec((1, tm, width), tok), jax.ShapeDtypeStruct((bsz, t, width), dtype)

    outs = [o(A_WIDTH, MXU_DTYPE), o(A_WIDTH, MXU_DTYPE), o(A_WIDTH, MXU_DTYPE), o(BG_WIDTH, F32),
            o(C_WIDTH, MXU_DTYPE), o(C_WIDTH, MXU_DTYPE), o(C_WIDTH, MXU_DTYPE)]
    if combine:
        outs.append(o(d, F32))
    return pl.pallas_call(
        functools.partial(_in_proj_body, combine),
        grid=(bsz, t // tm),
        in_specs=in_specs,
        out_specs=[s for s, _ in outs],
        out_shape=[s for _, s in outs],
        compiler_params=_cparams(("parallel", "arbitrary")),
        name="in_proj",
    )(*args)


def _diff_attn_body(lam_init, ctx_len, q_ref, k_ref, v_ref, lam_ref, g_ref, o_ref):
    q = q_ref[0]
    lane = lax.broadcasted_iota(jnp.int32, (1, LANES), 1)
    zero = jnp.zeros_like(q)
    q1 = jnp.where(lane < A_QK, q, zero)
    q2 = jnp.where(lane >= A_QK, q, zero)
    lm = lam_ref[...]
    lam = (jnp.exp(jnp.sum(lm[0:1] * lm[1:2], axis=1, keepdims=True))
           - jnp.exp(jnp.sum(lm[2:3] * lm[3:4], axis=1, keepdims=True)) + lam_init)

    def softmax_parts(qm, k):
        s = lax.dot_general(qm, k, (((1,), (1,)), ((), ())), preferred_element_type=F32)
        p = jnp.exp(s - jnp.max(s, axis=-1, keepdims=True))
        return p, jnp.sum(p, axis=-1, keepdims=True)

    def attend(n_keys):
        k = k_ref[0, :n_keys, :]
        v = v_ref[0, :n_keys, :]
        p1, l1 = softmax_parts(q1, k)
        p2, l2 = softmax_parts(q2, k)
        dmat = p1 * (1.0 / l1) - p2 * (lam / l2)
        o = jnp.dot(dmat.astype(MXU_DTYPE), v, preferred_element_type=F32)
        o_ref[0] = (_rms(o, g_ref[...]) * (1.0 - lam_init)).astype(o_ref.dtype)

    @pl.when(pl.program_id(2) == 0)
    def _():
        attend(ctx_len)

    @pl.when(pl.program_id(2) > 0)
    def _():
        attend(k_ref.shape[1])


def _diff_attention(layer, qa, ka, va, lam, g_sub):
    bsz, t, _ = qa.shape
    tq = TOK_TILE
    lam_init = 0.8 - 0.6 * math.exp(-0.3 * layer)
    kv_spec = pl.BlockSpec((1, t, LANES), lambda b, h, i: (b, 0, h))
    return pl.pallas_call(
        functools.partial(_diff_attn_body, lam_init, TOK_TILE),
        grid=(bsz, A_HEADS, t // tq),
        in_specs=[pl.BlockSpec((1, tq, LANES), lambda b, h, i: (b, i, h)), kv_spec, kv_spec,
                  pl.BlockSpec((4, A_QK), lambda b, h, i: (0, 0)),
                  pl.BlockSpec((1, A_V), lambda b, h, i: (0, 0))],
        out_specs=pl.BlockSpec((1, tq, LANES), lambda b, h, i: (b, i, h)),
        out_shape=jax.ShapeDtypeStruct((bsz, t, A_WIDTH), MXU_DTYPE),
        compiler_params=_cparams(("parallel", "parallel", "arbitrary")),
        name="diff_attention",
    )(qa, ka, va, lam, g_sub.reshape(1, A_V))


def _gla_body(f_ref, r_ref, wdec_ref, bdec_ref, of_ref, ob_ref, sf_ref, sb_ref, e_ref, b_ref, qs_ref):
    c = B_CHUNK
    n_chunks = TOK_TILE // c

    @pl.when(pl.program_id(1) == 0)
    def _():
        sf_ref[...] = jnp.zeros_like(sf_ref)
        sb_ref[...] = jnp.zeros_like(sb_ref)

    row = lax.broadcasted_iota(jnp.int32, (c, c), 0)
    col = lax.broadcasted_iota(jnp.int32, (c, c), 1)
    tri_f = (col <= row).astype(F32)
    tri_b = (col >= row).astype(F32)
    s_iota = lax.broadcasted_iota(jnp.int32, (c, LANES), 0)
    head_of_k = lax.broadcasted_iota(jnp.int32, (B_QK, B_WIDTH), 0) // B_DK
    head_of_v = lax.broadcasted_iota(jnp.int32, (B_QK, B_WIDTH), 1) // B_DV
    expand = (head_of_k == head_of_v).astype(MXU_DTYPE)
    same_head_t = (lax.broadcasted_iota(jnp.int32, (B_WIDTH, B_QK), 0) // B_DV
                   == lax.broadcasted_iota(jnp.int32, (B_WIDTH, B_QK), 1) // B_DK)

    def chunk(src_ref, lo, backward, st_ref, out_ref):
        q = src_ref[0, lo:lo + c, 0:B_QK] * (B_DK ** -0.5)
        k = src_ref[0, lo:lo + c, B_QK:2 * B_QK]
        v = src_ref[0, lo:lo + c, 2 * B_QK:2 * B_QK + B_WIDTH]
        gl = src_ref[0, lo:lo + c, 2 * B_QK + 2 * B_WIDTH:BG_WIDTH]
        d0 = B_QK if backward else 0
        z = jnp.dot(gl, wdec_ref[:, d0:d0 + B_QK], precision=HI, preferred_element_type=F32) \
            + bdec_ref[:, d0:d0 + B_QK]
        log_a = (jnp.minimum(z, 0.0) - jnp.log(1.0 + jnp.exp(-jnp.abs(z)))) / B_GATE_TAU
        b = jnp.dot(tri_b if backward else tri_f, log_a, precision=HI, preferred_element_type=F32)

        def fill(tt, carry):
            bt = b_ref[pl.ds(tt, 1), :]
            qt = qs_ref[pl.ds(tt, 1), :]
            keep = (s_iota >= tt) if backward else (s_iota <= tt)
            e = jnp.exp(jnp.where(keep, bt - b, -jnp.inf)) * (qt * k)
            e_ref[pl.ds(pl.multiple_of(tt * c, c), c), :] = e.astype(e_ref.dtype)
            return carry

        b_ref[...] = b
        qs_ref[...] = q
        lax.fori_loop(0, c, fill, 0)
        a_exp = jnp.dot(e_ref[...], expand, preferred_element_type=F32)
        o_intra = jnp.sum(a_exp.reshape(c, c, B_WIDTH) * v[None, :, :], axis=1)
        st = st_ref[...]
        o_inter = lax.dot_general(q * jnp.exp(b), st, (((1,), (1,)), ((), ())),
                                  preferred_element_type=F32)
        out_ref[0, lo:lo + c, :] = o_intra + o_inter
        b_end = b[0:1, :] if backward else b[c - 1:c, :]
        kd = k * jnp.exp(b_end - b)
        upd = lax.dot_general(v, kd, (((0,), (0,)), ((), ())), preferred_element_type=F32)
        st_ref[...] = jnp.exp(b_end) * st + jnp.where(same_head_t, upd, 0.0)

    for ci in range(n_chunks):
        chunk(f_ref, ci * c, False, sf_ref, of_ref)
        chunk(r_ref, (n_chunks - 1 - ci) * c, True, sb_ref, ob_ref)


def _gla_scan(bg, w_dec, b_dec):
    bsz, t, _ = bg.shape
    n = t // TOK_TILE
    rev = lambda b, i: (b, jnp.where(i == 0, 0, n - i), 0)
    fwd = lambda b, i: (b, i, 0)
    wdec = jnp.zeros((GATE_PAD, 2 * B_QK), F32)
    wdec = wdec.at[:B_GATE_RANK, :B_QK].set(w_dec[0]).at[B_GATE_RANK:2 * B_GATE_RANK, B_QK:].set(w_dec[1])
    bdec = b_dec.reshape(1, 2 * B_QK)
    o_shape = jax.ShapeDtypeStruct((bsz, t, B_WIDTH), F32)
    return pl.pallas_call(
        _gla_body,
        grid=(bsz, n),
        in_specs=[pl.BlockSpec((1, TOK_TILE, BG_WIDTH), fwd),
                  pl.BlockSpec((1, TOK_TILE, BG_WIDTH), rev),
                  pl.BlockSpec((GATE_PAD, 2 * B_QK), lambda b, i: (0, 0)),
                  pl.BlockSpec((1, 2 * B_QK), lambda b, i: (0, 0))],
        out_specs=[pl.BlockSpec((1, TOK_TILE, B_WIDTH), fwd),
                   pl.BlockSpec((1, TOK_TILE, B_WIDTH), rev)],
        out_shape=[o_shape, o_shape],
        scratch_shapes=[pltpu.VMEM((B_WIDTH, B_QK), F32), pltpu.VMEM((B_WIDTH, B_QK), F32),
                        pltpu.VMEM((B_CHUNK * B_CHUNK, LANES), MXU_DTYPE),
                        pltpu.VMEM((B_CHUNK, LANES), F32), pltpu.VMEM((B_CHUNK, LANES), F32)],
        compiler_params=_cparams(("parallel", "arbitrary")),
        name="gla_scan",
    )(bg, bg, wdec, bdec)


def _na_bias_tables(rpb, rows):
    n_tiles = rows // NA_TILE_ROWS
    wr = min(NA_ROWS, rows)
    qr = np.arange(NA_TILE_ROWS)[:, None, None, None]
    cq = np.arange(GRID_W)[None, :, None, None]
    kw = np.arange(NA_KEY_TILES * NA_TILE_ROWS)[None, None, :, None]
    ck = np.arange(GRID_W)[None, None, None, :]
    cs = np.clip(cq - NA_COLS // 2, 0, GRID_W - NA_COLS)
    col_ok = (ck >= cs) & (ck < cs + NA_COLS)
    dc = np.clip(ck - cq, -(NA_COLS - 1), NA_COLS - 1) + (NA_COLS - 1)
    tabs = []
    for j in (0, 1, n_tiles - 1):
        r = j * NA_TILE_ROWS + qr
        kr = int(np.clip(j - 1, 0, n_tiles - NA_KEY_TILES)) * NA_TILE_ROWS + kw
        start = np.clip(r - wr // 2, 0, rows - wr)
        ok = (kr >= start) & (kr < start + wr) & col_ok
        dr = np.clip(kr - r + (NA_ROWS - 1), 0, 2 * NA_ROWS - 2)
        shape = np.broadcast_shapes(ok.shape, dr.shape, dc.shape)
        n_q, n_k = NA_TILE_ROWS * GRID_W, NA_KEY_TILES * NA_TILE_ROWS * GRID_W
        dr_i = np.broadcast_to(dr, shape).reshape(n_q, n_k)
        dc_i = np.broadcast_to(dc, shape).reshape(n_q, n_k)
        ok_i = np.broadcast_to(ok, shape).reshape(n_q, n_k)
        tabs.append(jnp.where(ok_i[None], rpb[:, dr_i, dc_i].astype(F32), -jnp.inf))
    return jnp.stack(tabs)


def _na_body(q_ref, k0_ref, k1_ref, k2_ref, kc_ref, v0_ref, v1_ref, v2_ref, vc_ref, m_ref, o_ref):
    q = q_ref[0]
    lane = lax.broadcasted_iota(jnp.int32, (1, LANES), 1)
    zero = jnp.zeros_like(q)
    nt = (((1,), (1,)), ((), ()))

    def scores(qm, k_ref):
        return lax.dot_general(qm, k_ref[0], nt, preferred_element_type=F32)

    def head_out(hh, windows):
        qm = jnp.where((lane >= hh * C_DH) & (lane < (hh + 1) * C_DH), q, zero)
        s = [scores(qm, kc_ref)]
        for w, k_ref in enumerate(windows):
            s.append(scores(qm, k_ref) + m_ref[0, hh, :, w * TOK_TILE:(w + 1) * TOK_TILE])
        m = functools.reduce(jnp.maximum, [jnp.max(x, axis=-1, keepdims=True) for x in s])
        p = [jnp.exp(x - m) for x in s]
        den = functools.reduce(jnp.add, [jnp.sum(x, axis=-1, keepdims=True) for x in p])
        vals = [vc_ref] + [v0_ref, v1_ref, v2_ref][:len(windows)]
        o = functools.reduce(jnp.add, [jnp.dot(x.astype(MXU_DTYPE), v_ref[0], preferred_element_type=F32)
                                       for x, v_ref in zip(p, vals)])
        return o * (1.0 / den)

    def emit(windows):
        o0 = head_out(0, windows)
        o1 = head_out(1, windows)
        o_ref[0] = jnp.where(lane < C_DH, o0, o1).astype(o_ref.dtype)

    @pl.when(pl.program_id(1) == 0)
    def _():
        emit([])

    @pl.when(pl.program_id(1) > 0)
    def _():
        emit([k0_ref, k1_ref, k2_ref])


def _neighborhood_attention(qn, kn, vn, bias_tabs):
    bsz, t, _ = qn.shape
    n = t // TOK_TILE
    n_lat = n - 1

    def win(w):
        def index(hp, i, b):
            kb0 = jnp.clip(i - 2, 0, n_lat - NA_KEY_TILES)
            return (b, kb0 + 1 + w, hp)
        return pl.BlockSpec((1, TOK_TILE, LANES), index)

    def cls(hp, i, b):
        j = i - 1
        return (jnp.where(j <= 0, 0, jnp.where(j == n_lat - 1, 2, 1)), hp, 0, 0)

    own = pl.BlockSpec((1, TOK_TILE, LANES), lambda hp, i, b: (b, i, hp))
    ctx = pl.BlockSpec((1, TOK_TILE, LANES), lambda hp, i, b: (b, 0, hp))
    heads_per_step = LANES // C_DH
    return pl.pallas_call(
        _na_body,
        grid=(C_HEADS // heads_per_step, n, bsz),
        in_specs=[own, win(0), win(1), win(2), ctx, win(0), win(1), win(2), ctx,
                  pl.BlockSpec((1, heads_per_step, TOK_TILE, NA_KEY_TILES * TOK_TILE), cls)],
        out_specs=own,
        out_shape=jax.ShapeDtypeStruct((bsz, t, C_WIDTH), MXU_DTYPE),
        compiler_params=_cparams(("parallel", "parallel", "arbitrary")),
        name="neighborhood_attention",
    )(qn, kn, kn, kn, kn, vn, vn, vn, vn, bias_tabs)


def _out_proj_body(ya_ref, of_ref, ob_ref, r_ref, yn_ref, x_ref, mod_ref, gg_ref, g2_ref, w_ref,
                   wr_ref, br_ref, xo_ref, h_ref, lg_ref):
    o = of_ref[0] + ob_ref[0]
    hi = lax.broadcasted_iota(jnp.int32, (B_WIDTH, B_WIDTH), 0) // B_DV
    hj = lax.broadcasted_iota(jnp.int32, (B_WIDTH, B_WIDTH), 1) // B_DV
    head_mean = jnp.where(hi == hj, 1.0 / B_DV, 0.0).astype(F32)
    ms = jnp.dot(o * o, head_mean, precision=HI, preferred_element_type=F32)
    yb = o * lax.rsqrt(ms + NORM_EPS) * gg_ref[...] * _silu(r_ref[0])
    mix = (jnp.dot(ya_ref[0], w_ref[0:A_WIDTH, :], preferred_element_type=F32)
           + jnp.dot(yb.astype(MXU_DTYPE), w_ref[A_WIDTH:A_WIDTH + B_WIDTH, :], preferred_element_type=F32)
           + jnp.dot(yn_ref[0], w_ref[A_WIDTH + B_WIDTH:, :], preferred_element_type=F32))
    x = x_ref[0] + mod_ref[2:3, :] * mix
    xo_ref[0] = x
    h = _rms(x, g2_ref[...]) * (1.0 + mod_ref[4:5, :]) + mod_ref[3:4, :]
    h_ref[0] = h.astype(h_ref.dtype)
    lg_ref[0] = jnp.dot(h, wr_ref[...], precision=HI, preferred_element_type=F32) + br_ref[...]


def _out_proj(layer, ya, o_f, o_b, bg, yn, x, modsel, g_gla, g2, w_out_b, w_router, b_router):
    bsz, t, d = x.shape
    tm = TOK_TILE
    tok = lambda b, i: (b, i, 0)
    const = lambda b, i: (0, 0)
    r_block = (2 * B_QK + B_WIDTH) // B_WIDTH
    return pl.pallas_call(
        _out_proj_body,
        grid=(bsz, t // tm),
        in_specs=[pl.BlockSpec((1, tm, A_WIDTH), tok),
                  pl.BlockSpec((1, tm, B_WIDTH), tok), pl.BlockSpec((1, tm, B_WIDTH), tok),
                  pl.BlockSpec((1, tm, B_WIDTH), lambda b, i: (b, i, r_block)),
                  pl.BlockSpec((1, tm, C_WIDTH), tok),
                  pl.BlockSpec((1, tm, d), tok),
                  pl.BlockSpec((None, None, None, 6, d), lambda b, i: (layer, b, jnp.minimum(i, 1), 0, 0)),
                  pl.BlockSpec((1, B_WIDTH), const), pl.BlockSpec((1, d), const),
                  pl.BlockSpec(w_out_b.shape, const),
                  pl.BlockSpec((d, ROUTER_PAD), const), pl.BlockSpec((1, ROUTER_PAD), const)],
        out_specs=[pl.BlockSpec((1, tm, d), tok), pl.BlockSpec((1, tm, d), tok),
                   pl.BlockSpec((1, tm, ROUTER_PAD), tok)],
        out_shape=[jax.ShapeDtypeStruct((bsz, t, d), F32), jax.ShapeDtypeStruct((bsz, t, d), MXU_DTYPE),
                   jax.ShapeDtypeStruct((bsz, t, ROUTER_PAD), F32)],
        compiler_params=_cparams(("parallel", "arbitrary")),
        name="out_proj",
    )(ya, o_f, o_b, bg, yn, x, modsel, jnp.tile(g_gla, B_HEADS).reshape(1, B_WIDTH), g2.reshape(1, d),
      w_out_b, w_router, b_router)


def _route(logits):
    logit_g = logits[:, :N_GROUPS]
    grp = jnp.argmax(logit_g, axis=-1)
    p_grp = jnp.take_along_axis(jax.nn.softmax(logit_g, axis=-1), grp[:, None], axis=1)
    logit_e = logits[:, N_GROUPS:N_GROUPS + N_EXPERTS].reshape(-1, N_GROUPS, EXPERTS_PER_GROUP)
    logit_e = jnp.take_along_axis(logit_e, grp[:, None, None], axis=1)[:, 0]
    top_v, top_i = lax.top_k(logit_e, TOP_K)
    gate = jax.nn.softmax(top_v, axis=-1) * p_grp
    expert = grp[:, None] * EXPERTS_PER_GROUP + top_i
    return expert.astype(jnp.int32), gate


def _dispatch_plan(expert, n_blocks):
    n_assign = expert.size
    flat = expert.reshape(-1)
    order = jnp.argsort(flat)
    e_sorted = flat[order]
    counts = jnp.bincount(flat, length=N_EXPERTS)
    padded = (counts + EXPERT_TILE - 1) // EXPERT_TILE * EXPERT_TILE
    start = jnp.cumsum(counts) - counts
    pad_end = jnp.cumsum(padded)
    pad_start = pad_end - padded
    dest_sorted = (pad_start[e_sorted] + jnp.arange(n_assign) - start[e_sorted]).astype(jnp.int32)
    dest = jnp.zeros((n_assign,), jnp.int32).at[order].set(dest_sorted)
    src_tok = jnp.zeros((n_blocks * EXPERT_TILE,), jnp.int32).at[dest_sorted].set((order // TOP_K).astype(jnp.int32))
    blk_start = jnp.arange(n_blocks) * EXPERT_TILE
    blk_expert = jnp.minimum(jnp.searchsorted(pad_end, blk_start, side='right'), N_EXPERTS - 1).astype(jnp.int32)
    blk_used = (blk_start < pad_end[-1]).astype(jnp.int32)
    return src_tok, dest.reshape(-1, TOP_K), blk_expert, blk_used


def _expert_body(be_ref, used_ref, x_ref, gate_ref, wu_ref, wd_ref, o_ref, wub_ref, wdb_ref):
    i = pl.program_id(0)
    prev = be_ref[jnp.maximum(i - 1, 0)]

    @pl.when((i == 0) | (be_ref[i] != prev))
    def _():
        wub_ref[...] = wu_ref[...].astype(wub_ref.dtype)
        wdb_ref[...] = wd_ref[...].astype(wdb_ref.dtype)

    @pl.when(used_ref[i] > 0)
    def _():
        gu = jnp.dot(x_ref[...], wub_ref[...], preferred_element_type=F32)
        act = _silu(gu[:, :EXPERT_HIDDEN]) * gu[:, EXPERT_HIDDEN:]
        y = jnp.dot(act.astype(MXU_DTYPE), wdb_ref[...], preferred_element_type=F32)
        o_ref[...] = y * gate_ref[...]

    @pl.when(used_ref[i] == 0)
    def _():
        o_ref[...] = jnp.zeros_like(o_ref)


def _expert_ffn(layer, buf, row_gate, blk_expert, blk_used, w_up, w_down):
    n_rows, d = buf.shape
    n_blocks = n_rows // EXPERT_TILE
    h2 = w_up.shape[-1]
    grid_spec = pltpu.PrefetchScalarGridSpec(
        num_scalar_prefetch=2,
        grid=(n_blocks,),
        in_specs=[pl.BlockSpec((EXPERT_TILE, d), lambda i, be, us: (i, 0)),
                  pl.BlockSpec((EXPERT_TILE, 1), lambda i, be, us: (i, 0)),
                  pl.BlockSpec((None, None, d, h2), lambda i, be, us: (layer, be[i], 0, 0)),
                  pl.BlockSpec((None, None, h2 // 2, d), lambda i, be, us: (layer, be[i], 0, 0))],
        out_specs=pl.BlockSpec((EXPERT_TILE, d), lambda i, be, us: (i, 0)),
        scratch_shapes=[pltpu.VMEM((d, h2), MXU_DTYPE), pltpu.VMEM((h2 // 2, d), MXU_DTYPE)],
    )
    return pl.pallas_call(
        _expert_body,
        grid_spec=grid_spec,
        out_shape=jax.ShapeDtypeStruct((n_rows, d), F32),
        compiler_params=_cparams(("arbitrary",)),
        name="expert_ffn",
    )(blk_expert, blk_used, buf, row_gate, w_up, w_down)


def _moe(layer, h2, logits, w_up, w_down):
    bsz, t, d = h2.shape
    n_tok = bsz * t
    n_assign = n_tok * TOP_K
    n_blocks = -(-(n_assign + N_EXPERTS * (EXPERT_TILE - 1)) // EXPERT_TILE)
    expert, gate = _route(logits.reshape(n_tok, ROUTER_PAD))
    src_tok, dest, blk_expert, blk_used = _dispatch_plan(expert, n_blocks)
    buf = jnp.take(h2.reshape(n_tok, d), src_tok, axis=0)
    row_gate = jnp.zeros((n_blocks * EXPERT_TILE,), F32).at[dest.reshape(-1)].set(gate.reshape(-1))
    y = _expert_ffn(layer, buf, row_gate[:, None], blk_expert, blk_used, w_up, w_down)
    y0 = jnp.take(y, dest[:, 0], axis=0).reshape(bsz, t, d)
    y1 = jnp.take(y, dest[:, 1], axis=0).reshape(bsz, t, d)
    return y0, y1


def _final_body(x_ref, y0_ref, y1_ref, mod_ref, g_ref, o_ref):
    x = x_ref[0] + mod_ref[5:6, :] * (y0_ref[0] + y1_ref[0])
    o_ref[0] = _rms(x, g_ref[...])


def _final_norm(layer, x, y0, y1, modsel, g, ctx_tiles):
    bsz, t, d = x.shape
    tm = TOK_TILE
    lat = lambda b, i: (b, i + ctx_tiles, 0)
    return pl.pallas_call(
        _final_body,
        grid=(bsz, t // tm - ctx_tiles),
        in_specs=[pl.BlockSpec((1, tm, d), lat)] * 3
        + [pl.BlockSpec((None, None, None, 6, d), lambda b, i: (layer, b, 1, 0, 0)),
           pl.BlockSpec((1, d), lambda b, i: (0, 0))],
        out_specs=pl.BlockSpec((1, tm, d), lambda b, i: (b, i, 0)),
        out_shape=jax.ShapeDtypeStruct((bsz, t - ctx_tiles * tm, d), F32),
        compiler_params=_cparams(("parallel", "arbitrary")),
        name="final_norm",
    )(x, y0, y1, modsel, g.reshape(1, d))


def _rope_tables(n_ctx, n_lat):
    t = jnp.arange(n_lat)
    row = (t // GRID_W).astype(F32)
    col = (t % GRID_W).astype(F32)
    n_freq = HEAD_DIM // 4
    inv = ROPE_THETA ** (-jnp.arange(n_freq, dtype=F32) / n_freq)
    ang_r = row[:, None] * inv
    ang_c = col[:, None] * inv
    cr, sr, cc, sc = jnp.cos(ang_r), jnp.sin(ang_r), jnp.cos(ang_c), jnp.sin(ang_c)
    z = jnp.zeros_like(sr)
    cos = jnp.concatenate([cr, cr, cc, cc], axis=-1)
    above = jnp.concatenate([-sr, z, -sc, z], axis=-1)
    below = jnp.concatenate([z, sr, z, sc], axis=-1)
    reps = LANES // HEAD_DIM

    def full(tab, ctx_value):
        tab = jnp.tile(tab, (1, reps))
        return jnp.concatenate([jnp.full((n_ctx, LANES), ctx_value, F32), tab], axis=0)

    return full(cos, 1.0), full(above, 0.0), full(below, 0.0)


def _pack_w_in(w_in):
    parts = jnp.split(w_in, np.cumsum(IN_SIZES)[:-1].tolist(), axis=-1)
    qa, ka, va, qb, kb, vb, rb, gb, qn, kn, vn = parts
    gb = jnp.pad(gb, ((0, 0), (0, GATE_PAD - gb.shape[-1])))
    return jnp.concatenate([qa, ka, va, qb, kb, vb, rb, gb, qn, kn, vn], axis=-1).astype(MXU_DTYPE)


def kernel(x, c, ctx, c_ctx, w_mod, b_mod, norm1_g, norm2_g, w_in, w_out, diff_lambda, diff_sub_g,
           gla_w_decay, gla_b_decay, gla_norm_g, na_rel_bias, w_router_group, b_router_group,
           w_router_expert, b_router_expert, w_expert_up, w_expert_down, final_g):
    bsz, seq, d = x.shape
    n_ctx = ctx.shape[1]
    depth = w_mod.shape[0]
    assert n_ctx == TOK_TILE and seq % (NA_TILE_ROWS * GRID_W) == 0 and d % LANES == 0
    assert seq // (NA_TILE_ROWS * GRID_W) >= NA_KEY_TILES
    mod_rows = -(-(bsz + 1) // 8) * 8
    cvec = jnp.zeros((mod_rows, d), F32).at[:bsz].set(c).at[bsz].set(c_ctx)
    mod = _modulation(cvec, w_mod, b_mod).reshape(depth, mod_rows, 6, d)
    modsel = jnp.stack([jnp.broadcast_to(mod[:, bsz][:, None], (depth, bsz, 6, d)), mod[:, :bsz]], axis=2)
    rope_tabs = _rope_tables(n_ctx, seq)
    xt = jnp.concatenate([ctx, x], axis=1)
    y0 = y1 = None
    for l in range(depth):
        outs = _in_proj(l, xt, y0, y1, modsel, norm1_g[l], _pack_w_in(w_in[l]), rope_tabs)
        qa, ka, va, bg, qn, kn, vn = outs[:7]
        if y0 is not None:
            xt = outs[7]
        ya = _diff_attention(l, qa, ka, va, diff_lambda[l], diff_sub_g[l])
        o_f, o_b = _gla_scan(bg, gla_w_decay[l], gla_b_decay[l])
        yn = _neighborhood_attention(qn, kn, vn, _na_bias_tables(na_rel_bias[l], seq // GRID_W))
        w_router = jnp.pad(jnp.concatenate([w_router_group[l], w_router_expert[l]], axis=-1),
                           ((0, 0), (0, ROUTER_PAD - N_GROUPS - N_EXPERTS)))
        b_router = jnp.pad(jnp.concatenate([b_router_group[l], b_router_expert[l]]),
                           (0, ROUTER_PAD - N_GROUPS - N_EXPERTS)).reshape(1, ROUTER_PAD)
        xt, h2, logits = _out_proj(l, ya, o_f, o_b, bg, yn, xt, modsel, gla_norm_g[l], norm2_g[l],
                                   w_out[l].astype(MXU_DTYPE), w_router, b_router)
        y0, y1 = _moe(l, h2, logits, w_expert_up, w_expert_down)
    return _final_norm(depth - 1, xt, y0, y1, modsel, final_g, n_ctx // TOK_TILE)
```

```python
import functools
import math

import numpy as np
import jax
import jax.numpy as jnp
from jax import lax
from jax.experimental import pallas as pl
from jax.experimental.pallas import tpu as pltpu
from jax.experimental.pallas import tpu_sc as plsc

F32 = jnp.float32
MXU_DTYPE = jnp.bfloat16
HI = lax.Precision.HIGHEST

GRID_W = 64
HEAD_DIM = 64
ROPE_THETA = 10000.0
NORM_EPS = 1e-6

A_HEADS = 4
A_QK = HEAD_DIM
A_V = 2 * HEAD_DIM
B_HEADS = 4
B_DK = HEAD_DIM // 2
B_DV = HEAD_DIM
B_GATE_RANK = 16
B_GATE_TAU = 16.0
B_CHUNK = 64
C_HEADS = 4
C_DH = HEAD_DIM
NA_ROWS = 8
NA_COLS = 16

A_WIDTH = A_HEADS * A_V
B_WIDTH = B_HEADS * B_DV
C_WIDTH = C_HEADS * C_DH
B_QK = B_HEADS * B_DK
IN_SIZES = (A_HEADS * 2 * A_QK, A_HEADS * 2 * A_QK, A_WIDTH,
            B_QK, B_QK, B_WIDTH, B_WIDTH, 2 * B_GATE_RANK,
            C_WIDTH, C_WIDTH, C_WIDTH)

N_GROUPS = 4
EXPERTS_PER_GROUP = 8
N_EXPERTS = N_GROUPS * EXPERTS_PER_GROUP
TOP_K = 2
EXPERT_HIDDEN = 512

LANES = 128
TOK_TILE = 256
NA_TILE_ROWS = 4
NA_KEY_TILES = 3
EXPERT_TILE = 256
GATE_PAD = LANES
BG_WIDTH = 2 * B_QK + 2 * B_WIDTH + GATE_PAD
IN_PAD_WIDTH = 3 * A_WIDTH + BG_WIDTH + 3 * C_WIDTH
ROUTER_PAD = LANES
ROUTE_EXPERT = 0
ROUTE_GATE = 2
SC_CORES = 2
SC_SUBCORES = 16
SC_ROW_BUFFER_BYTES = 128 * 1024
VMEM_LIMIT = 48 * 1024 * 1024


def _silu(x):
    return x * (1.0 / (1.0 + jnp.exp(-x)))


def _cparams(sem):
    return pltpu.CompilerParams(dimension_semantics=sem, vmem_limit_bytes=VMEM_LIMIT)


def _mod_body(c_ref, w_ref, b_ref, o_ref):
    a = _silu(c_ref[...])
    o_ref[...] = jnp.dot(a, w_ref[...], precision=HI, preferred_element_type=F32) + b_ref[...]


def _modulation(cvec, w_mod, b_mod):
    depth, d, d6 = w_mod.shape
    rows = cvec.shape[0]
    return pl.pallas_call(
        _mod_body,
        grid=(depth, d6 // d),
        in_specs=[pl.BlockSpec((rows, d), lambda l, j: (0, 0)),
                  pl.BlockSpec((None, d, d), lambda l, j: (l, 0, j)),
                  pl.BlockSpec((None, 1, d), lambda l, j: (l, 0, j))],
        out_specs=pl.BlockSpec((None, rows, d), lambda l, j: (l, 0, j)),
        out_shape=jax.ShapeDtypeStruct((depth, rows, d6), F32),
        compiler_params=_cparams(("arbitrary", "arbitrary")),
        name="modulation",
    )(cvec, w_mod, b_mod.reshape(depth, 1, d6))


def _rms(x, g):
    return x * lax.rsqrt(jnp.mean(x * x, axis=-1, keepdims=True) + NORM_EPS) * g


def _rope(x, cos, sa, sb):
    return x * cos + pltpu.roll(x, LANES - 16, 1) * sa + pltpu.roll(x, 16, 1) * sb


def _moe_residual(x_ref, y0_ref, y1_ref, rt_ref, mod_ref):
    rt = rt_ref[0]
    moe = rt[:, ROUTE_GATE:ROUTE_GATE + 1] * y0_ref[0] + rt[:, ROUTE_GATE + 1:ROUTE_GATE + 2] * y1_ref[0]
    return x_ref[0] + mod_ref[5:6, :] * moe


def _in_proj_body(combine, *refs):
    if combine:
        (x_ref, y0_ref, y1_ref, rt_ref, pmod_ref, mod_ref, g_ref, w_ref, cos_ref, sa_ref, sb_ref,
         qa_ref, ka_ref, va_ref, bg_ref, qn_ref, kn_ref, vn_ref, xo_ref) = refs
        x = _moe_residual(x_ref, y0_ref, y1_ref, rt_ref, pmod_ref)
        xo_ref[0] = x
    else:
        (x_ref, mod_ref, g_ref, w_ref, cos_ref, sa_ref, sb_ref,
         qa_ref, ka_ref, va_ref, bg_ref, qn_ref, kn_ref, vn_ref) = refs
        x = x_ref[0]
    h = _rms(x, g_ref[...]) * (1.0 + mod_ref[1:2, :]) + mod_ref[0:1, :]
    hb = h.astype(MXU_DTYPE)

    def proj(lo, hi):
        return jnp.dot(hb, w_ref[:, lo:hi], preferred_element_type=F32)

    cos, sa, sb = cos_ref[...], sa_ref[...], sb_ref[...]
    for hh in range(A_HEADS):
        lo = hh * LANES
        q = proj(lo, lo + LANES)
        qa_ref[0, :, lo:lo + LANES] = (_rope(q, cos, sa, sb) * (A_QK ** -0.5)).astype(qa_ref.dtype)
        k = proj(A_WIDTH + lo, A_WIDTH + lo + LANES)
        ka_ref[0, :, lo:lo + LANES] = _rope(k, cos, sa, sb).astype(ka_ref.dtype)
    o = 2 * A_WIDTH
    va_ref[0] = proj(o, o + A_WIDTH).astype(va_ref.dtype)
    o += A_WIDTH
    bg_ref[0] = proj(o, o + BG_WIDTH)
    o += BG_WIDTH
    qn_ref[0] = (proj(o, o + C_WIDTH) * (C_DH ** -0.5)).astype(qn_ref.dtype)
    o += C_WIDTH
    kn_ref[0] = proj(o, o + C_WIDTH).astype(kn_ref.dtype)
    o += C_WIDTH
    vn_ref[0] = proj(o, o + C_WIDTH).astype(vn_ref.dtype)


def _moe_specs(tm, d, row_block):
    return [pl.BlockSpec((None, 1, tm, d), lambda b, i: (0, b, row_block(i), 0)),
            pl.BlockSpec((None, 1, tm, d), lambda b, i: (1, b, row_block(i), 0)),
            pl.BlockSpec((1, tm, ROUTER_PAD), lambda b, i: (b, row_block(i), 0))]


def _in_proj(layer, x, ymoe, route, modsel, g1, w_in_p, rope_tabs):
    bsz, t, d = x.shape
    tm = TOK_TILE
    combine = ymoe is not None
    tok = lambda b, i: (b, i, 0)
    x_spec = pl.BlockSpec((1, tm, d), tok)

    def mod_spec(l):
        return pl.BlockSpec((None, None, None, 6, d), lambda b, i: (l, b, jnp.minimum(i, 1), 0, 0))

    tab_spec = pl.BlockSpec((tm, LANES), lambda b, i: (i, 0))
    in_specs = [x_spec]
    args = [x]
    if combine:
        in_specs += _moe_specs(tm, d, lambda i: i) + [mod_spec(layer - 1)]
        args += [ymoe, ymoe, route, modsel]
    in_specs += [mod_spec(layer), pl.BlockSpec((1, d), lambda b, i: (0, 0)),
                 pl.BlockSpec((d, IN_PAD_WIDTH), lambda b, i: (0, 0)), tab_spec, tab_spec, tab_spec]
    args += [modsel, g1.reshape(1, d), w_in_p, *rope_tabs]

    def o(width, dtype):
        return pl.BlockSpec((1, tm, width), tok), jax.ShapeDtypeStruct((bsz, t, width), dtype)

    outs = [o(A_WIDTH, MXU_DTYPE), o(A_WIDTH, MXU_DTYPE), o(A_WIDTH, MXU_DTYPE), o(BG_WIDTH, F32),
            o(C_WIDTH, MXU_DTYPE), o(C_WIDTH, MXU_DTYPE), o(C_WIDTH, MXU_DTYPE)]
    if combine:
        outs.append(o(d, F32))
    return pl.pallas_call(
        functools.partial(_in_proj_body, combine),
        grid=(bsz, t // tm),
        in_specs=in_specs,
        out_specs=[s for s, _ in outs],
        out_shape=[s for _, s in outs],
        compiler_params=_cparams(("parallel", "arbitrary")),
        name="in_proj",
    )(*args)


def _diff_attn_body(lam_init, ctx_len, q_ref, k_ref, v_ref, lam_ref, g_ref, o_ref):
    q = q_ref[0]
    lane = lax.broadcasted_iota(jnp.int32, (1, LANES), 1)
    zero = jnp.zeros_like(q)
    q1 = jnp.where(lane < A_QK, q, zero)
    q2 = jnp.where(lane >= A_QK, q, zero)
    lm = lam_ref[...]
    lam = (jnp.exp(jnp.sum(lm[0:1] * lm[1:2], axis=1, keepdims=True))
           - jnp.exp(jnp.sum(lm[2:3] * lm[3:4], axis=1, keepdims=True)) + lam_init)

    def softmax_parts(qm, k):
        s = lax.dot_general(qm, k, (((1,), (1,)), ((), ())), preferred_element_type=F32)
        p = jnp.exp(s - jnp.max(s, axis=-1, keepdims=True))
        return p, jnp.sum(p, axis=-1, keepdims=True)

    def attend(n_keys):
        k = k_ref[0, :n_keys, :]
        v = v_ref[0, :n_keys, :]
        p1, l1 = softmax_parts(q1, k)
        p2, l2 = softmax_parts(q2, k)
        dmat = p1 * (1.0 / l1) - p2 * (lam / l2)
        o = jnp.dot(dmat.astype(MXU_DTYPE), v, preferred_element_type=F32)
        o_ref[0] = (_rms(o, g_ref[...]) * (1.0 - lam_init)).astype(o_ref.dtype)

    @pl.when(pl.program_id(2) == 0)
    def _():
        attend(ctx_len)

    @pl.when(pl.program_id(2) > 0)
    def _():
        attend(k_ref.shape[1])


def _diff_attention(layer, qa, ka, va, lam, g_sub):
    bsz, t, _ = qa.shape
    tq = TOK_TILE
    lam_init = 0.8 - 0.6 * math.exp(-0.3 * layer)
    kv_spec = pl.BlockSpec((1, t, LANES), lambda b, h, i: (b, 0, h))
    return pl.pallas_call(
        functools.partial(_diff_attn_body, lam_init, TOK_TILE),
        grid=(bsz, A_HEADS, t // tq),
        in_specs=[pl.BlockSpec((1, tq, LANES), lambda b, h, i: (b, i, h)), kv_spec, kv_spec,
                  pl.BlockSpec((4, A_QK), lambda b, h, i: (0, 0)),
                  pl.BlockSpec((1, A_V), lambda b, h, i: (0, 0))],
        out_specs=pl.BlockSpec((1, tq, LANES), lambda b, h, i: (b, i, h)),
        out_shape=jax.ShapeDtypeStruct((bsz, t, A_WIDTH), MXU_DTYPE),
        compiler_params=_cparams(("parallel", "parallel", "arbitrary")),
        name="diff_attention",
    )(qa, ka, va, lam, g_sub.reshape(1, A_V))


def _gla_body(f_ref, r_ref, wdec_ref, bdec_ref, of_ref, ob_ref, sf_ref, sb_ref, e_ref, b_ref, qs_ref):
    c = B_CHUNK
    n_chunks = TOK_TILE // c

    @pl.when(pl.program_id(1) == 0)
    def _():
        sf_ref[...] = jnp.zeros_like(sf_ref)
        sb_ref[...] = jnp.zeros_like(sb_ref)

    row = lax.broadcasted_iota(jnp.int32, (c, c), 0)
    col = lax.broadcasted_iota(jnp.int32, (c, c), 1)
    tri_f = (col <= row).astype(F32)
    tri_b = (col >= row).astype(F32)
    s_iota = lax.broadcasted_iota(jnp.int32, (c, LANES), 0)
    head_of_k = lax.broadcasted_iota(jnp.int32, (B_QK, B_WIDTH), 0) // B_DK
    head_of_v = lax.broadcasted_iota(jnp.int32, (B_QK, B_WIDTH), 1) // B_DV
    expand = (head_of_k == head_of_v).astype(MXU_DTYPE)
    same_head_t = (lax.broadcasted_iota(jnp.int32, (B_WIDTH, B_QK), 0) // B_DV
                   == lax.broadcasted_iota(jnp.int32, (B_WIDTH, B_QK), 1) // B_DK)

    def chunk(src_ref, lo, backward, st_ref, out_ref):
        q = src_ref[0, lo:lo + c, 0:B_QK] * (B_DK ** -0.5)
        k = src_ref[0, lo:lo + c, B_QK:2 * B_QK]
        v = src_ref[0, lo:lo + c, 2 * B_QK:2 * B_QK + B_WIDTH]
        gl = src_ref[0, lo:lo + c, 2 * B_QK + 2 * B_WIDTH:BG_WIDTH]
        d0 = B_QK if backward else 0
        z = jnp.dot(gl, wdec_ref[:, d0:d0 + B_QK], precision=HI, preferred_element_type=F32) \
            + bdec_ref[:, d0:d0 + B_QK]
        log_a = (jnp.minimum(z, 0.0) - jnp.log(1.0 + jnp.exp(-jnp.abs(z)))) / B_GATE_TAU
        b = jnp.dot(tri_b if backward else tri_f, log_a, precision=HI, preferred_element_type=F32)

        def fill(tt, carry):
            bt = b_ref[pl.ds(tt, 1), :]
            qt = qs_ref[pl.ds(tt, 1), :]
            keep = (s_iota >= tt) if backward else (s_iota <= tt)
            e = jnp.exp(jnp.where(keep, bt - b, -jnp.inf)) * (qt * k)
            e_ref[pl.ds(pl.multiple_of(tt * c, c), c), :] = e.astype(e_ref.dtype)
            return carry

        b_ref[...] = b
        qs_ref[...] = q
        lax.fori_loop(0, c, fill, 0)
        a_exp = jnp.dot(e_ref[...], expand, preferred_element_type=F32)
        o_intra = jnp.sum(a_exp.reshape(c, c, B_WIDTH) * v[None, :, :], axis=1)
        st = st_ref[...]
        o_inter = lax.dot_general(q * jnp.exp(b), st, (((1,), (1,)), ((), ())),
                                  preferred_element_type=F32)
        out_ref[0, lo:lo + c, :] = o_intra + o_inter
        b_end = b[0:1, :] if backward else b[c - 1:c, :]
        kd = k * jnp.exp(b_end - b)
        upd = lax.dot_general(v, kd, (((0,), (0,)), ((), ())), preferred_element_type=F32)
        st_ref[...] = jnp.exp(b_end) * st + jnp.where(same_head_t, upd, 0.0)

    for ci in range(n_chunks):
        chunk(f_ref, ci * c, False, sf_ref, of_ref)
        chunk(r_ref, (n_chunks - 1 - ci) * c, True, sb_ref, ob_ref)


def _gla_scan(bg, w_dec, b_dec):
    bsz, t, _ = bg.shape
    n = t // TOK_TILE
    rev = lambda b, i: (b, jnp.where(i == 0, 0, n - i), 0)
    fwd = lambda b, i: (b, i, 0)
    wdec = jnp.zeros((GATE_PAD, 2 * B_QK), F32)
    wdec = wdec.at[:B_GATE_RANK, :B_QK].set(w_dec[0]).at[B_GATE_RANK:2 * B_GATE_RANK, B_QK:].set(w_dec[1])
    bdec = b_dec.reshape(1, 2 * B_QK)
    o_shape = jax.ShapeDtypeStruct((bsz, t, B_WIDTH), F32)
    return pl.pallas_call(
        _gla_body,
        grid=(bsz, n),
        in_specs=[pl.BlockSpec((1, TOK_TILE, BG_WIDTH), fwd),
                  pl.BlockSpec((1, TOK_TILE, BG_WIDTH), rev),
                  pl.BlockSpec((GATE_PAD, 2 * B_QK), lambda b, i: (0, 0)),
                  pl.BlockSpec((1, 2 * B_QK), lambda b, i: (0, 0))],
        out_specs=[pl.BlockSpec((1, TOK_TILE, B_WIDTH), fwd),
                   pl.BlockSpec((1, TOK_TILE, B_WIDTH), rev)],
        out_shape=[o_shape, o_shape],
        scratch_shapes=[pltpu.VMEM((B_WIDTH, B_QK), F32), pltpu.VMEM((B_WIDTH, B_QK), F32),
                        pltpu.VMEM((B_CHUNK * B_CHUNK, LANES), MXU_DTYPE),
                        pltpu.VMEM((B_CHUNK, LANES), F32), pltpu.VMEM((B_CHUNK, LANES), F32)],
        compiler_params=_cparams(("parallel", "arbitrary")),
        name="gla_scan",
    )(bg, bg, wdec, bdec)


def _na_bias_tables(rpb, rows):
    n_tiles = rows // NA_TILE_ROWS
    wr = min(NA_ROWS, rows)
    qr = np.arange(NA_TILE_ROWS)[:, None, None, None]
    cq = np.arange(GRID_W)[None, :, None, None]
    kw = np.arange(NA_KEY_TILES * NA_TILE_ROWS)[None, None, :, None]
    ck = np.arange(GRID_W)[None, None, None, :]
    cs = np.clip(cq - NA_COLS // 2, 0, GRID_W - NA_COLS)
    col_ok = (ck >= cs) & (ck < cs + NA_COLS)
    dc = np.clip(ck - cq, -(NA_COLS - 1), NA_COLS - 1) + (NA_COLS - 1)
    tabs = []
    for j in (0, 1, n_tiles - 1):
        r = j * NA_TILE_ROWS + qr
        kr = int(np.clip(j - 1, 0, n_tiles - NA_KEY_TILES)) * NA_TILE_ROWS + kw
        start = np.clip(r - wr // 2, 0, rows - wr)
        ok = (kr >= start) & (kr < start + wr) & col_ok
        dr = np.clip(kr - r + (NA_ROWS - 1), 0, 2 * NA_ROWS - 2)
        shape = np.broadcast_shapes(ok.shape, dr.shape, dc.shape)
        n_q, n_k = NA_TILE_ROWS * GRID_W, NA_KEY_TILES * NA_TILE_ROWS * GRID_W
        dr_i = np.broadcast_to(dr, shape).reshape(n_q, n_k)
        dc_i = np.broadcast_to(dc, shape).reshape(n_q, n_k)
        ok_i = np.broadcast_to(ok, shape).reshape(n_q, n_k)
        tabs.append(jnp.where(ok_i[None], rpb[:, dr_i, dc_i].astype(F32), -jnp.inf))
    return jnp.stack(tabs)


def _na_body(q_ref, k0_ref, k1_ref, k2_ref, kc_ref, v0_ref, v1_ref, v2_ref, vc_ref, m_ref, o_ref):
    q = q_ref[0]
    lane = lax.broadcasted_iota(jnp.int32, (1, LANES), 1)
    zero = jnp.zeros_like(q)
    nt = (((1,), (1,)), ((), ()))

    def scores(qm, k_ref):
        return lax.dot_general(qm, k_ref[0], nt, preferred_element_type=F32)

    def head_out(hh, windows):
        qm = jnp.where((lane >= hh * C_DH) & (lane < (hh + 1) * C_DH), q, zero)
        s = [scores(qm, kc_ref)]
        for w, k_ref in enumerate(windows):
            s.append(scores(qm, k_ref) + m_ref[0, hh, :, w * TOK_TILE:(w + 1) * TOK_TILE])
        m = functools.reduce(jnp.maximum, [jnp.max(x, axis=-1, keepdims=True) for x in s])
        p = [jnp.exp(x - m) for x in s]
        den = functools.reduce(jnp.add, [jnp.sum(x, axis=-1, keepdims=True) for x in p])
        vals = [vc_ref] + [v0_ref, v1_ref, v2_ref][:len(windows)]
        o = functools.reduce(jnp.add, [jnp.dot(x.astype(MXU_DTYPE), v_ref[0], preferred_element_type=F32)
                                       for x, v_ref in zip(p, vals)])
        return o * (1.0 / den)

    def emit(windows):
        o0 = head_out(0, windows)
        o1 = head_out(1, windows)
        o_ref[0] = jnp.where(lane < C_DH, o0, o1).astype(o_ref.dtype)

    @pl.when(pl.program_id(1) == 0)
    def _():
        emit([])

    @pl.when(pl.program_id(1) > 0)
    def _():
        emit([k0_ref, k1_ref, k2_ref])


def _neighborhood_attention(qn, kn, vn, bias_tabs):
    bsz, t, _ = qn.shape
    n = t // TOK_TILE
    n_lat = n - 1

    def win(w):
        def index(hp, i, b):
            kb0 = jnp.clip(i - 2, 0, n_lat - NA_KEY_TILES)
            return (b, kb0 + 1 + w, hp)
        return pl.BlockSpec((1, TOK_TILE, LANES), index)

    def cls(hp, i, b):
        j = i - 1
        return (jnp.where(j <= 0, 0, jnp.where(j == n_lat - 1, 2, 1)), hp, 0, 0)

    own = pl.BlockSpec((1, TOK_TILE, LANES), lambda hp, i, b: (b, i, hp))
    ctx = pl.BlockSpec((1, TOK_TILE, LANES), lambda hp, i, b: (b, 0, hp))
    heads_per_step = LANES // C_DH
    return pl.pallas_call(
        _na_body,
        grid=(C_HEADS // heads_per_step, n, bsz),
        in_specs=[own, win(0), win(1), win(2), ctx, win(0), win(1), win(2), ctx,
                  pl.BlockSpec((1, heads_per_step, TOK_TILE, NA_KEY_TILES * TOK_TILE), cls)],
        out_specs=own,
        out_shape=jax.ShapeDtypeStruct((bsz, t, C_WIDTH), MXU_DTYPE),
        compiler_params=_cparams(("parallel", "parallel", "arbitrary")),
        name="neighborhood_attention",
    )(qn, kn, kn, kn, kn, vn, vn, vn, vn, bias_tabs)


def _out_proj_body(ya_ref, of_ref, ob_ref, r_ref, yn_ref, x_ref, mod_ref, gg_ref, g2_ref, w_ref,
                   wr_ref, br_ref, xo_ref, h_ref, rt_ref):
    o = of_ref[0] + ob_ref[0]
    hi = lax.broadcasted_iota(jnp.int32, (B_WIDTH, B_WIDTH), 0) // B_DV
    hj = lax.broadcasted_iota(jnp.int32, (B_WIDTH, B_WIDTH), 1) // B_DV
    head_mean = jnp.where(hi == hj, 1.0 / B_DV, 0.0).astype(F32)
    ms = jnp.dot(o * o, head_mean, precision=HI, preferred_element_type=F32)
    yb = o * lax.rsqrt(ms + NORM_EPS) * gg_ref[...] * _silu(r_ref[0])
    mix = (jnp.dot(ya_ref[0], w_ref[0:A_WIDTH, :], preferred_element_type=F32)
           + jnp.dot(yb.astype(MXU_DTYPE), w_ref[A_WIDTH:A_WIDTH + B_WIDTH, :], preferred_element_type=F32)
           + jnp.dot(yn_ref[0], w_ref[A_WIDTH + B_WIDTH:, :], preferred_element_type=F32))
    x = x_ref[0] + mod_ref[2:3, :] * mix
    xo_ref[0] = x
    h = _rms(x, g2_ref[...]) * (1.0 + mod_ref[4:5, :]) + mod_ref[3:4, :]
    h_ref[0] = h.astype(h_ref.dtype)
    logits = jnp.dot(h, wr_ref[...], precision=HI, preferred_element_type=F32) + br_ref[...]
    rt_ref[0] = _route(logits)


def _route(lg):
    lane = lax.broadcasted_iota(jnp.int32, lg.shape, 1)
    big = jnp.int32(ROUTER_PAD)

    def top(mask):
        v = jnp.max(jnp.where(mask, lg, -jnp.inf), axis=-1, keepdims=True)
        i = jnp.min(jnp.where(mask & (lg == v), lane, big), axis=-1, keepdims=True)
        return v, i

    g_mask = lane < N_GROUPS
    g_max, grp = top(g_mask)
    p_grp = 1.0 / jnp.sum(jnp.where(g_mask, jnp.exp(lg - g_max), 0.0), axis=-1, keepdims=True)
    e_lo = N_GROUPS + grp * EXPERTS_PER_GROUP
    e_mask = (lane >= e_lo) & (lane < e_lo + EXPERTS_PER_GROUP)
    v1, i1 = top(e_mask)
    v2, i2 = top(e_mask & (lane != i1))
    r = jnp.exp(v2 - v1)
    gate1 = p_grp / (1.0 + r)
    gate2 = p_grp * r / (1.0 + r)
    out = jnp.where(lane == ROUTE_EXPERT, (i1 - N_GROUPS).astype(F32),
                    jnp.where(lane == ROUTE_EXPERT + 1, (i2 - N_GROUPS).astype(F32),
                              jnp.where(lane == ROUTE_GATE, gate1,
                                        jnp.where(lane == ROUTE_GATE + 1, gate2, 0.0))))
    return out


def _out_proj(layer, ya, o_f, o_b, bg, yn, x, modsel, g_gla, g2, w_out_b, w_router, b_router):
    bsz, t, d = x.shape
    tm = TOK_TILE
    tok = lambda b, i: (b, i, 0)
    const = lambda b, i: (0, 0)
    r_block = (2 * B_QK + B_WIDTH) // B_WIDTH
    return pl.pallas_call(
        _out_proj_body,
        grid=(bsz, t // tm),
        in_specs=[pl.BlockSpec((1, tm, A_WIDTH), tok),
                  pl.BlockSpec((1, tm, B_WIDTH), tok), pl.BlockSpec((1, tm, B_WIDTH), tok),
                  pl.BlockSpec((1, tm, B_WIDTH), lambda b, i: (b, i, r_block)),
                  pl.BlockSpec((1, tm, C_WIDTH), tok),
                  pl.BlockSpec((1, tm, d), tok),
                  pl.BlockSpec((None, None, None, 6, d), lambda b, i: (layer, b, jnp.minimum(i, 1), 0, 0)),
                  pl.BlockSpec((1, B_WIDTH), const), pl.BlockSpec((1, d), const),
                  pl.BlockSpec(w_out_b.shape, const),
                  pl.BlockSpec((d, ROUTER_PAD), const), pl.BlockSpec((1, ROUTER_PAD), const)],
        out_specs=[pl.BlockSpec((1, tm, d), tok), pl.BlockSpec((1, tm, d), tok),
                   pl.BlockSpec((1, tm, ROUTER_PAD), tok)],
        out_shape=[jax.ShapeDtypeStruct((bsz, t, d), F32), jax.ShapeDtypeStruct((bsz, t, d), F32),
                   jax.ShapeDtypeStruct((bsz, t, ROUTER_PAD), F32)],
        compiler_params=_cparams(("parallel", "arbitrary")),
        name="out_proj",
    )(ya, o_f, o_b, bg, yn, x, modsel, jnp.tile(g_gla, B_HEADS).reshape(1, B_WIDTH), g2.reshape(1, d),
      w_out_b, w_router, b_router)


def _dispatch_plan(expert, n_blocks):
    n_tok = expert.shape[0]
    flat = expert.reshape(-1)
    n_assign = flat.shape[0]
    order = jnp.argsort(flat).astype(jnp.int32)
    rank = jnp.argsort(order).astype(jnp.int32)
    counts = jnp.sum((flat[:, None] == jnp.arange(N_EXPERTS, dtype=jnp.int32)[None, :]).astype(jnp.int32), axis=0)
    padded = (counts + EXPERT_TILE - 1) // EXPERT_TILE * EXPERT_TILE
    start = jnp.cumsum(counts) - counts
    pad_end = jnp.cumsum(padded)
    pad_start = pad_end - padded
    shift = (pad_start - start).astype(jnp.int32)
    dest = rank + shift[flat]
    blk_start = jnp.arange(n_blocks, dtype=jnp.int32) * EXPERT_TILE
    blk_expert = jnp.minimum(jnp.searchsorted(pad_end, blk_start, side='right'), N_EXPERTS - 1).astype(jnp.int32)
    blk_used = (blk_start < pad_end[-1]).astype(jnp.int32)
    row = jnp.arange(n_blocks * EXPERT_TILE, dtype=jnp.int32)
    e_row = jnp.repeat(blk_expert, EXPERT_TILE)
    real = (row - pad_start[e_row].astype(jnp.int32)) < counts[e_row]
    src_assign = order[jnp.clip(row - shift[e_row], 0, n_assign - 1)]
    src_tok = jnp.where(real, src_assign // TOP_K, 0)
    dest = dest.reshape(n_tok, TOP_K).T.reshape(-1)
    return src_tok, dest, blk_expert, blk_used


def _sc_gather(table, idx):
    n_rows = idx.shape[0]
    width = table.shape[1]
    n_workers = SC_CORES * SC_SUBCORES
    per_worker = n_rows // n_workers
    chunk = SC_ROW_BUFFER_BYTES // (width * table.dtype.itemsize)
    n_chunks = per_worker // chunk
    assert per_worker * n_workers == n_rows and n_chunks * chunk == per_worker and n_chunks % 2 == 0
    assert n_chunks >= 4 and chunk <= LANES
    mesh = plsc.VectorSubcoreMesh(core_axis_name="core", subcore_axis_name="subcore")

    def body(table_hbm, idx_hbm, out_hbm, idx_v, rows_v, gather_sem, write_sem):
        worker = lax.axis_index("subcore") * SC_CORES + lax.axis_index("core")
        base = worker * per_worker
        pltpu.sync_copy(idx_hbm.at[worker], idx_v)

        def gather(j, slot):
            return pltpu.make_async_copy(table_hbm.at[idx_v.at[j]], rows_v.at[slot], gather_sem.at[slot])

        def write(j, slot):
            return pltpu.make_async_copy(rows_v.at[slot], out_hbm.at[pl.ds(base + j * chunk, chunk)],
                                         write_sem.at[slot])

        gather(0, 0).start()
        gather(0, 0).wait()
        gather(1, 1).start()
        write(0, 0).start()

        @pl.loop(1, n_chunks - 1, step=2)
        def _(j):
            for s in range(2):
                slot = (1 + s) % 2
                gather(j + s, slot).wait()
                write(j + s - 1, 1 - slot).wait()
                gather(j + s + 1, 1 - slot).start()
                write(j + s, slot).start()

        last = n_chunks - 1
        gather(last, 1).wait()
        write(last, 1).start()
        write(last - 1, 0).wait()
        write(last, 1).wait()

    return pl.kernel(
        body,
        out_type=jax.ShapeDtypeStruct((n_rows, width), table.dtype),
        mesh=mesh,
        scratch_types=[pltpu.VMEM((n_chunks, chunk), jnp.int32),
                       pltpu.VMEM((2, chunk, width), table.dtype),
                       pltpu.SemaphoreType.DMA((2,)), pltpu.SemaphoreType.DMA((2,))],
        name="sc_row_gather",
    )(table, idx.reshape(n_workers, n_chunks, chunk))


def _expert_body(be_ref, used_ref, x_ref, wu_ref, wd_ref, o_ref, wub_ref, wdb_ref):
    i = pl.program_id(0)
    prev = be_ref[jnp.maximum(i - 1, 0)]

    @pl.when((i == 0) | (be_ref[i] != prev))
    def _():
        wub_ref[...] = wu_ref[...].astype(wub_ref.dtype)
        wdb_ref[...] = wd_ref[...].astype(wdb_ref.dtype)

    @pl.when(used_ref[i] > 0)
    def _():
        gu = jnp.dot(x_ref[...].astype(MXU_DTYPE), wub_ref[...], preferred_element_type=F32)
        act = _silu(gu[:, :EXPERT_HIDDEN]) * gu[:, EXPERT_HIDDEN:]
        o_ref[...] = jnp.dot(act.astype(MXU_DTYPE), wdb_ref[...], preferred_element_type=F32)

    @pl.when(used_ref[i] == 0)
    def _():
        o_ref[...] = jnp.zeros_like(o_ref)


def _expert_ffn(layer, buf, blk_expert, blk_used, w_up, w_down):
    n_rows, d = buf.shape
    n_blocks = n_rows // EXPERT_TILE
    h2 = w_up.shape[-1]
    grid_spec = pltpu.PrefetchScalarGridSpec(
        num_scalar_prefetch=2,
        grid=(n_blocks,),
        in_specs=[pl.BlockSpec((EXPERT_TILE, d), lambda i, be, us: (i, 0)),
                  pl.BlockSpec((None, None, d, h2), lambda i, be, us: (layer, be[i], 0, 0)),
                  pl.BlockSpec((None, None, h2 // 2, d), lambda i, be, us: (layer, be[i], 0, 0))],
        out_specs=pl.BlockSpec((EXPERT_TILE, d), lambda i, be, us: (i, 0)),
        scratch_shapes=[pltpu.VMEM((d, h2), MXU_DTYPE), pltpu.VMEM((h2 // 2, d), MXU_DTYPE)],
    )
    return pl.pallas_call(
        _expert_body,
        grid_spec=grid_spec,
        out_shape=jax.ShapeDtypeStruct((n_rows, d), F32),
        compiler_params=_cparams(("arbitrary",)),
        name="expert_ffn",
    )(blk_expert, blk_used, buf, w_up, w_down)


def _moe(layer, h2, route, w_up, w_down):
    bsz, t, d = h2.shape
    n_tok = bsz * t
    n_assign = n_tok * TOP_K
    n_blocks = -(-(n_assign + N_EXPERTS * (EXPERT_TILE - 1)) // EXPERT_TILE)
    expert = route.reshape(n_tok, ROUTER_PAD)[:, ROUTE_EXPERT:ROUTE_EXPERT + TOP_K].astype(jnp.int32)
    src_tok, dest, blk_expert, blk_used = _dispatch_plan(expert, n_blocks)
    buf = _sc_gather(h2.reshape(n_tok, d), src_tok)
    y = _expert_ffn(layer, buf, blk_expert, blk_used, w_up, w_down)
    return _sc_gather(y, dest).reshape(TOP_K, bsz, t, d)


def _final_body(x_ref, y0_ref, y1_ref, rt_ref, mod_ref, g_ref, o_ref):
    o_ref[0] = _rms(_moe_residual(x_ref, y0_ref, y1_ref, rt_ref, mod_ref), g_ref[...])


def _final_norm(layer, x, ymoe, route, modsel, g, ctx_tiles):
    bsz, t, d = x.shape
    tm = TOK_TILE
    lat = lambda b, i: (b, i + ctx_tiles, 0)
    return pl.pallas_call(
        _final_body,
        grid=(bsz, t // tm - ctx_tiles),
        in_specs=[pl.BlockSpec((1, tm, d), lat)] + _moe_specs(tm, d, lambda i: i + ctx_tiles)
        + [pl.BlockSpec((None, None, None, 6, d), lambda b, i: (layer, b, 1, 0, 0)),
           pl.BlockSpec((1, d), lambda b, i: (0, 0))],
        out_specs=pl.BlockSpec((1, tm, d), lambda b, i: (b, i, 0)),
        out_shape=jax.ShapeDtypeStruct((bsz, t - ctx_tiles * tm, d), F32),
        compiler_params=_cparams(("parallel", "arbitrary")),
        name="final_norm",
    )(x, ymoe, ymoe, route, modsel, g.reshape(1, d))


def _rope_tables(n_ctx, n_lat):
    t = jnp.arange(n_lat)
    row = (t // GRID_W).astype(F32)
    col = (t % GRID_W).astype(F32)
    n_freq = HEAD_DIM // 4
    inv = ROPE_THETA ** (-jnp.arange(n_freq, dtype=F32) / n_freq)
    ang_r = row[:, None] * inv
    ang_c = col[:, None] * inv
    cr, sr, cc, sc = jnp.cos(ang_r), jnp.sin(ang_r), jnp.cos(ang_c), jnp.sin(ang_c)
    z = jnp.zeros_like(sr)
    cos = jnp.concatenate([cr, cr, cc, cc], axis=-1)
    above = jnp.concatenate([-sr, z, -sc, z], axis=-1)
    below = jnp.concatenate([z, sr, z, sc], axis=-1)
    reps = LANES // HEAD_DIM

    def full(tab, ctx_value):
        tab = jnp.tile(tab, (1, reps))
        return jnp.concatenate([jnp.full((n_ctx, LANES), ctx_value, F32), tab], axis=0)

    return full(cos, 1.0), full(above, 0.0), full(below, 0.0)


def _pack_w_in(w_in):
    parts = jnp.split(w_in, np.cumsum(IN_SIZES)[:-1].tolist(), axis=-1)
    qa, ka, va, qb, kb, vb, rb, gb, qn, kn, vn = parts
    gb = jnp.pad(gb, ((0, 0), (0, GATE_PAD - gb.shape[-1])))
    return jnp.concatenate([qa, ka, va, qb, kb, vb, rb, gb, qn, kn, vn], axis=-1).astype(MXU_DTYPE)


def kernel(x, c, ctx, c_ctx, w_mod, b_mod, norm1_g, norm2_g, w_in, w_out, diff_lambda, diff_sub_g,
           gla_w_decay, gla_b_decay, gla_norm_g, na_rel_bias, w_router_group, b_router_group,
           w_router_expert, b_router_expert, w_expert_up, w_expert_down, final_g):
    bsz, seq, d = x.shape
    n_ctx = ctx.shape[1]
    depth = w_mod.shape[0]
    assert n_ctx == TOK_TILE and seq % (NA_TILE_ROWS * GRID_W) == 0 and d % LANES == 0
    assert seq // (NA_TILE_ROWS * GRID_W) >= NA_KEY_TILES
    mod_rows = -(-(bsz + 1) // 8) * 8
    cvec = jnp.zeros((mod_rows, d), F32).at[:bsz].set(c).at[bsz].set(c_ctx)
    mod = _modulation(cvec, w_mod, b_mod).reshape(depth, mod_rows, 6, d)
    modsel = jnp.stack([jnp.broadcast_to(mod[:, bsz][:, None], (depth, bsz, 6, d)), mod[:, :bsz]], axis=2)
    rope_tabs = _rope_tables(n_ctx, seq)
    xt = jnp.concatenate([ctx, x], axis=1)
    ymoe = route = None
    for l in range(depth):
        outs = _in_proj(l, xt, ymoe, route, modsel, norm1_g[l], _pack_w_in(w_in[l]), rope_tabs)
        qa, ka, va, bg, qn, kn, vn = outs[:7]
        if ymoe is not None:
            xt = outs[7]
        ya = _diff_attention(l, qa, ka, va, diff_lambda[l], diff_sub_g[l])
        o_f, o_b = _gla_scan(bg, gla_w_decay[l], gla_b_decay[l])
        yn = _neighborhood_attention(qn, kn, vn, _na_bias_tables(na_rel_bias[l], seq // GRID_W))
        w_router = jnp.pad(jnp.concatenate([w_router_group[l], w_router_expert[l]], axis=-1),
                           ((0, 0), (0, ROUTER_PAD - N_GROUPS - N_EXPERTS)))
        b_router = jnp.pad(jnp.concatenate([b_router_group[l], b_router_expert[l]]),
                           (0, ROUTER_PAD - N_GROUPS - N_EXPERTS)).reshape(1, ROUTER_PAD)
        xt, h2, route = _out_proj(l, ya, o_f, o_b, bg, yn, xt, modsel, gla_norm_g[l], norm2_g[l],
                                  w_out[l].astype(MXU_DTYPE), w_router, b_router)
        ymoe = _moe(l, h2, route, w_expert_up, w_expert_down)
    return _final_norm(depth - 1, xt, ymoe, route, modsel, final_g, n_ctx // TOK_TILE)
```

```python
import functools
import math

import numpy as np
import jax
import jax.numpy as jnp
from jax import lax
from jax.experimental import pallas as pl
from jax.experimental.pallas import tpu as pltpu
from jax.experimental.pallas import tpu_sc as plsc

F32 = jnp.float32
MXU_DTYPE = jnp.bfloat16
HI = lax.Precision.HIGHEST

GRID_W = 64
HEAD_DIM = 64
ROPE_THETA = 10000.0
NORM_EPS = 1e-6

A_HEADS = 4
A_QK = HEAD_DIM
A_V = 2 * HEAD_DIM
B_HEADS = 4
B_DK = HEAD_DIM // 2
B_DV = HEAD_DIM
B_GATE_RANK = 16
B_GATE_TAU = 16.0
B_CHUNK = 64
C_HEADS = 4
C_DH = HEAD_DIM
NA_ROWS = 8
NA_COLS = 16

A_WIDTH = A_HEADS * A_V
B_WIDTH = B_HEADS * B_DV
C_WIDTH = C_HEADS * C_DH
B_QK = B_HEADS * B_DK
IN_SIZES = (A_HEADS * 2 * A_QK, A_HEADS * 2 * A_QK, A_WIDTH,
            B_QK, B_QK, B_WIDTH, B_WIDTH, 2 * B_GATE_RANK,
            C_WIDTH, C_WIDTH, C_WIDTH)

N_GROUPS = 4
EXPERTS_PER_GROUP = 8
N_EXPERTS = N_GROUPS * EXPERTS_PER_GROUP
TOP_K = 2
EXPERT_HIDDEN = 512

LANES = 128
TOK_TILE = 256
NA_TILE_ROWS = 4
NA_KEY_TILES = 3
EXPERT_TILE = 256
GATE_PAD = LANES
BG_WIDTH = 2 * B_QK + 2 * B_WIDTH + GATE_PAD
IN_PAD_WIDTH = 3 * A_WIDTH + BG_WIDTH + 3 * C_WIDTH
ROUTER_PAD = LANES
ROUTE_EXPERT = 0
ROUTE_GATE = 2
SC_CORES = 2
SC_SUBCORES = 16
SC_ROW_BUFFER_BYTES = 128 * 1024
VMEM_LIMIT = 48 * 1024 * 1024


def _silu(x):
    return x * (1.0 / (1.0 + jnp.exp(-x)))


def _cparams(sem):
    return pltpu.CompilerParams(dimension_semantics=sem, vmem_limit_bytes=VMEM_LIMIT)


def _mod_body(c_ref, w_ref, b_ref, o_ref):
    a = _silu(c_ref[...])
    o_ref[...] = jnp.dot(a, w_ref[...], precision=HI, preferred_element_type=F32) + b_ref[...]


def _modulation(cvec, w_mod, b_mod):
    depth, d, d6 = w_mod.shape
    rows = cvec.shape[0]
    return pl.pallas_call(
        _mod_body,
        grid=(depth, d6 // d),
        in_specs=[pl.BlockSpec((rows, d), lambda l, j: (0, 0)),
                  pl.BlockSpec((None, d, d), lambda l, j: (l, 0, j)),
                  pl.BlockSpec((None, 1, d), lambda l, j: (l, 0, j))],
        out_specs=pl.BlockSpec((None, rows, d), lambda l, j: (l, 0, j)),
        out_shape=jax.ShapeDtypeStruct((depth, rows, d6), F32),
        compiler_params=_cparams(("arbitrary", "arbitrary")),
        name="modulation",
    )(cvec, w_mod, b_mod.reshape(depth, 1, d6))


def _rms(x, g):
    return x * lax.rsqrt(jnp.mean(x * x, axis=-1, keepdims=True) + NORM_EPS) * g


def _rope(x, cos, sa, sb):
    return x * cos + pltpu.roll(x, LANES - 16, 1) * sa + pltpu.roll(x, 16, 1) * sb


def _moe_residual(x_ref, y0_ref, y1_ref, rt_ref, mod_ref):
    rt = rt_ref[0]
    moe = rt[:, ROUTE_GATE:ROUTE_GATE + 1] * y0_ref[0] + rt[:, ROUTE_GATE + 1:ROUTE_GATE + 2] * y1_ref[0]
    return x_ref[0] + mod_ref[5:6, :] * moe


def _in_proj_body(combine, *refs):
    if combine:
        (x_ref, y0_ref, y1_ref, rt_ref, pmod_ref, mod_ref, g_ref, w_ref, cos_ref, sa_ref, sb_ref,
         qa_ref, ka_ref, va_ref, bg_ref, qn_ref, kn_ref, vn_ref, xo_ref) = refs
        x = _moe_residual(x_ref, y0_ref, y1_ref, rt_ref, pmod_ref)
        xo_ref[0] = x
    else:
        (x_ref, mod_ref, g_ref, w_ref, cos_ref, sa_ref, sb_ref,
         qa_ref, ka_ref, va_ref, bg_ref, qn_ref, kn_ref, vn_ref) = refs
        x = x_ref[0]
    h = _rms(x, g_ref[...]) * (1.0 + mod_ref[1:2, :]) + mod_ref[0:1, :]
    hb = h.astype(MXU_DTYPE)

    def proj(lo, hi):
        return jnp.dot(hb, w_ref[:, lo:hi], preferred_element_type=F32)

    cos, sa, sb = cos_ref[...], sa_ref[...], sb_ref[...]
    for hh in range(A_HEADS):
        lo = hh * LANES
        q = proj(lo, lo + LANES)
        qa_ref[0, :, lo:lo + LANES] = (_rope(q, cos, sa, sb) * (A_QK ** -0.5)).astype(qa_ref.dtype)
        k = proj(A_WIDTH + lo, A_WIDTH + lo + LANES)
        ka_ref[0, :, lo:lo + LANES] = _rope(k, cos, sa, sb).astype(ka_ref.dtype)
    o = 2 * A_WIDTH
    va_ref[0] = proj(o, o + A_WIDTH).astype(va_ref.dtype)
    o += A_WIDTH
    bg_ref[0] = proj(o, o + BG_WIDTH)
    o += BG_WIDTH
    qn_ref[0] = (proj(o, o + C_WIDTH) * (C_DH ** -0.5)).astype(qn_ref.dtype)
    o += C_WIDTH
    kn_ref[0] = proj(o, o + C_WIDTH).astype(kn_ref.dtype)
    o += C_WIDTH
    vn_ref[0] = proj(o, o + C_WIDTH).astype(vn_ref.dtype)


def _moe_specs(tm, d, row_block):
    return [pl.BlockSpec((None, 1, tm, d), lambda b, i: (0, b, row_block(i), 0)),
            pl.BlockSpec((None, 1, tm, d), lambda b, i: (1, b, row_block(i), 0)),
            pl.BlockSpec((1, tm, ROUTER_PAD), lambda b, i: (b, row_block(i), 0))]


def _in_proj(layer, x, ymoe, route, modsel, g1, w_in_p, rope_tabs):
    bsz, t, d = x.shape
    tm = TOK_TILE
    combine = ymoe is not None
    tok = lambda b, i: (b, i, 0)
    x_spec = pl.BlockSpec((1, tm, d), tok)

    def mod_spec(l):
        return pl.BlockSpec((None, None, None, 6, d), lambda b, i: (l, b, jnp.minimum(i, 1), 0, 0))

    tab_spec = pl.BlockSpec((tm, LANES), lambda b, i: (i, 0))
    in_specs = [x_spec]
    args = [x]
    if combine:
        in_specs += _moe_specs(tm, d, lambda i: i) + [mod_spec(layer - 1)]
        args += [ymoe, ymoe, route, modsel]
    in_specs += [mod_spec(layer), pl.BlockSpec((1, d), lambda b, i: (0, 0)),
                 pl.BlockSpec((d, IN_PAD_WIDTH), lambda b, i: (0, 0)), tab_spec, tab_spec, tab_spec]
    args += [modsel, g1.reshape(1, d), w_in_p, *rope_tabs]

    def o(width, dtype):
        return pl.BlockSpec((1, tm, width), tok), jax.ShapeDtypeStruct((bsz, t, width), dtype)

    outs = [o(A_WIDTH, MXU_DTYPE), o(A_WIDTH, MXU_DTYPE), o(A_WIDTH, MXU_DTYPE), o(BG_WIDTH, F32),
            o(C_WIDTH, MXU_DTYPE), o(C_WIDTH, MXU_DTYPE), o(C_WIDTH, MXU_DTYPE)]
    if combine:
        outs.append(o(d, F32))
    return pl.pallas_call(
        functools.partial(_in_proj_body, combine),
        grid=(bsz, t // tm),
        in_specs=in_specs,
        out_specs=[s for s, _ in outs],
        out_shape=[s for _, s in outs],
        compiler_params=_cparams(("parallel", "arbitrary")),
        name="in_proj",
    )(*args)


def _diff_attn_body(lam_init, ctx_len, q_ref, k_ref, v_ref, lam_ref, g_ref, o_ref):
    q = q_ref[0]
    lane = lax.broadcasted_iota(jnp.int32, (1, LANES), 1)
    zero = jnp.zeros_like(q)
    q1 = jnp.where(lane < A_QK, q, zero)
    q2 = jnp.where(lane >= A_QK, q, zero)
    lm = lam_ref[...]
    lam = (jnp.exp(jnp.sum(lm[0:1] * lm[1:2], axis=1, keepdims=True))
           - jnp.exp(jnp.sum(lm[2:3] * lm[3:4], axis=1, keepdims=True)) + lam_init)

    def softmax_parts(qm, k):
        s = lax.dot_general(qm, k, (((1,), (1,)), ((), ())), preferred_element_type=F32)
        p = jnp.exp(s - jnp.max(s, axis=-1, keepdims=True))
        return p, jnp.sum(p, axis=-1, keepdims=True)

    def attend(n_keys):
        k = k_ref[0, :n_keys, :]
        v = v_ref[0, :n_keys, :]
        p1, l1 = softmax_parts(q1, k)
        p2, l2 = softmax_parts(q2, k)
        dmat = p1 * (1.0 / l1) - p2 * (lam / l2)
        o = jnp.dot(dmat.astype(MXU_DTYPE), v, preferred_element_type=F32)
        o_ref[0] = (_rms(o, g_ref[...]) * (1.0 - lam_init)).astype(o_ref.dtype)

    @pl.when(pl.program_id(2) == 0)
    def _():
        attend(ctx_len)

    @pl.when(pl.program_id(2) > 0)
    def _():
        attend(k_ref.shape[1])


def _diff_attention(layer, qa, ka, va, lam, g_sub):
    bsz, t, _ = qa.shape
    tq = TOK_TILE
    lam_init = 0.8 - 0.6 * math.exp(-0.3 * layer)
    kv_spec = pl.BlockSpec((1, t, LANES), lambda b, h, i: (b, 0, h))
    return pl.pallas_call(
        functools.partial(_diff_attn_body, lam_init, TOK_TILE),
        grid=(bsz, A_HEADS, t // tq),
        in_specs=[pl.BlockSpec((1, tq, LANES), lambda b, h, i: (b, i, h)), kv_spec, kv_spec,
                  pl.BlockSpec((4, A_QK), lambda b, h, i: (0, 0)),
                  pl.BlockSpec((1, A_V), lambda b, h, i: (0, 0))],
        out_specs=pl.BlockSpec((1, tq, LANES), lambda b, h, i: (b, i, h)),
        out_shape=jax.ShapeDtypeStruct((bsz, t, A_WIDTH), MXU_DTYPE),
        compiler_params=_cparams(("parallel", "parallel", "arbitrary")),
        name="diff_attention",
    )(qa, ka, va, lam, g_sub.reshape(1, A_V))


def _gla_body(f_ref, r_ref, wdec_ref, bdec_ref, of_ref, ob_ref, sf_ref, sb_ref, e_ref, b_ref, qs_ref):
    c = B_CHUNK
    n_chunks = TOK_TILE // c

    @pl.when(pl.program_id(1) == 0)
    def _():
        sf_ref[...] = jnp.zeros_like(sf_ref)
        sb_ref[...] = jnp.zeros_like(sb_ref)

    row = lax.broadcasted_iota(jnp.int32, (c, c), 0)
    col = lax.broadcasted_iota(jnp.int32, (c, c), 1)
    tri_f = (col <= row).astype(F32)
    tri_b = (col >= row).astype(F32)
    s_iota = lax.broadcasted_iota(jnp.int32, (c, LANES), 0)
    head_of_k = lax.broadcasted_iota(jnp.int32, (B_QK, B_WIDTH), 0) // B_DK
    head_of_v = lax.broadcasted_iota(jnp.int32, (B_QK, B_WIDTH), 1) // B_DV
    expand = (head_of_k == head_of_v).astype(MXU_DTYPE)
    same_head_t = (lax.broadcasted_iota(jnp.int32, (B_WIDTH, B_QK), 0) // B_DV
                   == lax.broadcasted_iota(jnp.int32, (B_WIDTH, B_QK), 1) // B_DK)

    def chunk(src_ref, lo, backward, st_ref, out_ref):
        q = src_ref[0, lo:lo + c, 0:B_QK] * (B_DK ** -0.5)
        k = src_ref[0, lo:lo + c, B_QK:2 * B_QK]
        v = src_ref[0, lo:lo + c, 2 * B_QK:2 * B_QK + B_WIDTH]
        gl = src_ref[0, lo:lo + c, 2 * B_QK + 2 * B_WIDTH:BG_WIDTH]
        d0 = B_QK if backward else 0
        z = jnp.dot(gl, wdec_ref[:, d0:d0 + B_QK], precision=HI, preferred_element_type=F32) \
            + bdec_ref[:, d0:d0 + B_QK]
        log_a = (jnp.minimum(z, 0.0) - jnp.log(1.0 + jnp.exp(-jnp.abs(z)))) / B_GATE_TAU
        b = jnp.dot(tri_b if backward else tri_f, log_a, precision=HI, preferred_element_type=F32)

        def fill(tt, carry):
            bt = b_ref[pl.ds(tt, 1), :]
            qt = qs_ref[pl.ds(tt, 1), :]
            keep = (s_iota >= tt) if backward else (s_iota <= tt)
            e = jnp.exp(jnp.where(keep, bt - b, -jnp.inf)) * (qt * k)
            e_ref[pl.ds(pl.multiple_of(tt * c, c), c), :] = e.astype(e_ref.dtype)
            return carry

        b_ref[...] = b
        qs_ref[...] = q
        lax.fori_loop(0, c, fill, 0)
        a_exp = jnp.dot(e_ref[...], expand, preferred_element_type=F32)
        o_intra = jnp.sum(a_exp.reshape(c, c, B_WIDTH) * v[None, :, :], axis=1)
        st = st_ref[...]
        o_inter = lax.dot_general(q * jnp.exp(b), st, (((1,), (1,)), ((), ())),
                                  preferred_element_type=F32)
        out_ref[0, lo:lo + c, :] = o_intra + o_inter
        b_end = b[0:1, :] if backward else b[c - 1:c, :]
        kd = k * jnp.exp(b_end - b)
        upd = lax.dot_general(v, kd, (((0,), (0,)), ((), ())), preferred_element_type=F32)
        st_ref[...] = jnp.exp(b_end) * st + jnp.where(same_head_t, upd, 0.0)

    for ci in range(n_chunks):
        chunk(f_ref, ci * c, False, sf_ref, of_ref)
        chunk(r_ref, (n_chunks - 1 - ci) * c, True, sb_ref, ob_ref)


def _gla_scan(bg, w_dec, b_dec):
    bsz, t, _ = bg.shape
    n = t // TOK_TILE
    rev = lambda b, i: (b, jnp.where(i == 0, 0, n - i), 0)
    fwd = lambda b, i: (b, i, 0)
    wdec = jnp.zeros((GATE_PAD, 2 * B_QK), F32)
    wdec = wdec.at[:B_GATE_RANK, :B_QK].set(w_dec[0]).at[B_GATE_RANK:2 * B_GATE_RANK, B_QK:].set(w_dec[1])
    bdec = b_dec.reshape(1, 2 * B_QK)
    o_shape = jax.ShapeDtypeStruct((bsz, t, B_WIDTH), F32)
    return pl.pallas_call(
        _gla_body,
        grid=(bsz, n),
        in_specs=[pl.BlockSpec((1, TOK_TILE, BG_WIDTH), fwd),
                  pl.BlockSpec((1, TOK_TILE, BG_WIDTH), rev),
                  pl.BlockSpec((GATE_PAD, 2 * B_QK), lambda b, i: (0, 0)),
                  pl.BlockSpec((1, 2 * B_QK), lambda b, i: (0, 0))],
        out_specs=[pl.BlockSpec((1, TOK_TILE, B_WIDTH), fwd),
                   pl.BlockSpec((1, TOK_TILE, B_WIDTH), rev)],
        out_shape=[o_shape, o_shape],
        scratch_shapes=[pltpu.VMEM((B_WIDTH, B_QK), F32), pltpu.VMEM((B_WIDTH, B_QK), F32),
                        pltpu.VMEM((B_CHUNK * B_CHUNK, LANES), MXU_DTYPE),
                        pltpu.VMEM((B_CHUNK, LANES), F32), pltpu.VMEM((B_CHUNK, LANES), F32)],
        compiler_params=_cparams(("parallel", "arbitrary")),
        name="gla_scan",
    )(bg, bg, wdec, bdec)


def _na_bias_tables(rpb, rows):
    n_tiles = rows // NA_TILE_ROWS
    wr = min(NA_ROWS, rows)
    n_dr, n_dc = 2 * NA_ROWS - 1, 2 * NA_COLS - 1
    cq = np.arange(GRID_W)[:, None]
    ck = np.arange(GRID_W)[None, :]
    cs = np.clip(cq - NA_COLS // 2, 0, GRID_W - NA_COLS)
    col_ok = (ck >= cs) & (ck < cs + NA_COLS)
    dc = np.clip(ck - cq, -(NA_COLS - 1), NA_COLS - 1) + (NA_COLS - 1)
    onehot = (dc.reshape(1, -1) == np.arange(n_dc)[:, None]).astype(np.float32)
    by_col = jnp.dot(rpb.astype(F32).reshape(-1, n_dc), onehot, precision=HI)
    by_col = jnp.where(col_ok.reshape(1, 1, GRID_W, GRID_W),
                       by_col.reshape(C_HEADS, n_dr, GRID_W, GRID_W), -jnp.inf)
    masked = jnp.full((C_HEADS, GRID_W, GRID_W), -jnp.inf, F32)
    tabs = []
    for j in (0, 1, n_tiles - 1):
        kr0 = int(np.clip(j - 1, 0, n_tiles - NA_KEY_TILES)) * NA_TILE_ROWS
        q_rows = []
        for qr in range(NA_TILE_ROWS):
            r = j * NA_TILE_ROWS + qr
            start = int(np.clip(r - wr // 2, 0, rows - wr))
            blocks = []
            for kw in range(NA_KEY_TILES * NA_TILE_ROWS):
                kr = kr0 + kw
                blocks.append(by_col[:, kr - r + NA_ROWS - 1] if start <= kr < start + wr else masked)
            q_rows.append(jnp.concatenate(blocks, axis=-1))
        tabs.append(jnp.concatenate(q_rows, axis=1))
    return jnp.stack(tabs)


def _na_body(q_ref, k0_ref, k1_ref, k2_ref, kc_ref, v0_ref, v1_ref, v2_ref, vc_ref, m_ref, o_ref):
    q = q_ref[0]
    lane = lax.broadcasted_iota(jnp.int32, (1, LANES), 1)
    zero = jnp.zeros_like(q)
    nt = (((1,), (1,)), ((), ()))

    def scores(qm, k_ref):
        return lax.dot_general(qm, k_ref[0], nt, preferred_element_type=F32)

    def head_out(hh, windows):
        qm = jnp.where((lane >= hh * C_DH) & (lane < (hh + 1) * C_DH), q, zero)
        s = [scores(qm, kc_ref)]
        for w, k_ref in enumerate(windows):
            s.append(scores(qm, k_ref) + m_ref[0, hh, :, w * TOK_TILE:(w + 1) * TOK_TILE])
        m = functools.reduce(jnp.maximum, [jnp.max(x, axis=-1, keepdims=True) for x in s])
        p = [jnp.exp(x - m) for x in s]
        den = functools.reduce(jnp.add, [jnp.sum(x, axis=-1, keepdims=True) for x in p])
        vals = [vc_ref] + [v0_ref, v1_ref, v2_ref][:len(windows)]
        o = functools.reduce(jnp.add, [jnp.dot(x.astype(MXU_DTYPE), v_ref[0], preferred_element_type=F32)
                                       for x, v_ref in zip(p, vals)])
        return o * (1.0 / den)

    def emit(windows):
        o0 = head_out(0, windows)
        o1 = head_out(1, windows)
        o_ref[0] = jnp.where(lane < C_DH, o0, o1).astype(o_ref.dtype)

    @pl.when(pl.program_id(1) == 0)
    def _():
        emit([])

    @pl.when(pl.program_id(1) > 0)
    def _():
        emit([k0_ref, k1_ref, k2_ref])


def _neighborhood_attention(qn, kn, vn, bias_tabs):
    bsz, t, _ = qn.shape
    n = t // TOK_TILE
    n_lat = n - 1

    def win(w):
        def index(hp, i, b):
            kb0 = jnp.clip(i - 2, 0, n_lat - NA_KEY_TILES)
            return (b, kb0 + 1 + w, hp)
        return pl.BlockSpec((1, TOK_TILE, LANES), index)

    def cls(hp, i, b):
        j = i - 1
        return (jnp.where(j <= 0, 0, jnp.where(j == n_lat - 1, 2, 1)), hp, 0, 0)

    own = pl.BlockSpec((1, TOK_TILE, LANES), lambda hp, i, b: (b, i, hp))
    ctx = pl.BlockSpec((1, TOK_TILE, LANES), lambda hp, i, b: (b, 0, hp))
    heads_per_step = LANES // C_DH
    return pl.pallas_call(
        _na_body,
        grid=(C_HEADS // heads_per_step, n, bsz),
        in_specs=[own, win(0), win(1), win(2), ctx, win(0), win(1), win(2), ctx,
                  pl.BlockSpec((1, heads_per_step, TOK_TILE, NA_KEY_TILES * TOK_TILE), cls)],
        out_specs=own,
        out_shape=jax.ShapeDtypeStruct((bsz, t, C_WIDTH), MXU_DTYPE),
        compiler_params=_cparams(("parallel", "parallel", "arbitrary")),
        name="neighborhood_attention",
    )(qn, kn, kn, kn, kn, vn, vn, vn, vn, bias_tabs)


def _out_proj_body(ya_ref, of_ref, ob_ref, r_ref, yn_ref, x_ref, mod_ref, gg_ref, g2_ref, w_ref,
                   wr_ref, br_ref, xo_ref, h_ref, rt_ref):
    o = of_ref[0] + ob_ref[0]
    hi = lax.broadcasted_iota(jnp.int32, (B_WIDTH, B_WIDTH), 0) // B_DV
    hj = lax.broadcasted_iota(jnp.int32, (B_WIDTH, B_WIDTH), 1) // B_DV
    head_mean = jnp.where(hi == hj, 1.0 / B_DV, 0.0).astype(F32)
    ms = jnp.dot(o * o, head_mean, precision=HI, preferred_element_type=F32)
    yb = o * lax.rsqrt(ms + NORM_EPS) * gg_ref[...] * _silu(r_ref[0])
    mix = (jnp.dot(ya_ref[0], w_ref[0:A_WIDTH, :], preferred_element_type=F32)
           + jnp.dot(yb.astype(MXU_DTYPE), w_ref[A_WIDTH:A_WIDTH + B_WIDTH, :], preferred_element_type=F32)
           + jnp.dot(yn_ref[0], w_ref[A_WIDTH + B_WIDTH:, :], preferred_element_type=F32))
    x = x_ref[0] + mod_ref[2:3, :] * mix
    xo_ref[0] = x
    h = _rms(x, g2_ref[...]) * (1.0 + mod_ref[4:5, :]) + mod_ref[3:4, :]
    h_ref[0] = h.astype(h_ref.dtype)
    logits = jnp.dot(h, wr_ref[...], precision=HI, preferred_element_type=F32) + br_ref[...]
    rt_ref[0] = _route(logits)


def _route(lg):
    lane = lax.broadcasted_iota(jnp.int32, lg.shape, 1)
    big = jnp.int32(ROUTER_PAD)

    def top(mask):
        v = jnp.max(jnp.where(mask, lg, -jnp.inf), axis=-1, keepdims=True)
        i = jnp.min(jnp.where(mask & (lg == v), lane, big), axis=-1, keepdims=True)
        return v, i

    g_mask = lane < N_GROUPS
    g_max, grp = top(g_mask)
    p_grp = 1.0 / jnp.sum(jnp.where(g_mask, jnp.exp(lg - g_max), 0.0), axis=-1, keepdims=True)
    e_lo = N_GROUPS + grp * EXPERTS_PER_GROUP
    e_mask = (lane >= e_lo) & (lane < e_lo + EXPERTS_PER_GROUP)
    v1, i1 = top(e_mask)
    v2, i2 = top(e_mask & (lane != i1))
    r = jnp.exp(v2 - v1)
    gate1 = p_grp / (1.0 + r)
    gate2 = p_grp * r / (1.0 + r)
    out = jnp.where(lane == ROUTE_EXPERT, (i1 - N_GROUPS).astype(F32),
                    jnp.where(lane == ROUTE_EXPERT + 1, (i2 - N_GROUPS).astype(F32),
                              jnp.where(lane == ROUTE_GATE, gate1,
                                        jnp.where(lane == ROUTE_GATE + 1, gate2, 0.0))))
    return out


def _out_proj(layer, ya, o_f, o_b, bg, yn, x, modsel, g_gla, g2, w_out_b, w_router, b_router):
    bsz, t, d = x.shape
    tm = TOK_TILE
    tok = lambda b, i: (b, i, 0)
    const = lambda b, i: (0, 0)
    r_block = (2 * B_QK + B_WIDTH) // B_WIDTH
    return pl.pallas_call(
        _out_proj_body,
        grid=(bsz, t // tm),
        in_specs=[pl.BlockSpec((1, tm, A_WIDTH), tok),
                  pl.BlockSpec((1, tm, B_WIDTH), tok), pl.BlockSpec((1, tm, B_WIDTH), tok),
                  pl.BlockSpec((1, tm, B_WIDTH), lambda b, i: (b, i, r_block)),
                  pl.BlockSpec((1, tm, C_WIDTH), tok),
                  pl.BlockSpec((1, tm, d), tok),
                  pl.BlockSpec((None, None, None, 6, d), lambda b, i: (layer, b, jnp.minimum(i, 1), 0, 0)),
                  pl.BlockSpec((1, B_WIDTH), const), pl.BlockSpec((1, d), const),
                  pl.BlockSpec(w_out_b.shape, const),
                  pl.BlockSpec((d, ROUTER_PAD), const), pl.BlockSpec((1, ROUTER_PAD), const)],
        out_specs=[pl.BlockSpec((1, tm, d), tok), pl.BlockSpec((1, tm, d), tok),
                   pl.BlockSpec((1, tm, ROUTER_PAD), tok)],
        out_shape=[jax.ShapeDtypeStruct((bsz, t, d), F32), jax.ShapeDtypeStruct((bsz, t, d), F32),
                   jax.ShapeDtypeStruct((bsz, t, ROUTER_PAD), F32)],
        compiler_params=_cparams(("parallel", "arbitrary")),
        name="out_proj",
    )(ya, o_f, o_b, bg, yn, x, modsel, jnp.tile(g_gla, B_HEADS).reshape(1, B_WIDTH), g2.reshape(1, d),
      w_out_b, w_router, b_router)


def _dispatch_plan(expert, n_blocks):
    n_tok = expert.shape[0]
    flat = expert.reshape(-1)
    n_assign = flat.shape[0]
    order = jnp.argsort(flat).astype(jnp.int32)
    rank = jnp.argsort(order).astype(jnp.int32)
    counts = jnp.sum((flat[:, None] == jnp.arange(N_EXPERTS, dtype=jnp.int32)[None, :]).astype(jnp.int32), axis=0)
    padded = (counts + EXPERT_TILE - 1) // EXPERT_TILE * EXPERT_TILE
    start = jnp.cumsum(counts) - counts
    pad_end = jnp.cumsum(padded)
    pad_start = pad_end - padded
    shift = (pad_start - start).astype(jnp.int32)
    dest = rank + shift[flat]
    blk_start = jnp.arange(n_blocks, dtype=jnp.int32) * EXPERT_TILE
    blk_expert = jnp.minimum(jnp.searchsorted(pad_end, blk_start, side='right'), N_EXPERTS - 1).astype(jnp.int32)
    blk_used = (blk_start < pad_end[-1]).astype(jnp.int32)
    row = jnp.arange(n_blocks * EXPERT_TILE, dtype=jnp.int32)
    e_row = jnp.repeat(blk_expert, EXPERT_TILE)
    real = (row - pad_start[e_row].astype(jnp.int32)) < counts[e_row]
    src_assign = order[jnp.clip(row - shift[e_row], 0, n_assign - 1)]
    src_tok = jnp.where(real, src_assign // TOP_K, 0)
    dest = dest.reshape(n_tok, TOP_K).T.reshape(-1)
    return src_tok, dest, blk_expert, blk_used


def _sc_gather(table, idx):
    n_rows = idx.shape[0]
    width = table.shape[1]
    n_workers = SC_CORES * SC_SUBCORES
    per_worker = n_rows // n_workers
    chunk = SC_ROW_BUFFER_BYTES // (width * table.dtype.itemsize)
    n_chunks = per_worker // chunk
    assert per_worker * n_workers == n_rows and n_chunks * chunk == per_worker and n_chunks % 2 == 0
    assert n_chunks >= 4 and chunk <= LANES
    mesh = plsc.VectorSubcoreMesh(core_axis_name="core", subcore_axis_name="subcore")

    def body(table_hbm, idx_hbm, out_hbm, idx_v, rows_v, gather_sem, write_sem):
        worker = lax.axis_index("subcore") * SC_CORES + lax.axis_index("core")
        base = worker * per_worker
        pltpu.sync_copy(idx_hbm.at[worker], idx_v)

        def gather(j, slot):
            return pltpu.make_async_copy(table_hbm.at[idx_v.at[j]], rows_v.at[slot], gather_sem.at[slot])

        def write(j, slot):
            return pltpu.make_async_copy(rows_v.at[slot], out_hbm.at[pl.ds(base + j * chunk, chunk)],
                                         write_sem.at[slot])

        gather(0, 0).start()
        gather(0, 0).wait()
        gather(1, 1).start()
        write(0, 0).start()

        @pl.loop(1, n_chunks - 1, step=2)
        def _(j):
            for s in range(2):
                slot = (1 + s) % 2
                gather(j + s, slot).wait()
                write(j + s - 1, 1 - slot).wait()
                gather(j + s + 1, 1 - slot).start()
                write(j + s, slot).start()

        last = n_chunks - 1
        gather(last, 1).wait()
        write(last, 1).start()
        write(last - 1, 0).wait()
        write(last, 1).wait()

    return pl.kernel(
        body,
        out_type=jax.ShapeDtypeStruct((n_rows, width), table.dtype),
        mesh=mesh,
        scratch_types=[pltpu.VMEM((n_chunks, chunk), jnp.int32),
                       pltpu.VMEM((2, chunk, width), table.dtype),
                       pltpu.SemaphoreType.DMA((2,)), pltpu.SemaphoreType.DMA((2,))],
        name="sc_row_gather",
    )(table, idx.reshape(n_workers, n_chunks, chunk))


def _expert_body(be_ref, used_ref, x_ref, wu_ref, wd_ref, o_ref, wub_ref, wdb_ref):
    i = pl.program_id(0)
    prev = be_ref[jnp.maximum(i - 1, 0)]

    @pl.when((i == 0) | (be_ref[i] != prev))
    def _():
        wub_ref[...] = wu_ref[...].astype(wub_ref.dtype)
        wdb_ref[...] = wd_ref[...].astype(wdb_ref.dtype)

    @pl.when(used_ref[i] > 0)
    def _():
        gu = jnp.dot(x_ref[...].astype(MXU_DTYPE), wub_ref[...], preferred_element_type=F32)
        act = _silu(gu[:, :EXPERT_HIDDEN]) * gu[:, EXPERT_HIDDEN:]
        o_ref[...] = jnp.dot(act.astype(MXU_DTYPE), wdb_ref[...], preferred_element_type=F32)

    @pl.when(used_ref[i] == 0)
    def _():
        o_ref[...] = jnp.zeros_like(o_ref)


def _expert_ffn(layer, buf, blk_expert, blk_used, w_up, w_down):
    n_rows, d = buf.shape
    n_blocks = n_rows // EXPERT_TILE
    h2 = w_up.shape[-1]
    grid_spec = pltpu.PrefetchScalarGridSpec(
        num_scalar_prefetch=2,
        grid=(n_blocks,),
        in_specs=[pl.BlockSpec((EXPERT_TILE, d), lambda i, be, us: (i, 0)),
                  pl.BlockSpec((None, None, d, h2), lambda i, be, us: (layer, be[i], 0, 0)),
                  pl.BlockSpec((None, None, h2 // 2, d), lambda i, be, us: (layer, be[i], 0, 0))],
        out_specs=pl.BlockSpec((EXPERT_TILE, d), lambda i, be, us: (i, 0)),
        scratch_shapes=[pltpu.VMEM((d, h2), MXU_DTYPE), pltpu.VMEM((h2 // 2, d), MXU_DTYPE)],
    )
    return pl.pallas_call(
        _expert_body,
        grid_spec=grid_spec,
        out_shape=jax.ShapeDtypeStruct((n_rows, d), F32),
        compiler_params=_cparams(("arbitrary",)),
        name="expert_ffn",
    )(blk_expert, blk_used, buf, w_up, w_down)


def _moe(layer, h2, route, w_up, w_down):
    bsz, t, d = h2.shape
    n_tok = bsz * t
    n_assign = n_tok * TOP_K
    n_blocks = -(-(n_assign + N_EXPERTS * (EXPERT_TILE - 1)) // EXPERT_TILE)
    expert = route.reshape(n_tok, ROUTER_PAD)[:, ROUTE_EXPERT:ROUTE_EXPERT + TOP_K].astype(jnp.int32)
    src_tok, dest, blk_expert, blk_used = _dispatch_plan(expert, n_blocks)
    buf = _sc_gather(h2.reshape(n_tok, d), src_tok)
    y = _expert_ffn(layer, buf, blk_expert, blk_used, w_up, w_down)
    return _sc_gather(y, dest).reshape(TOP_K, bsz, t, d)


def _final_body(x_ref, y0_ref, y1_ref, rt_ref, mod_ref, g_ref, o_ref):
    o_ref[0] = _rms(_moe_residual(x_ref, y0_ref, y1_ref, rt_ref, mod_ref), g_ref[...])


def _final_norm(layer, x, ymoe, route, modsel, g, ctx_tiles):
    bsz, t, d = x.shape
    tm = TOK_TILE
    lat = lambda b, i: (b, i + ctx_tiles, 0)
    return pl.pallas_call(
        _final_body,
        grid=(bsz, t // tm - ctx_tiles),
        in_specs=[pl.BlockSpec((1, tm, d), lat)] + _moe_specs(tm, d, lambda i: i + ctx_tiles)
        + [pl.BlockSpec((None, None, None, 6, d), lambda b, i: (layer, b, 1, 0, 0)),
           pl.BlockSpec((1, d), lambda b, i: (0, 0))],
        out_specs=pl.BlockSpec((1, tm, d), lambda b, i: (b, i, 0)),
        out_shape=jax.ShapeDtypeStruct((bsz, t - ctx_tiles * tm, d), F32),
        compiler_params=_cparams(("parallel", "arbitrary")),
        name="final_norm",
    )(x, ymoe, ymoe, route, modsel, g.reshape(1, d))


def _rope_tables(n_ctx, n_lat):
    t = jnp.arange(n_lat)
    row = (t // GRID_W).astype(F32)
    col = (t % GRID_W).astype(F32)
    n_freq = HEAD_DIM // 4
    inv = ROPE_THETA ** (-jnp.arange(n_freq, dtype=F32) / n_freq)
    ang_r = row[:, None] * inv
    ang_c = col[:, None] * inv
    cr, sr, cc, sc = jnp.cos(ang_r), jnp.sin(ang_r), jnp.cos(ang_c), jnp.sin(ang_c)
    z = jnp.zeros_like(sr)
    cos = jnp.concatenate([cr, cr, cc, cc], axis=-1)
    above = jnp.concatenate([-sr, z, -sc, z], axis=-1)
    below = jnp.concatenate([z, sr, z, sc], axis=-1)
    reps = LANES // HEAD_DIM

    def full(tab, ctx_value):
        tab = jnp.tile(tab, (1, reps))
        return jnp.concatenate([jnp.full((n_ctx, LANES), ctx_value, F32), tab], axis=0)

    return full(cos, 1.0), full(above, 0.0), full(below, 0.0)


def _pack_w_in(w_in):
    parts = jnp.split(w_in, np.cumsum(IN_SIZES)[:-1].tolist(), axis=-1)
    qa, ka, va, qb, kb, vb, rb, gb, qn, kn, vn = parts
    gb = jnp.pad(gb, ((0, 0), (0, GATE_PAD - gb.shape[-1])))
    return jnp.concatenate([qa, ka, va, qb, kb, vb, rb, gb, qn, kn, vn], axis=-1).astype(MXU_DTYPE)


def kernel(x, c, ctx, c_ctx, w_mod, b_mod, norm1_g, norm2_g, w_in, w_out, diff_lambda, diff_sub_g,
           gla_w_decay, gla_b_decay, gla_norm_g, na_rel_bias, w_router_group, b_router_group,
           w_router_expert, b_router_expert, w_expert_up, w_expert_down, final_g):
    bsz, seq, d = x.shape
    n_ctx = ctx.shape[1]
    depth = w_mod.shape[0]
    assert n_ctx == TOK_TILE and seq % (NA_TILE_ROWS * GRID_W) == 0 and d % LANES == 0
    assert seq // (NA_TILE_ROWS * GRID_W) >= NA_KEY_TILES
    mod_rows = -(-(bsz + 1) // 8) * 8
    cvec = jnp.zeros((mod_rows, d), F32).at[:bsz].set(c).at[bsz].set(c_ctx)
    mod = _modulation(cvec, w_mod, b_mod).reshape(depth, mod_rows, 6, d)
    modsel = jnp.stack([jnp.broadcast_to(mod[:, bsz][:, None], (depth, bsz, 6, d)), mod[:, :bsz]], axis=2)
    rope_tabs = _rope_tables(n_ctx, seq)
    xt = jnp.concatenate([ctx, x], axis=1)
    ymoe = route = None
    for l in range(depth):
        outs = _in_proj(l, xt, ymoe, route, modsel, norm1_g[l], _pack_w_in(w_in[l]), rope_tabs)
        qa, ka, va, bg, qn, kn, vn = outs[:7]
        if ymoe is not None:
            xt = outs[7]
        ya = _diff_attention(l, qa, ka, va, diff_lambda[l], diff_sub_g[l])
        o_f, o_b = _gla_scan(bg, gla_w_decay[l], gla_b_decay[l])
        yn = _neighborhood_attention(qn, kn, vn, _na_bias_tables(na_rel_bias[l], seq // GRID_W))
        w_router = jnp.pad(jnp.concatenate([w_router_group[l], w_router_expert[l]], axis=-1),
                           ((0, 0), (0, ROUTER_PAD - N_GROUPS - N_EXPERTS)))
        b_router = jnp.pad(jnp.concatenate([b_router_group[l], b_router_expert[l]]),
                           (0, ROUTER_PAD - N_GROUPS - N_EXPERTS)).reshape(1, ROUTER_PAD)
        xt, h2, route = _out_proj(l, ya, o_f, o_b, bg, yn, xt, modsel, gla_norm_g[l], norm2_g[l],
                                  w_out[l].astype(MXU_DTYPE), w_router, b_router)
        ymoe = _moe(l, h2, route, w_expert_up, w_expert_down)
    return _final_norm(depth - 1, xt, ymoe, route, modsel, final_g, n_ctx // TOK_TILE)
```

```python
import functools
import math

import numpy as np
import jax
import jax.numpy as jnp
from jax import lax
from jax.experimental import pallas as pl
from jax.experimental.pallas import tpu as pltpu
from jax.experimental.pallas import tpu_sc as plsc

F32 = jnp.float32
MXU_DTYPE = jnp.bfloat16
HI = lax.Precision.HIGHEST

GRID_W = 64
HEAD_DIM = 64
ROPE_THETA = 10000.0
NORM_EPS = 1e-6

A_HEADS = 4
A_QK = HEAD_DIM
A_V = 2 * HEAD_DIM
B_HEADS = 4
B_DK = HEAD_DIM // 2
B_DV = HEAD_DIM
B_GATE_RANK = 16
B_GATE_TAU = 16.0
B_CHUNK = 64
GLA_SUB = 16
C_HEADS = 4
C_DH = HEAD_DIM
NA_ROWS = 8
NA_COLS = 16

A_WIDTH = A_HEADS * A_V
B_WIDTH = B_HEADS * B_DV
C_WIDTH = C_HEADS * C_DH
B_QK = B_HEADS * B_DK
IN_SIZES = (A_HEADS * 2 * A_QK, A_HEADS * 2 * A_QK, A_WIDTH,
            B_QK, B_QK, B_WIDTH, B_WIDTH, 2 * B_GATE_RANK,
            C_WIDTH, C_WIDTH, C_WIDTH)

N_GROUPS = 4
EXPERTS_PER_GROUP = 8
N_EXPERTS = N_GROUPS * EXPERTS_PER_GROUP
TOP_K = 2
EXPERT_HIDDEN = 512

LANES = 128
TOK_TILE = 256
NA_TILE_ROWS = 4
NA_KEY_TILES = 3
EXPERT_TILE = 256
GATE_PAD = LANES
BG_WIDTH = 2 * B_QK + 2 * B_WIDTH + GATE_PAD
IN_PAD_WIDTH = 3 * A_WIDTH + BG_WIDTH + 3 * C_WIDTH
ROUTER_PAD = LANES
ROUTE_EXPERT = 0
ROUTE_GATE = 2
SC_CORES = 2
SC_SUBCORES = 16
SC_ROW_BUFFER_BYTES = 128 * 1024
VMEM_LIMIT = 48 * 1024 * 1024


def _split_bf16(x, pieces):
    out = []
    for _ in range(pieces):
        p = x.astype(jnp.bfloat16)
        out.append(p)
        x = x - p.astype(F32)
    return out


def _silu(x):
    return x * (1.0 / (1.0 + jnp.exp(-x)))


def _cparams(sem):
    return pltpu.CompilerParams(dimension_semantics=sem, vmem_limit_bytes=VMEM_LIMIT)


def _mod_body(c_ref, w_ref, b_ref, o_ref):
    a = _silu(c_ref[...])
    o_ref[...] = jnp.dot(a, w_ref[...], precision=HI, preferred_element_type=F32) + b_ref[...]


def _modulation(cvec, w_mod, b_mod):
    depth, d, d6 = w_mod.shape
    rows = cvec.shape[0]
    return pl.pallas_call(
        _mod_body,
        grid=(depth, d6 // d),
        in_specs=[pl.BlockSpec((rows, d), lambda l, j: (0, 0)),
                  pl.BlockSpec((None, d, d), lambda l, j: (l, 0, j)),
                  pl.BlockSpec((None, 1, d), lambda l, j: (l, 0, j))],
        out_specs=pl.BlockSpec((None, rows, d), lambda l, j: (l, 0, j)),
        out_shape=jax.ShapeDtypeStruct((depth, rows, d6), F32),
        compiler_params=_cparams(("arbitrary", "arbitrary")),
        name="modulation",
    )(cvec, w_mod, b_mod.reshape(depth, 1, d6))


def _rms(x, g):
    return x * lax.rsqrt(jnp.mean(x * x, axis=-1, keepdims=True) + NORM_EPS) * g


def _rope(x, cos, sa, sb):
    return x * cos + pltpu.roll(x, LANES - 16, 1) * sa + pltpu.roll(x, 16, 1) * sb


def _moe_residual(x_ref, y0_ref, y1_ref, rt_ref, mod_ref):
    rt = rt_ref[0]
    moe = rt[:, ROUTE_GATE:ROUTE_GATE + 1] * y0_ref[0] + rt[:, ROUTE_GATE + 1:ROUTE_GATE + 2] * y1_ref[0]
    return x_ref[0] + mod_ref[5:6, :] * moe


def _in_proj_body(combine, *refs):
    if combine:
        (x_ref, y0_ref, y1_ref, rt_ref, pmod_ref, mod_ref, g_ref, w_ref, cos_ref, sa_ref, sb_ref,
         qa_ref, ka_ref, va_ref, bg_ref, qn_ref, kn_ref, vn_ref, xo_ref) = refs
        x = _moe_residual(x_ref, y0_ref, y1_ref, rt_ref, pmod_ref)
        xo_ref[0] = x
    else:
        (x_ref, mod_ref, g_ref, w_ref, cos_ref, sa_ref, sb_ref,
         qa_ref, ka_ref, va_ref, bg_ref, qn_ref, kn_ref, vn_ref) = refs
        x = x_ref[0]
    h = _rms(x, g_ref[...]) * (1.0 + mod_ref[1:2, :]) + mod_ref[0:1, :]
    hb = h.astype(MXU_DTYPE)

    def proj(lo, hi):
        return jnp.dot(hb, w_ref[:, lo:hi], preferred_element_type=F32)

    cos, sa, sb = cos_ref[...], sa_ref[...], sb_ref[...]
    for hh in range(A_HEADS):
        lo = hh * LANES
        q = proj(lo, lo + LANES)
        qa_ref[0, :, lo:lo + LANES] = (_rope(q, cos, sa, sb) * (A_QK ** -0.5)).astype(qa_ref.dtype)
        k = proj(A_WIDTH + lo, A_WIDTH + lo + LANES)
        ka_ref[0, :, lo:lo + LANES] = _rope(k, cos, sa, sb).astype(ka_ref.dtype)
    o = 2 * A_WIDTH
    va_ref[0] = proj(o, o + A_WIDTH).astype(va_ref.dtype)
    o += A_WIDTH
    bg_ref[0] = proj(o, o + BG_WIDTH)
    o += BG_WIDTH
    qn_ref[0] = (proj(o, o + C_WIDTH) * (C_DH ** -0.5)).astype(qn_ref.dtype)
    o += C_WIDTH
    kn_ref[0] = proj(o, o + C_WIDTH).astype(kn_ref.dtype)
    o += C_WIDTH
    vn_ref[0] = proj(o, o + C_WIDTH).astype(vn_ref.dtype)


def _moe_specs(tm, d, row_block):
    return [pl.BlockSpec((None, 1, tm, d), lambda b, i: (0, b, row_block(i), 0)),
            pl.BlockSpec((None, 1, tm, d), lambda b, i: (1, b, row_block(i), 0)),
            pl.BlockSpec((1, tm, ROUTER_PAD), lambda b, i: (b, row_block(i), 0))]


def _in_proj(layer, x, ymoe, route, modsel, g1, w_in_p, rope_tabs):
    bsz, t, d = x.shape
    tm = TOK_TILE
    combine = ymoe is not None
    tok = lambda b, i: (b, i, 0)
    x_spec = pl.BlockSpec((1, tm, d), tok)

    def mod_spec(l):
        return pl.BlockSpec((None, None, None, 6, d), lambda b, i: (l, b, jnp.minimum(i, 1), 0, 0))

    tab_spec = pl.BlockSpec((tm, LANES), lambda b, i: (i, 0))
    in_specs = [x_spec]
    args = [x]
    if combine:
        in_specs += _moe_specs(tm, d, lambda i: i) + [mod_spec(layer - 1)]
        args += [ymoe, ymoe, route, modsel]
    in_specs += [mod_spec(layer), pl.BlockSpec((1, d), lambda b, i: (0, 0)),
                 pl.BlockSpec((d, IN_PAD_WIDTH), lambda b, i: (0, 0)), tab_spec, tab_spec, tab_spec]
    args += [modsel, g1.reshape(1, d), w_in_p, *rope_tabs]

    def o(width, dtype):
        return pl.BlockSpec((1, tm, width), tok), jax.ShapeDtypeStruct((bsz, t, width), dtype)

    outs = [o(A_WIDTH, MXU_DTYPE), o(A_WIDTH, MXU_DTYPE), o(A_WIDTH, MXU_DTYPE), o(BG_WIDTH, F32),
            o(C_WIDTH, MXU_DTYPE), o(C_WIDTH, MXU_DTYPE), o(C_WIDTH, MXU_DTYPE)]
    if combine:
        outs.append(o(d, F32))
    return pl.pallas_call(
        functools.partial(_in_proj_body, combine),
        grid=(bsz, t // tm),
        in_specs=in_specs,
        out_specs=[s for s, _ in outs],
        out_shape=[s for _, s in outs],
        compiler_params=_cparams(("parallel", "arbitrary")),
        name="in_proj",
    )(*args)


def _diff_attn_body(lam_init, ctx_len, q_ref, k_ref, v_ref, lam_ref, g_ref, o_ref):
    q = q_ref[0]
    lane = lax.broadcasted_iota(jnp.int32, (1, LANES), 1)
    zero = jnp.zeros_like(q)
    q1 = jnp.where(lane < A_QK, q, zero)
    q2 = jnp.where(lane >= A_QK, q, zero)
    lm = lam_ref[...]
    lam = (jnp.exp(jnp.sum(lm[0:1] * lm[1:2], axis=1, keepdims=True))
           - jnp.exp(jnp.sum(lm[2:3] * lm[3:4], axis=1, keepdims=True)) + lam_init)

    def softmax_parts(qm, k):
        s = lax.dot_general(qm, k, (((1,), (1,)), ((), ())), preferred_element_type=F32)
        p = jnp.exp(s - jnp.max(s, axis=-1, keepdims=True))
        return p, jnp.sum(p, axis=-1, keepdims=True)

    def attend(n_keys):
        k = k_ref[0, :n_keys, :]
        v = v_ref[0, :n_keys, :]
        p1, l1 = softmax_parts(q1, k)
        p2, l2 = softmax_parts(q2, k)
        dmat = p1 * (1.0 / l1) - p2 * (lam / l2)
        o = jnp.dot(dmat.astype(MXU_DTYPE), v, preferred_element_type=F32)
        o_ref[0] = (_rms(o, g_ref[...]) * (1.0 - lam_init)).astype(o_ref.dtype)

    @pl.when(pl.program_id(2) == 0)
    def _():
        attend(ctx_len)

    @pl.when(pl.program_id(2) > 0)
    def _():
        attend(k_ref.shape[1])


def _diff_attention(layer, qa, ka, va, lam, g_sub):
    bsz, t, _ = qa.shape
    tq = TOK_TILE
    lam_init = 0.8 - 0.6 * math.exp(-0.3 * layer)
    kv_spec = pl.BlockSpec((1, t, LANES), lambda b, h, i: (b, 0, h))
    return pl.pallas_call(
        functools.partial(_diff_attn_body, lam_init, TOK_TILE),
        grid=(bsz, A_HEADS, t // tq),
        in_specs=[pl.BlockSpec((1, tq, LANES), lambda b, h, i: (b, i, h)), kv_spec, kv_spec,
                  pl.BlockSpec((4, A_QK), lambda b, h, i: (0, 0)),
                  pl.BlockSpec((1, A_V), lambda b, h, i: (0, 0))],
        out_specs=pl.BlockSpec((1, tq, LANES), lambda b, h, i: (b, i, h)),
        out_shape=jax.ShapeDtypeStruct((bsz, t, A_WIDTH), MXU_DTYPE),
        compiler_params=_cparams(("parallel", "parallel", "arbitrary")),
        name="diff_attention",
    )(qa, ka, va, lam, g_sub.reshape(1, A_V))


def _gla_body(f_ref, r_ref, wdec_ref, bdec_ref, of_ref, ob_ref, sf_ref, sb_ref, e_ref, b_ref, qs_ref):
    c = B_CHUNK
    n_chunks = TOK_TILE // c

    @pl.when(pl.program_id(1) == 0)
    def _():
        sf_ref[...] = jnp.zeros_like(sf_ref)
        sb_ref[...] = jnp.zeros_like(sb_ref)

    sub = GLA_SUB
    n_sub = c // sub
    nt = (((1,), (1,)), ((), ()))
    t_row = lax.broadcasted_iota(jnp.int32, (TOK_TILE, TOK_TILE), 0)
    t_col = lax.broadcasted_iota(jnp.int32, (TOK_TILE, TOK_TILE), 1)
    same_chunk = (t_row // c) == (t_col // c)
    tri_f = (same_chunk & (t_col <= t_row)).astype(jnp.bfloat16)
    tri_b = (same_chunk & (t_col >= t_row)).astype(jnp.bfloat16)
    s_iota = lax.broadcasted_iota(jnp.int32, (sub, LANES), 0)
    idx = lax.broadcasted_iota(jnp.int32, (c, LANES), 0)
    head_of_k = lax.broadcasted_iota(jnp.int32, (B_QK, B_WIDTH), 0) // B_DK
    head_of_v = lax.broadcasted_iota(jnp.int32, (B_QK, B_WIDTH), 1) // B_DV
    expand = (head_of_k == head_of_v).astype(MXU_DTYPE)
    same_head_t = (lax.broadcasted_iota(jnp.int32, (B_WIDTH, B_QK), 0) // B_DV
                   == lax.broadcasted_iota(jnp.int32, (B_WIDTH, B_QK), 1) // B_DK)
    n_ref = n_sub - 1
    kt_keep = (lax.broadcasted_iota(jnp.int32, (B_HEADS * c, n_ref * B_QK), 0) // c
               == (lax.broadcasted_iota(jnp.int32, (B_HEADS * c, n_ref * B_QK), 1) % B_QK) // B_DK)
    vx_keep = (lax.broadcasted_iota(jnp.int32, (B_HEADS * c, B_WIDTH), 0) // c
               == lax.broadcasted_iota(jnp.int32, (B_HEADS * c, B_WIDTH), 1) // B_DV)
    pick = (lax.broadcasted_iota(jnp.int32, (c, c * sub), 1) // sub
            == lax.broadcasted_iota(jnp.int32, (c, c * sub), 0)).astype(MXU_DTYPE)

    def log_decay(src_ref, backward):
        gl = src_ref[0, :, 2 * B_QK + 2 * B_WIDTH:BG_WIDTH]
        d0 = B_QK if backward else 0
        z = jnp.dot(gl, wdec_ref[:, d0:d0 + B_QK], precision=HI, preferred_element_type=F32) \
            + bdec_ref[:, d0:d0 + B_QK]
        log_a = (jnp.minimum(z, 0.0) - jnp.log(1.0 + jnp.exp(-jnp.abs(z)))) / B_GATE_TAU
        tri = tri_b if backward else tri_f
        return sum(jnp.dot(tri, p, preferred_element_type=F32) for p in _split_bf16(log_a, 3))

    def chunk(src_ref, b_all, lo, backward, st_ref, out_ref):
        q = src_ref[0, lo:lo + c, 0:B_QK] * (B_DK ** -0.5)
        k = src_ref[0, lo:lo + c, B_QK:2 * B_QK]
        v = src_ref[0, lo:lo + c, 2 * B_QK:2 * B_QK + B_WIDTH]
        b = b_all[lo:lo + c]
        b_ref[...] = b
        qs_ref[...] = q
        blk = ((c - 1 - idx) if backward else idx) // sub

        q_parts, k_parts = [], []
        for m, late, early in ((1, blk == 1, blk == 0), (2, blk >= 2, blk <= 1), (3, blk == 3, blk == 2)):
            r_row = (c - 1 - sub * m) if backward else sub * m
            r = b_ref[r_row:r_row + 1, :]
            q_parts.append(q * jnp.exp(jnp.where(late, b - r, -jnp.inf)))
            k_parts.append(k * jnp.exp(jnp.where(early, r - b, -jnp.inf)))
        q_cat = jnp.concatenate(q_parts, axis=1).astype(MXU_DTYPE)
        k_cat = jnp.concatenate(k_parts, axis=1)
        k_exp = jnp.where(kt_keep, jnp.concatenate([k_cat] * B_HEADS, axis=0), 0.0).astype(MXU_DTYPE)
        a_off = lax.dot_general(q_cat, k_exp, nt, preferred_element_type=F32)
        v_exp = jnp.where(vx_keep, jnp.concatenate([v] * B_HEADS, axis=0), 0.0).astype(MXU_DTYPE)
        o_off = jnp.dot(a_off.astype(MXU_DTYPE), v_exp, preferred_element_type=F32)

        for tt in range(c):
            lo_s = tt // sub * sub
            keep = (s_iota >= tt - lo_s) if backward else (s_iota <= tt - lo_s)
            bt = b_ref[tt:tt + 1, :]
            qt = qs_ref[tt:tt + 1, :]
            e = jnp.exp(jnp.where(keep, bt - b[lo_s:lo_s + sub], -jnp.inf)) * (qt * k[lo_s:lo_s + sub])
            e_ref[tt * sub:(tt + 1) * sub, :] = e.astype(e_ref.dtype)
        a_exp = jnp.dot(e_ref[...], expand, preferred_element_type=F32)
        prod = a_exp.reshape(n_sub, sub, sub, B_WIDTH) * v.reshape(n_sub, 1, sub, B_WIDTH)
        o_diag = jnp.dot(pick, prod.reshape(c * sub, B_WIDTH).astype(MXU_DTYPE), preferred_element_type=F32)

        st = st_ref[...]
        o_inter = lax.dot_general(q * jnp.exp(b), st, nt, preferred_element_type=F32)
        out_ref[0, lo:lo + c, :] = o_off + o_diag + o_inter
        b_end = b[0:1, :] if backward else b[c - 1:c, :]
        kd = k * jnp.exp(b_end - b)
        upd = lax.dot_general(v, kd, (((0,), (0,)), ((), ())), preferred_element_type=F32)
        st_ref[...] = jnp.exp(b_end) * st + jnp.where(same_head_t, upd, 0.0)

    b_fwd = log_decay(f_ref, False)
    b_bwd = log_decay(r_ref, True)
    for ci in range(n_chunks):
        chunk(f_ref, b_fwd, ci * c, False, sf_ref, of_ref)
        chunk(r_ref, b_bwd, (n_chunks - 1 - ci) * c, True, sb_ref, ob_ref)


def _gla_scan(bg, w_dec, b_dec):
    bsz, t, _ = bg.shape
    n = t // TOK_TILE
    rev = lambda b, i: (b, jnp.where(i == 0, 0, n - i), 0)
    fwd = lambda b, i: (b, i, 0)
    wdec = jnp.zeros((GATE_PAD, 2 * B_QK), F32)
    wdec = wdec.at[:B_GATE_RANK, :B_QK].set(w_dec[0]).at[B_GATE_RANK:2 * B_GATE_RANK, B_QK:].set(w_dec[1])
    bdec = b_dec.reshape(1, 2 * B_QK)
    o_shape = jax.ShapeDtypeStruct((bsz, t, B_WIDTH), F32)
    return pl.pallas_call(
        _gla_body,
        grid=(bsz, n),
        in_specs=[pl.BlockSpec((1, TOK_TILE, BG_WIDTH), fwd),
                  pl.BlockSpec((1, TOK_TILE, BG_WIDTH), rev),
                  pl.BlockSpec((GATE_PAD, 2 * B_QK), lambda b, i: (0, 0)),
                  pl.BlockSpec((1, 2 * B_QK), lambda b, i: (0, 0))],
        out_specs=[pl.BlockSpec((1, TOK_TILE, B_WIDTH), fwd),
                   pl.BlockSpec((1, TOK_TILE, B_WIDTH), rev)],
        out_shape=[o_shape, o_shape],
        scratch_shapes=[pltpu.VMEM((B_WIDTH, B_QK), F32), pltpu.VMEM((B_WIDTH, B_QK), F32),
                        pltpu.VMEM((B_CHUNK * GLA_SUB, LANES), MXU_DTYPE),
                        pltpu.VMEM((B_CHUNK, LANES), F32), pltpu.VMEM((B_CHUNK, LANES), F32)],
        compiler_params=_cparams(("parallel", "arbitrary")),
        name="gla_scan",
    )(bg, bg, wdec, bdec)


def _na_bias_tables(rpb, rows):
    n_tiles = rows // NA_TILE_ROWS
    wr = min(NA_ROWS, rows)
    n_dr, n_dc = 2 * NA_ROWS - 1, 2 * NA_COLS - 1
    cq = np.arange(GRID_W)[:, None]
    ck = np.arange(GRID_W)[None, :]
    cs = np.clip(cq - NA_COLS // 2, 0, GRID_W - NA_COLS)
    col_ok = (ck >= cs) & (ck < cs + NA_COLS)
    dc = np.clip(ck - cq, -(NA_COLS - 1), NA_COLS - 1) + (NA_COLS - 1)
    onehot = (dc.reshape(1, -1) == np.arange(n_dc)[:, None]).astype(np.float32)
    by_col = jnp.dot(rpb.astype(F32).reshape(-1, n_dc), onehot, precision=HI)
    by_col = jnp.where(col_ok.reshape(1, 1, GRID_W, GRID_W),
                       by_col.reshape(C_HEADS, n_dr, GRID_W, GRID_W), -jnp.inf)
    masked = jnp.full((C_HEADS, GRID_W, GRID_W), -jnp.inf, F32)
    tabs = []
    for j in (0, 1, n_tiles - 1):
        kr0 = int(np.clip(j - 1, 0, n_tiles - NA_KEY_TILES)) * NA_TILE_ROWS
        q_rows = []
        for qr in range(NA_TILE_ROWS):
            r = j * NA_TILE_ROWS + qr
            start = int(np.clip(r - wr // 2, 0, rows - wr))
            blocks = []
            for kw in range(NA_KEY_TILES * NA_TILE_ROWS):
                kr = kr0 + kw
                blocks.append(by_col[:, kr - r + NA_ROWS - 1] if start <= kr < start + wr else masked)
            q_rows.append(jnp.concatenate(blocks, axis=-1))
        tabs.append(jnp.concatenate(q_rows, axis=1))
    return jnp.stack(tabs)


def _na_body(q_ref, k0_ref, k1_ref, k2_ref, kc_ref, v0_ref, v1_ref, v2_ref, vc_ref, m_ref, o_ref):
    q = q_ref[0]
    lane = lax.broadcasted_iota(jnp.int32, (1, LANES), 1)
    zero = jnp.zeros_like(q)
    nt = (((1,), (1,)), ((), ()))

    def scores(qm, k_ref):
        return lax.dot_general(qm, k_ref[0], nt, preferred_element_type=F32)

    def head_out(hh, windows):
        qm = jnp.where((lane >= hh * C_DH) & (lane < (hh + 1) * C_DH), q, zero)
        s = [scores(qm, kc_ref)]
        for w, k_ref in enumerate(windows):
            s.append(scores(qm, k_ref) + m_ref[0, hh, :, w * TOK_TILE:(w + 1) * TOK_TILE])
        m = functools.reduce(jnp.maximum, [jnp.max(x, axis=-1, keepdims=True) for x in s])
        p = [jnp.exp(x - m) for x in s]
        den = functools.reduce(jnp.add, [jnp.sum(x, axis=-1, keepdims=True) for x in p])
        vals = [vc_ref] + [v0_ref, v1_ref, v2_ref][:len(windows)]
        o = functools.reduce(jnp.add, [jnp.dot(x.astype(MXU_DTYPE), v_ref[0], preferred_element_type=F32)
                                       for x, v_ref in zip(p, vals)])
        return o * (1.0 / den)

    def emit(windows):
        o0 = head_out(0, windows)
        o1 = head_out(1, windows)
        o_ref[0] = jnp.where(lane < C_DH, o0, o1).astype(o_ref.dtype)

    @pl.when(pl.program_id(1) == 0)
    def _():
        emit([])

    @pl.when(pl.program_id(1) > 0)
    def _():
        emit([k0_ref, k1_ref, k2_ref])


def _neighborhood_attention(qn, kn, vn, bias_tabs):
    bsz, t, _ = qn.shape
    n = t // TOK_TILE
    n_lat = n - 1

    def win(w):
        def index(hp, i, b):
            kb0 = jnp.clip(i - 2, 0, n_lat - NA_KEY_TILES)
            return (b, kb0 + 1 + w, hp)
        return pl.BlockSpec((1, TOK_TILE, LANES), index)

    def cls(hp, i, b):
        j = i - 1
        return (jnp.where(j <= 0, 0, jnp.where(j == n_lat - 1, 2, 1)), hp, 0, 0)

    own = pl.BlockSpec((1, TOK_TILE, LANES), lambda hp, i, b: (b, i, hp))
    ctx = pl.BlockSpec((1, TOK_TILE, LANES), lambda hp, i, b: (b, 0, hp))
    heads_per_step = LANES // C_DH
    return pl.pallas_call(
        _na_body,
        grid=(C_HEADS // heads_per_step, n, bsz),
        in_specs=[own, win(0), win(1), win(2), ctx, win(0), win(1), win(2), ctx,
                  pl.BlockSpec((1, heads_per_step, TOK_TILE, NA_KEY_TILES * TOK_TILE), cls)],
        out_specs=own,
        out_shape=jax.ShapeDtypeStruct((bsz, t, C_WIDTH), MXU_DTYPE),
        compiler_params=_cparams(("parallel", "parallel", "arbitrary")),
        name="neighborhood_attention",
    )(qn, kn, kn, kn, kn, vn, vn, vn, vn, bias_tabs)


def _out_proj_body(ya_ref, of_ref, ob_ref, r_ref, yn_ref, x_ref, mod_ref, gg_ref, g2_ref, w_ref,
                   wr_ref, br_ref, xo_ref, h_ref, rt_ref):
    o = of_ref[0] + ob_ref[0]
    hi = lax.broadcasted_iota(jnp.int32, (B_WIDTH, B_WIDTH), 0) // B_DV
    hj = lax.broadcasted_iota(jnp.int32, (B_WIDTH, B_WIDTH), 1) // B_DV
    head_mean = jnp.where(hi == hj, 1.0 / B_DV, 0.0).astype(jnp.bfloat16)
    ms = sum(jnp.dot(p, head_mean, preferred_element_type=F32) for p in _split_bf16(o * o, 2))
    yb = o * lax.rsqrt(ms + NORM_EPS) * gg_ref[...] * _silu(r_ref[0])
    mix = (jnp.dot(ya_ref[0], w_ref[0:A_WIDTH, :], preferred_element_type=F32)
           + jnp.dot(yb.astype(MXU_DTYPE), w_ref[A_WIDTH:A_WIDTH + B_WIDTH, :], preferred_element_type=F32)
           + jnp.dot(yn_ref[0], w_ref[A_WIDTH + B_WIDTH:, :], preferred_element_type=F32))
    x = x_ref[0] + mod_ref[2:3, :] * mix
    xo_ref[0] = x
    h = _rms(x, g2_ref[...]) * (1.0 + mod_ref[4:5, :]) + mod_ref[3:4, :]
    h_ref[0] = h.astype(h_ref.dtype)
    h_hi, h_lo = _split_bf16(h, 2)
    both = jnp.dot(h_hi, wr_ref[...], preferred_element_type=F32)
    logits = (both[:, :ROUTER_PAD] + both[:, ROUTER_PAD:]
              + jnp.dot(h_lo, wr_ref[:, :ROUTER_PAD], preferred_element_type=F32) + br_ref[...])
    rt_ref[0] = _route(logits)


def _route(lg):
    lane = lax.broadcasted_iota(jnp.int32, lg.shape, 1)
    big = jnp.int32(ROUTER_PAD)

    def top(mask):
        v = jnp.max(jnp.where(mask, lg, -jnp.inf), axis=-1, keepdims=True)
        i = jnp.min(jnp.where(mask & (lg == v), lane, big), axis=-1, keepdims=True)
        return v, i

    g_mask = lane < N_GROUPS
    g_max, grp = top(g_mask)
    p_grp = 1.0 / jnp.sum(jnp.where(g_mask, jnp.exp(lg - g_max), 0.0), axis=-1, keepdims=True)
    e_lo = N_GROUPS + grp * EXPERTS_PER_GROUP
    e_mask = (lane >= e_lo) & (lane < e_lo + EXPERTS_PER_GROUP)
    v1, i1 = top(e_mask)
    v2, i2 = top(e_mask & (lane != i1))
    r = jnp.exp(v2 - v1)
    gate1 = p_grp / (1.0 + r)
    gate2 = p_grp * r / (1.0 + r)
    out = jnp.where(lane == ROUTE_EXPERT, (i1 - N_GROUPS).astype(F32),
                    jnp.where(lane == ROUTE_EXPERT + 1, (i2 - N_GROUPS).astype(F32),
                              jnp.where(lane == ROUTE_GATE, gate1,
                                        jnp.where(lane == ROUTE_GATE + 1, gate2, 0.0))))
    return out


def _out_proj(layer, ya, o_f, o_b, bg, yn, x, modsel, g_gla, g2, w_out_b, w_router, b_router):
    bsz, t, d = x.shape
    tm = TOK_TILE
    tok = lambda b, i: (b, i, 0)
    const = lambda b, i: (0, 0)
    r_block = (2 * B_QK + B_WIDTH) // B_WIDTH
    return pl.pallas_call(
        _out_proj_body,
        grid=(bsz, t // tm),
        in_specs=[pl.BlockSpec((1, tm, A_WIDTH), tok),
                  pl.BlockSpec((1, tm, B_WIDTH), tok), pl.BlockSpec((1, tm, B_WIDTH), tok),
                  pl.BlockSpec((1, tm, B_WIDTH), lambda b, i: (b, i, r_block)),
                  pl.BlockSpec((1, tm, C_WIDTH), tok),
                  pl.BlockSpec((1, tm, d), tok),
                  pl.BlockSpec((None, None, None, 6, d), lambda b, i: (layer, b, jnp.minimum(i, 1), 0, 0)),
                  pl.BlockSpec((1, B_WIDTH), const), pl.BlockSpec((1, d), const),
                  pl.BlockSpec(w_out_b.shape, const),
                  pl.BlockSpec((d, 2 * ROUTER_PAD), const), pl.BlockSpec((1, ROUTER_PAD), const)],
        out_specs=[pl.BlockSpec((1, tm, d), tok), pl.BlockSpec((1, tm, d), tok),
                   pl.BlockSpec((1, tm, ROUTER_PAD), tok)],
        out_shape=[jax.ShapeDtypeStruct((bsz, t, d), F32), jax.ShapeDtypeStruct((bsz, t, d), F32),
                   jax.ShapeDtypeStruct((bsz, t, ROUTER_PAD), F32)],
        compiler_params=_cparams(("parallel", "arbitrary")),
        name="out_proj",
    )(ya, o_f, o_b, bg, yn, x, modsel, jnp.tile(g_gla, B_HEADS).reshape(1, B_WIDTH), g2.reshape(1, d),
      w_out_b, w_router, b_router)


def _dispatch_plan(expert, n_blocks):
    n_tok = expert.shape[0]
    flat = expert.reshape(-1)
    n_assign = flat.shape[0]
    order = jnp.argsort(flat).astype(jnp.int32)
    rank = jnp.argsort(order).astype(jnp.int32)
    counts = jnp.sum((flat[:, None] == jnp.arange(N_EXPERTS, dtype=jnp.int32)[None, :]).astype(jnp.int32), axis=0)
    padded = (counts + EXPERT_TILE - 1) // EXPERT_TILE * EXPERT_TILE
    start = jnp.cumsum(counts) - counts
    pad_end = jnp.cumsum(padded)
    pad_start = pad_end - padded
    shift = (pad_start - start).astype(jnp.int32)
    dest = rank + shift[flat]
    blk_start = jnp.arange(n_blocks, dtype=jnp.int32) * EXPERT_TILE
    blk_expert = jnp.sum((pad_end[None, :] <= blk_start[:, None]).astype(jnp.int32), axis=1)
    blk_expert = jnp.minimum(blk_expert, N_EXPERTS - 1)
    blk_used = (blk_start < pad_end[-1]).astype(jnp.int32)
    row = jnp.arange(n_blocks * EXPERT_TILE, dtype=jnp.int32)
    e_row = jnp.repeat(blk_expert, EXPERT_TILE)
    real = (row - pad_start[e_row].astype(jnp.int32)) < counts[e_row]
    src_assign = order[jnp.clip(row - shift[e_row], 0, n_assign - 1)]
    src_tok = jnp.where(real, src_assign // TOP_K, 0)
    dest = dest.reshape(n_tok, TOP_K).T.reshape(-1)
    return src_tok, dest, blk_expert, blk_used


def _sc_gather(table, idx):
    n_rows = idx.shape[0]
    width = table.shape[1]
    n_workers = SC_CORES * SC_SUBCORES
    per_worker = n_rows // n_workers
    chunk = SC_ROW_BUFFER_BYTES // (width * table.dtype.itemsize)
    n_chunks = per_worker // chunk
    assert per_worker * n_workers == n_rows and n_chunks * chunk == per_worker and n_chunks % 2 == 0
    assert n_chunks >= 4 and chunk <= LANES
    mesh = plsc.VectorSubcoreMesh(core_axis_name="core", subcore_axis_name="subcore")

    def body(table_hbm, idx_hbm, out_hbm, idx_v, rows_v, gather_sem, write_sem):
        worker = lax.axis_index("subcore") * SC_CORES + lax.axis_index("core")
        base = worker * per_worker
        pltpu.sync_copy(idx_hbm.at[worker], idx_v)

        def gather(j, slot):
            return pltpu.make_async_copy(table_hbm.at[idx_v.at[j]], rows_v.at[slot], gather_sem.at[slot])

        def write(j, slot):
            return pltpu.make_async_copy(rows_v.at[slot], out_hbm.at[pl.ds(base + j * chunk, chunk)],
                                         write_sem.at[slot])

        gather(0, 0).start()
        gather(0, 0).wait()
        gather(1, 1).start()
        write(0, 0).start()

        @pl.loop(1, n_chunks - 1, step=2)
        def _(j):
            for s in range(2):
                slot = (1 + s) % 2
                gather(j + s, slot).wait()
                write(j + s - 1, 1 - slot).wait()
                gather(j + s + 1, 1 - slot).start()
                write(j + s, slot).start()

        last = n_chunks - 1
        gather(last, 1).wait()
        write(last, 1).start()
        write(last - 1, 0).wait()
        write(last, 1).wait()

    return pl.kernel(
        body,
        out_type=jax.ShapeDtypeStruct((n_rows, width), table.dtype),
        mesh=mesh,
        scratch_types=[pltpu.VMEM((n_chunks, chunk), jnp.int32),
                       pltpu.VMEM((2, chunk, width), table.dtype),
                       pltpu.SemaphoreType.DMA((2,)), pltpu.SemaphoreType.DMA((2,))],
        name="sc_row_gather",
    )(table, idx.reshape(n_workers, n_chunks, chunk))


def _expert_body(be_ref, used_ref, x_ref, wu_ref, wd_ref, o_ref, wub_ref, wdb_ref):
    i = pl.program_id(0)
    prev = be_ref[jnp.maximum(i - 1, 0)]

    @pl.when((i == 0) | (be_ref[i] != prev))
    def _():
        wub_ref[...] = wu_ref[...].astype(wub_ref.dtype)
        wdb_ref[...] = wd_ref[...].astype(wdb_ref.dtype)

    @pl.when(used_ref[i] > 0)
    def _():
        gu = jnp.dot(x_ref[...].astype(MXU_DTYPE), wub_ref[...], preferred_element_type=F32)
        act = _silu(gu[:, :EXPERT_HIDDEN]) * gu[:, EXPERT_HIDDEN:]
        o_ref[...] = jnp.dot(act.astype(MXU_DTYPE), wdb_ref[...], preferred_element_type=F32)

    @pl.when(used_ref[i] == 0)
    def _():
        o_ref[...] = jnp.zeros_like(o_ref)


def _expert_ffn(layer, buf, blk_expert, blk_used, w_up, w_down):
    n_rows, d = buf.shape
    n_blocks = n_rows // EXPERT_TILE
    h2 = w_up.shape[-1]
    grid_spec = pltpu.PrefetchScalarGridSpec(
        num_scalar_prefetch=2,
        grid=(n_blocks,),
        in_specs=[pl.BlockSpec((EXPERT_TILE, d), lambda i, be, us: (i, 0)),
                  pl.BlockSpec((None, None, d, h2), lambda i, be, us: (layer, be[i], 0, 0)),
                  pl.BlockSpec((None, None, h2 // 2, d), lambda i, be, us: (layer, be[i], 0, 0))],
        out_specs=pl.BlockSpec((EXPERT_TILE, d), lambda i, be, us: (i, 0)),
        scratch_shapes=[pltpu.VMEM((d, h2), MXU_DTYPE), pltpu.VMEM((h2 // 2, d), MXU_DTYPE)],
    )
    return pl.pallas_call(
        _expert_body,
        grid_spec=grid_spec,
        out_shape=jax.ShapeDtypeStruct((n_rows, d), F32),
        compiler_params=_cparams(("arbitrary",)),
        name="expert_ffn",
    )(blk_expert, blk_used, buf, w_up, w_down)


def _moe(layer, h2, route, w_up, w_down):
    bsz, t, d = h2.shape
    n_tok = bsz * t
    n_assign = n_tok * TOP_K
    n_blocks = -(-(n_assign + N_EXPERTS * (EXPERT_TILE - 1)) // EXPERT_TILE)
    expert = route.reshape(n_tok, ROUTER_PAD)[:, ROUTE_EXPERT:ROUTE_EXPERT + TOP_K].astype(jnp.int32)
    src_tok, dest, blk_expert, blk_used = _dispatch_plan(expert, n_blocks)
    buf = _sc_gather(h2.reshape(n_tok, d), src_tok)
    y = _expert_ffn(layer, buf, blk_expert, blk_used, w_up, w_down)
    return _sc_gather(y, dest).reshape(TOP_K, bsz, t, d)


def _final_body(x_ref, y0_ref, y1_ref, rt_ref, mod_ref, g_ref, o_ref):
    o_ref[0] = _rms(_moe_residual(x_ref, y0_ref, y1_ref, rt_ref, mod_ref), g_ref[...])


def _final_norm(layer, x, ymoe, route, modsel, g, ctx_tiles):
    bsz, t, d = x.shape
    tm = TOK_TILE
    lat = lambda b, i: (b, i + ctx_tiles, 0)
    return pl.pallas_call(
        _final_body,
        grid=(bsz, t // tm - ctx_tiles),
        in_specs=[pl.BlockSpec((1, tm, d), lat)] + _moe_specs(tm, d, lambda i: i + ctx_tiles)
        + [pl.BlockSpec((None, None, None, 6, d), lambda b, i: (layer, b, 1, 0, 0)),
           pl.BlockSpec((1, d), lambda b, i: (0, 0))],
        out_specs=pl.BlockSpec((1, tm, d), lambda b, i: (b, i, 0)),
        out_shape=jax.ShapeDtypeStruct((bsz, t - ctx_tiles * tm, d), F32),
        compiler_params=_cparams(("parallel", "arbitrary")),
        name="final_norm",
    )(x, ymoe, ymoe, route, modsel, g.reshape(1, d))


def _rope_tables(n_ctx, n_lat):
    t = jnp.arange(n_lat)
    row = (t // GRID_W).astype(F32)
    col = (t % GRID_W).astype(F32)
    n_freq = HEAD_DIM // 4
    inv = ROPE_THETA ** (-jnp.arange(n_freq, dtype=F32) / n_freq)
    ang_r = row[:, None] * inv
    ang_c = col[:, None] * inv
    cr, sr, cc, sc = jnp.cos(ang_r), jnp.sin(ang_r), jnp.cos(ang_c), jnp.sin(ang_c)
    z = jnp.zeros_like(sr)
    cos = jnp.concatenate([cr, cr, cc, cc], axis=-1)
    above = jnp.concatenate([-sr, z, -sc, z], axis=-1)
    below = jnp.concatenate([z, sr, z, sc], axis=-1)
    reps = LANES // HEAD_DIM

    def full(tab, ctx_value):
        tab = jnp.tile(tab, (1, reps))
        return jnp.concatenate([jnp.full((n_ctx, LANES), ctx_value, F32), tab], axis=0)

    return full(cos, 1.0), full(above, 0.0), full(below, 0.0)


def _pack_w_in(w_in):
    parts = jnp.split(w_in, np.cumsum(IN_SIZES)[:-1].tolist(), axis=-1)
    qa, ka, va, qb, kb, vb, rb, gb, qn, kn, vn = parts
    gb = jnp.pad(gb, ((0, 0), (0, GATE_PAD - gb.shape[-1])))
    return jnp.concatenate([qa, ka, va, qb, kb, vb, rb, gb, qn, kn, vn], axis=-1).astype(MXU_DTYPE)


def kernel(x, c, ctx, c_ctx, w_mod, b_mod, norm1_g, norm2_g, w_in, w_out, diff_lambda, diff_sub_g,
           gla_w_decay, gla_b_decay, gla_norm_g, na_rel_bias, w_router_group, b_router_group,
           w_router_expert, b_router_expert, w_expert_up, w_expert_down, final_g):
    bsz, seq, d = x.shape
    n_ctx = ctx.shape[1]
    depth = w_mod.shape[0]
    assert n_ctx == TOK_TILE and seq % (NA_TILE_ROWS * GRID_W) == 0 and d % LANES == 0
    assert seq // (NA_TILE_ROWS * GRID_W) >= NA_KEY_TILES
    mod_rows = -(-(bsz + 1) // 8) * 8
    cvec = jnp.zeros((mod_rows, d), F32).at[:bsz].set(c).at[bsz].set(c_ctx)
    mod = _modulation(cvec, w_mod, b_mod).reshape(depth, mod_rows, 6, d)
    modsel = jnp.stack([jnp.broadcast_to(mod[:, bsz][:, None], (depth, bsz, 6, d)), mod[:, :bsz]], axis=2)
    rope_tabs = _rope_tables(n_ctx, seq)
    xt = jnp.concatenate([ctx, x], axis=1)
    ymoe = route = None
    for l in range(depth):
        outs = _in_proj(l, xt, ymoe, route, modsel, norm1_g[l], _pack_w_in(w_in[l]), rope_tabs)
        qa, ka, va, bg, qn, kn, vn = outs[:7]
        if ymoe is not None:
            xt = outs[7]
        ya = _diff_attention(l, qa, ka, va, diff_lambda[l], diff_sub_g[l])
        o_f, o_b = _gla_scan(bg, gla_w_decay[l], gla_b_decay[l])
        yn = _neighborhood_attention(qn, kn, vn, _na_bias_tables(na_rel_bias[l], seq // GRID_W))
        w_router = jnp.pad(jnp.concatenate([w_router_group[l], w_router_expert[l]], axis=-1),
                           ((0, 0), (0, ROUTER_PAD - N_GROUPS - N_EXPERTS)))
        w_router = jnp.concatenate(_split_bf16(w_router, 2), axis=-1)
        b_router = jnp.pad(jnp.concatenate([b_router_group[l], b_router_expert[l]]),
                           (0, ROUTER_PAD - N_GROUPS - N_EXPERTS)).reshape(1, ROUTER_PAD)
        xt, h2, route = _out_proj(l, ya, o_f, o_b, bg, yn, xt, modsel, gla_norm_g[l], norm2_g[l],
                                  w_out[l].astype(MXU_DTYPE), w_router, b_router)
        ymoe = _moe(l, h2, route, w_expert_up, w_expert_down)
    return _final_norm(depth - 1, xt, ymoe, route, modsel, final_g, n_ctx // TOK_TILE)
```

```python
import functools
import math

import numpy as np
import jax
import jax.numpy as jnp
from jax import lax
from jax.experimental import pallas as pl
from jax.experimental.pallas import tpu as pltpu
from jax.experimental.pallas import tpu_sc as plsc

F32 = jnp.float32
MXU_DTYPE = jnp.bfloat16
HI = lax.Precision.HIGHEST

GRID_W = 64
HEAD_DIM = 64
ROPE_THETA = 10000.0
NORM_EPS = 1e-6

A_HEADS = 4
A_QK = HEAD_DIM
A_V = 2 * HEAD_DIM
B_HEADS = 4
B_DK = HEAD_DIM // 2
B_DV = HEAD_DIM
B_GATE_RANK = 16
B_GATE_TAU = 16.0
B_CHUNK = 64
A_KEY_CHUNK = 1024
LOG2_E = math.log2(math.e)
GLA_SUB = 16
C_HEADS = 4
C_DH = HEAD_DIM
NA_ROWS = 8
NA_COLS = 16

A_WIDTH = A_HEADS * A_V
B_WIDTH = B_HEADS * B_DV
C_WIDTH = C_HEADS * C_DH
B_QK = B_HEADS * B_DK
IN_SIZES = (A_HEADS * 2 * A_QK, A_HEADS * 2 * A_QK, A_WIDTH,
            B_QK, B_QK, B_WIDTH, B_WIDTH, 2 * B_GATE_RANK,
            C_WIDTH, C_WIDTH, C_WIDTH)

N_GROUPS = 4
EXPERTS_PER_GROUP = 8
N_EXPERTS = N_GROUPS * EXPERTS_PER_GROUP
TOP_K = 2
EXPERT_HIDDEN = 512

LANES = 128
TOK_TILE = 256
NA_TILE_ROWS = 4
NA_KEY_TILES = 3
EXPERT_TILE = 256
GATE_PAD = LANES
BG_WIDTH = 2 * B_QK + 2 * B_WIDTH + GATE_PAD
IN_PAD_WIDTH = 3 * A_WIDTH + BG_WIDTH + 3 * C_WIDTH
ROUTER_PAD = LANES
ROUTE_EXPERT = 0
ROUTE_GATE = 2
SC_CORES = 2
SC_SUBCORES = 16
SC_ROW_BUFFER_BYTES = 128 * 1024
VMEM_LIMIT = 48 * 1024 * 1024


def _split_bf16(x, pieces):
    out = []
    for _ in range(pieces):
        p = x.astype(jnp.bfloat16)
        out.append(p)
        x = x - p.astype(F32)
    return out


def _silu(x):
    return x * (1.0 / (1.0 + jnp.exp(-x)))


def _cparams(sem):
    return pltpu.CompilerParams(dimension_semantics=sem, vmem_limit_bytes=VMEM_LIMIT)


def _mod_body(c_ref, w_ref, b_ref, o_ref):
    a = _silu(c_ref[...])
    o_ref[...] = jnp.dot(a, w_ref[...], precision=HI, preferred_element_type=F32) + b_ref[...]


def _modulation(cvec, w_mod, b_mod):
    depth, d, d6 = w_mod.shape
    rows = cvec.shape[0]
    return pl.pallas_call(
        _mod_body,
        grid=(depth, d6 // d),
        in_specs=[pl.BlockSpec((rows, d), lambda l, j: (0, 0)),
                  pl.BlockSpec((None, d, d), lambda l, j: (l, 0, j)),
                  pl.BlockSpec((None, 1, d), lambda l, j: (l, 0, j))],
        out_specs=pl.BlockSpec((None, rows, d), lambda l, j: (l, 0, j)),
        out_shape=jax.ShapeDtypeStruct((depth, rows, d6), F32),
        compiler_params=_cparams(("arbitrary", "arbitrary")),
        name="modulation",
    )(cvec, w_mod, b_mod.reshape(depth, 1, d6))


def _rms(x, g):
    return x * lax.rsqrt(jnp.mean(x * x, axis=-1, keepdims=True) + NORM_EPS) * g


def _rope(x, cos, sa, sb):
    return x * cos + pltpu.roll(x, LANES - 16, 1) * sa + pltpu.roll(x, 16, 1) * sb


def _moe_residual(x_ref, y0_ref, y1_ref, rt_ref, mod_ref):
    rt = rt_ref[0]
    moe = rt[:, ROUTE_GATE:ROUTE_GATE + 1] * y0_ref[0] + rt[:, ROUTE_GATE + 1:ROUTE_GATE + 2] * y1_ref[0]
    return x_ref[0] + mod_ref[5:6, :] * moe


def _in_proj_body(combine, *refs):
    if combine:
        (x_ref, y0_ref, y1_ref, rt_ref, pmod_ref, mod_ref, g_ref, w_ref, cos_ref, sa_ref, sb_ref,
         qa_ref, ka_ref, va_ref, bg_ref, qn_ref, kn_ref, vn_ref, xo_ref) = refs
        x = _moe_residual(x_ref, y0_ref, y1_ref, rt_ref, pmod_ref)
        xo_ref[0] = x
    else:
        (x_ref, mod_ref, g_ref, w_ref, cos_ref, sa_ref, sb_ref,
         qa_ref, ka_ref, va_ref, bg_ref, qn_ref, kn_ref, vn_ref) = refs
        x = x_ref[0]
    h = _rms(x, g_ref[...]) * (1.0 + mod_ref[1:2, :]) + mod_ref[0:1, :]
    hb = h.astype(MXU_DTYPE)

    def proj(lo, hi):
        return jnp.dot(hb, w_ref[:, lo:hi], preferred_element_type=F32)

    cos, sa, sb = cos_ref[...], sa_ref[...], sb_ref[...]
    for hh in range(A_HEADS):
        lo = hh * LANES
        q = proj(lo, lo + LANES)
        qa_ref[0, :, lo:lo + LANES] = (_rope(q, cos, sa, sb) * (A_QK ** -0.5 * LOG2_E)).astype(qa_ref.dtype)
        k = proj(A_WIDTH + lo, A_WIDTH + lo + LANES)
        ka_ref[0, :, lo:lo + LANES] = _rope(k, cos, sa, sb).astype(ka_ref.dtype)
    o = 2 * A_WIDTH
    va_ref[0] = proj(o, o + A_WIDTH).T.astype(va_ref.dtype)
    o += A_WIDTH
    bg_ref[0] = proj(o, o + BG_WIDTH)
    o += BG_WIDTH
    qn_ref[0] = (proj(o, o + C_WIDTH) * (C_DH ** -0.5)).astype(qn_ref.dtype)
    o += C_WIDTH
    kn_ref[0] = proj(o, o + C_WIDTH).astype(kn_ref.dtype)
    o += C_WIDTH
    vn_ref[0] = proj(o, o + C_WIDTH).astype(vn_ref.dtype)


def _moe_specs(tm, d, row_block):
    return [pl.BlockSpec((None, 1, tm, d), lambda b, i: (0, b, row_block(i), 0)),
            pl.BlockSpec((None, 1, tm, d), lambda b, i: (1, b, row_block(i), 0)),
            pl.BlockSpec((1, tm, ROUTER_PAD), lambda b, i: (b, row_block(i), 0))]


def _in_proj(layer, x, ymoe, route, modsel, g1, w_in_p, rope_tabs):
    bsz, t, d = x.shape
    tm = TOK_TILE
    combine = ymoe is not None
    tok = lambda b, i: (b, i, 0)
    x_spec = pl.BlockSpec((1, tm, d), tok)

    def mod_spec(l):
        return pl.BlockSpec((None, None, None, 6, d), lambda b, i: (l, b, jnp.minimum(i, 1), 0, 0))

    tab_spec = pl.BlockSpec((tm, LANES), lambda b, i: (i, 0))
    in_specs = [x_spec]
    args = [x]
    if combine:
        in_specs += _moe_specs(tm, d, lambda i: i) + [mod_spec(layer - 1)]
        args += [ymoe, ymoe, route, modsel]
    in_specs += [mod_spec(layer), pl.BlockSpec((1, d), lambda b, i: (0, 0)),
                 pl.BlockSpec((d, IN_PAD_WIDTH), lambda b, i: (0, 0)), tab_spec, tab_spec, tab_spec]
    args += [modsel, g1.reshape(1, d), w_in_p, *rope_tabs]

    def o(width, dtype):
        return pl.BlockSpec((1, tm, width), tok), jax.ShapeDtypeStruct((bsz, t, width), dtype)

    va_t = (pl.BlockSpec((1, A_WIDTH, tm), lambda b, i: (b, 0, i)),
            jax.ShapeDtypeStruct((bsz, A_WIDTH, t), MXU_DTYPE))
    outs = [o(A_WIDTH, MXU_DTYPE), o(A_WIDTH, MXU_DTYPE), va_t, o(BG_WIDTH, F32),
            o(C_WIDTH, MXU_DTYPE), o(C_WIDTH, MXU_DTYPE), o(C_WIDTH, MXU_DTYPE)]
    if combine:
        outs.append(o(d, F32))
    return pl.pallas_call(
        functools.partial(_in_proj_body, combine),
        grid=(bsz, t // tm),
        in_specs=in_specs,
        out_specs=[s for s, _ in outs],
        out_shape=[s for _, s in outs],
        compiler_params=_cparams(("parallel", "arbitrary")),
        name="in_proj",
    )(*args)


def _diff_attn_body(lam_init, ctx_len, q_ref, k_ref, vt_ref, lam_ref, g_ref, o_ref):
    q = q_ref[0]
    lane = lax.broadcasted_iota(jnp.int32, (1, LANES), 1)
    zero = jnp.zeros_like(q)
    q1 = jnp.where(lane < A_QK, q, zero)
    q2 = jnp.where(lane >= A_QK, q, zero)
    lm = lam_ref[...]
    lam = (jnp.exp(jnp.sum(lm[0:1] * lm[1:2], axis=1, keepdims=True))
           - jnp.exp(jnp.sum(lm[2:3] * lm[3:4], axis=1, keepdims=True)) + lam_init)

    nt = (((1,), (1,)), ((), ()))
    tq = q.shape[0]

    def attend(n_keys):
        def init():
            return (jnp.full((1, tq), -jnp.inf, F32), jnp.zeros((1, tq), F32), jnp.zeros((A_V, tq), F32))

        bounds = [0, min(ctx_len, n_keys)] + list(range(ctx_len + A_KEY_CHUNK, n_keys + 1, A_KEY_CHUNK))
        chunks = list(zip(bounds[:-1], bounds[1:]))

        def scores(qm, c):
            k_c = k_ref[0, chunks[c][0]:chunks[c][1], :]
            return lax.dot_general(k_c, qm, nt, preferred_element_type=F32)

        def probs(state, s):
            m_old, l_old, acc = state
            m_new = jnp.maximum(m_old, jnp.max(s, axis=0, keepdims=True))
            alpha = jnp.exp2(m_old - m_new)
            p = jnp.exp2(s - m_new)
            l_new = alpha * l_old + jnp.sum(p, axis=0, keepdims=True)
            return p.astype(MXU_DTYPE), alpha, (m_new, l_new, acc)

        def accumulate(state, alpha, p, c):
            m, l, acc = state
            vt_c = vt_ref[0, :, chunks[c][0]:chunks[c][1]]
            return m, l, alpha * acc + jnp.dot(vt_c, p, preferred_element_type=F32)

        st1, st2 = init(), init()
        s1, s2 = scores(q1, 0), scores(q2, 0)
        for c in range(len(chunks)):
            p1, a1, st1 = probs(st1, s1)
            if c + 1 < len(chunks):
                s1 = scores(q1, c + 1)
            st1 = accumulate(st1, a1, p1, c)
            p2, a2, st2 = probs(st2, s2)
            if c + 1 < len(chunks):
                s2 = scores(q2, c + 1)
            st2 = accumulate(st2, a2, p2, c)
        o_t = st1[2] * (1.0 / st1[1]) - st2[2] * (lam / st2[1])
        o_ref[0] = (_rms(o_t.T, g_ref[...]) * (1.0 - lam_init)).astype(o_ref.dtype)

    @pl.when(pl.program_id(2) == 0)
    def _():
        attend(ctx_len)

    @pl.when(pl.program_id(2) > 0)
    def _():
        attend(k_ref.shape[1])


def _diff_attention(layer, qa, ka, va, lam, g_sub):
    bsz, t, _ = qa.shape
    tq = TOK_TILE
    lam_init = 0.8 - 0.6 * math.exp(-0.3 * layer)
    kv_spec = pl.BlockSpec((1, t, LANES), lambda b, h, i: (b, 0, h))
    return pl.pallas_call(
        functools.partial(_diff_attn_body, lam_init, TOK_TILE),
        grid=(bsz, A_HEADS, t // tq),
        in_specs=[pl.BlockSpec((1, tq, LANES), lambda b, h, i: (b, i, h)), kv_spec,
                  pl.BlockSpec((1, A_V, t), lambda b, h, i: (b, h, 0)),
                  pl.BlockSpec((4, A_QK), lambda b, h, i: (0, 0)),
                  pl.BlockSpec((1, A_V), lambda b, h, i: (0, 0))],
        out_specs=pl.BlockSpec((1, tq, LANES), lambda b, h, i: (b, i, h)),
        out_shape=jax.ShapeDtypeStruct((bsz, t, A_WIDTH), MXU_DTYPE),
        compiler_params=_cparams(("parallel", "parallel", "arbitrary")),
        name="diff_attention",
    )(qa, ka, va, lam, g_sub.reshape(1, A_V))


def _gla_body(f_ref, r_ref, wdec_ref, bdec_ref, of_ref, ob_ref, sf_ref, sb_ref, e_ref, b_ref, qs_ref):
    c = B_CHUNK
    n_chunks = TOK_TILE // c

    @pl.when(pl.program_id(1) == 0)
    def _():
        sf_ref[...] = jnp.zeros_like(sf_ref)
        sb_ref[...] = jnp.zeros_like(sb_ref)

    sub = GLA_SUB
    n_sub = c // sub
    nt = (((1,), (1,)), ((), ()))
    t_row = lax.broadcasted_iota(jnp.int32, (TOK_TILE, TOK_TILE), 0)
    t_col = lax.broadcasted_iota(jnp.int32, (TOK_TILE, TOK_TILE), 1)
    same_chunk = (t_row // c) == (t_col // c)
    tri_f = (same_chunk & (t_col <= t_row)).astype(jnp.bfloat16)
    tri_b = (same_chunk & (t_col >= t_row)).astype(jnp.bfloat16)
    s_iota = lax.broadcasted_iota(jnp.int32, (sub, LANES), 0)
    idx = lax.broadcasted_iota(jnp.int32, (c, LANES), 0)
    head_of_k = lax.broadcasted_iota(jnp.int32, (B_QK, B_WIDTH), 0) // B_DK
    head_of_v = lax.broadcasted_iota(jnp.int32, (B_QK, B_WIDTH), 1) // B_DV
    expand = (head_of_k == head_of_v).astype(MXU_DTYPE)
    same_head_t = (lax.broadcasted_iota(jnp.int32, (B_WIDTH, B_QK), 0) // B_DV
                   == lax.broadcasted_iota(jnp.int32, (B_WIDTH, B_QK), 1) // B_DK)
    n_ref = n_sub - 1
    kt_keep = (lax.broadcasted_iota(jnp.int32, (B_HEADS * c, n_ref * B_QK), 0) // c
               == (lax.broadcasted_iota(jnp.int32, (B_HEADS * c, n_ref * B_QK), 1) % B_QK) // B_DK)
    vx_keep = (lax.broadcasted_iota(jnp.int32, (B_HEADS * c, B_WIDTH), 0) // c
               == lax.broadcasted_iota(jnp.int32, (B_HEADS * c, B_WIDTH), 1) // B_DV)
    pick = (lax.broadcasted_iota(jnp.int32, (c, c * sub), 1) // sub
            == lax.broadcasted_iota(jnp.int32, (c, c * sub), 0)).astype(MXU_DTYPE)

    def log_decay(src_ref, backward):
        gl = src_ref[0, :, 2 * B_QK + 2 * B_WIDTH:BG_WIDTH]
        d0 = B_QK if backward else 0
        z = jnp.dot(gl, wdec_ref[:, d0:d0 + B_QK], precision=HI, preferred_element_type=F32) \
            + bdec_ref[:, d0:d0 + B_QK]
        log_a = (jnp.minimum(z, 0.0) - jnp.log(1.0 + jnp.exp(-jnp.abs(z)))) / B_GATE_TAU
        tri = tri_b if backward else tri_f
        return sum(jnp.dot(tri, p, preferred_element_type=F32) for p in _split_bf16(log_a, 3))

    def chunk(src_ref, b_all, lo, backward, st_ref, out_ref):
        q = src_ref[0, lo:lo + c, 0:B_QK] * (B_DK ** -0.5)
        k = src_ref[0, lo:lo + c, B_QK:2 * B_QK]
        v = src_ref[0, lo:lo + c, 2 * B_QK:2 * B_QK + B_WIDTH]
        b = b_all[lo:lo + c]
        b_ref[...] = b
        qs_ref[...] = q
        blk = ((c - 1 - idx) if backward else idx) // sub

        q_parts, k_parts = [], []
        for m, late, early in ((1, blk == 1, blk == 0), (2, blk >= 2, blk <= 1), (3, blk == 3, blk == 2)):
            r_row = (c - 1 - sub * m) if backward else sub * m
            r = b_ref[r_row:r_row + 1, :]
            q_parts.append(q * jnp.exp(jnp.where(late, b - r, -jnp.inf)))
            k_parts.append(k * jnp.exp(jnp.where(early, r - b, -jnp.inf)))
        q_cat = jnp.concatenate(q_parts, axis=1).astype(MXU_DTYPE)
        k_cat = jnp.concatenate(k_parts, axis=1)
        k_exp = jnp.where(kt_keep, jnp.concatenate([k_cat] * B_HEADS, axis=0), 0.0).astype(MXU_DTYPE)
        a_off = lax.dot_general(q_cat, k_exp, nt, preferred_element_type=F32)
        v_exp = jnp.where(vx_keep, jnp.concatenate([v] * B_HEADS, axis=0), 0.0).astype(MXU_DTYPE)
        o_off = jnp.dot(a_off.astype(MXU_DTYPE), v_exp, preferred_element_type=F32)

        for tt in range(c):
            lo_s = tt // sub * sub
            keep = (s_iota >= tt - lo_s) if backward else (s_iota <= tt - lo_s)
            bt = b_ref[tt:tt + 1, :]
            qt = qs_ref[tt:tt + 1, :]
            e = jnp.exp(jnp.where(keep, bt - b[lo_s:lo_s + sub], -jnp.inf)) * (qt * k[lo_s:lo_s + sub])
            e_ref[tt * sub:(tt + 1) * sub, :] = e.astype(e_ref.dtype)
        a_exp = jnp.dot(e_ref[...], expand, preferred_element_type=F32)
        prod = a_exp.reshape(n_sub, sub, sub, B_WIDTH) * v.reshape(n_sub, 1, sub, B_WIDTH)
        o_diag = jnp.dot(pick, prod.reshape(c * sub, B_WIDTH).astype(MXU_DTYPE), preferred_element_type=F32)

        st = st_ref[...]
        o_inter = lax.dot_general(q * jnp.exp(b), st, nt, preferred_element_type=F32)
        out_ref[0, lo:lo + c, :] = o_off + o_diag + o_inter
        b_end = b[0:1, :] if backward else b[c - 1:c, :]
        kd = k * jnp.exp(b_end - b)
        upd = lax.dot_general(v, kd, (((0,), (0,)), ((), ())), preferred_element_type=F32)
        st_ref[...] = jnp.exp(b_end) * st + jnp.where(same_head_t, upd, 0.0)

    b_fwd = log_decay(f_ref, False)
    b_bwd = log_decay(r_ref, True)
    for ci in range(n_chunks):
        chunk(f_ref, b_fwd, ci * c, False, sf_ref, of_ref)
        chunk(r_ref, b_bwd, (n_chunks - 1 - ci) * c, True, sb_ref, ob_ref)


def _gla_scan(bg, w_dec, b_dec):
    bsz, t, _ = bg.shape
    n = t // TOK_TILE
    rev = lambda b, i: (b, jnp.where(i == 0, 0, n - i), 0)
    fwd = lambda b, i: (b, i, 0)
    wdec = jnp.zeros((GATE_PAD, 2 * B_QK), F32)
    wdec = wdec.at[:B_GATE_RANK, :B_QK].set(w_dec[0]).at[B_GATE_RANK:2 * B_GATE_RANK, B_QK:].set(w_dec[1])
    bdec = b_dec.reshape(1, 2 * B_QK)
    o_shape = jax.ShapeDtypeStruct((bsz, t, B_WIDTH), F32)
    return pl.pallas_call(
        _gla_body,
        grid=(bsz, n),
        in_specs=[pl.BlockSpec((1, TOK_TILE, BG_WIDTH), fwd),
                  pl.BlockSpec((1, TOK_TILE, BG_WIDTH), rev),
                  pl.BlockSpec((GATE_PAD, 2 * B_QK), lambda b, i: (0, 0)),
                  pl.BlockSpec((1, 2 * B_QK), lambda b, i: (0, 0))],
        out_specs=[pl.BlockSpec((1, TOK_TILE, B_WIDTH), fwd),
                   pl.BlockSpec((1, TOK_TILE, B_WIDTH), rev)],
        out_shape=[o_shape, o_shape],
        scratch_shapes=[pltpu.VMEM((B_WIDTH, B_QK), F32), pltpu.VMEM((B_WIDTH, B_QK), F32),
                        pltpu.VMEM((B_CHUNK * GLA_SUB, LANES), MXU_DTYPE),
                        pltpu.VMEM((B_CHUNK, LANES), F32), pltpu.VMEM((B_CHUNK, LANES), F32)],
        compiler_params=_cparams(("parallel", "arbitrary")),
        name="gla_scan",
    )(bg, bg, wdec, bdec)


def _na_bias_tables(rpb, rows):
    n_tiles = rows // NA_TILE_ROWS
    wr = min(NA_ROWS, rows)
    n_dr, n_dc = 2 * NA_ROWS - 1, 2 * NA_COLS - 1
    cq = np.arange(GRID_W)[:, None]
    ck = np.arange(GRID_W)[None, :]
    cs = np.clip(cq - NA_COLS // 2, 0, GRID_W - NA_COLS)
    col_ok = (ck >= cs) & (ck < cs + NA_COLS)
    dc = np.clip(ck - cq, -(NA_COLS - 1), NA_COLS - 1) + (NA_COLS - 1)
    onehot = (dc.reshape(1, -1) == np.arange(n_dc)[:, None]).astype(np.float32)
    by_col = jnp.dot(rpb.astype(F32).reshape(-1, n_dc), onehot, precision=HI)
    by_col = jnp.where(col_ok.reshape(1, 1, GRID_W, GRID_W),
                       by_col.reshape(C_HEADS, n_dr, GRID_W, GRID_W), -jnp.inf)
    masked = jnp.full((C_HEADS, GRID_W, GRID_W), -jnp.inf, F32)
    tabs = []
    for j in (0, 1, n_tiles - 1):
        kr0 = int(np.clip(j - 1, 0, n_tiles - NA_KEY_TILES)) * NA_TILE_ROWS
        q_rows = []
        for qr in range(NA_TILE_ROWS):
            r = j * NA_TILE_ROWS + qr
            start = int(np.clip(r - wr // 2, 0, rows - wr))
            blocks = []
            for kw in range(NA_KEY_TILES * NA_TILE_ROWS):
                kr = kr0 + kw
                blocks.append(by_col[:, kr - r + NA_ROWS - 1] if start <= kr < start + wr else masked)
            q_rows.append(jnp.concatenate(blocks, axis=-1))
        tabs.append(jnp.concatenate(q_rows, axis=1))
    return jnp.stack(tabs)


def _na_body(q_ref, k0_ref, k1_ref, k2_ref, kc_ref, v0_ref, v1_ref, v2_ref, vc_ref, m_ref, o_ref):
    q = q_ref[0]
    lane = lax.broadcasted_iota(jnp.int32, (1, LANES), 1)
    zero = jnp.zeros_like(q)
    nt = (((1,), (1,)), ((), ()))

    def scores(qm, k_ref):
        return lax.dot_general(qm, k_ref[0], nt, preferred_element_type=F32)

    def head_out(hh, windows):
        qm = jnp.where((lane >= hh * C_DH) & (lane < (hh + 1) * C_DH), q, zero)
        s = [scores(qm, kc_ref)]
        for w, k_ref in enumerate(windows):
            s.append(scores(qm, k_ref) + m_ref[0, hh, :, w * TOK_TILE:(w + 1) * TOK_TILE])
        m = functools.reduce(jnp.maximum, [jnp.max(x, axis=-1, keepdims=True) for x in s])
        p = [jnp.exp(x - m) for x in s]
        den = functools.reduce(jnp.add, [jnp.sum(x, axis=-1, keepdims=True) for x in p])
        vals = [vc_ref] + [v0_ref, v1_ref, v2_ref][:len(windows)]
        o = functools.reduce(jnp.add, [jnp.dot(x.astype(MXU_DTYPE), v_ref[0], preferred_element_type=F32)
                                       for x, v_ref in zip(p, vals)])
        return o * (1.0 / den)

    def emit(windows):
        o0 = head_out(0, windows)
        o1 = head_out(1, windows)
        o_ref[0] = jnp.where(lane < C_DH, o0, o1).astype(o_ref.dtype)

    @pl.when(pl.program_id(1) == 0)
    def _():
        emit([])

    @pl.when(pl.program_id(1) > 0)
    def _():
        emit([k0_ref, k1_ref, k2_ref])


def _neighborhood_attention(qn, kn, vn, bias_tabs):
    bsz, t, _ = qn.shape
    n = t // TOK_TILE
    n_lat = n - 1

    def win(w):
        def index(hp, i, b):
            kb0 = jnp.clip(i - 2, 0, n_lat - NA_KEY_TILES)
            return (b, kb0 + 1 + w, hp)
        return pl.BlockSpec((1, TOK_TILE, LANES), index)

    def cls(hp, i, b):
        j = i - 1
        return (jnp.where(j <= 0, 0, jnp.where(j == n_lat - 1, 2, 1)), hp, 0, 0)

    own = pl.BlockSpec((1, TOK_TILE, LANES), lambda hp, i, b: (b, i, hp))
    ctx = pl.BlockSpec((1, TOK_TILE, LANES), lambda hp, i, b: (b, 0, hp))
    heads_per_step = LANES // C_DH
    return pl.pallas_call(
        _na_body,
        grid=(C_HEADS // heads_per_step, n, bsz),
        in_specs=[own, win(0), win(1), win(2), ctx, win(0), win(1), win(2), ctx,
                  pl.BlockSpec((1, heads_per_step, TOK_TILE, NA_KEY_TILES * TOK_TILE), cls)],
        out_specs=own,
        out_shape=jax.ShapeDtypeStruct((bsz, t, C_WIDTH), MXU_DTYPE),
        compiler_params=_cparams(("parallel", "parallel", "arbitrary")),
        name="neighborhood_attention",
    )(qn, kn, kn, kn, kn, vn, vn, vn, vn, bias_tabs)


def _out_proj_body(ya_ref, of_ref, ob_ref, r_ref, yn_ref, x_ref, mod_ref, gg_ref, g2_ref, w_ref,
                   wr_ref, br_ref, xo_ref, h_ref, rt_ref):
    o = of_ref[0] + ob_ref[0]
    hi = lax.broadcasted_iota(jnp.int32, (B_WIDTH, B_WIDTH), 0) // B_DV
    hj = lax.broadcasted_iota(jnp.int32, (B_WIDTH, B_WIDTH), 1) // B_DV
    head_mean = jnp.where(hi == hj, 1.0 / B_DV, 0.0).astype(jnp.bfloat16)
    ms = sum(jnp.dot(p, head_mean, preferred_element_type=F32) for p in _split_bf16(o * o, 2))
    yb = o * lax.rsqrt(ms + NORM_EPS) * gg_ref[...] * _silu(r_ref[0])
    mix = (jnp.dot(ya_ref[0], w_ref[0:A_WIDTH, :], preferred_element_type=F32)
           + jnp.dot(yb.astype(MXU_DTYPE), w_ref[A_WIDTH:A_WIDTH + B_WIDTH, :], preferred_element_type=F32)
           + jnp.dot(yn_ref[0], w_ref[A_WIDTH + B_WIDTH:, :], preferred_element_type=F32))
    x = x_ref[0] + mod_ref[2:3, :] * mix
    xo_ref[0] = x
    h = _rms(x, g2_ref[...]) * (1.0 + mod_ref[4:5, :]) + mod_ref[3:4, :]
    h_ref[0] = h.astype(h_ref.dtype)
    h_hi, h_lo = _split_bf16(h, 2)
    both = jnp.dot(h_hi, wr_ref[...], preferred_element_type=F32)
    logits = (both[:, :ROUTER_PAD] + both[:, ROUTER_PAD:]
              + jnp.dot(h_lo, wr_ref[:, :ROUTER_PAD], preferred_element_type=F32) + br_ref[...])
    rt_ref[0] = _route(logits)


def _route(lg):
    lane = lax.broadcasted_iota(jnp.int32, lg.shape, 1)
    big = jnp.int32(ROUTER_PAD)

    def top(mask):
        v = jnp.max(jnp.where(mask, lg, -jnp.inf), axis=-1, keepdims=True)
        i = jnp.min(jnp.where(mask & (lg == v), lane, big), axis=-1, keepdims=True)
        return v, i

    g_mask = lane < N_GROUPS
    g_max, grp = top(g_mask)
    p_grp = 1.0 / jnp.sum(jnp.where(g_mask, jnp.exp(lg - g_max), 0.0), axis=-1, keepdims=True)
    e_lo = N_GROUPS + grp * EXPERTS_PER_GROUP
    e_mask = (lane >= e_lo) & (lane < e_lo + EXPERTS_PER_GROUP)
    v1, i1 = top(e_mask)
    v2, i2 = top(e_mask & (lane != i1))
    r = jnp.exp(v2 - v1)
    gate1 = p_grp / (1.0 + r)
    gate2 = p_grp * r / (1.0 + r)
    out = jnp.where(lane == ROUTE_EXPERT, (i1 - N_GROUPS).astype(F32),
                    jnp.where(lane == ROUTE_EXPERT + 1, (i2 - N_GROUPS).astype(F32),
                              jnp.where(lane == ROUTE_GATE, gate1,
                                        jnp.where(lane == ROUTE_GATE + 1, gate2, 0.0))))
    return out


def _out_proj(layer, ya, o_f, o_b, bg, yn, x, modsel, g_gla, g2, w_out_b, w_router, b_router):
    bsz, t, d = x.shape
    tm = TOK_TILE
    tok = lambda b, i: (b, i, 0)
    const = lambda b, i: (0, 0)
    r_block = (2 * B_QK + B_WIDTH) // B_WIDTH
    return pl.pallas_call(
        _out_proj_body,
        grid=(bsz, t // tm),
        in_specs=[pl.BlockSpec((1, tm, A_WIDTH), tok),
                  pl.BlockSpec((1, tm, B_WIDTH), tok), pl.BlockSpec((1, tm, B_WIDTH), tok),
                  pl.BlockSpec((1, tm, B_WIDTH), lambda b, i: (b, i, r_block)),
                  pl.BlockSpec((1, tm, C_WIDTH), tok),
                  pl.BlockSpec((1, tm, d), tok),
                  pl.BlockSpec((None, None, None, 6, d), lambda b, i: (layer, b, jnp.minimum(i, 1), 0, 0)),
                  pl.BlockSpec((1, B_WIDTH), const), pl.BlockSpec((1, d), const),
                  pl.BlockSpec(w_out_b.shape, const),
                  pl.BlockSpec((d, 2 * ROUTER_PAD), const), pl.BlockSpec((1, ROUTER_PAD), const)],
        out_specs=[pl.BlockSpec((1, tm, d), tok), pl.BlockSpec((1, tm, d), tok),
                   pl.BlockSpec((1, tm, ROUTER_PAD), tok)],
        out_shape=[jax.ShapeDtypeStruct((bsz, t, d), F32), jax.ShapeDtypeStruct((bsz, t, d), F32),
                   jax.ShapeDtypeStruct((bsz, t, ROUTER_PAD), F32)],
        compiler_params=_cparams(("parallel", "arbitrary")),
        name="out_proj",
    )(ya, o_f, o_b, bg, yn, x, modsel, jnp.tile(g_gla, B_HEADS).reshape(1, B_WIDTH), g2.reshape(1, d),
      w_out_b, w_router, b_router)


def _dispatch_plan(expert, n_blocks):
    n_tok = expert.shape[0]
    flat = expert.reshape(-1)
    n_assign = flat.shape[0]
    order = jnp.argsort(flat).astype(jnp.int32)
    rank = jnp.argsort(order).astype(jnp.int32)
    counts = jnp.sum((flat[:, None] == jnp.arange(N_EXPERTS, dtype=jnp.int32)[None, :]).astype(jnp.int32), axis=0)
    padded = (counts + EXPERT_TILE - 1) // EXPERT_TILE * EXPERT_TILE
    start = jnp.cumsum(counts) - counts
    pad_end = jnp.cumsum(padded)
    pad_start = pad_end - padded
    shift = (pad_start - start).astype(jnp.int32)
    dest = rank + shift[flat]
    blk_start = jnp.arange(n_blocks, dtype=jnp.int32) * EXPERT_TILE
    blk_expert = jnp.sum((pad_end[None, :] <= blk_start[:, None]).astype(jnp.int32), axis=1)
    blk_expert = jnp.minimum(blk_expert, N_EXPERTS - 1)
    blk_used = (blk_start < pad_end[-1]).astype(jnp.int32)
    row = jnp.arange(n_blocks * EXPERT_TILE, dtype=jnp.int32)
    e_row = jnp.repeat(blk_expert, EXPERT_TILE)
    real = (row - pad_start[e_row].astype(jnp.int32)) < counts[e_row]
    src_assign = order[jnp.clip(row - shift[e_row], 0, n_assign - 1)]
    src_tok = jnp.where(real, src_assign // TOP_K, 0)
    dest = dest.reshape(n_tok, TOP_K).T.reshape(-1)
    return src_tok, dest, blk_expert, blk_used


def _sc_gather(table, idx):
    n_rows = idx.shape[0]
    width = table.shape[1]
    n_workers = SC_CORES * SC_SUBCORES
    per_worker = n_rows // n_workers
    chunk = SC_ROW_BUFFER_BYTES // (width * table.dtype.itemsize)
    n_chunks = per_worker // chunk
    assert per_worker * n_workers == n_rows and n_chunks * chunk == per_worker and n_chunks % 2 == 0
    assert n_chunks >= 4 and chunk <= LANES
    mesh = plsc.VectorSubcoreMesh(core_axis_name="core", subcore_axis_name="subcore")

    def body(table_hbm, idx_hbm, out_hbm, idx_v, rows_v, gather_sem, write_sem):
        worker = lax.axis_index("subcore") * SC_CORES + lax.axis_index("core")
        base = worker * per_worker
        pltpu.sync_copy(idx_hbm.at[worker], idx_v)

        def gather(j, slot):
            return pltpu.make_async_copy(table_hbm.at[idx_v.at[j]], rows_v.at[slot], gather_sem.at[slot])

        def write(j, slot):
            return pltpu.make_async_copy(rows_v.at[slot], out_hbm.at[pl.ds(base + j * chunk, chunk)],
                                         write_sem.at[slot])

        gather(0, 0).start()
        gather(0, 0).wait()
        gather(1, 1).start()
        write(0, 0).start()

        @pl.loop(1, n_chunks - 1, step=2)
        def _(j):
            for s in range(2):
                slot = (1 + s) % 2
                gather(j + s, slot).wait()
                write(j + s - 1, 1 - slot).wait()
                gather(j + s + 1, 1 - slot).start()
                write(j + s, slot).start()

        last = n_chunks - 1
        gather(last, 1).wait()
        write(last, 1).start()
        write(last - 1, 0).wait()
        write(last, 1).wait()

    return pl.kernel(
        body,
        out_type=jax.ShapeDtypeStruct((n_rows, width), table.dtype),
        mesh=mesh,
        scratch_types=[pltpu.VMEM((n_chunks, chunk), jnp.int32),
                       pltpu.VMEM((2, chunk, width), table.dtype),
                       pltpu.SemaphoreType.DMA((2,)), pltpu.SemaphoreType.DMA((2,))],
        name="sc_row_gather",
    )(table, idx.reshape(n_workers, n_chunks, chunk))


def _expert_body(be_ref, used_ref, x_ref, wu_ref, wd_ref, o_ref, wub_ref, wdb_ref):
    i = pl.program_id(0)
    prev = be_ref[jnp.maximum(i - 1, 0)]

    @pl.when((i == 0) | (be_ref[i] != prev))
    def _():
        wub_ref[...] = wu_ref[...].astype(wub_ref.dtype)
        wdb_ref[...] = wd_ref[...].astype(wdb_ref.dtype)

    @pl.when(used_ref[i] > 0)
    def _():
        gu = jnp.dot(x_ref[...].astype(MXU_DTYPE), wub_ref[...], preferred_element_type=F32)
        act = _silu(gu[:, :EXPERT_HIDDEN]) * gu[:, EXPERT_HIDDEN:]
        o_ref[...] = jnp.dot(act.astype(MXU_DTYPE), wdb_ref[...], preferred_element_type=F32)

    @pl.when(used_ref[i] == 0)
    def _():
        o_ref[...] = jnp.zeros_like(o_ref)


def _expert_ffn(layer, buf, blk_expert, blk_used, w_up, w_down):
    n_rows, d = buf.shape
    n_blocks = n_rows // EXPERT_TILE
    h2 = w_up.shape[-1]
    grid_spec = pltpu.PrefetchScalarGridSpec(
        num_scalar_prefetch=2,
        grid=(n_blocks,),
        in_specs=[pl.BlockSpec((EXPERT_TILE, d), lambda i, be, us: (i, 0)),
                  pl.BlockSpec((None, None, d, h2), lambda i, be, us: (layer, be[i], 0, 0)),
                  pl.BlockSpec((None, None, h2 // 2, d), lambda i, be, us: (layer, be[i], 0, 0))],
        out_specs=pl.BlockSpec((EXPERT_TILE, d), lambda i, be, us: (i, 0)),
        scratch_shapes=[pltpu.VMEM((d, h2), MXU_DTYPE), pltpu.VMEM((h2 // 2, d), MXU_DTYPE)],
    )
    return pl.pallas_call(
        _expert_body,
        grid_spec=grid_spec,
        out_shape=jax.ShapeDtypeStruct((n_rows, d), F32),
        compiler_params=_cparams(("arbitrary",)),
        name="expert_ffn",
    )(blk_expert, blk_used, buf, w_up, w_down)


def _moe(layer, h2, route, w_up, w_down):
    bsz, t, d = h2.shape
    n_tok = bsz * t
    n_assign = n_tok * TOP_K
    n_blocks = -(-(n_assign + N_EXPERTS * (EXPERT_TILE - 1)) // EXPERT_TILE)
    expert = route.reshape(n_tok, ROUTER_PAD)[:, ROUTE_EXPERT:ROUTE_EXPERT + TOP_K].astype(jnp.int32)
    src_tok, dest, blk_expert, blk_used = _dispatch_plan(expert, n_blocks)
    buf = _sc_gather(h2.reshape(n_tok, d), src_tok)
    y = _expert_ffn(layer, buf, blk_expert, blk_used, w_up, w_down)
    return _sc_gather(y, dest).reshape(TOP_K, bsz, t, d)


def _final_body(x_ref, y0_ref, y1_ref, rt_ref, mod_ref, g_ref, o_ref):
    o_ref[0] = _rms(_moe_residual(x_ref, y0_ref, y1_ref, rt_ref, mod_ref), g_ref[...])


def _final_norm(layer, x, ymoe, route, modsel, g, ctx_tiles):
    bsz, t, d = x.shape
    tm = TOK_TILE
    lat = lambda b, i: (b, i + ctx_tiles, 0)
    return pl.pallas_call(
        _final_body,
        grid=(bsz, t // tm - ctx_tiles),
        in_specs=[pl.BlockSpec((1, tm, d), lat)] + _moe_specs(tm, d, lambda i: i + ctx_tiles)
        + [pl.BlockSpec((None, None, None, 6, d), lambda b, i: (layer, b, 1, 0, 0)),
           pl.BlockSpec((1, d), lambda b, i: (0, 0))],
        out_specs=pl.BlockSpec((1, tm, d), lambda b, i: (b, i, 0)),
        out_shape=jax.ShapeDtypeStruct((bsz, t - ctx_tiles * tm, d), F32),
        compiler_params=_cparams(("parallel", "arbitrary")),
        name="final_norm",
    )(x, ymoe, ymoe, route, modsel, g.reshape(1, d))


def _rope_tables(n_ctx, n_lat):
    t = jnp.arange(n_lat)
    row = (t // GRID_W).astype(F32)
    col = (t % GRID_W).astype(F32)
    n_freq = HEAD_DIM // 4
    inv = ROPE_THETA ** (-jnp.arange(n_freq, dtype=F32) / n_freq)
    ang_r = row[:, None] * inv
    ang_c = col[:, None] * inv
    cr, sr, cc, sc = jnp.cos(ang_r), jnp.sin(ang_r), jnp.cos(ang_c), jnp.sin(ang_c)
    z = jnp.zeros_like(sr)
    cos = jnp.concatenate([cr, cr, cc, cc], axis=-1)
    above = jnp.concatenate([-sr, z, -sc, z], axis=-1)
    below = jnp.concatenate([z, sr, z, sc], axis=-1)
    reps = LANES // HEAD_DIM

    def full(tab, ctx_value):
        tab = jnp.tile(tab, (1, reps))
        return jnp.concatenate([jnp.full((n_ctx, LANES), ctx_value, F32), tab], axis=0)

    return full(cos, 1.0), full(above, 0.0), full(below, 0.0)


def _pack_w_in(w_in):
    parts = jnp.split(w_in, np.cumsum(IN_SIZES)[:-1].tolist(), axis=-1)
    qa, ka, va, qb, kb, vb, rb, gb, qn, kn, vn = parts
    gb = jnp.pad(gb, ((0, 0), (0, GATE_PAD - gb.shape[-1])))
    return jnp.concatenate([qa, ka, va, qb, kb, vb, rb, gb, qn, kn, vn], axis=-1).astype(MXU_DTYPE)


def kernel(x, c, ctx, c_ctx, w_mod, b_mod, norm1_g, norm2_g, w_in, w_out, diff_lambda, diff_sub_g,
           gla_w_decay, gla_b_decay, gla_norm_g, na_rel_bias, w_router_group, b_router_group,
           w_router_expert, b_router_expert, w_expert_up, w_expert_down, final_g):
    bsz, seq, d = x.shape
    n_ctx = ctx.shape[1]
    depth = w_mod.shape[0]
    assert n_ctx == TOK_TILE and seq % (NA_TILE_ROWS * GRID_W) == 0 and d % LANES == 0
    assert seq // (NA_TILE_ROWS * GRID_W) >= NA_KEY_TILES
    mod_rows = -(-(bsz + 1) // 8) * 8
    cvec = jnp.zeros((mod_rows, d), F32).at[:bsz].set(c).at[bsz].set(c_ctx)
    mod = _modulation(cvec, w_mod, b_mod).reshape(depth, mod_rows, 6, d)
    modsel = jnp.stack([jnp.broadcast_to(mod[:, bsz][:, None], (depth, bsz, 6, d)), mod[:, :bsz]], axis=2)
    rope_tabs = _rope_tables(n_ctx, seq)
    xt = jnp.concatenate([ctx, x], axis=1)
    ymoe = route = None
    for l in range(depth):
        outs = _in_proj(l, xt, ymoe, route, modsel, norm1_g[l], _pack_w_in(w_in[l]), rope_tabs)
        qa, ka, va, bg, qn, kn, vn = outs[:7]
        if ymoe is not None:
            xt = outs[7]
        ya = _diff_attention(l, qa, ka, va, diff_lambda[l], diff_sub_g[l])
        o_f, o_b = _gla_scan(bg, gla_w_decay[l], gla_b_decay[l])
        yn = _neighborhood_attention(qn, kn, vn, _na_bias_tables(na_rel_bias[l], seq // GRID_W))
        w_router = jnp.pad(jnp.concatenate([w_router_group[l], w_router_expert[l]], axis=-1),
                           ((0, 0), (0, ROUTER_PAD - N_GROUPS - N_EXPERTS)))
        w_router = jnp.concatenate(_split_bf16(w_router, 2), axis=-1)
        b_router = jnp.pad(jnp.concatenate([b_router_group[l], b_router_expert[l]]),
                           (0, ROUTER_PAD - N_GROUPS - N_EXPERTS)).reshape(1, ROUTER_PAD)
        xt, h2, route = _out_proj(l, ya, o_f, o_b, bg, yn, xt, modsel, gla_norm_g[l], norm2_g[l],
                                  w_out[l].astype(MXU_DTYPE), w_router, b_router)
        ymoe = _moe(l, h2, route, w_expert_up, w_expert_down)
    return _final_norm(depth - 1, xt, ymoe, route, modsel, final_g, n_ctx // TOK_TILE)
```

```python
import functools
import math

import numpy as np
import jax
import jax.numpy as jnp
from jax import lax
from jax.experimental import pallas as pl
from jax.experimental.pallas import tpu as pltpu
from jax.experimental.pallas import tpu_sc as plsc

F32 = jnp.float32
MXU_DTYPE = jnp.bfloat16
HI = lax.Precision.HIGHEST

GRID_W = 64
HEAD_DIM = 64
ROPE_THETA = 10000.0
NORM_EPS = 1e-6

A_HEADS = 4
A_QK = HEAD_DIM
A_V = 2 * HEAD_DIM
B_HEADS = 4
B_DK = HEAD_DIM // 2
B_DV = HEAD_DIM
B_GATE_RANK = 16
B_GATE_TAU = 16.0
B_CHUNK = 64
A_KEY_CHUNK = 1024
LOG2_E = math.log2(math.e)
GLA_SUB = 16
C_HEADS = 4
C_DH = HEAD_DIM
NA_ROWS = 8
NA_COLS = 16

A_WIDTH = A_HEADS * A_V
B_WIDTH = B_HEADS * B_DV
C_WIDTH = C_HEADS * C_DH
B_QK = B_HEADS * B_DK
IN_SIZES = (A_HEADS * 2 * A_QK, A_HEADS * 2 * A_QK, A_WIDTH,
            B_QK, B_QK, B_WIDTH, B_WIDTH, 2 * B_GATE_RANK,
            C_WIDTH, C_WIDTH, C_WIDTH)

N_GROUPS = 4
EXPERTS_PER_GROUP = 8
N_EXPERTS = N_GROUPS * EXPERTS_PER_GROUP
TOP_K = 2
EXPERT_HIDDEN = 512

LANES = 128
TOK_TILE = 256
NA_TILE_ROWS = 4
NA_KEY_TILES = 3
BATCH_GROUPS = 2
EXPERT_TILE = 256
GATE_PAD = LANES
BG_WIDTH = 2 * B_QK + 2 * B_WIDTH + GATE_PAD
IN_PAD_WIDTH = 3 * A_WIDTH + BG_WIDTH + 3 * C_WIDTH
ROUTER_PAD = LANES
ROUTE_EXPERT = 0
ROUTE_GATE = 2
SC_CORES = 2
SC_SUBCORES = 16
SC_ROW_BUFFER_BYTES = 128 * 1024
VMEM_LIMIT = 48 * 1024 * 1024


def _split_bf16(x, pieces):
    out = []
    for _ in range(pieces):
        p = x.astype(jnp.bfloat16)
        out.append(p)
        x = x - p.astype(F32)
    return out


def _silu(x):
    return x * (1.0 / (1.0 + jnp.exp(-x)))


def _cparams(sem):
    return pltpu.CompilerParams(dimension_semantics=sem, vmem_limit_bytes=VMEM_LIMIT)


def _mod_body(c_ref, w_ref, b_ref, o_ref):
    a = _silu(c_ref[...])
    o_ref[...] = jnp.dot(a, w_ref[...], precision=HI, preferred_element_type=F32) + b_ref[...]


def _modulation(cvec, w_mod, b_mod):
    depth, d, d6 = w_mod.shape
    rows = cvec.shape[0]
    return pl.pallas_call(
        _mod_body,
        grid=(depth, d6 // d),
        in_specs=[pl.BlockSpec((rows, d), lambda l, j: (0, 0)),
                  pl.BlockSpec((None, d, d), lambda l, j: (l, 0, j)),
                  pl.BlockSpec((None, 1, d), lambda l, j: (l, 0, j))],
        out_specs=pl.BlockSpec((None, rows, d), lambda l, j: (l, 0, j)),
        out_shape=jax.ShapeDtypeStruct((depth, rows, d6), F32),
        compiler_params=_cparams(("arbitrary", "arbitrary")),
        name="modulation",
    )(cvec, w_mod, b_mod.reshape(depth, 1, d6))


def _rms(x, g):
    return x * lax.rsqrt(jnp.mean(x * x, axis=-1, keepdims=True) + NORM_EPS) * g


def _rope(x, cos, sa, sb):
    return x * cos + pltpu.roll(x, LANES - 16, 1) * sa + pltpu.roll(x, 16, 1) * sb


def _moe_residual(x_ref, y0_ref, y1_ref, rt_ref, mod_ref):
    rt = rt_ref[0]
    moe = rt[:, ROUTE_GATE:ROUTE_GATE + 1] * y0_ref[0] + rt[:, ROUTE_GATE + 1:ROUTE_GATE + 2] * y1_ref[0]
    return x_ref[0] + mod_ref[5:6, :] * moe


def _in_proj_body(combine, *refs):
    if combine:
        (x_ref, y0_ref, y1_ref, rt_ref, pmod_ref, mod_ref, g_ref, w_ref, cos_ref, sa_ref, sb_ref,
         qa_ref, ka_ref, va_ref, bg_ref, qn_ref, kn_ref, vn_ref, xo_ref) = refs
        x = _moe_residual(x_ref, y0_ref, y1_ref, rt_ref, pmod_ref)
        xo_ref[0] = x
    else:
        (x_ref, mod_ref, g_ref, w_ref, cos_ref, sa_ref, sb_ref,
         qa_ref, ka_ref, va_ref, bg_ref, qn_ref, kn_ref, vn_ref) = refs
        x = x_ref[0]
    h = _rms(x, g_ref[...]) * (1.0 + mod_ref[1:2, :]) + mod_ref[0:1, :]
    hb = h.astype(MXU_DTYPE)

    def proj(lo, hi):
        return jnp.dot(hb, w_ref[:, lo:hi], preferred_element_type=F32)

    cos, sa, sb = cos_ref[...], sa_ref[...], sb_ref[...]
    for hh in range(A_HEADS):
        lo = hh * LANES
        q = proj(lo, lo + LANES)
        qa_ref[0, :, lo:lo + LANES] = (_rope(q, cos, sa, sb) * (A_QK ** -0.5 * LOG2_E)).astype(qa_ref.dtype)
        k = proj(A_WIDTH + lo, A_WIDTH + lo + LANES)
        ka_ref[0, :, lo:lo + LANES] = _rope(k, cos, sa, sb).astype(ka_ref.dtype)
    o = 2 * A_WIDTH
    va_ref[0] = proj(o, o + A_WIDTH).T.astype(va_ref.dtype)
    o += A_WIDTH
    bg_ref[0] = proj(o, o + BG_WIDTH)
    o += BG_WIDTH
    qn_ref[0] = (proj(o, o + C_WIDTH) * (C_DH ** -0.5)).astype(qn_ref.dtype)
    o += C_WIDTH
    kn_ref[0] = proj(o, o + C_WIDTH).astype(kn_ref.dtype)
    o += C_WIDTH
    vn_ref[0] = proj(o, o + C_WIDTH).astype(vn_ref.dtype)


def _moe_specs(tm, d, row_block):
    return [pl.BlockSpec((None, 1, tm, d), lambda b, i: (0, b, row_block(i), 0)),
            pl.BlockSpec((None, 1, tm, d), lambda b, i: (1, b, row_block(i), 0)),
            pl.BlockSpec((1, tm, ROUTER_PAD), lambda b, i: (b, row_block(i), 0))]


def _in_proj(layer, x, ymoe, route, modsel, g1, w_in_p, rope_tabs):
    bsz, t, d = x.shape
    tm = TOK_TILE
    combine = ymoe is not None
    tok = lambda b, i: (b, i, 0)
    x_spec = pl.BlockSpec((1, tm, d), tok)

    def mod_spec(l):
        return pl.BlockSpec((None, None, None, 6, d), lambda b, i: (l, b, jnp.minimum(i, 1), 0, 0))

    tab_spec = pl.BlockSpec((tm, LANES), lambda b, i: (i, 0))
    in_specs = [x_spec]
    args = [x]
    if combine:
        in_specs += _moe_specs(tm, d, lambda i: i) + [mod_spec(layer - 1)]
        args += [ymoe, ymoe, route, modsel]
    in_specs += [mod_spec(layer), pl.BlockSpec((1, d), lambda b, i: (0, 0)),
                 pl.BlockSpec((d, IN_PAD_WIDTH), lambda b, i: (0, 0)), tab_spec, tab_spec, tab_spec]
    args += [modsel, g1.reshape(1, d), w_in_p, *rope_tabs]

    def o(width, dtype):
        return pl.BlockSpec((1, tm, width), tok), jax.ShapeDtypeStruct((bsz, t, width), dtype)

    va_t = (pl.BlockSpec((1, A_WIDTH, tm), lambda b, i: (b, 0, i)),
            jax.ShapeDtypeStruct((bsz, A_WIDTH, t), MXU_DTYPE))
    outs = [o(A_WIDTH, MXU_DTYPE), o(A_WIDTH, MXU_DTYPE), va_t, o(BG_WIDTH, F32),
            o(C_WIDTH, MXU_DTYPE), o(C_WIDTH, MXU_DTYPE), o(C_WIDTH, MXU_DTYPE)]
    if combine:
        outs.append(o(d, F32))
    return pl.pallas_call(
        functools.partial(_in_proj_body, combine),
        grid=(bsz, t // tm),
        in_specs=in_specs,
        out_specs=[s for s, _ in outs],
        out_shape=[s for _, s in outs],
        compiler_params=_cparams(("parallel", "arbitrary")),
        name="in_proj",
    )(*args)


def _diff_attn_body(lam_init, ctx_len, q_ref, k_ref, vt_ref, lam_ref, g_ref, o_ref):
    q = q_ref[0]
    lane = lax.broadcasted_iota(jnp.int32, (1, LANES), 1)
    zero = jnp.zeros_like(q)
    q1 = jnp.where(lane < A_QK, q, zero)
    q2 = jnp.where(lane >= A_QK, q, zero)
    lm = lam_ref[...]
    lam = (jnp.exp(jnp.sum(lm[0:1] * lm[1:2], axis=1, keepdims=True))
           - jnp.exp(jnp.sum(lm[2:3] * lm[3:4], axis=1, keepdims=True)) + lam_init)

    nt = (((1,), (1,)), ((), ()))
    tq = q.shape[0]

    def attend(n_keys):
        def init():
            return (jnp.full((1, tq), -jnp.inf, F32), jnp.zeros((1, tq), F32), jnp.zeros((A_V, tq), F32))

        bounds = [0, min(ctx_len, n_keys)] + list(range(ctx_len + A_KEY_CHUNK, n_keys + 1, A_KEY_CHUNK))
        chunks = list(zip(bounds[:-1], bounds[1:]))

        def scores(qm, c):
            k_c = k_ref[0, chunks[c][0]:chunks[c][1], :]
            return lax.dot_general(k_c, qm, nt, preferred_element_type=F32)

        def probs(state, s):
            m_old, l_old, acc = state
            m_new = jnp.maximum(m_old, jnp.max(s, axis=0, keepdims=True))
            alpha = jnp.exp2(m_old - m_new)
            p = jnp.exp2(s - m_new)
            l_new = alpha * l_old + jnp.sum(p, axis=0, keepdims=True)
            return p.astype(MXU_DTYPE), alpha, (m_new, l_new, acc)

        def accumulate(state, alpha, p, c):
            m, l, acc = state
            vt_c = vt_ref[0, :, chunks[c][0]:chunks[c][1]]
            return m, l, alpha * acc + jnp.dot(vt_c, p, preferred_element_type=F32)

        st1, st2 = init(), init()
        s1, s2 = scores(q1, 0), scores(q2, 0)
        for c in range(len(chunks)):
            p1, a1, st1 = probs(st1, s1)
            if c + 1 < len(chunks):
                s1 = scores(q1, c + 1)
            st1 = accumulate(st1, a1, p1, c)
            p2, a2, st2 = probs(st2, s2)
            if c + 1 < len(chunks):
                s2 = scores(q2, c + 1)
            st2 = accumulate(st2, a2, p2, c)
        o_t = st1[2] * (1.0 / st1[1]) - st2[2] * (lam / st2[1])
        o_ref[0] = (_rms(o_t.T, g_ref[...]) * (1.0 - lam_init)).astype(o_ref.dtype)

    @pl.when(pl.program_id(2) == 0)
    def _():
        attend(ctx_len)

    @pl.when(pl.program_id(2) > 0)
    def _():
        attend(k_ref.shape[1])


def _diff_attention(layer, qa, ka, va, lam, g_sub):
    bsz, t, _ = qa.shape
    tq = TOK_TILE
    lam_init = 0.8 - 0.6 * math.exp(-0.3 * layer)
    kv_spec = pl.BlockSpec((1, t, LANES), lambda b, h, i: (b, 0, h))
    return pl.pallas_call(
        functools.partial(_diff_attn_body, lam_init, TOK_TILE),
        grid=(bsz, A_HEADS, t // tq),
        in_specs=[pl.BlockSpec((1, tq, LANES), lambda b, h, i: (b, i, h)), kv_spec,
                  pl.BlockSpec((1, A_V, t), lambda b, h, i: (b, h, 0)),
                  pl.BlockSpec((4, A_QK), lambda b, h, i: (0, 0)),
                  pl.BlockSpec((1, A_V), lambda b, h, i: (0, 0))],
        out_specs=pl.BlockSpec((1, tq, LANES), lambda b, h, i: (b, i, h)),
        out_shape=jax.ShapeDtypeStruct((bsz, t, A_WIDTH), MXU_DTYPE),
        compiler_params=_cparams(("parallel", "parallel", "arbitrary")),
        name="diff_attention",
    )(qa, ka, va, lam, g_sub.reshape(1, A_V))


def _gla_body(f_ref, r_ref, wdec_ref, bdec_ref, of_ref, ob_ref, sf_ref, sb_ref, e_ref, b_ref, qs_ref):
    c = B_CHUNK
    n_chunks = TOK_TILE // c

    @pl.when(pl.program_id(1) == 0)
    def _():
        sf_ref[...] = jnp.zeros_like(sf_ref)
        sb_ref[...] = jnp.zeros_like(sb_ref)

    sub = GLA_SUB
    n_sub = c // sub
    nt = (((1,), (1,)), ((), ()))
    t_row = lax.broadcasted_iota(jnp.int32, (TOK_TILE, TOK_TILE), 0)
    t_col = lax.broadcasted_iota(jnp.int32, (TOK_TILE, TOK_TILE), 1)
    same_chunk = (t_row // c) == (t_col // c)
    tri_f = (same_chunk & (t_col <= t_row)).astype(jnp.bfloat16)
    tri_b = (same_chunk & (t_col >= t_row)).astype(jnp.bfloat16)
    s_iota = lax.broadcasted_iota(jnp.int32, (sub, LANES), 0)
    idx = lax.broadcasted_iota(jnp.int32, (c, LANES), 0)
    head_of_k = lax.broadcasted_iota(jnp.int32, (B_QK, B_WIDTH), 0) // B_DK
    head_of_v = lax.broadcasted_iota(jnp.int32, (B_QK, B_WIDTH), 1) // B_DV
    expand = (head_of_k == head_of_v).astype(MXU_DTYPE)
    same_head_t = (lax.broadcasted_iota(jnp.int32, (B_WIDTH, B_QK), 0) // B_DV
                   == lax.broadcasted_iota(jnp.int32, (B_WIDTH, B_QK), 1) // B_DK)
    n_ref = n_sub - 1
    kt_keep = (lax.broadcasted_iota(jnp.int32, (B_HEADS * c, n_ref * B_QK), 0) // c
               == (lax.broadcasted_iota(jnp.int32, (B_HEADS * c, n_ref * B_QK), 1) % B_QK) // B_DK)
    vx_keep = (lax.broadcasted_iota(jnp.int32, (B_HEADS * c, B_WIDTH), 0) // c
               == lax.broadcasted_iota(jnp.int32, (B_HEADS * c, B_WIDTH), 1) // B_DV)
    pick = (lax.broadcasted_iota(jnp.int32, (c, c * sub), 1) // sub
            == lax.broadcasted_iota(jnp.int32, (c, c * sub), 0)).astype(MXU_DTYPE)

    def log_decay(src_ref, backward):
        gl = src_ref[0, :, 2 * B_QK + 2 * B_WIDTH:BG_WIDTH]
        d0 = B_QK if backward else 0
        z = jnp.dot(gl, wdec_ref[:, d0:d0 + B_QK], precision=HI, preferred_element_type=F32) \
            + bdec_ref[:, d0:d0 + B_QK]
        log_a = (jnp.minimum(z, 0.0) - jnp.log(1.0 + jnp.exp(-jnp.abs(z)))) / B_GATE_TAU
        tri = tri_b if backward else tri_f
        return sum(jnp.dot(tri, p, preferred_element_type=F32) for p in _split_bf16(log_a, 3))

    def chunk(src_ref, b_all, lo, backward, st_ref, out_ref):
        q = src_ref[0, lo:lo + c, 0:B_QK] * (B_DK ** -0.5)
        k = src_ref[0, lo:lo + c, B_QK:2 * B_QK]
        v = src_ref[0, lo:lo + c, 2 * B_QK:2 * B_QK + B_WIDTH]
        b = b_all[lo:lo + c]
        b_ref[...] = b
        qs_ref[...] = q
        blk = ((c - 1 - idx) if backward else idx) // sub

        q_parts, k_parts = [], []
        for m, late, early in ((1, blk == 1, blk == 0), (2, blk >= 2, blk <= 1), (3, blk == 3, blk == 2)):
            r_row = (c - 1 - sub * m) if backward else sub * m
            r = b_ref[r_row:r_row + 1, :]
            q_parts.append(q * jnp.exp(jnp.where(late, b - r, -jnp.inf)))
            k_parts.append(k * jnp.exp(jnp.where(early, r - b, -jnp.inf)))
        q_cat = jnp.concatenate(q_parts, axis=1).astype(MXU_DTYPE)
        k_cat = jnp.concatenate(k_parts, axis=1)
        k_exp = jnp.where(kt_keep, jnp.concatenate([k_cat] * B_HEADS, axis=0), 0.0).astype(MXU_DTYPE)
        a_off = lax.dot_general(q_cat, k_exp, nt, preferred_element_type=F32)
        v_exp = jnp.where(vx_keep, jnp.concatenate([v] * B_HEADS, axis=0), 0.0).astype(MXU_DTYPE)
        o_off = jnp.dot(a_off.astype(MXU_DTYPE), v_exp, preferred_element_type=F32)

        for tt in range(c):
            lo_s = tt // sub * sub
            keep = (s_iota >= tt - lo_s) if backward else (s_iota <= tt - lo_s)
            bt = b_ref[tt:tt + 1, :]
            qt = qs_ref[tt:tt + 1, :]
            e = jnp.exp(jnp.where(keep, bt - b[lo_s:lo_s + sub], -jnp.inf)) * (qt * k[lo_s:lo_s + sub])
            e_ref[tt * sub:(tt + 1) * sub, :] = e.astype(e_ref.dtype)
        a_exp = jnp.dot(e_ref[...], expand, preferred_element_type=F32)
        prod = a_exp.reshape(n_sub, sub, sub, B_WIDTH) * v.reshape(n_sub, 1, sub, B_WIDTH)
        o_diag = jnp.dot(pick, prod.reshape(c * sub, B_WIDTH).astype(MXU_DTYPE), preferred_element_type=F32)

        st = st_ref[...]
        o_inter = lax.dot_general(q * jnp.exp(b), st, nt, preferred_element_type=F32)
        out_ref[0, lo:lo + c, :] = o_off + o_diag + o_inter
        b_end = b[0:1, :] if backward else b[c - 1:c, :]
        kd = k * jnp.exp(b_end - b)
        upd = lax.dot_general(v, kd, (((0,), (0,)), ((), ())), preferred_element_type=F32)
        st_ref[...] = jnp.exp(b_end) * st + jnp.where(same_head_t, upd, 0.0)

    b_fwd = log_decay(f_ref, False)
    b_bwd = log_decay(r_ref, True)
    for ci in range(n_chunks):
        chunk(f_ref, b_fwd, ci * c, False, sf_ref, of_ref)
        chunk(r_ref, b_bwd, (n_chunks - 1 - ci) * c, True, sb_ref, ob_ref)


def _gla_scan(bg, w_dec, b_dec):
    bsz, t, _ = bg.shape
    n = t // TOK_TILE
    rev = lambda b, i: (b, jnp.where(i == 0, 0, n - i), 0)
    fwd = lambda b, i: (b, i, 0)
    wdec = jnp.zeros((GATE_PAD, 2 * B_QK), F32)
    wdec = wdec.at[:B_GATE_RANK, :B_QK].set(w_dec[0]).at[B_GATE_RANK:2 * B_GATE_RANK, B_QK:].set(w_dec[1])
    bdec = b_dec.reshape(1, 2 * B_QK)
    o_shape = jax.ShapeDtypeStruct((bsz, t, B_WIDTH), F32)
    return pl.pallas_call(
        _gla_body,
        grid=(bsz, n),
        in_specs=[pl.BlockSpec((1, TOK_TILE, BG_WIDTH), fwd),
                  pl.BlockSpec((1, TOK_TILE, BG_WIDTH), rev),
                  pl.BlockSpec((GATE_PAD, 2 * B_QK), lambda b, i: (0, 0)),
                  pl.BlockSpec((1, 2 * B_QK), lambda b, i: (0, 0))],
        out_specs=[pl.BlockSpec((1, TOK_TILE, B_WIDTH), fwd),
                   pl.BlockSpec((1, TOK_TILE, B_WIDTH), rev)],
        out_shape=[o_shape, o_shape],
        scratch_shapes=[pltpu.VMEM((B_WIDTH, B_QK), F32), pltpu.VMEM((B_WIDTH, B_QK), F32),
                        pltpu.VMEM((B_CHUNK * GLA_SUB, LANES), MXU_DTYPE),
                        pltpu.VMEM((B_CHUNK, LANES), F32), pltpu.VMEM((B_CHUNK, LANES), F32)],
        compiler_params=_cparams(("parallel", "arbitrary")),
        name="gla_scan",
    )(bg, bg, wdec, bdec)


def _na_bias_tables(rpb, rows):
    n_tiles = rows // NA_TILE_ROWS
    wr = min(NA_ROWS, rows)
    n_dr, n_dc = 2 * NA_ROWS - 1, 2 * NA_COLS - 1
    cq = np.arange(GRID_W)[:, None]
    ck = np.arange(GRID_W)[None, :]
    cs = np.clip(cq - NA_COLS // 2, 0, GRID_W - NA_COLS)
    col_ok = (ck >= cs) & (ck < cs + NA_COLS)
    dc = np.clip(ck - cq, -(NA_COLS - 1), NA_COLS - 1) + (NA_COLS - 1)
    onehot = (dc.reshape(1, -1) == np.arange(n_dc)[:, None]).astype(np.float32)
    by_col = jnp.dot(rpb.astype(F32).reshape(-1, n_dc), onehot, precision=HI)
    by_col = jnp.where(col_ok.reshape(1, 1, GRID_W, GRID_W),
                       by_col.reshape(C_HEADS, n_dr, GRID_W, GRID_W), -jnp.inf)
    masked = jnp.full((C_HEADS, GRID_W, GRID_W), -jnp.inf, F32)
    tabs = []
    for j in (0, 1, n_tiles - 1):
        kr0 = int(np.clip(j - 1, 0, n_tiles - NA_KEY_TILES)) * NA_TILE_ROWS
        q_rows = []
        for qr in range(NA_TILE_ROWS):
            r = j * NA_TILE_ROWS + qr
            start = int(np.clip(r - wr // 2, 0, rows - wr))
            blocks = []
            for kw in range(NA_KEY_TILES * NA_TILE_ROWS):
                kr = kr0 + kw
                blocks.append(by_col[:, kr - r + NA_ROWS - 1] if start <= kr < start + wr else masked)
            q_rows.append(jnp.concatenate(blocks, axis=-1))
        tabs.append(jnp.concatenate(q_rows, axis=1))
    return jnp.stack(tabs)


def _na_body(q_ref, k0_ref, k1_ref, k2_ref, kc_ref, v0_ref, v1_ref, v2_ref, vc_ref, m_ref, o_ref):
    q = q_ref[0]
    lane = lax.broadcasted_iota(jnp.int32, (1, LANES), 1)
    zero = jnp.zeros_like(q)
    nt = (((1,), (1,)), ((), ()))

    def scores(qm, k_ref):
        return lax.dot_general(qm, k_ref[0], nt, preferred_element_type=F32)

    def head_out(hh, windows):
        qm = jnp.where((lane >= hh * C_DH) & (lane < (hh + 1) * C_DH), q, zero)
        s = [scores(qm, kc_ref)]
        for w, k_ref in enumerate(windows):
            s.append(scores(qm, k_ref) + m_ref[0, hh, :, w * TOK_TILE:(w + 1) * TOK_TILE])
        m = functools.reduce(jnp.maximum, [jnp.max(x, axis=-1, keepdims=True) for x in s])
        p = [jnp.exp(x - m) for x in s]
        den = functools.reduce(jnp.add, [jnp.sum(x, axis=-1, keepdims=True) for x in p])
        vals = [vc_ref] + [v0_ref, v1_ref, v2_ref][:len(windows)]
        o = functools.reduce(jnp.add, [jnp.dot(x.astype(MXU_DTYPE), v_ref[0], preferred_element_type=F32)
                                       for x, v_ref in zip(p, vals)])
        return o * (1.0 / den)

    def emit(windows):
        o0 = head_out(0, windows)
        o1 = head_out(1, windows)
        o_ref[0] = jnp.where(lane < C_DH, o0, o1).astype(o_ref.dtype)

    @pl.when(pl.program_id(1) == 0)
    def _():
        emit([])

    @pl.when(pl.program_id(1) > 0)
    def _():
        emit([k0_ref, k1_ref, k2_ref])


def _neighborhood_attention(qn, kn, vn, bias_tabs):
    bsz, t, _ = qn.shape
    n = t // TOK_TILE
    n_lat = n - 1

    def win(w):
        def index(hp, i, b):
            kb0 = jnp.clip(i - 2, 0, n_lat - NA_KEY_TILES)
            return (b, kb0 + 1 + w, hp)
        return pl.BlockSpec((1, TOK_TILE, LANES), index)

    def cls(hp, i, b):
        j = i - 1
        return (jnp.where(j <= 0, 0, jnp.where(j == n_lat - 1, 2, 1)), hp, 0, 0)

    own = pl.BlockSpec((1, TOK_TILE, LANES), lambda hp, i, b: (b, i, hp))
    ctx = pl.BlockSpec((1, TOK_TILE, LANES), lambda hp, i, b: (b, 0, hp))
    heads_per_step = LANES // C_DH
    return pl.pallas_call(
        _na_body,
        grid=(C_HEADS // heads_per_step, n, bsz),
        in_specs=[own, win(0), win(1), win(2), ctx, win(0), win(1), win(2), ctx,
                  pl.BlockSpec((1, heads_per_step, TOK_TILE, NA_KEY_TILES * TOK_TILE), cls)],
        out_specs=own,
        out_shape=jax.ShapeDtypeStruct((bsz, t, C_WIDTH), MXU_DTYPE),
        compiler_params=_cparams(("parallel", "parallel", "arbitrary")),
        name="neighborhood_attention",
    )(qn, kn, kn, kn, kn, vn, vn, vn, vn, bias_tabs)


def _out_proj_body(ya_ref, of_ref, ob_ref, r_ref, yn_ref, x_ref, mod_ref, gg_ref, g2_ref, w_ref,
                   wr_ref, br_ref, xo_ref, h_ref, rt_ref):
    o = of_ref[0] + ob_ref[0]
    hi = lax.broadcasted_iota(jnp.int32, (B_WIDTH, B_WIDTH), 0) // B_DV
    hj = lax.broadcasted_iota(jnp.int32, (B_WIDTH, B_WIDTH), 1) // B_DV
    head_mean = jnp.where(hi == hj, 1.0 / B_DV, 0.0).astype(jnp.bfloat16)
    ms = sum(jnp.dot(p, head_mean, preferred_element_type=F32) for p in _split_bf16(o * o, 2))
    yb = o * lax.rsqrt(ms + NORM_EPS) * gg_ref[...] * _silu(r_ref[0])
    mix = (jnp.dot(ya_ref[0], w_ref[0:A_WIDTH, :], preferred_element_type=F32)
           + jnp.dot(yb.astype(MXU_DTYPE), w_ref[A_WIDTH:A_WIDTH + B_WIDTH, :], preferred_element_type=F32)
           + jnp.dot(yn_ref[0], w_ref[A_WIDTH + B_WIDTH:, :], preferred_element_type=F32))
    x = x_ref[0] + mod_ref[2:3, :] * mix
    xo_ref[0] = x
    h = _rms(x, g2_ref[...]) * (1.0 + mod_ref[4:5, :]) + mod_ref[3:4, :]
    h_ref[0] = h.astype(h_ref.dtype)
    h_hi, h_lo = _split_bf16(h, 2)
    both = jnp.dot(h_hi, wr_ref[...], preferred_element_type=F32)
    logits = (both[:, :ROUTER_PAD] + both[:, ROUTER_PAD:]
              + jnp.dot(h_lo, wr_ref[:, :ROUTER_PAD], preferred_element_type=F32) + br_ref[...])
    rt_ref[0] = _route(logits)


def _route(lg):
    lane = lax.broadcasted_iota(jnp.int32, lg.shape, 1)
    big = jnp.int32(ROUTER_PAD)

    def top(mask):
        v = jnp.max(jnp.where(mask, lg, -jnp.inf), axis=-1, keepdims=True)
        i = jnp.min(jnp.where(mask & (lg == v), lane, big), axis=-1, keepdims=True)
        return v, i

    g_mask = lane < N_GROUPS
    g_max, grp = top(g_mask)
    p_grp = 1.0 / jnp.sum(jnp.where(g_mask, jnp.exp(lg - g_max), 0.0), axis=-1, keepdims=True)
    e_lo = N_GROUPS + grp * EXPERTS_PER_GROUP
    e_mask = (lane >= e_lo) & (lane < e_lo + EXPERTS_PER_GROUP)
    v1, i1 = top(e_mask)
    v2, i2 = top(e_mask & (lane != i1))
    r = jnp.exp(v2 - v1)
    gate1 = p_grp / (1.0 + r)
    gate2 = p_grp * r / (1.0 + r)
    out = jnp.where(lane == ROUTE_EXPERT, (i1 - N_GROUPS).astype(F32),
                    jnp.where(lane == ROUTE_EXPERT + 1, (i2 - N_GROUPS).astype(F32),
                              jnp.where(lane == ROUTE_GATE, gate1,
                                        jnp.where(lane == ROUTE_GATE + 1, gate2, 0.0))))
    return out


def _out_proj(layer, ya, o_f, o_b, bg, yn, x, modsel, g_gla, g2, w_out_b, w_router, b_router):
    bsz, t, d = x.shape
    tm = TOK_TILE
    tok = lambda b, i: (b, i, 0)
    const = lambda b, i: (0, 0)
    r_block = (2 * B_QK + B_WIDTH) // B_WIDTH
    return pl.pallas_call(
        _out_proj_body,
        grid=(bsz, t // tm),
        in_specs=[pl.BlockSpec((1, tm, A_WIDTH), tok),
                  pl.BlockSpec((1, tm, B_WIDTH), tok), pl.BlockSpec((1, tm, B_WIDTH), tok),
                  pl.BlockSpec((1, tm, B_WIDTH), lambda b, i: (b, i, r_block)),
                  pl.BlockSpec((1, tm, C_WIDTH), tok),
                  pl.BlockSpec((1, tm, d), tok),
                  pl.BlockSpec((None, None, None, 6, d), lambda b, i: (layer, b, jnp.minimum(i, 1), 0, 0)),
                  pl.BlockSpec((1, B_WIDTH), const), pl.BlockSpec((1, d), const),
                  pl.BlockSpec(w_out_b.shape, const),
                  pl.BlockSpec((d, 2 * ROUTER_PAD), const), pl.BlockSpec((1, ROUTER_PAD), const)],
        out_specs=[pl.BlockSpec((1, tm, d), tok), pl.BlockSpec((1, tm, d), tok),
                   pl.BlockSpec((1, tm, ROUTER_PAD), tok)],
        out_shape=[jax.ShapeDtypeStruct((bsz, t, d), F32), jax.ShapeDtypeStruct((bsz, t, d), F32),
                   jax.ShapeDtypeStruct((bsz, t, ROUTER_PAD), F32)],
        compiler_params=_cparams(("parallel", "arbitrary")),
        name="out_proj",
    )(ya, o_f, o_b, bg, yn, x, modsel, jnp.tile(g_gla, B_HEADS).reshape(1, B_WIDTH), g2.reshape(1, d),
      w_out_b, w_router, b_router)


def _dispatch_plan(expert, n_blocks):
    n_tok = expert.shape[0]
    flat = expert.reshape(-1)
    n_assign = flat.shape[0]
    order = jnp.argsort(flat).astype(jnp.int32)
    rank = jnp.argsort(order).astype(jnp.int32)
    counts = jnp.sum((flat[:, None] == jnp.arange(N_EXPERTS, dtype=jnp.int32)[None, :]).astype(jnp.int32), axis=0)
    padded = (counts + EXPERT_TILE - 1) // EXPERT_TILE * EXPERT_TILE
    start = jnp.cumsum(counts) - counts
    pad_end = jnp.cumsum(padded)
    pad_start = pad_end - padded
    shift = (pad_start - start).astype(jnp.int32)
    dest = rank + shift[flat]
    blk_start = jnp.arange(n_blocks, dtype=jnp.int32) * EXPERT_TILE
    blk_expert = jnp.sum((pad_end[None, :] <= blk_start[:, None]).astype(jnp.int32), axis=1)
    blk_expert = jnp.minimum(blk_expert, N_EXPERTS - 1)
    blk_used = (blk_start < pad_end[-1]).astype(jnp.int32)
    row = jnp.arange(n_blocks * EXPERT_TILE, dtype=jnp.int32)
    e_row = jnp.repeat(blk_expert, EXPERT_TILE)
    real = (row - pad_start[e_row].astype(jnp.int32)) < counts[e_row]
    src_assign = order[jnp.clip(row - shift[e_row], 0, n_assign - 1)]
    src_tok = jnp.where(real, src_assign // TOP_K, 0)
    dest = dest.reshape(n_tok, TOP_K).T.reshape(-1)
    return src_tok, dest, blk_expert, blk_used


def _sc_gather(table, idx):
    n_rows = idx.shape[0]
    width = table.shape[1]
    n_workers = SC_CORES * SC_SUBCORES
    per_worker = n_rows // n_workers
    chunk = SC_ROW_BUFFER_BYTES // (width * table.dtype.itemsize)
    n_chunks = per_worker // chunk
    assert per_worker * n_workers == n_rows and n_chunks * chunk == per_worker and n_chunks % 2 == 0
    assert n_chunks >= 4 and chunk <= LANES
    mesh = plsc.VectorSubcoreMesh(core_axis_name="core", subcore_axis_name="subcore")

    def body(table_hbm, idx_hbm, out_hbm, idx_v, rows_v, gather_sem, write_sem):
        worker = lax.axis_index("subcore") * SC_CORES + lax.axis_index("core")
        base = worker * per_worker
        pltpu.sync_copy(idx_hbm.at[worker], idx_v)

        def gather(j, slot):
            return pltpu.make_async_copy(table_hbm.at[idx_v.at[j]], rows_v.at[slot], gather_sem.at[slot])

        def write(j, slot):
            return pltpu.make_async_copy(rows_v.at[slot], out_hbm.at[pl.ds(base + j * chunk, chunk)],
                                         write_sem.at[slot])

        gather(0, 0).start()
        gather(0, 0).wait()
        gather(1, 1).start()
        write(0, 0).start()

        @pl.loop(1, n_chunks - 1, step=2)
        def _(j):
            for s in range(2):
                slot = (1 + s) % 2
                gather(j + s, slot).wait()
                write(j + s - 1, 1 - slot).wait()
                gather(j + s + 1, 1 - slot).start()
                write(j + s, slot).start()

        last = n_chunks - 1
        gather(last, 1).wait()
        write(last, 1).start()
        write(last - 1, 0).wait()
        write(last, 1).wait()

    return pl.kernel(
        body,
        out_type=jax.ShapeDtypeStruct((n_rows, width), table.dtype),
        mesh=mesh,
        scratch_types=[pltpu.VMEM((n_chunks, chunk), jnp.int32),
                       pltpu.VMEM((2, chunk, width), table.dtype),
                       pltpu.SemaphoreType.DMA((2,)), pltpu.SemaphoreType.DMA((2,))],
        name="sc_row_gather",
    )(table, idx.reshape(n_workers, n_chunks, chunk))


def _expert_body(be_ref, used_ref, x_ref, wu_ref, wd_ref, o_ref, wub_ref, wdb_ref):
    i = pl.program_id(0)
    prev = be_ref[jnp.maximum(i - 1, 0)]

    @pl.when((i == 0) | (be_ref[i] != prev))
    def _():
        wub_ref[...] = wu_ref[...].astype(wub_ref.dtype)
        wdb_ref[...] = wd_ref[...].astype(wdb_ref.dtype)

    @pl.when(used_ref[i] > 0)
    def _():
        gu = jnp.dot(x_ref[...].astype(MXU_DTYPE), wub_ref[...], preferred_element_type=F32)
        act = _silu(gu[:, :EXPERT_HIDDEN]) * gu[:, EXPERT_HIDDEN:]
        o_ref[...] = jnp.dot(act.astype(MXU_DTYPE), wdb_ref[...], preferred_element_type=F32)

    @pl.when(used_ref[i] == 0)
    def _():
        o_ref[...] = jnp.zeros_like(o_ref)


def _expert_ffn(layer, buf, blk_expert, blk_used, w_up, w_down):
    n_rows, d = buf.shape
    n_blocks = n_rows // EXPERT_TILE
    h2 = w_up.shape[-1]
    grid_spec = pltpu.PrefetchScalarGridSpec(
        num_scalar_prefetch=2,
        grid=(n_blocks,),
        in_specs=[pl.BlockSpec((EXPERT_TILE, d), lambda i, be, us: (i, 0)),
                  pl.BlockSpec((None, None, d, h2), lambda i, be, us: (layer, be[i], 0, 0)),
                  pl.BlockSpec((None, None, h2 // 2, d), lambda i, be, us: (layer, be[i], 0, 0))],
        out_specs=pl.BlockSpec((EXPERT_TILE, d), lambda i, be, us: (i, 0)),
        scratch_shapes=[pltpu.VMEM((d, h2), MXU_DTYPE), pltpu.VMEM((h2 // 2, d), MXU_DTYPE)],
    )
    return pl.pallas_call(
        _expert_body,
        grid_spec=grid_spec,
        out_shape=jax.ShapeDtypeStruct((n_rows, d), F32),
        compiler_params=_cparams(("arbitrary",)),
        name="expert_ffn",
    )(blk_expert, blk_used, buf, w_up, w_down)


def _moe(layer, h2, route, w_up, w_down):
    bsz, t, d = h2.shape
    n_tok = bsz * t
    n_assign = n_tok * TOP_K
    n_blocks = -(-(n_assign + N_EXPERTS * (EXPERT_TILE - 1)) // EXPERT_TILE)
    expert = route.reshape(n_tok, ROUTER_PAD)[:, ROUTE_EXPERT:ROUTE_EXPERT + TOP_K].astype(jnp.int32)
    src_tok, dest, blk_expert, blk_used = _dispatch_plan(expert, n_blocks)
    buf = _sc_gather(h2.reshape(n_tok, d), src_tok)
    y = _expert_ffn(layer, buf, blk_expert, blk_used, w_up, w_down)
    return _sc_gather(y, dest).reshape(TOP_K, bsz, t, d)


def _final_body(x_ref, y0_ref, y1_ref, rt_ref, mod_ref, g_ref, o_ref):
    o_ref[0] = _rms(_moe_residual(x_ref, y0_ref, y1_ref, rt_ref, mod_ref), g_ref[...])


def _final_norm(layer, x, ymoe, route, modsel, g, ctx_tiles):
    bsz, t, d = x.shape
    tm = TOK_TILE
    lat = lambda b, i: (b, i + ctx_tiles, 0)
    return pl.pallas_call(
        _final_body,
        grid=(bsz, t // tm - ctx_tiles),
        in_specs=[pl.BlockSpec((1, tm, d), lat)] + _moe_specs(tm, d, lambda i: i + ctx_tiles)
        + [pl.BlockSpec((None, None, None, 6, d), lambda b, i: (layer, b, 1, 0, 0)),
           pl.BlockSpec((1, d), lambda b, i: (0, 0))],
        out_specs=pl.BlockSpec((1, tm, d), lambda b, i: (b, i, 0)),
        out_shape=jax.ShapeDtypeStruct((bsz, t - ctx_tiles * tm, d), F32),
        compiler_params=_cparams(("parallel", "arbitrary")),
        name="final_norm",
    )(x, ymoe, ymoe, route, modsel, g.reshape(1, d))


def _rope_tables(n_ctx, n_lat):
    t = jnp.arange(n_lat)
    row = (t // GRID_W).astype(F32)
    col = (t % GRID_W).astype(F32)
    n_freq = HEAD_DIM // 4
    inv = ROPE_THETA ** (-jnp.arange(n_freq, dtype=F32) / n_freq)
    ang_r = row[:, None] * inv
    ang_c = col[:, None] * inv
    cr, sr, cc, sc = jnp.cos(ang_r), jnp.sin(ang_r), jnp.cos(ang_c), jnp.sin(ang_c)
    z = jnp.zeros_like(sr)
    cos = jnp.concatenate([cr, cr, cc, cc], axis=-1)
    above = jnp.concatenate([-sr, z, -sc, z], axis=-1)
    below = jnp.concatenate([z, sr, z, sc], axis=-1)
    reps = LANES // HEAD_DIM

    def full(tab, ctx_value):
        tab = jnp.tile(tab, (1, reps))
        return jnp.concatenate([jnp.full((n_ctx, LANES), ctx_value, F32), tab], axis=0)

    return full(cos, 1.0), full(above, 0.0), full(below, 0.0)


def _pack_w_in(w_in):
    parts = jnp.split(w_in, np.cumsum(IN_SIZES)[:-1].tolist(), axis=-1)
    qa, ka, va, qb, kb, vb, rb, gb, qn, kn, vn = parts
    gb = jnp.pad(gb, ((0, 0), (0, GATE_PAD - gb.shape[-1])))
    return jnp.concatenate([qa, ka, va, qb, kb, vb, rb, gb, qn, kn, vn], axis=-1).astype(MXU_DTYPE)


def kernel(x, c, ctx, c_ctx, w_mod, b_mod, norm1_g, norm2_g, w_in, w_out, diff_lambda, diff_sub_g,
           gla_w_decay, gla_b_decay, gla_norm_g, na_rel_bias, w_router_group, b_router_group,
           w_router_expert, b_router_expert, w_expert_up, w_expert_down, final_g):
    bsz, seq, d = x.shape
    n_ctx = ctx.shape[1]
    depth = w_mod.shape[0]
    assert n_ctx == TOK_TILE and seq % (NA_TILE_ROWS * GRID_W) == 0 and d % LANES == 0
    assert seq // (NA_TILE_ROWS * GRID_W) >= NA_KEY_TILES
    mod_rows = -(-(bsz + 1) // 8) * 8
    cvec = jnp.zeros((mod_rows, d), F32).at[:bsz].set(c).at[bsz].set(c_ctx)
    mod = _modulation(cvec, w_mod, b_mod).reshape(depth, mod_rows, 6, d)
    modsel = jnp.stack([jnp.broadcast_to(mod[:, bsz][:, None], (depth, bsz, 6, d)), mod[:, :bsz]], axis=2)
    rope_tabs = _rope_tables(n_ctx, seq)
    xt = jnp.concatenate([ctx, x], axis=1)
    n_groups = BATCH_GROUPS if bsz % BATCH_GROUPS == 0 else 1
    gb = bsz // n_groups
    groups = [dict(xt=xt[g * gb:(g + 1) * gb], mod=modsel[:, g * gb:(g + 1) * gb], ymoe=None, route=None)
              for g in range(n_groups)]
    for l in range(depth):
        w_in_p = _pack_w_in(w_in[l])
        bias_tabs = _na_bias_tables(na_rel_bias[l], seq // GRID_W)
        w_router = jnp.pad(jnp.concatenate([w_router_group[l], w_router_expert[l]], axis=-1),
                           ((0, 0), (0, ROUTER_PAD - N_GROUPS - N_EXPERTS)))
        w_router = jnp.concatenate(_split_bf16(w_router, 2), axis=-1)
        b_router = jnp.pad(jnp.concatenate([b_router_group[l], b_router_expert[l]]),
                           (0, ROUTER_PAD - N_GROUPS - N_EXPERTS)).reshape(1, ROUTER_PAD)
        w_out_b = w_out[l].astype(MXU_DTYPE)
        for st in groups:
            outs = _in_proj(l, st["xt"], st["ymoe"], st["route"], st["mod"], norm1_g[l], w_in_p, rope_tabs)
            qa, ka, va, bg, qn, kn, vn = outs[:7]
            if st["ymoe"] is not None:
                st["xt"] = outs[7]
            ya = _diff_attention(l, qa, ka, va, diff_lambda[l], diff_sub_g[l])
            o_f, o_b = _gla_scan(bg, gla_w_decay[l], gla_b_decay[l])
            yn = _neighborhood_attention(qn, kn, vn, bias_tabs)
            st["xt"], h2, st["route"] = _out_proj(l, ya, o_f, o_b, bg, yn, st["xt"], st["mod"], gla_norm_g[l],
                                                  norm2_g[l], w_out_b, w_router, b_router)
            st["ymoe"] = _moe(l, h2, st["route"], w_expert_up, w_expert_down)
    outs = [_final_norm(depth - 1, st["xt"], st["ymoe"], st["route"], st["mod"], final_g, n_ctx // TOK_TILE)
            for st in groups]
    return jnp.concatenate(outs, axis=0)
```

```python
import functools
import math

import numpy as np
import jax
import jax.numpy as jnp
from jax import lax
from jax.experimental import pallas as pl
from jax.experimental.pallas import tpu as pltpu
from jax.experimental.pallas import tpu_sc as plsc

F32 = jnp.float32
MXU_DTYPE = jnp.bfloat16
HI = lax.Precision.HIGHEST

GRID_W = 64
HEAD_DIM = 64
ROPE_THETA = 10000.0
NORM_EPS = 1e-6

A_HEADS = 4
A_QK = HEAD_DIM
A_V = 2 * HEAD_DIM
B_HEADS = 4
B_DK = HEAD_DIM // 2
B_DV = HEAD_DIM
B_GATE_RANK = 16
B_GATE_TAU = 16.0
B_CHUNK = 64
A_KEY_CHUNK = 1024
LOG2_E = math.log2(math.e)
GLA_SUB = 16
C_HEADS = 4
C_DH = HEAD_DIM
NA_ROWS = 8
NA_COLS = 16

A_WIDTH = A_HEADS * A_V
B_WIDTH = B_HEADS * B_DV
C_WIDTH = C_HEADS * C_DH
B_QK = B_HEADS * B_DK
IN_SIZES = (A_HEADS * 2 * A_QK, A_HEADS * 2 * A_QK, A_WIDTH,
            B_QK, B_QK, B_WIDTH, B_WIDTH, 2 * B_GATE_RANK,
            C_WIDTH, C_WIDTH, C_WIDTH)

N_GROUPS = 4
EXPERTS_PER_GROUP = 8
N_EXPERTS = N_GROUPS * EXPERTS_PER_GROUP
TOP_K = 2
EXPERT_HIDDEN = 512

LANES = 128
TOK_TILE = 256
NA_TILE_ROWS = 4
NA_KEY_TILES = 3
BATCH_GROUPS = 1
EXPERT_TILE = 256
GATE_PAD = LANES
BG_WIDTH = 2 * B_QK + 2 * B_WIDTH + GATE_PAD
IN_PAD_WIDTH = 3 * A_WIDTH + BG_WIDTH + 3 * C_WIDTH
ROUTER_PAD = LANES
ROUTE_EXPERT = 0
ROUTE_GATE = 2
SC_CORES = 2
SC_SUBCORES = 16
SC_ROW_BUFFER_BYTES = 128 * 1024
VMEM_LIMIT = 48 * 1024 * 1024


def _split_bf16(x, pieces):
    out = []
    for _ in range(pieces):
        p = x.astype(jnp.bfloat16)
        out.append(p)
        x = x - p.astype(F32)
    return out


def _pack_bf16_pairs(x):
    w = x.shape[1] // 2
    bits = lax.bitcast_convert_type(x.astype(jnp.bfloat16).astype(F32), jnp.uint32)
    return (bits[:, :w] >> 16) | (bits[:, w:] & jnp.uint32(0xFFFF0000))


def _unpack_bf16_pairs(u):
    lo = lax.bitcast_convert_type(u << 16, F32)
    hi = lax.bitcast_convert_type(u & jnp.uint32(0xFFFF0000), F32)
    return lo, hi


def _silu(x):
    return x * (1.0 / (1.0 + jnp.exp(-x)))


def _cparams(sem):
    return pltpu.CompilerParams(dimension_semantics=sem, vmem_limit_bytes=VMEM_LIMIT)


def _mod_body(c_ref, w_ref, b_ref, o_ref):
    a = _silu(c_ref[...])
    o_ref[...] = jnp.dot(a, w_ref[...], precision=HI, preferred_element_type=F32) + b_ref[...]


def _modulation(cvec, w_mod, b_mod):
    depth, d, d6 = w_mod.shape
    rows = cvec.shape[0]
    return pl.pallas_call(
        _mod_body,
        grid=(depth, d6 // d),
        in_specs=[pl.BlockSpec((rows, d), lambda l, j: (0, 0)),
                  pl.BlockSpec((None, d, d), lambda l, j: (l, 0, j)),
                  pl.BlockSpec((None, 1, d), lambda l, j: (l, 0, j))],
        out_specs=pl.BlockSpec((None, rows, d), lambda l, j: (l, 0, j)),
        out_shape=jax.ShapeDtypeStruct((depth, rows, d6), F32),
        compiler_params=_cparams(("arbitrary", "arbitrary")),
        name="modulation",
    )(cvec, w_mod, b_mod.reshape(depth, 1, d6))


def _rms(x, g):
    return x * lax.rsqrt(jnp.mean(x * x, axis=-1, keepdims=True) + NORM_EPS) * g


def _rope(x, cos, sa, sb):
    return x * cos + pltpu.roll(x, LANES - 16, 1) * sa + pltpu.roll(x, 16, 1) * sb


def _moe_residual(x_ref, y0_ref, y1_ref, rt_ref, mod_ref):
    rt = rt_ref[0]
    g0, g1 = rt[:, ROUTE_GATE:ROUTE_GATE + 1], rt[:, ROUTE_GATE + 1:ROUTE_GATE + 2]
    lo0, hi0 = _unpack_bf16_pairs(y0_ref[0])
    lo1, hi1 = _unpack_bf16_pairs(y1_ref[0])
    moe = jnp.concatenate([g0 * lo0 + g1 * lo1, g0 * hi0 + g1 * hi1], axis=1)
    return x_ref[0] + mod_ref[5:6, :] * moe


def _in_proj_body(combine, *refs):
    if combine:
        (x_ref, y0_ref, y1_ref, rt_ref, pmod_ref, mod_ref, g_ref, w_ref, cos_ref, sa_ref, sb_ref,
         qa_ref, ka_ref, va_ref, bg_ref, qn_ref, kn_ref, vn_ref, xo_ref) = refs
        x = _moe_residual(x_ref, y0_ref, y1_ref, rt_ref, pmod_ref)
        xo_ref[0] = x
    else:
        (x_ref, mod_ref, g_ref, w_ref, cos_ref, sa_ref, sb_ref,
         qa_ref, ka_ref, va_ref, bg_ref, qn_ref, kn_ref, vn_ref) = refs
        x = x_ref[0]
    h = _rms(x, g_ref[...]) * (1.0 + mod_ref[1:2, :]) + mod_ref[0:1, :]
    hb = h.astype(MXU_DTYPE)

    def proj(lo, hi):
        return jnp.dot(hb, w_ref[:, lo:hi], preferred_element_type=F32)

    cos, sa, sb = cos_ref[...], sa_ref[...], sb_ref[...]
    for hh in range(A_HEADS):
        lo = hh * LANES
        q = proj(lo, lo + LANES)
        qa_ref[0, :, lo:lo + LANES] = (_rope(q, cos, sa, sb) * (A_QK ** -0.5 * LOG2_E)).astype(qa_ref.dtype)
        k = proj(A_WIDTH + lo, A_WIDTH + lo + LANES)
        ka_ref[0, :, lo:lo + LANES] = _rope(k, cos, sa, sb).astype(ka_ref.dtype)
    o = 2 * A_WIDTH
    va_ref[0] = proj(o, o + A_WIDTH).T.astype(va_ref.dtype)
    o += A_WIDTH
    bg_ref[0] = proj(o, o + BG_WIDTH)
    o += BG_WIDTH
    qn_ref[0] = (proj(o, o + C_WIDTH) * (C_DH ** -0.5)).astype(qn_ref.dtype)
    o += C_WIDTH
    kn_ref[0] = proj(o, o + C_WIDTH).astype(kn_ref.dtype)
    o += C_WIDTH
    vn_ref[0] = proj(o, o + C_WIDTH).astype(vn_ref.dtype)


def _moe_specs(tm, d, row_block):
    return [pl.BlockSpec((None, 1, tm, d // 2), lambda b, i: (0, b, row_block(i), 0)),
            pl.BlockSpec((None, 1, tm, d // 2), lambda b, i: (1, b, row_block(i), 0)),
            pl.BlockSpec((1, tm, ROUTER_PAD), lambda b, i: (b, row_block(i), 0))]


def _in_proj(layer, x, ymoe, route, modsel, g1, w_in_p, rope_tabs):
    bsz, t, d = x.shape
    tm = TOK_TILE
    combine = ymoe is not None
    tok = lambda b, i: (b, i, 0)
    x_spec = pl.BlockSpec((1, tm, d), tok)

    def mod_spec(l):
        return pl.BlockSpec((None, None, None, 6, d), lambda b, i: (l, b, jnp.minimum(i, 1), 0, 0))

    tab_spec = pl.BlockSpec((tm, LANES), lambda b, i: (i, 0))
    in_specs = [x_spec]
    args = [x]
    if combine:
        in_specs += _moe_specs(tm, d, lambda i: i) + [mod_spec(layer - 1)]
        args += [ymoe, ymoe, route, modsel]
    in_specs += [mod_spec(layer), pl.BlockSpec((1, d), lambda b, i: (0, 0)),
                 pl.BlockSpec((d, IN_PAD_WIDTH), lambda b, i: (0, 0)), tab_spec, tab_spec, tab_spec]
    args += [modsel, g1.reshape(1, d), w_in_p, *rope_tabs]

    def o(width, dtype):
        return pl.BlockSpec((1, tm, width), tok), jax.ShapeDtypeStruct((bsz, t, width), dtype)

    va_t = (pl.BlockSpec((1, A_WIDTH, tm), lambda b, i: (b, 0, i)),
            jax.ShapeDtypeStruct((bsz, A_WIDTH, t), MXU_DTYPE))
    outs = [o(A_WIDTH, MXU_DTYPE), o(A_WIDTH, MXU_DTYPE), va_t, o(BG_WIDTH, F32),
            o(C_WIDTH, MXU_DTYPE), o(C_WIDTH, MXU_DTYPE), o(C_WIDTH, MXU_DTYPE)]
    if combine:
        outs.append(o(d, F32))
    return pl.pallas_call(
        functools.partial(_in_proj_body, combine),
        grid=(bsz, t // tm),
        in_specs=in_specs,
        out_specs=[s for s, _ in outs],
        out_shape=[s for _, s in outs],
        compiler_params=_cparams(("parallel", "arbitrary")),
        name="in_proj",
    )(*args)


def _diff_attn_body(lam_init, ctx_len, q_ref, k_ref, vt_ref, lam_ref, g_ref, o_ref):
    q = q_ref[0]
    lane = lax.broadcasted_iota(jnp.int32, (1, LANES), 1)
    zero = jnp.zeros_like(q)
    q1 = jnp.where(lane < A_QK, q, zero)
    q2 = jnp.where(lane >= A_QK, q, zero)
    lm = lam_ref[...]
    lam = (jnp.exp(jnp.sum(lm[0:1] * lm[1:2], axis=1, keepdims=True))
           - jnp.exp(jnp.sum(lm[2:3] * lm[3:4], axis=1, keepdims=True)) + lam_init)

    nt = (((1,), (1,)), ((), ()))
    tq = q.shape[0]

    def attend(n_keys):
        def init():
            return (jnp.full((1, tq), -jnp.inf, F32), jnp.zeros((1, tq), F32), jnp.zeros((A_V, tq), F32))

        bounds = [0, min(ctx_len, n_keys)] + list(range(ctx_len + A_KEY_CHUNK, n_keys + 1, A_KEY_CHUNK))
        chunks = list(zip(bounds[:-1], bounds[1:]))

        def scores(qm, c):
            k_c = k_ref[0, chunks[c][0]:chunks[c][1], :]
            return lax.dot_general(k_c, qm, nt, preferred_element_type=F32)

        def probs(state, s):
            m_old, l_old, acc = state
            m_new = jnp.maximum(m_old, jnp.max(s, axis=0, keepdims=True))
            alpha = jnp.exp2(m_old - m_new)
            p = jnp.exp2(s - m_new)
            l_new = alpha * l_old + jnp.sum(p, axis=0, keepdims=True)
            return p.astype(MXU_DTYPE), alpha, (m_new, l_new, acc)

        def accumulate(state, alpha, p, c):
            m, l, acc = state
            vt_c = vt_ref[0, :, chunks[c][0]:chunks[c][1]]
            return m, l, alpha * acc + jnp.dot(vt_c, p, preferred_element_type=F32)

        st1, st2 = init(), init()
        s1, s2 = scores(q1, 0), scores(q2, 0)
        for c in range(len(chunks)):
            p1, a1, st1 = probs(st1, s1)
            if c + 1 < len(chunks):
                s1 = scores(q1, c + 1)
            st1 = accumulate(st1, a1, p1, c)
            p2, a2, st2 = probs(st2, s2)
            if c + 1 < len(chunks):
                s2 = scores(q2, c + 1)
            st2 = accumulate(st2, a2, p2, c)
        o_t = st1[2] * (1.0 / st1[1]) - st2[2] * (lam / st2[1])
        o_ref[0] = (_rms(o_t.T, g_ref[...]) * (1.0 - lam_init)).astype(o_ref.dtype)

    @pl.when(pl.program_id(2) == 0)
    def _():
        attend(ctx_len)

    @pl.when(pl.program_id(2) > 0)
    def _():
        attend(k_ref.shape[1])


def _diff_attention(layer, qa, ka, va, lam, g_sub):
    bsz, t, _ = qa.shape
    tq = TOK_TILE
    lam_init = 0.8 - 0.6 * math.exp(-0.3 * layer)
    kv_spec = pl.BlockSpec((1, t, LANES), lambda b, h, i: (b, 0, h))
    return pl.pallas_call(
        functools.partial(_diff_attn_body, lam_init, TOK_TILE),
        grid=(bsz, A_HEADS, t // tq),
        in_specs=[pl.BlockSpec((1, tq, LANES), lambda b, h, i: (b, i, h)), kv_spec,
                  pl.BlockSpec((1, A_V, t), lambda b, h, i: (b, h, 0)),
                  pl.BlockSpec((4, A_QK), lambda b, h, i: (0, 0)),
                  pl.BlockSpec((1, A_V), lambda b, h, i: (0, 0))],
        out_specs=pl.BlockSpec((1, tq, LANES), lambda b, h, i: (b, i, h)),
        out_shape=jax.ShapeDtypeStruct((bsz, t, A_WIDTH), MXU_DTYPE),
        compiler_params=_cparams(("parallel", "parallel", "arbitrary")),
        name="diff_attention",
    )(qa, ka, va, lam, g_sub.reshape(1, A_V))


def _gla_body(f_ref, r_ref, wdec_ref, bdec_ref, of_ref, ob_ref, sf_ref, sb_ref, e_ref, b_ref, qs_ref):
    c = B_CHUNK
    n_chunks = TOK_TILE // c

    @pl.when(pl.program_id(1) == 0)
    def _():
        sf_ref[...] = jnp.zeros_like(sf_ref)
        sb_ref[...] = jnp.zeros_like(sb_ref)

    sub = GLA_SUB
    n_sub = c // sub
    nt = (((1,), (1,)), ((), ()))
    t_row = lax.broadcasted_iota(jnp.int32, (TOK_TILE, TOK_TILE), 0)
    t_col = lax.broadcasted_iota(jnp.int32, (TOK_TILE, TOK_TILE), 1)
    same_chunk = (t_row // c) == (t_col // c)
    tri_f = (same_chunk & (t_col <= t_row)).astype(jnp.bfloat16)
    tri_b = (same_chunk & (t_col >= t_row)).astype(jnp.bfloat16)
    s_iota = lax.broadcasted_iota(jnp.int32, (sub, LANES), 0)
    idx = lax.broadcasted_iota(jnp.int32, (c, LANES), 0)
    head_of_k = lax.broadcasted_iota(jnp.int32, (B_QK, B_WIDTH), 0) // B_DK
    head_of_v = lax.broadcasted_iota(jnp.int32, (B_QK, B_WIDTH), 1) // B_DV
    expand = (head_of_k == head_of_v).astype(MXU_DTYPE)
    same_head_t = (lax.broadcasted_iota(jnp.int32, (B_WIDTH, B_QK), 0) // B_DV
                   == lax.broadcasted_iota(jnp.int32, (B_WIDTH, B_QK), 1) // B_DK)
    n_ref = n_sub - 1
    kt_keep = (lax.broadcasted_iota(jnp.int32, (B_HEADS * c, n_ref * B_QK), 0) // c
               == (lax.broadcasted_iota(jnp.int32, (B_HEADS * c, n_ref * B_QK), 1) % B_QK) // B_DK)
    vx_keep = (lax.broadcasted_iota(jnp.int32, (B_HEADS * c, B_WIDTH), 0) // c
               == lax.broadcasted_iota(jnp.int32, (B_HEADS * c, B_WIDTH), 1) // B_DV)
    pick = (lax.broadcasted_iota(jnp.int32, (c, c * sub), 1) // sub
            == lax.broadcasted_iota(jnp.int32, (c, c * sub), 0)).astype(MXU_DTYPE)

    def log_decay(src_ref, backward):
        gl = src_ref[0, :, 2 * B_QK + 2 * B_WIDTH:BG_WIDTH]
        d0 = B_QK if backward else 0
        z = jnp.dot(gl, wdec_ref[:, d0:d0 + B_QK], precision=HI, preferred_element_type=F32) \
            + bdec_ref[:, d0:d0 + B_QK]
        log_a = (jnp.minimum(z, 0.0) - jnp.log(1.0 + jnp.exp(-jnp.abs(z)))) / B_GATE_TAU
        tri = tri_b if backward else tri_f
        return sum(jnp.dot(tri, p, preferred_element_type=F32) for p in _split_bf16(log_a, 3))

    def chunk(src_ref, b_all, lo, backward, st_ref, out_ref):
        q = src_ref[0, lo:lo + c, 0:B_QK] * (B_DK ** -0.5)
        k = src_ref[0, lo:lo + c, B_QK:2 * B_QK]
        v = src_ref[0, lo:lo + c, 2 * B_QK:2 * B_QK + B_WIDTH]
        b = b_all[lo:lo + c]
        b_ref[...] = b
        qs_ref[...] = q
        blk = ((c - 1 - idx) if backward else idx) // sub

        q_parts, k_parts = [], []
        for m, late, early in ((1, blk == 1, blk == 0), (2, blk >= 2, blk <= 1), (3, blk == 3, blk == 2)):
            r_row = (c - 1 - sub * m) if backward else sub * m
            r = b_ref[r_row:r_row + 1, :]
            q_parts.append(q * jnp.exp(jnp.where(late, b - r, -jnp.inf)))
            k_parts.append(k * jnp.exp(jnp.where(early, r - b, -jnp.inf)))
        q_cat = jnp.concatenate(q_parts, axis=1).astype(MXU_DTYPE)
        k_cat = jnp.concatenate(k_parts, axis=1)
        k_exp = jnp.where(kt_keep, jnp.concatenate([k_cat] * B_HEADS, axis=0), 0.0).astype(MXU_DTYPE)
        a_off = lax.dot_general(q_cat, k_exp, nt, preferred_element_type=F32)
        v_exp = jnp.where(vx_keep, jnp.concatenate([v] * B_HEADS, axis=0), 0.0).astype(MXU_DTYPE)
        o_off = jnp.dot(a_off.astype(MXU_DTYPE), v_exp, preferred_element_type=F32)

        for tt in range(c):
            lo_s = tt // sub * sub
            keep = (s_iota >= tt - lo_s) if backward else (s_iota <= tt - lo_s)
            bt = b_ref[tt:tt + 1, :]
            qt = qs_ref[tt:tt + 1, :]
            e = jnp.exp(jnp.where(keep, bt - b[lo_s:lo_s + sub], -jnp.inf)) * (qt * k[lo_s:lo_s + sub])
            e_ref[tt * sub:(tt + 1) * sub, :] = e.astype(e_ref.dtype)
        a_exp = jnp.dot(e_ref[...], expand, preferred_element_type=F32)
        prod = a_exp.reshape(n_sub, sub, sub, B_WIDTH) * v.reshape(n_sub, 1, sub, B_WIDTH)
        o_diag = jnp.dot(pick, prod.reshape(c * sub, B_WIDTH).astype(MXU_DTYPE), preferred_element_type=F32)

        st = st_ref[...]
        o_inter = lax.dot_general(q * jnp.exp(b), st, nt, preferred_element_type=F32)
        out_ref[0, lo:lo + c, :] = o_off + o_diag + o_inter
        b_end = b[0:1, :] if backward else b[c - 1:c, :]
        kd = k * jnp.exp(b_end - b)
        upd = lax.dot_general(v, kd, (((0,), (0,)), ((), ())), preferred_element_type=F32)
        st_ref[...] = jnp.exp(b_end) * st + jnp.where(same_head_t, upd, 0.0)

    b_fwd = log_decay(f_ref, False)
    b_bwd = log_decay(r_ref, True)
    for ci in range(n_chunks):
        chunk(f_ref, b_fwd, ci * c, False, sf_ref, of_ref)
        chunk(r_ref, b_bwd, (n_chunks - 1 - ci) * c, True, sb_ref, ob_ref)


def _gla_scan(bg, w_dec, b_dec):
    bsz, t, _ = bg.shape
    n = t // TOK_TILE
    rev = lambda b, i: (b, jnp.where(i == 0, 0, n - i), 0)
    fwd = lambda b, i: (b, i, 0)
    wdec = jnp.zeros((GATE_PAD, 2 * B_QK), F32)
    wdec = wdec.at[:B_GATE_RANK, :B_QK].set(w_dec[0]).at[B_GATE_RANK:2 * B_GATE_RANK, B_QK:].set(w_dec[1])
    bdec = b_dec.reshape(1, 2 * B_QK)
    o_shape = jax.ShapeDtypeStruct((bsz, t, B_WIDTH), F32)
    return pl.pallas_call(
        _gla_body,
        grid=(bsz, n),
        in_specs=[pl.BlockSpec((1, TOK_TILE, BG_WIDTH), fwd),
                  pl.BlockSpec((1, TOK_TILE, BG_WIDTH), rev),
                  pl.BlockSpec((GATE_PAD, 2 * B_QK), lambda b, i: (0, 0)),
                  pl.BlockSpec((1, 2 * B_QK), lambda b, i: (0, 0))],
        out_specs=[pl.BlockSpec((1, TOK_TILE, B_WIDTH), fwd),
                   pl.BlockSpec((1, TOK_TILE, B_WIDTH), rev)],
        out_shape=[o_shape, o_shape],
        scratch_shapes=[pltpu.VMEM((B_WIDTH, B_QK), F32), pltpu.VMEM((B_WIDTH, B_QK), F32),
                        pltpu.VMEM((B_CHUNK * GLA_SUB, LANES), MXU_DTYPE),
                        pltpu.VMEM((B_CHUNK, LANES), F32), pltpu.VMEM((B_CHUNK, LANES), F32)],
        compiler_params=_cparams(("parallel", "arbitrary")),
        name="gla_scan",
    )(bg, bg, wdec, bdec)


def _na_bias_tables(rpb, rows):
    n_tiles = rows // NA_TILE_ROWS
    wr = min(NA_ROWS, rows)
    n_dr, n_dc = 2 * NA_ROWS - 1, 2 * NA_COLS - 1
    cq = np.arange(GRID_W)[:, None]
    ck = np.arange(GRID_W)[None, :]
    cs = np.clip(cq - NA_COLS // 2, 0, GRID_W - NA_COLS)
    col_ok = (ck >= cs) & (ck < cs + NA_COLS)
    dc = np.clip(ck - cq, -(NA_COLS - 1), NA_COLS - 1) + (NA_COLS - 1)
    onehot = (dc.reshape(1, -1) == np.arange(n_dc)[:, None]).astype(np.float32)
    by_col = jnp.dot(rpb.astype(F32).reshape(-1, n_dc), onehot, precision=HI)
    by_col = jnp.where(col_ok.reshape(1, 1, GRID_W, GRID_W),
                       by_col.reshape(C_HEADS, n_dr, GRID_W, GRID_W), -jnp.inf)
    masked = jnp.full((C_HEADS, GRID_W, GRID_W), -jnp.inf, F32)
    tabs = []
    for j in (0, 1, n_tiles - 1):
        kr0 = int(np.clip(j - 1, 0, n_tiles - NA_KEY_TILES)) * NA_TILE_ROWS
        q_rows = []
        for qr in range(NA_TILE_ROWS):
            r = j * NA_TILE_ROWS + qr
            start = int(np.clip(r - wr // 2, 0, rows - wr))
            blocks = []
            for kw in range(NA_KEY_TILES * NA_TILE_ROWS):
                kr = kr0 + kw
                blocks.append(by_col[:, kr - r + NA_ROWS - 1] if start <= kr < start + wr else masked)
            q_rows.append(jnp.concatenate(blocks, axis=-1))
        tabs.append(jnp.concatenate(q_rows, axis=1))
    return jnp.stack(tabs)


def _na_body(q_ref, k0_ref, k1_ref, k2_ref, kc_ref, v0_ref, v1_ref, v2_ref, vc_ref, m_ref, o_ref):
    q = q_ref[0]
    lane = lax.broadcasted_iota(jnp.int32, (1, LANES), 1)
    zero = jnp.zeros_like(q)
    nt = (((1,), (1,)), ((), ()))

    def scores(qm, k_ref):
        return lax.dot_general(qm, k_ref[0], nt, preferred_element_type=F32)

    def head_out(hh, windows):
        qm = jnp.where((lane >= hh * C_DH) & (lane < (hh + 1) * C_DH), q, zero)
        s = [scores(qm, kc_ref)]
        for w, k_ref in enumerate(windows):
            s.append(scores(qm, k_ref) + m_ref[0, hh, :, w * TOK_TILE:(w + 1) * TOK_TILE])
        m = functools.reduce(jnp.maximum, [jnp.max(x, axis=-1, keepdims=True) for x in s])
        p = [jnp.exp(x - m) for x in s]
        den = functools.reduce(jnp.add, [jnp.sum(x, axis=-1, keepdims=True) for x in p])
        vals = [vc_ref] + [v0_ref, v1_ref, v2_ref][:len(windows)]
        o = functools.reduce(jnp.add, [jnp.dot(x.astype(MXU_DTYPE), v_ref[0], preferred_element_type=F32)
                                       for x, v_ref in zip(p, vals)])
        return o * (1.0 / den)

    def emit(windows):
        o0 = head_out(0, windows)
        o1 = head_out(1, windows)
        o_ref[0] = jnp.where(lane < C_DH, o0, o1).astype(o_ref.dtype)

    @pl.when(pl.program_id(1) == 0)
    def _():
        emit([])

    @pl.when(pl.program_id(1) > 0)
    def _():
        emit([k0_ref, k1_ref, k2_ref])


def _neighborhood_attention(qn, kn, vn, bias_tabs):
    bsz, t, _ = qn.shape
    n = t // TOK_TILE
    n_lat = n - 1

    def win(w):
        def index(hp, i, b):
            kb0 = jnp.clip(i - 2, 0, n_lat - NA_KEY_TILES)
            return (b, kb0 + 1 + w, hp)
        return pl.BlockSpec((1, TOK_TILE, LANES), index)

    def cls(hp, i, b):
        j = i - 1
        return (jnp.where(j <= 0, 0, jnp.where(j == n_lat - 1, 2, 1)), hp, 0, 0)

    own = pl.BlockSpec((1, TOK_TILE, LANES), lambda hp, i, b: (b, i, hp))
    ctx = pl.BlockSpec((1, TOK_TILE, LANES), lambda hp, i, b: (b, 0, hp))
    heads_per_step = LANES // C_DH
    return pl.pallas_call(
        _na_body,
        grid=(C_HEADS // heads_per_step, n, bsz),
        in_specs=[own, win(0), win(1), win(2), ctx, win(0), win(1), win(2), ctx,
                  pl.BlockSpec((1, heads_per_step, TOK_TILE, NA_KEY_TILES * TOK_TILE), cls)],
        out_specs=own,
        out_shape=jax.ShapeDtypeStruct((bsz, t, C_WIDTH), MXU_DTYPE),
        compiler_params=_cparams(("parallel", "parallel", "arbitrary")),
        name="neighborhood_attention",
    )(qn, kn, kn, kn, kn, vn, vn, vn, vn, bias_tabs)


def _out_proj_body(ya_ref, of_ref, ob_ref, r_ref, yn_ref, x_ref, mod_ref, gg_ref, g2_ref, w_ref,
                   wr_ref, br_ref, xo_ref, h_ref, rt_ref):
    o = of_ref[0] + ob_ref[0]
    hi = lax.broadcasted_iota(jnp.int32, (B_WIDTH, B_WIDTH), 0) // B_DV
    hj = lax.broadcasted_iota(jnp.int32, (B_WIDTH, B_WIDTH), 1) // B_DV
    head_mean = jnp.where(hi == hj, 1.0 / B_DV, 0.0).astype(jnp.bfloat16)
    ms = sum(jnp.dot(p, head_mean, preferred_element_type=F32) for p in _split_bf16(o * o, 2))
    yb = o * lax.rsqrt(ms + NORM_EPS) * gg_ref[...] * _silu(r_ref[0])
    mix = (jnp.dot(ya_ref[0], w_ref[0:A_WIDTH, :], preferred_element_type=F32)
           + jnp.dot(yb.astype(MXU_DTYPE), w_ref[A_WIDTH:A_WIDTH + B_WIDTH, :], preferred_element_type=F32)
           + jnp.dot(yn_ref[0], w_ref[A_WIDTH + B_WIDTH:, :], preferred_element_type=F32))
    x = x_ref[0] + mod_ref[2:3, :] * mix
    xo_ref[0] = x
    h = _rms(x, g2_ref[...]) * (1.0 + mod_ref[4:5, :]) + mod_ref[3:4, :]
    h_ref[0] = _pack_bf16_pairs(h)
    h_hi, h_lo = _split_bf16(h, 2)
    both = jnp.dot(h_hi, wr_ref[...], preferred_element_type=F32)
    logits = (both[:, :ROUTER_PAD] + both[:, ROUTER_PAD:]
              + jnp.dot(h_lo, wr_ref[:, :ROUTER_PAD], preferred_element_type=F32) + br_ref[...])
    rt_ref[0] = _route(logits)


def _route(lg):
    lane = lax.broadcasted_iota(jnp.int32, lg.shape, 1)
    big = jnp.int32(ROUTER_PAD)

    def top(mask):
        v = jnp.max(jnp.where(mask, lg, -jnp.inf), axis=-1, keepdims=True)
        i = jnp.min(jnp.where(mask & (lg == v), lane, big), axis=-1, keepdims=True)
        return v, i

    g_mask = lane < N_GROUPS
    g_max, grp = top(g_mask)
    p_grp = 1.0 / jnp.sum(jnp.where(g_mask, jnp.exp(lg - g_max), 0.0), axis=-1, keepdims=True)
    e_lo = N_GROUPS + grp * EXPERTS_PER_GROUP
    e_mask = (lane >= e_lo) & (lane < e_lo + EXPERTS_PER_GROUP)
    v1, i1 = top(e_mask)
    v2, i2 = top(e_mask & (lane != i1))
    r = jnp.exp(v2 - v1)
    gate1 = p_grp / (1.0 + r)
    gate2 = p_grp * r / (1.0 + r)
    out = jnp.where(lane == ROUTE_EXPERT, (i1 - N_GROUPS).astype(F32),
                    jnp.where(lane == ROUTE_EXPERT + 1, (i2 - N_GROUPS).astype(F32),
                              jnp.where(lane == ROUTE_GATE, gate1,
                                        jnp.where(lane == ROUTE_GATE + 1, gate2, 0.0))))
    return out


def _out_proj(layer, ya, o_f, o_b, bg, yn, x, modsel, g_gla, g2, w_out_b, w_router, b_router):
    bsz, t, d = x.shape
    tm = TOK_TILE
    tok = lambda b, i: (b, i, 0)
    const = lambda b, i: (0, 0)
    r_block = (2 * B_QK + B_WIDTH) // B_WIDTH
    return pl.pallas_call(
        _out_proj_body,
        grid=(bsz, t // tm),
        in_specs=[pl.BlockSpec((1, tm, A_WIDTH), tok),
                  pl.BlockSpec((1, tm, B_WIDTH), tok), pl.BlockSpec((1, tm, B_WIDTH), tok),
                  pl.BlockSpec((1, tm, B_WIDTH), lambda b, i: (b, i, r_block)),
                  pl.BlockSpec((1, tm, C_WIDTH), tok),
                  pl.BlockSpec((1, tm, d), tok),
                  pl.BlockSpec((None, None, None, 6, d), lambda b, i: (layer, b, jnp.minimum(i, 1), 0, 0)),
                  pl.BlockSpec((1, B_WIDTH), const), pl.BlockSpec((1, d), const),
                  pl.BlockSpec(w_out_b.shape, const),
                  pl.BlockSpec((d, 2 * ROUTER_PAD), const), pl.BlockSpec((1, ROUTER_PAD), const)],
        out_specs=[pl.BlockSpec((1, tm, d), tok), pl.BlockSpec((1, tm, d // 2), tok),
                   pl.BlockSpec((1, tm, ROUTER_PAD), tok)],
        out_shape=[jax.ShapeDtypeStruct((bsz, t, d), F32), jax.ShapeDtypeStruct((bsz, t, d // 2), jnp.uint32),
                   jax.ShapeDtypeStruct((bsz, t, ROUTER_PAD), F32)],
        compiler_params=_cparams(("parallel", "arbitrary")),
        name="out_proj",
    )(ya, o_f, o_b, bg, yn, x, modsel, jnp.tile(g_gla, B_HEADS).reshape(1, B_WIDTH), g2.reshape(1, d),
      w_out_b, w_router, b_router)


def _dispatch_plan(expert, n_blocks):
    n_tok = expert.shape[0]
    flat = expert.reshape(-1)
    n_assign = flat.shape[0]
    order = jnp.argsort(flat).astype(jnp.int32)
    rank = jnp.argsort(order).astype(jnp.int32)
    counts = jnp.sum((flat[:, None] == jnp.arange(N_EXPERTS, dtype=jnp.int32)[None, :]).astype(jnp.int32), axis=0)
    padded = (counts + EXPERT_TILE - 1) // EXPERT_TILE * EXPERT_TILE
    start = jnp.cumsum(counts) - counts
    pad_end = jnp.cumsum(padded)
    pad_start = pad_end - padded
    shift = (pad_start - start).astype(jnp.int32)
    dest = rank + shift[flat]
    blk_start = jnp.arange(n_blocks, dtype=jnp.int32) * EXPERT_TILE
    blk_expert = jnp.sum((pad_end[None, :] <= blk_start[:, None]).astype(jnp.int32), axis=1)
    blk_expert = jnp.minimum(blk_expert, N_EXPERTS - 1)
    blk_used = (blk_start < pad_end[-1]).astype(jnp.int32)
    row = jnp.arange(n_blocks * EXPERT_TILE, dtype=jnp.int32)
    e_row = jnp.repeat(blk_expert, EXPERT_TILE)
    real = (row - pad_start[e_row].astype(jnp.int32)) < counts[e_row]
    src_assign = order[jnp.clip(row - shift[e_row], 0, n_assign - 1)]
    src_tok = jnp.where(real, src_assign // TOP_K, 0)
    dest = dest.reshape(n_tok, TOP_K).T.reshape(-1)
    return src_tok, dest, blk_expert, blk_used


def _sc_gather(table, idx):
    n_rows = idx.shape[0]
    width = table.shape[1]
    n_workers = SC_CORES * SC_SUBCORES
    per_worker = n_rows // n_workers
    chunk = SC_ROW_BUFFER_BYTES // (width * table.dtype.itemsize)
    n_chunks = per_worker // chunk
    assert per_worker * n_workers == n_rows and n_chunks * chunk == per_worker and n_chunks % 2 == 0
    assert n_chunks >= 4 and chunk <= LANES
    mesh = plsc.VectorSubcoreMesh(core_axis_name="core", subcore_axis_name="subcore")

    def body(table_hbm, idx_hbm, out_hbm, idx_v, rows_v, gather_sem, write_sem):
        worker = lax.axis_index("subcore") * SC_CORES + lax.axis_index("core")
        base = worker * per_worker
        pltpu.sync_copy(idx_hbm.at[worker], idx_v)

        def gather(j, slot):
            return pltpu.make_async_copy(table_hbm.at[idx_v.at[j]], rows_v.at[slot], gather_sem.at[slot])

        def write(j, slot):
            return pltpu.make_async_copy(rows_v.at[slot], out_hbm.at[pl.ds(base + j * chunk, chunk)],
                                         write_sem.at[slot])

        gather(0, 0).start()
        gather(0, 0).wait()
        gather(1, 1).start()
        write(0, 0).start()

        @pl.loop(1, n_chunks - 1, step=2)
        def _(j):
            for s in range(2):
                slot = (1 + s) % 2
                gather(j + s, slot).wait()
                write(j + s - 1, 1 - slot).wait()
                gather(j + s + 1, 1 - slot).start()
                write(j + s, slot).start()

        last = n_chunks - 1
        gather(last, 1).wait()
        write(last, 1).start()
        write(last - 1, 0).wait()
        write(last, 1).wait()

    return pl.kernel(
        body,
        out_type=jax.ShapeDtypeStruct((n_rows, width), table.dtype),
        mesh=mesh,
        scratch_types=[pltpu.VMEM((n_chunks, chunk), jnp.int32),
                       pltpu.VMEM((2, chunk, width), table.dtype),
                       pltpu.SemaphoreType.DMA((2,)), pltpu.SemaphoreType.DMA((2,))],
        name="sc_row_gather",
    )(table, idx.reshape(n_workers, n_chunks, chunk))


def _expert_body(be_ref, used_ref, x_ref, wu_ref, wd_ref, o_ref, wub_ref, wdb_ref):
    i = pl.program_id(0)
    prev = be_ref[jnp.maximum(i - 1, 0)]

    @pl.when((i == 0) | (be_ref[i] != prev))
    def _():
        wub_ref[...] = wu_ref[...].astype(wub_ref.dtype)
        wdb_ref[...] = wd_ref[...].astype(wdb_ref.dtype)

    @pl.when(used_ref[i] > 0)
    def _():
        lo, hi = _unpack_bf16_pairs(x_ref[...])
        half = lo.shape[1]
        gu = (jnp.dot(lo.astype(MXU_DTYPE), wub_ref[:half, :], preferred_element_type=F32)
              + jnp.dot(hi.astype(MXU_DTYPE), wub_ref[half:, :], preferred_element_type=F32))
        act = _silu(gu[:, :EXPERT_HIDDEN]) * gu[:, EXPERT_HIDDEN:]
        o_ref[...] = _pack_bf16_pairs(jnp.dot(act.astype(MXU_DTYPE), wdb_ref[...], preferred_element_type=F32))

    @pl.when(used_ref[i] == 0)
    def _():
        o_ref[...] = jnp.zeros_like(o_ref)


def _expert_ffn(layer, buf, blk_expert, blk_used, w_up, w_down):
    n_rows, packed = buf.shape
    d = 2 * packed
    n_blocks = n_rows // EXPERT_TILE
    h2 = w_up.shape[-1]
    grid_spec = pltpu.PrefetchScalarGridSpec(
        num_scalar_prefetch=2,
        grid=(n_blocks,),
        in_specs=[pl.BlockSpec((EXPERT_TILE, packed), lambda i, be, us: (i, 0)),
                  pl.BlockSpec((None, None, d, h2), lambda i, be, us: (layer, be[i], 0, 0)),
                  pl.BlockSpec((None, None, h2 // 2, d), lambda i, be, us: (layer, be[i], 0, 0))],
        out_specs=pl.BlockSpec((EXPERT_TILE, packed), lambda i, be, us: (i, 0)),
        scratch_shapes=[pltpu.VMEM((d, h2), MXU_DTYPE), pltpu.VMEM((h2 // 2, d), MXU_DTYPE)],
    )
    return pl.pallas_call(
        _expert_body,
        grid_spec=grid_spec,
        out_shape=jax.ShapeDtypeStruct((n_rows, packed), jnp.uint32),
        compiler_params=_cparams(("arbitrary",)),
        name="expert_ffn",
    )(blk_expert, blk_used, buf, w_up, w_down)


def _moe(layer, h2, route, w_up, w_down):
    bsz, t, packed = h2.shape
    n_tok = bsz * t
    n_assign = n_tok * TOP_K
    n_blocks = -(-(n_assign + N_EXPERTS * (EXPERT_TILE - 1)) // EXPERT_TILE)
    expert = route.reshape(n_tok, ROUTER_PAD)[:, ROUTE_EXPERT:ROUTE_EXPERT + TOP_K].astype(jnp.int32)
    src_tok, dest, blk_expert, blk_used = _dispatch_plan(expert, n_blocks)
    buf = _sc_gather(h2.reshape(n_tok, packed), src_tok)
    y = _expert_ffn(layer, buf, blk_expert, blk_used, w_up, w_down)
    return _sc_gather(y, dest).reshape(TOP_K, bsz, t, packed)


def _final_body(x_ref, y0_ref, y1_ref, rt_ref, mod_ref, g_ref, o_ref):
    o_ref[0] = _rms(_moe_residual(x_ref, y0_ref, y1_ref, rt_ref, mod_ref), g_ref[...])


def _final_norm(layer, x, ymoe, route, modsel, g, ctx_tiles):
    bsz, t, d = x.shape
    tm = TOK_TILE
    lat = lambda b, i: (b, i + ctx_tiles, 0)
    return pl.pallas_call(
        _final_body,
        grid=(bsz, t // tm - ctx_tiles),
        in_specs=[pl.BlockSpec((1, tm, d), lat)] + _moe_specs(tm, d, lambda i: i + ctx_tiles)
        + [pl.BlockSpec((None, None, None, 6, d), lambda b, i: (layer, b, 1, 0, 0)),
           pl.BlockSpec((1, d), lambda b, i: (0, 0))],
        out_specs=pl.BlockSpec((1, tm, d), lambda b, i: (b, i, 0)),
        out_shape=jax.ShapeDtypeStruct((bsz, t - ctx_tiles * tm, d), F32),
        compiler_params=_cparams(("parallel", "arbitrary")),
        name="final_norm",
    )(x, ymoe, ymoe, route, modsel, g.reshape(1, d))


def _rope_tables(n_ctx, n_lat):
    t = jnp.arange(n_lat)
    row = (t // GRID_W).astype(F32)
    col = (t % GRID_W).astype(F32)
    n_freq = HEAD_DIM // 4
    inv = ROPE_THETA ** (-jnp.arange(n_freq, dtype=F32) / n_freq)
    ang_r = row[:, None] * inv
    ang_c = col[:, None] * inv
    cr, sr, cc, sc = jnp.cos(ang_r), jnp.sin(ang_r), jnp.cos(ang_c), jnp.sin(ang_c)
    z = jnp.zeros_like(sr)
    cos = jnp.concatenate([cr, cr, cc, cc], axis=-1)
    above = jnp.concatenate([-sr, z, -sc, z], axis=-1)
    below = jnp.concatenate([z, sr, z, sc], axis=-1)
    reps = LANES // HEAD_DIM

    def full(tab, ctx_value):
        tab = jnp.tile(tab, (1, reps))
        return jnp.concatenate([jnp.full((n_ctx, LANES), ctx_value, F32), tab], axis=0)

    return full(cos, 1.0), full(above, 0.0), full(below, 0.0)


def _pack_w_in(w_in):
    parts = jnp.split(w_in, np.cumsum(IN_SIZES)[:-1].tolist(), axis=-1)
    qa, ka, va, qb, kb, vb, rb, gb, qn, kn, vn = parts
    gb = jnp.pad(gb, ((0, 0), (0, GATE_PAD - gb.shape[-1])))
    return jnp.concatenate([qa, ka, va, qb, kb, vb, rb, gb, qn, kn, vn], axis=-1).astype(MXU_DTYPE)


def kernel(x, c, ctx, c_ctx, w_mod, b_mod, norm1_g, norm2_g, w_in, w_out, diff_lambda, diff_sub_g,
           gla_w_decay, gla_b_decay, gla_norm_g, na_rel_bias, w_router_group, b_router_group,
           w_router_expert, b_router_expert, w_expert_up, w_expert_down, final_g):
    bsz, seq, d = x.shape
    n_ctx = ctx.shape[1]
    depth = w_mod.shape[0]
    assert n_ctx == TOK_TILE and seq % (NA_TILE_ROWS * GRID_W) == 0 and d % LANES == 0
    assert seq // (NA_TILE_ROWS * GRID_W) >= NA_KEY_TILES
    mod_rows = -(-(bsz + 1) // 8) * 8
    cvec = jnp.zeros((mod_rows, d), F32).at[:bsz].set(c).at[bsz].set(c_ctx)
    mod = _modulation(cvec, w_mod, b_mod).reshape(depth, mod_rows, 6, d)
    modsel = jnp.stack([jnp.broadcast_to(mod[:, bsz][:, None], (depth, bsz, 6, d)), mod[:, :bsz]], axis=2)
    rope_tabs = _rope_tables(n_ctx, seq)
    xt = jnp.concatenate([ctx, x], axis=1)
    n_groups = BATCH_GROUPS if bsz % BATCH_GROUPS == 0 else 1
    gb = bsz // n_groups
    groups = [dict(xt=xt[g * gb:(g + 1) * gb], mod=modsel[:, g * gb:(g + 1) * gb], ymoe=None, route=None)
              for g in range(n_groups)]
    for l in range(depth):
        w_in_p = _pack_w_in(w_in[l])
        bias_tabs = _na_bias_tables(na_rel_bias[l], seq // GRID_W)
        w_router = jnp.pad(jnp.concatenate([w_router_group[l], w_router_expert[l]], axis=-1),
                           ((0, 0), (0, ROUTER_PAD - N_GROUPS - N_EXPERTS)))
        w_router = jnp.concatenate(_split_bf16(w_router, 2), axis=-1)
        b_router = jnp.pad(jnp.concatenate([b_router_group[l], b_router_expert[l]]),
                           (0, ROUTER_PAD - N_GROUPS - N_EXPERTS)).reshape(1, ROUTER_PAD)
        w_out_b = w_out[l].astype(MXU_DTYPE)
        for st in groups:
            outs = _in_proj(l, st["xt"], st["ymoe"], st["route"], st["mod"], norm1_g[l], w_in_p, rope_tabs)
            qa, ka, va, bg, qn, kn, vn = outs[:7]
            if st["ymoe"] is not None:
                st["xt"] = outs[7]
            ya = _diff_attention(l, qa, ka, va, diff_lambda[l], diff_sub_g[l])
            o_f, o_b = _gla_scan(bg, gla_w_decay[l], gla_b_decay[l])
            yn = _neighborhood_attention(qn, kn, vn, bias_tabs)
            st["xt"], h2, st["route"] = _out_proj(l, ya, o_f, o_b, bg, yn, st["xt"], st["mod"], gla_norm_g[l],
                                                  norm2_g[l], w_out_b, w_router, b_router)
            st["ymoe"] = _moe(l, h2, st["route"], w_expert_up, w_expert_down)
    outs = [_final_norm(depth - 1, st["xt"], st["ymoe"], st["route"], st["mod"], final_g, n_ctx // TOK_TILE)
            for st in groups]
    return jnp.concatenate(outs, axis=0)
```

```python
import functools
import math

import numpy as np
import jax
import jax.numpy as jnp
from jax import lax
from jax.experimental import pallas as pl
from jax.experimental.pallas import tpu as pltpu
from jax.experimental.pallas import tpu_sc as plsc

F32 = jnp.float32
MXU_DTYPE = jnp.bfloat16
HI = lax.Precision.HIGHEST

GRID_W = 64
HEAD_DIM = 64
ROPE_THETA = 10000.0
NORM_EPS = 1e-6

A_HEADS = 4
A_QK = HEAD_DIM
A_V = 2 * HEAD_DIM
B_HEADS = 4
B_DK = HEAD_DIM // 2
B_DV = HEAD_DIM
B_GATE_RANK = 16
B_GATE_TAU = 16.0
B_CHUNK = 64
LOG2_E = math.log2(math.e)
GLA_SUB = 16
C_HEADS = 4
C_DH = HEAD_DIM
NA_ROWS = 8
NA_COLS = 16

A_WIDTH = A_HEADS * A_V
B_WIDTH = B_HEADS * B_DV
C_WIDTH = C_HEADS * C_DH
B_QK = B_HEADS * B_DK
IN_SIZES = (A_HEADS * 2 * A_QK, A_HEADS * 2 * A_QK, A_WIDTH,
            B_QK, B_QK, B_WIDTH, B_WIDTH, 2 * B_GATE_RANK,
            C_WIDTH, C_WIDTH, C_WIDTH)

N_GROUPS = 4
EXPERTS_PER_GROUP = 8
N_EXPERTS = N_GROUPS * EXPERTS_PER_GROUP
TOP_K = 2
EXPERT_HIDDEN = 512

LANES = 128
TOK_TILE = 256
NA_TILE_ROWS = 4
NA_KEY_TILES = 3
BATCH_GROUPS = 1
EXPERT_TILE = 256
GATE_PAD = LANES
BG_WIDTH = 2 * B_QK + 2 * B_WIDTH + GATE_PAD
IN_PAD_WIDTH = 3 * A_WIDTH + BG_WIDTH + 3 * C_WIDTH
ROUTER_PAD = LANES
ROUTE_EXPERT = 0
ROUTE_GATE = 2
SC_CORES = 2
SC_SUBCORES = 16
SC_ROW_BUFFER_BYTES = 128 * 1024
VMEM_LIMIT = 48 * 1024 * 1024


def _split_bf16(x, pieces):
    out = []
    for _ in range(pieces):
        p = x.astype(jnp.bfloat16)
        out.append(p)
        x = x - p.astype(F32)
    return out


def _pack_bf16_pairs(x):
    w = x.shape[1] // 2
    bits = lax.bitcast_convert_type(x.astype(jnp.bfloat16).astype(F32), jnp.uint32)
    return (bits[:, :w] >> 16) | (bits[:, w:] & jnp.uint32(0xFFFF0000))


def _unpack_bf16_pairs(u):
    lo = lax.bitcast_convert_type(u << 16, F32)
    hi = lax.bitcast_convert_type(u & jnp.uint32(0xFFFF0000), F32)
    return lo, hi


def _silu(x):
    return x * (1.0 / (1.0 + jnp.exp(-x)))


def _cparams(sem):
    return pltpu.CompilerParams(dimension_semantics=sem, vmem_limit_bytes=VMEM_LIMIT)


def _mod_body(c_ref, w_ref, b_ref, o_ref):
    a = _silu(c_ref[...])
    o_ref[...] = jnp.dot(a, w_ref[...], precision=HI, preferred_element_type=F32) + b_ref[...]


def _modulation(cvec, w_mod, b_mod):
    depth, d, d6 = w_mod.shape
    rows = cvec.shape[0]
    return pl.pallas_call(
        _mod_body,
        grid=(depth, d6 // d),
        in_specs=[pl.BlockSpec((rows, d), lambda l, j: (0, 0)),
                  pl.BlockSpec((None, d, d), lambda l, j: (l, 0, j)),
                  pl.BlockSpec((None, 1, d), lambda l, j: (l, 0, j))],
        out_specs=pl.BlockSpec((None, rows, d), lambda l, j: (l, 0, j)),
        out_shape=jax.ShapeDtypeStruct((depth, rows, d6), F32),
        compiler_params=_cparams(("arbitrary", "arbitrary")),
        name="modulation",
    )(cvec, w_mod, b_mod.reshape(depth, 1, d6))


def _rms(x, g):
    return x * lax.rsqrt(jnp.mean(x * x, axis=-1, keepdims=True) + NORM_EPS) * g


def _rope(x, cos, sa, sb):
    return x * cos + pltpu.roll(x, LANES - 16, 1) * sa + pltpu.roll(x, 16, 1) * sb


def _moe_residual(x_ref, y0_ref, y1_ref, rt_ref, mod_ref):
    rt = rt_ref[0]
    g0, g1 = rt[:, ROUTE_GATE:ROUTE_GATE + 1], rt[:, ROUTE_GATE + 1:ROUTE_GATE + 2]
    lo0, hi0 = _unpack_bf16_pairs(y0_ref[0])
    lo1, hi1 = _unpack_bf16_pairs(y1_ref[0])
    moe = jnp.concatenate([g0 * lo0 + g1 * lo1, g0 * hi0 + g1 * hi1], axis=1)
    return x_ref[0] + mod_ref[5:6, :] * moe


def _in_proj_body(combine, *refs):
    if combine:
        (x_ref, y0_ref, y1_ref, rt_ref, pmod_ref, mod_ref, g_ref, w_ref, cos_ref, sa_ref, sb_ref,
         qa_ref, ka_ref, va_ref, bg_ref, qn_ref, kn_ref, vn_ref, xo_ref) = refs
        x = _moe_residual(x_ref, y0_ref, y1_ref, rt_ref, pmod_ref)
        xo_ref[0] = x
    else:
        (x_ref, mod_ref, g_ref, w_ref, cos_ref, sa_ref, sb_ref,
         qa_ref, ka_ref, va_ref, bg_ref, qn_ref, kn_ref, vn_ref) = refs
        x = x_ref[0]
    h = _rms(x, g_ref[...]) * (1.0 + mod_ref[1:2, :]) + mod_ref[0:1, :]
    hb = h.astype(MXU_DTYPE)

    def proj(lo, hi):
        return jnp.dot(hb, w_ref[:, lo:hi], preferred_element_type=F32)

    cos, sa, sb = cos_ref[...], sa_ref[...], sb_ref[...]
    for hh in range(A_HEADS):
        lo = hh * LANES
        q = proj(lo, lo + LANES)
        qa_ref[0, :, lo:lo + LANES] = (_rope(q, cos, sa, sb) * (A_QK ** -0.5 * LOG2_E)).astype(qa_ref.dtype)
        k = proj(A_WIDTH + lo, A_WIDTH + lo + LANES)
        ka_ref[0, :, lo:lo + LANES] = _rope(k, cos, sa, sb).astype(ka_ref.dtype)
    o = 2 * A_WIDTH
    va = proj(o, o + A_WIDTH)
    ones = jnp.ones((va.shape[0], A_V), F32)
    va_ref[0] = jnp.concatenate([piece for hh in range(A_HEADS)
                                 for piece in (va[:, hh * A_V:(hh + 1) * A_V], ones)],
                                axis=1).astype(va_ref.dtype)
    o += A_WIDTH
    bg_ref[0] = proj(o, o + BG_WIDTH)
    o += BG_WIDTH
    qn_ref[0] = (proj(o, o + C_WIDTH) * (C_DH ** -0.5)).astype(qn_ref.dtype)
    o += C_WIDTH
    kn_ref[0] = proj(o, o + C_WIDTH).astype(kn_ref.dtype)
    o += C_WIDTH
    vn_ref[0] = proj(o, o + C_WIDTH).astype(vn_ref.dtype)


def _moe_specs(tm, d, row_block):
    return [pl.BlockSpec((None, 1, tm, d // 2), lambda b, i: (0, b, row_block(i), 0)),
            pl.BlockSpec((None, 1, tm, d // 2), lambda b, i: (1, b, row_block(i), 0)),
            pl.BlockSpec((1, tm, ROUTER_PAD), lambda b, i: (b, row_block(i), 0))]


def _in_proj(layer, x, ymoe, route, modsel, g1, w_in_p, rope_tabs):
    bsz, t, d = x.shape
    tm = TOK_TILE
    combine = ymoe is not None
    tok = lambda b, i: (b, i, 0)
    x_spec = pl.BlockSpec((1, tm, d), tok)

    def mod_spec(l):
        return pl.BlockSpec((None, None, None, 6, d), lambda b, i: (l, b, jnp.minimum(i, 1), 0, 0))

    tab_spec = pl.BlockSpec((tm, LANES), lambda b, i: (i, 0))
    in_specs = [x_spec]
    args = [x]
    if combine:
        in_specs += _moe_specs(tm, d, lambda i: i) + [mod_spec(layer - 1)]
        args += [ymoe, ymoe, route, modsel]
    in_specs += [mod_spec(layer), pl.BlockSpec((1, d), lambda b, i: (0, 0)),
                 pl.BlockSpec((d, IN_PAD_WIDTH), lambda b, i: (0, 0)), tab_spec, tab_spec, tab_spec]
    args += [modsel, g1.reshape(1, d), w_in_p, *rope_tabs]

    def o(width, dtype):
        return pl.BlockSpec((1, tm, width), tok), jax.ShapeDtypeStruct((bsz, t, width), dtype)

    outs = [o(A_WIDTH, MXU_DTYPE), o(A_WIDTH, MXU_DTYPE), o(2 * A_WIDTH, MXU_DTYPE), o(BG_WIDTH, F32),
            o(C_WIDTH, MXU_DTYPE), o(C_WIDTH, MXU_DTYPE), o(C_WIDTH, MXU_DTYPE)]
    if combine:
        outs.append(o(d, F32))
    return pl.pallas_call(
        functools.partial(_in_proj_body, combine),
        grid=(bsz, t // tm),
        in_specs=in_specs,
        out_specs=[s for s, _ in outs],
        out_shape=[s for _, s in outs],
        compiler_params=_cparams(("parallel", "arbitrary")),
        name="in_proj",
    )(*args)


def _diff_rows(lam_init, q, k_ref, v_ref, n_keys, lam, g):
    lane = lax.broadcasted_iota(jnp.int32, (1, LANES), 1)
    zero = jnp.zeros_like(q)
    nt = (((1,), (1,)), ((), ()))
    k = k_ref[0, :n_keys, :]
    s1 = lax.dot_general(jnp.where(lane < A_QK, q, zero), k, nt, preferred_element_type=F32)
    s2 = lax.dot_general(jnp.where(lane >= A_QK, q, zero), k, nt, preferred_element_type=F32)

    def finish():
        v1 = v_ref[0, :n_keys, :]
        outs = []
        for s in (s1, s2):
            p = jnp.exp2((s - jnp.max(s, axis=-1, keepdims=True)).astype(MXU_DTYPE))
            outs.append(jnp.dot(p, v1, preferred_element_type=F32))
        o = (outs[0][:, :A_V] * (1.0 / outs[0][:, A_V:A_V + 1])
             - outs[1][:, :A_V] * (lam / outs[1][:, A_V:A_V + 1]))
        return _rms(o, g) * (1.0 - lam_init)

    return finish


def _diff_attn_body(lam_init, ctx_len, qa_ref, qb_ref, k_ref, v_ref, lam_ref, g_ref, oc_ref, ol_ref):
    lm = lam_ref[...]
    lam = (jnp.exp(jnp.sum(lm[0:1] * lm[1:2], axis=1, keepdims=True))
           - jnp.exp(jnp.sum(lm[2:3] * lm[3:4], axis=1, keepdims=True)) + lam_init)
    g = g_ref[...]
    rows = qa_ref.shape[1]

    @pl.when(pl.program_id(2) == 0)
    def _():
        oc_ref[0] = _diff_rows(lam_init, qa_ref[0], k_ref, v_ref, ctx_len, lam, g)().astype(oc_ref.dtype)

    @pl.when(pl.program_id(2) > 0)
    def _():
        n_keys = k_ref.shape[1]
        fin_a = _diff_rows(lam_init, qa_ref[0], k_ref, v_ref, n_keys, lam, g)
        fin_b = _diff_rows(lam_init, qb_ref[0], k_ref, v_ref, n_keys, lam, g)
        ol_ref[0, :rows, :] = fin_a().astype(ol_ref.dtype)
        ol_ref[0, rows:, :] = fin_b().astype(ol_ref.dtype)


def _diff_attention(layer, qa, ka, va, lam, g_sub):
    bsz, t, _ = qa.shape
    tq = TOK_TILE
    n_lat = (t - tq) // (2 * tq)
    assert n_lat * 2 * tq == t - tq
    lam_init = 0.8 - 0.6 * math.exp(-0.3 * layer)

    def q_spec(off):
        return pl.BlockSpec((1, tq, LANES), lambda b, h, i: (b, jnp.maximum(2 * i + off, 0), h))

    return pl.pallas_call(
        functools.partial(_diff_attn_body, lam_init, TOK_TILE),
        grid=(bsz, A_HEADS, 1 + n_lat),
        in_specs=[q_spec(-1), q_spec(0),
                  pl.BlockSpec((1, t, LANES), lambda b, h, i: (b, 0, h)),
                  pl.BlockSpec((1, t, 2 * A_V), lambda b, h, i: (b, 0, h)),
                  pl.BlockSpec((4, A_QK), lambda b, h, i: (0, 0)),
                  pl.BlockSpec((1, A_V), lambda b, h, i: (0, 0))],
        out_specs=[pl.BlockSpec((1, tq, LANES), lambda b, h, i: (b, 0, h)),
                   pl.BlockSpec((1, 2 * tq, LANES), lambda b, h, i: (b, jnp.maximum(i - 1, 0), h))],
        out_shape=[jax.ShapeDtypeStruct((bsz, tq, A_WIDTH), MXU_DTYPE),
                   jax.ShapeDtypeStruct((bsz, t - tq, A_WIDTH), MXU_DTYPE)],
        compiler_params=_cparams(("parallel", "parallel", "arbitrary")),
        name="diff_attention",
    )(qa, qa, ka, va, lam, g_sub.reshape(1, A_V))


def _gla_body(f_ref, r_ref, wdec_ref, bdec_ref, of_ref, ob_ref, sf_ref, sb_ref, e_ref, b_ref, qs_ref):
    c = B_CHUNK
    n_chunks = TOK_TILE // c

    @pl.when(pl.program_id(1) == 0)
    def _():
        sf_ref[...] = jnp.zeros_like(sf_ref)
        sb_ref[...] = jnp.zeros_like(sb_ref)

    sub = GLA_SUB
    n_sub = c // sub
    nt = (((1,), (1,)), ((), ()))
    t_row = lax.broadcasted_iota(jnp.int32, (TOK_TILE, TOK_TILE), 0)
    t_col = lax.broadcasted_iota(jnp.int32, (TOK_TILE, TOK_TILE), 1)
    same_chunk = (t_row // c) == (t_col // c)
    tri_f = (same_chunk & (t_col <= t_row)).astype(jnp.bfloat16)
    tri_b = (same_chunk & (t_col >= t_row)).astype(jnp.bfloat16)
    s_iota = lax.broadcasted_iota(jnp.int32, (sub, LANES), 0)
    idx = lax.broadcasted_iota(jnp.int32, (c, LANES), 0)
    head_of_k = lax.broadcasted_iota(jnp.int32, (B_QK, B_WIDTH), 0) // B_DK
    head_of_v = lax.broadcasted_iota(jnp.int32, (B_QK, B_WIDTH), 1) // B_DV
    expand = (head_of_k == head_of_v).astype(MXU_DTYPE)
    same_head_t = (lax.broadcasted_iota(jnp.int32, (B_WIDTH, B_QK), 0) // B_DV
                   == lax.broadcasted_iota(jnp.int32, (B_WIDTH, B_QK), 1) // B_DK)
    n_ref = n_sub - 1
    kt_keep = (lax.broadcasted_iota(jnp.int32, (B_HEADS * c, n_ref * B_QK), 0) // c
               == (lax.broadcasted_iota(jnp.int32, (B_HEADS * c, n_ref * B_QK), 1) % B_QK) // B_DK)
    vx_keep = (lax.broadcasted_iota(jnp.int32, (B_HEADS * c, B_WIDTH), 0) // c
               == lax.broadcasted_iota(jnp.int32, (B_HEADS * c, B_WIDTH), 1) // B_DV)
    pick = (lax.broadcasted_iota(jnp.int32, (c, c * sub), 1) // sub
            == lax.broadcasted_iota(jnp.int32, (c, c * sub), 0)).astype(MXU_DTYPE)

    def log_decay(src_ref, backward):
        gl = src_ref[0, :, 2 * B_QK + 2 * B_WIDTH:BG_WIDTH]
        d0 = B_QK if backward else 0
        z = jnp.dot(gl, wdec_ref[:, d0:d0 + B_QK], precision=HI, preferred_element_type=F32) \
            + bdec_ref[:, d0:d0 + B_QK]
        log_a = (jnp.minimum(z, 0.0) - jnp.log(1.0 + jnp.exp(-jnp.abs(z)))) / B_GATE_TAU
        tri = tri_b if backward else tri_f
        return sum(jnp.dot(tri, p, preferred_element_type=F32) for p in _split_bf16(log_a, 3))

    def chunk(src_ref, b_all, lo, backward, st_ref, out_ref):
        q = src_ref[0, lo:lo + c, 0:B_QK] * (B_DK ** -0.5)
        k = src_ref[0, lo:lo + c, B_QK:2 * B_QK]
        v = src_ref[0, lo:lo + c, 2 * B_QK:2 * B_QK + B_WIDTH]
        b = b_all[lo:lo + c]
        b_ref[...] = b
        qs_ref[...] = q
        blk = ((c - 1 - idx) if backward else idx) // sub

        q_parts, k_parts = [], []
        for m, late, early in ((1, blk == 1, blk == 0), (2, blk >= 2, blk <= 1), (3, blk == 3, blk == 2)):
            r_row = (c - 1 - sub * m) if backward else sub * m
            r = b_ref[r_row:r_row + 1, :]
            q_parts.append(q * jnp.exp(jnp.where(late, b - r, -jnp.inf)))
            k_parts.append(k * jnp.exp(jnp.where(early, r - b, -jnp.inf)))
        q_cat = jnp.concatenate(q_parts, axis=1).astype(MXU_DTYPE)
        k_cat = jnp.concatenate(k_parts, axis=1)
        k_exp = jnp.where(kt_keep, jnp.concatenate([k_cat] * B_HEADS, axis=0), 0.0).astype(MXU_DTYPE)
        a_off = lax.dot_general(q_cat, k_exp, nt, preferred_element_type=F32)
        v_exp = jnp.where(vx_keep, jnp.concatenate([v] * B_HEADS, axis=0), 0.0).astype(MXU_DTYPE)
        o_off = jnp.dot(a_off.astype(MXU_DTYPE), v_exp, preferred_element_type=F32)

        for tt in range(c):
            lo_s = tt // sub * sub
            keep = (s_iota >= tt - lo_s) if backward else (s_iota <= tt - lo_s)
            bt = b_ref[tt:tt + 1, :]
            qt = qs_ref[tt:tt + 1, :]
            e = jnp.exp(jnp.where(keep, bt - b[lo_s:lo_s + sub], -jnp.inf)) * (qt * k[lo_s:lo_s + sub])
            e_ref[tt * sub:(tt + 1) * sub, :] = e.astype(e_ref.dtype)
        a_exp = jnp.dot(e_ref[...], expand, preferred_element_type=F32)
        prod = a_exp.reshape(n_sub, sub, sub, B_WIDTH) * v.reshape(n_sub, 1, sub, B_WIDTH)
        o_diag = jnp.dot(pick, prod.reshape(c * sub, B_WIDTH).astype(MXU_DTYPE), preferred_element_type=F32)

        st = st_ref[...]
        o_inter = lax.dot_general(q * jnp.exp(b), st, nt, preferred_element_type=F32)
        out_ref[0, lo:lo + c, :] = o_off + o_diag + o_inter
        b_end = b[0:1, :] if backward else b[c - 1:c, :]
        kd = k * jnp.exp(b_end - b)
        upd = lax.dot_general(v, kd, (((0,), (0,)), ((), ())), preferred_element_type=F32)
        st_ref[...] = jnp.exp(b_end) * st + jnp.where(same_head_t, upd, 0.0)

    b_fwd = log_decay(f_ref, False)
    b_bwd = log_decay(r_ref, True)
    for ci in range(n_chunks):
        chunk(f_ref, b_fwd, ci * c, False, sf_ref, of_ref)
        chunk(r_ref, b_bwd, (n_chunks - 1 - ci) * c, True, sb_ref, ob_ref)


def _gla_scan(bg, w_dec, b_dec):
    bsz, t, _ = bg.shape
    n = t // TOK_TILE
    rev = lambda b, i: (b, jnp.where(i == 0, 0, n - i), 0)
    fwd = lambda b, i: (b, i, 0)
    wdec = jnp.zeros((GATE_PAD, 2 * B_QK), F32)
    wdec = wdec.at[:B_GATE_RANK, :B_QK].set(w_dec[0]).at[B_GATE_RANK:2 * B_GATE_RANK, B_QK:].set(w_dec[1])
    bdec = b_dec.reshape(1, 2 * B_QK)
    o_shape = jax.ShapeDtypeStruct((bsz, t, B_WIDTH), F32)
    return pl.pallas_call(
        _gla_body,
        grid=(bsz, n),
        in_specs=[pl.BlockSpec((1, TOK_TILE, BG_WIDTH), fwd),
                  pl.BlockSpec((1, TOK_TILE, BG_WIDTH), rev),
                  pl.BlockSpec((GATE_PAD, 2 * B_QK), lambda b, i: (0, 0)),
                  pl.BlockSpec((1, 2 * B_QK), lambda b, i: (0, 0))],
        out_specs=[pl.BlockSpec((1, TOK_TILE, B_WIDTH), fwd),
                   pl.BlockSpec((1, TOK_TILE, B_WIDTH), rev)],
        out_shape=[o_shape, o_shape],
        scratch_shapes=[pltpu.VMEM((B_WIDTH, B_QK), F32), pltpu.VMEM((B_WIDTH, B_QK), F32),
                        pltpu.VMEM((B_CHUNK * GLA_SUB, LANES), MXU_DTYPE),
                        pltpu.VMEM((B_CHUNK, LANES), F32), pltpu.VMEM((B_CHUNK, LANES), F32)],
        compiler_params=_cparams(("parallel", "arbitrary")),
        name="gla_scan",
    )(bg, bg, wdec, bdec)


def _na_bias_tables(rpb, rows):
    n_tiles = rows // NA_TILE_ROWS
    wr = min(NA_ROWS, rows)
    n_dr, n_dc = 2 * NA_ROWS - 1, 2 * NA_COLS - 1
    cq = np.arange(GRID_W)[:, None]
    ck = np.arange(GRID_W)[None, :]
    cs = np.clip(cq - NA_COLS // 2, 0, GRID_W - NA_COLS)
    col_ok = (ck >= cs) & (ck < cs + NA_COLS)
    dc = np.clip(ck - cq, -(NA_COLS - 1), NA_COLS - 1) + (NA_COLS - 1)
    onehot = (dc.reshape(1, -1) == np.arange(n_dc)[:, None]).astype(np.float32)
    by_col = jnp.dot(rpb.astype(F32).reshape(-1, n_dc), onehot, precision=HI)
    by_col = jnp.where(col_ok.reshape(1, 1, GRID_W, GRID_W),
                       by_col.reshape(C_HEADS, n_dr, GRID_W, GRID_W), -jnp.inf)
    masked = jnp.full((C_HEADS, GRID_W, GRID_W), -jnp.inf, F32)
    tabs = []
    for j in (0, 1, n_tiles - 1):
        kr0 = int(np.clip(j - 1, 0, n_tiles - NA_KEY_TILES)) * NA_TILE_ROWS
        q_rows = []
        for qr in range(NA_TILE_ROWS):
            r = j * NA_TILE_ROWS + qr
            start = int(np.clip(r - wr // 2, 0, rows - wr))
            blocks = []
            for kw in range(NA_KEY_TILES * NA_TILE_ROWS):
                kr = kr0 + kw
                blocks.append(by_col[:, kr - r + NA_ROWS - 1] if start <= kr < start + wr else masked)
            q_rows.append(jnp.concatenate(blocks, axis=-1))
        tabs.append(jnp.concatenate(q_rows, axis=1))
    return jnp.stack(tabs)


def _na_body(q_ref, k0_ref, k1_ref, k2_ref, kc_ref, v0_ref, v1_ref, v2_ref, vc_ref, m_ref, o_ref):
    q = q_ref[0]
    lane = lax.broadcasted_iota(jnp.int32, (1, LANES), 1)
    zero = jnp.zeros_like(q)
    nt = (((1,), (1,)), ((), ()))

    def scores(qm, k_ref):
        return lax.dot_general(qm, k_ref[0], nt, preferred_element_type=F32)

    def head_out(hh, windows):
        qm = jnp.where((lane >= hh * C_DH) & (lane < (hh + 1) * C_DH), q, zero)
        s = [scores(qm, kc_ref)]
        for w, k_ref in enumerate(windows):
            s.append(scores(qm, k_ref) + m_ref[0, hh, :, w * TOK_TILE:(w + 1) * TOK_TILE])
        m = functools.reduce(jnp.maximum, [jnp.max(x, axis=-1, keepdims=True) for x in s])
        p = [jnp.exp(x - m) for x in s]
        den = functools.reduce(jnp.add, [jnp.sum(x, axis=-1, keepdims=True) for x in p])
        vals = [vc_ref] + [v0_ref, v1_ref, v2_ref][:len(windows)]
        o = functools.reduce(jnp.add, [jnp.dot(x.astype(MXU_DTYPE), v_ref[0], preferred_element_type=F32)
                                       for x, v_ref in zip(p, vals)])
        return o * (1.0 / den)

    def emit(windows):
        o0 = head_out(0, windows)
        o1 = head_out(1, windows)
        o_ref[0] = jnp.where(lane < C_DH, o0, o1).astype(o_ref.dtype)

    @pl.when(pl.program_id(1) == 0)
    def _():
        emit([])

    @pl.when(pl.program_id(1) > 0)
    def _():
        emit([k0_ref, k1_ref, k2_ref])


def _neighborhood_attention(qn, kn, vn, bias_tabs):
    bsz, t, _ = qn.shape
    n = t // TOK_TILE
    n_lat = n - 1

    def win(w):
        def index(hp, i, b):
            kb0 = jnp.clip(i - 2, 0, n_lat - NA_KEY_TILES)
            return (b, kb0 + 1 + w, hp)
        return pl.BlockSpec((1, TOK_TILE, LANES), index)

    def cls(hp, i, b):
        j = i - 1
        return (jnp.where(j <= 0, 0, jnp.where(j == n_lat - 1, 2, 1)), hp, 0, 0)

    own = pl.BlockSpec((1, TOK_TILE, LANES), lambda hp, i, b: (b, i, hp))
    ctx = pl.BlockSpec((1, TOK_TILE, LANES), lambda hp, i, b: (b, 0, hp))
    heads_per_step = LANES // C_DH
    return pl.pallas_call(
        _na_body,
        grid=(C_HEADS // heads_per_step, n, bsz),
        in_specs=[own, win(0), win(1), win(2), ctx, win(0), win(1), win(2), ctx,
                  pl.BlockSpec((1, heads_per_step, TOK_TILE, NA_KEY_TILES * TOK_TILE), cls)],
        out_specs=own,
        out_shape=jax.ShapeDtypeStruct((bsz, t, C_WIDTH), MXU_DTYPE),
        compiler_params=_cparams(("parallel", "parallel", "arbitrary")),
        name="neighborhood_attention",
    )(qn, kn, kn, kn, kn, vn, vn, vn, vn, bias_tabs)


def _out_proj_body(yac_ref, yal_ref, of_ref, ob_ref, r_ref, yn_ref, x_ref, mod_ref, gg_ref, g2_ref, w_ref,
                   wr_ref, br_ref, xo_ref, h_ref, rt_ref):
    ya = jnp.where(pl.program_id(1) == 0, yac_ref[0], yal_ref[0])
    o = of_ref[0] + ob_ref[0]
    hi = lax.broadcasted_iota(jnp.int32, (B_WIDTH, B_WIDTH), 0) // B_DV
    hj = lax.broadcasted_iota(jnp.int32, (B_WIDTH, B_WIDTH), 1) // B_DV
    head_mean = jnp.where(hi == hj, 1.0 / B_DV, 0.0).astype(jnp.bfloat16)
    ms = sum(jnp.dot(p, head_mean, preferred_element_type=F32) for p in _split_bf16(o * o, 2))
    yb = o * lax.rsqrt(ms + NORM_EPS) * gg_ref[...] * _silu(r_ref[0])
    mix = (jnp.dot(ya, w_ref[0:A_WIDTH, :], preferred_element_type=F32)
           + jnp.dot(yb.astype(MXU_DTYPE), w_ref[A_WIDTH:A_WIDTH + B_WIDTH, :], preferred_element_type=F32)
           + jnp.dot(yn_ref[0], w_ref[A_WIDTH + B_WIDTH:, :], preferred_element_type=F32))
    x = x_ref[0] + mod_ref[2:3, :] * mix
    xo_ref[0] = x
    h = _rms(x, g2_ref[...]) * (1.0 + mod_ref[4:5, :]) + mod_ref[3:4, :]
    h_ref[0] = _pack_bf16_pairs(h)
    h_hi, h_lo = _split_bf16(h, 2)
    both = jnp.dot(h_hi, wr_ref[...], preferred_element_type=F32)
    logits = (both[:, :ROUTER_PAD] + both[:, ROUTER_PAD:]
              + jnp.dot(h_lo, wr_ref[:, :ROUTER_PAD], preferred_element_type=F32) + br_ref[...])
    rt_ref[0] = _route(logits)


def _route(lg):
    lane = lax.broadcasted_iota(jnp.int32, lg.shape, 1)
    big = jnp.int32(ROUTER_PAD)

    def top(mask):
        v = jnp.max(jnp.where(mask, lg, -jnp.inf), axis=-1, keepdims=True)
        i = jnp.min(jnp.where(mask & (lg == v), lane, big), axis=-1, keepdims=True)
        return v, i

    g_mask = lane < N_GROUPS
    g_max, grp = top(g_mask)
    p_grp = 1.0 / jnp.sum(jnp.where(g_mask, jnp.exp(lg - g_max), 0.0), axis=-1, keepdims=True)
    e_lo = N_GROUPS + grp * EXPERTS_PER_GROUP
    e_mask = (lane >= e_lo) & (lane < e_lo + EXPERTS_PER_GROUP)
    v1, i1 = top(e_mask)
    v2, i2 = top(e_mask & (lane != i1))
    r = jnp.exp(v2 - v1)
    gate1 = p_grp / (1.0 + r)
    gate2 = p_grp * r / (1.0 + r)
    out = jnp.where(lane == ROUTE_EXPERT, (i1 - N_GROUPS).astype(F32),
                    jnp.where(lane == ROUTE_EXPERT + 1, (i2 - N_GROUPS).astype(F32),
                              jnp.where(lane == ROUTE_GATE, gate1,
                                        jnp.where(lane == ROUTE_GATE + 1, gate2, 0.0))))
    return out


def _out_proj(layer, ya_ctx, ya_lat, o_f, o_b, bg, yn, x, modsel, g_gla, g2, w_out_b, w_router, b_router):
    bsz, t, d = x.shape
    tm = TOK_TILE
    tok = lambda b, i: (b, i, 0)
    const = lambda b, i: (0, 0)
    r_block = (2 * B_QK + B_WIDTH) // B_WIDTH
    return pl.pallas_call(
        _out_proj_body,
        grid=(bsz, t // tm),
        in_specs=[pl.BlockSpec((1, tm, A_WIDTH), lambda b, i: (b, 0, 0)),
                  pl.BlockSpec((1, tm, A_WIDTH), lambda b, i: (b, jnp.maximum(i - 1, 0), 0)),
                  pl.BlockSpec((1, tm, B_WIDTH), tok), pl.BlockSpec((1, tm, B_WIDTH), tok),
                  pl.BlockSpec((1, tm, B_WIDTH), lambda b, i: (b, i, r_block)),
                  pl.BlockSpec((1, tm, C_WIDTH), tok),
                  pl.BlockSpec((1, tm, d), tok),
                  pl.BlockSpec((None, None, None, 6, d), lambda b, i: (layer, b, jnp.minimum(i, 1), 0, 0)),
                  pl.BlockSpec((1, B_WIDTH), const), pl.BlockSpec((1, d), const),
                  pl.BlockSpec(w_out_b.shape, const),
                  pl.BlockSpec((d, 2 * ROUTER_PAD), const), pl.BlockSpec((1, ROUTER_PAD), const)],
        out_specs=[pl.BlockSpec((1, tm, d), tok), pl.BlockSpec((1, tm, d // 2), tok),
                   pl.BlockSpec((1, tm, ROUTER_PAD), tok)],
        out_shape=[jax.ShapeDtypeStruct((bsz, t, d), F32), jax.ShapeDtypeStruct((bsz, t, d // 2), jnp.uint32),
                   jax.ShapeDtypeStruct((bsz, t, ROUTER_PAD), F32)],
        compiler_params=_cparams(("parallel", "arbitrary")),
        name="out_proj",
    )(ya_ctx, ya_lat, o_f, o_b, bg, yn, x, modsel, jnp.tile(g_gla, B_HEADS).reshape(1, B_WIDTH), g2.reshape(1, d),
      w_out_b, w_router, b_router)


def _dispatch_plan(expert, n_blocks):
    n_tok = expert.shape[0]
    flat = expert.reshape(-1)
    n_assign = flat.shape[0]
    order = jnp.argsort(flat).astype(jnp.int32)
    rank = jnp.argsort(order).astype(jnp.int32)
    counts = jnp.sum((flat[:, None] == jnp.arange(N_EXPERTS, dtype=jnp.int32)[None, :]).astype(jnp.int32), axis=0)
    padded = (counts + EXPERT_TILE - 1) // EXPERT_TILE * EXPERT_TILE
    start = jnp.cumsum(counts) - counts
    pad_end = jnp.cumsum(padded)
    pad_start = pad_end - padded
    shift = (pad_start - start).astype(jnp.int32)
    dest = rank + shift[flat]
    blk_start = jnp.arange(n_blocks, dtype=jnp.int32) * EXPERT_TILE
    blk_expert = jnp.sum((pad_end[None, :] <= blk_start[:, None]).astype(jnp.int32), axis=1)
    blk_expert = jnp.minimum(blk_expert, N_EXPERTS - 1)
    blk_used = (blk_start < pad_end[-1]).astype(jnp.int32)
    row = jnp.arange(n_blocks * EXPERT_TILE, dtype=jnp.int32)
    e_row = jnp.repeat(blk_expert, EXPERT_TILE)
    real = (row - pad_start[e_row].astype(jnp.int32)) < counts[e_row]
    src_assign = order[jnp.clip(row - shift[e_row], 0, n_assign - 1)]
    src_tok = jnp.where(real, src_assign // TOP_K, 0)
    dest = dest.reshape(n_tok, TOP_K).T.reshape(-1)
    return src_tok, dest, blk_expert, blk_used


def _sc_gather(table, idx):
    n_rows = idx.shape[0]
    width = table.shape[1]
    n_workers = SC_CORES * SC_SUBCORES
    per_worker = n_rows // n_workers
    chunk = SC_ROW_BUFFER_BYTES // (width * table.dtype.itemsize)
    n_chunks = per_worker // chunk
    assert per_worker * n_workers == n_rows and n_chunks * chunk == per_worker and n_chunks % 2 == 0
    assert n_chunks >= 4 and chunk <= LANES
    mesh = plsc.VectorSubcoreMesh(core_axis_name="core", subcore_axis_name="subcore")

    def body(table_hbm, idx_hbm, out_hbm, idx_v, rows_v, gather_sem, write_sem):
        worker = lax.axis_index("subcore") * SC_CORES + lax.axis_index("core")
        base = worker * per_worker
        pltpu.sync_copy(idx_hbm.at[worker], idx_v)

        def gather(j, slot):
            return pltpu.make_async_copy(table_hbm.at[idx_v.at[j]], rows_v.at[slot], gather_sem.at[slot])

        def write(j, slot):
            return pltpu.make_async_copy(rows_v.at[slot], out_hbm.at[pl.ds(base + j * chunk, chunk)],
                                         write_sem.at[slot])

        gather(0, 0).start()
        gather(0, 0).wait()
        gather(1, 1).start()
        write(0, 0).start()

        @pl.loop(1, n_chunks - 1, step=2)
        def _(j):
            for s in range(2):
                slot = (1 + s) % 2
                gather(j + s, slot).wait()
                write(j + s - 1, 1 - slot).wait()
                gather(j + s + 1, 1 - slot).start()
                write(j + s, slot).start()

        last = n_chunks - 1
        gather(last, 1).wait()
        write(last, 1).start()
        write(last - 1, 0).wait()
        write(last, 1).wait()

    return pl.kernel(
        body,
        out_type=jax.ShapeDtypeStruct((n_rows, width), table.dtype),
        mesh=mesh,
        scratch_types=[pltpu.VMEM((n_chunks, chunk), jnp.int32),
                       pltpu.VMEM((2, chunk, width), table.dtype),
                       pltpu.SemaphoreType.DMA((2,)), pltpu.SemaphoreType.DMA((2,))],
        name="sc_row_gather",
    )(table, idx.reshape(n_workers, n_chunks, chunk))


def _expert_body(be_ref, used_ref, x_ref, wu_ref, wd_ref, o_ref, wub_ref, wdb_ref):
    i = pl.program_id(0)
    prev = be_ref[jnp.maximum(i - 1, 0)]

    @pl.when((i == 0) | (be_ref[i] != prev))
    def _():
        wub_ref[...] = wu_ref[...].astype(wub_ref.dtype)
        wdb_ref[...] = wd_ref[...].astype(wdb_ref.dtype)

    @pl.when(used_ref[i] > 0)
    def _():
        lo, hi = _unpack_bf16_pairs(x_ref[...])
        half = lo.shape[1]
        gu = (jnp.dot(lo.astype(MXU_DTYPE), wub_ref[:half, :], preferred_element_type=F32)
              + jnp.dot(hi.astype(MXU_DTYPE), wub_ref[half:, :], preferred_element_type=F32))
        act = _silu(gu[:, :EXPERT_HIDDEN]) * gu[:, EXPERT_HIDDEN:]
        o_ref[...] = _pack_bf16_pairs(jnp.dot(act.astype(MXU_DTYPE), wdb_ref[...], preferred_element_type=F32))

    @pl.when(used_ref[i] == 0)
    def _():
        o_ref[...] = jnp.zeros_like(o_ref)


def _expert_ffn(layer, buf, blk_expert, blk_used, w_up, w_down):
    n_rows, packed = buf.shape
    d = 2 * packed
    n_blocks = n_rows // EXPERT_TILE
    h2 = w_up.shape[-1]
    grid_spec = pltpu.PrefetchScalarGridSpec(
        num_scalar_prefetch=2,
        grid=(n_blocks,),
        in_specs=[pl.BlockSpec((EXPERT_TILE, packed), lambda i, be, us: (i, 0)),
                  pl.BlockSpec((None, None, d, h2), lambda i, be, us: (layer, be[i], 0, 0)),
                  pl.BlockSpec((None, None, h2 // 2, d), lambda i, be, us: (layer, be[i], 0, 0))],
        out_specs=pl.BlockSpec((EXPERT_TILE, packed), lambda i, be, us: (i, 0)),
        scratch_shapes=[pltpu.VMEM((d, h2), MXU_DTYPE), pltpu.VMEM((h2 // 2, d), MXU_DTYPE)],
    )
    return pl.pallas_call(
        _expert_body,
        grid_spec=grid_spec,
        out_shape=jax.ShapeDtypeStruct((n_rows, packed), jnp.uint32),
        compiler_params=_cparams(("arbitrary",)),
        name="expert_ffn",
    )(blk_expert, blk_used, buf, w_up, w_down)


def _moe(layer, h2, route, w_up, w_down):
    bsz, t, packed = h2.shape
    n_tok = bsz * t
    n_assign = n_tok * TOP_K
    n_blocks = -(-(n_assign + N_EXPERTS * (EXPERT_TILE - 1)) // EXPERT_TILE)
    expert = route.reshape(n_tok, ROUTER_PAD)[:, ROUTE_EXPERT:ROUTE_EXPERT + TOP_K].astype(jnp.int32)
    src_tok, dest, blk_expert, blk_used = _dispatch_plan(expert, n_blocks)
    buf = _sc_gather(h2.reshape(n_tok, packed), src_tok)
    y = _expert_ffn(layer, buf, blk_expert, blk_used, w_up, w_down)
    return _sc_gather(y, dest).reshape(TOP_K, bsz, t, packed)


def _final_body(x_ref, y0_ref, y1_ref, rt_ref, mod_ref, g_ref, o_ref):
    o_ref[0] = _rms(_moe_residual(x_ref, y0_ref, y1_ref, rt_ref, mod_ref), g_ref[...])


def _final_norm(layer, x, ymoe, route, modsel, g, ctx_tiles):
    bsz, t, d = x.shape
    tm = TOK_TILE
    lat = lambda b, i: (b, i + ctx_tiles, 0)
    return pl.pallas_call(
        _final_body,
        grid=(bsz, t // tm - ctx_tiles),
        in_specs=[pl.BlockSpec((1, tm, d), lat)] + _moe_specs(tm, d, lambda i: i + ctx_tiles)
        + [pl.BlockSpec((None, None, None, 6, d), lambda b, i: (layer, b, 1, 0, 0)),
           pl.BlockSpec((1, d), lambda b, i: (0, 0))],
        out_specs=pl.BlockSpec((1, tm, d), lambda b, i: (b, i, 0)),
        out_shape=jax.ShapeDtypeStruct((bsz, t - ctx_tiles * tm, d), F32),
        compiler_params=_cparams(("parallel", "arbitrary")),
        name="final_norm",
    )(x, ymoe, ymoe, route, modsel, g.reshape(1, d))


def _rope_tables(n_ctx, n_lat):
    t = jnp.arange(n_lat)
    row = (t // GRID_W).astype(F32)
    col = (t % GRID_W).astype(F32)
    n_freq = HEAD_DIM // 4
    inv = ROPE_THETA ** (-jnp.arange(n_freq, dtype=F32) / n_freq)
    ang_r = row[:, None] * inv
    ang_c = col[:, None] * inv
    cr, sr, cc, sc = jnp.cos(ang_r), jnp.sin(ang_r), jnp.cos(ang_c), jnp.sin(ang_c)
    z = jnp.zeros_like(sr)
    cos = jnp.concatenate([cr, cr, cc, cc], axis=-1)
    above = jnp.concatenate([-sr, z, -sc, z], axis=-1)
    below = jnp.concatenate([z, sr, z, sc], axis=-1)
    reps = LANES // HEAD_DIM

    def full(tab, ctx_value):
        tab = jnp.tile(tab, (1, reps))
        return jnp.concatenate([jnp.full((n_ctx, LANES), ctx_value, F32), tab], axis=0)

    return full(cos, 1.0), full(above, 0.0), full(below, 0.0)


def _pack_w_in(w_in):
    parts = jnp.split(w_in, np.cumsum(IN_SIZES)[:-1].tolist(), axis=-1)
    qa, ka, va, qb, kb, vb, rb, gb, qn, kn, vn = parts
    gb = jnp.pad(gb, ((0, 0), (0, GATE_PAD - gb.shape[-1])))
    return jnp.concatenate([qa, ka, va, qb, kb, vb, rb, gb, qn, kn, vn], axis=-1).astype(MXU_DTYPE)


def kernel(x, c, ctx, c_ctx, w_mod, b_mod, norm1_g, norm2_g, w_in, w_out, diff_lambda, diff_sub_g,
           gla_w_decay, gla_b_decay, gla_norm_g, na_rel_bias, w_router_group, b_router_group,
           w_router_expert, b_router_expert, w_expert_up, w_expert_down, final_g):
    bsz, seq, d = x.shape
    n_ctx = ctx.shape[1]
    depth = w_mod.shape[0]
    assert n_ctx == TOK_TILE and seq % (NA_TILE_ROWS * GRID_W) == 0 and d % LANES == 0
    assert seq // (NA_TILE_ROWS * GRID_W) >= NA_KEY_TILES
    mod_rows = -(-(bsz + 1) // 8) * 8
    cvec = jnp.zeros((mod_rows, d), F32).at[:bsz].set(c).at[bsz].set(c_ctx)
    mod = _modulation(cvec, w_mod, b_mod).reshape(depth, mod_rows, 6, d)
    modsel = jnp.stack([jnp.broadcast_to(mod[:, bsz][:, None], (depth, bsz, 6, d)), mod[:, :bsz]], axis=2)
    rope_tabs = _rope_tables(n_ctx, seq)
    xt = jnp.concatenate([ctx, x], axis=1)
    n_groups = BATCH_GROUPS if bsz % BATCH_GROUPS == 0 else 1
    gb = bsz // n_groups
    groups = [dict(xt=xt[g * gb:(g + 1) * gb], mod=modsel[:, g * gb:(g + 1) * gb], ymoe=None, route=None)
              for g in range(n_groups)]
    for l in range(depth):
        w_in_p = _pack_w_in(w_in[l])
        bias_tabs = _na_bias_tables(na_rel_bias[l], seq // GRID_W)
        w_router = jnp.pad(jnp.concatenate([w_router_group[l], w_router_expert[l]], axis=-1),
                           ((0, 0), (0, ROUTER_PAD - N_GROUPS - N_EXPERTS)))
        w_router = jnp.concatenate(_split_bf16(w_router, 2), axis=-1)
        b_router = jnp.pad(jnp.concatenate([b_router_group[l], b_router_expert[l]]),
                           (0, ROUTER_PAD - N_GROUPS - N_EXPERTS)).reshape(1, ROUTER_PAD)
        w_out_b = w_out[l].astype(MXU_DTYPE)
        for st in groups:
            outs = _in_proj(l, st["xt"], st["ymoe"], st["route"], st["mod"], norm1_g[l], w_in_p, rope_tabs)
            qa, ka, va, bg, qn, kn, vn = outs[:7]
            if st["ymoe"] is not None:
                st["xt"] = outs[7]
            ya_ctx, ya_lat = _diff_attention(l, qa, ka, va, diff_lambda[l], diff_sub_g[l])
            o_f, o_b = _gla_scan(bg, gla_w_decay[l], gla_b_decay[l])
            yn = _neighborhood_attention(qn, kn, vn, bias_tabs)
            st["xt"], h2, st["route"] = _out_proj(l, ya_ctx, ya_lat, o_f, o_b, bg, yn, st["xt"], st["mod"], gla_norm_g[l],
                                                  norm2_g[l], w_out_b, w_router, b_router)
            st["ymoe"] = _moe(l, h2, st["route"], w_expert_up, w_expert_down)
    outs = [_final_norm(depth - 1, st["xt"], st["ymoe"], st["route"], st["mod"], final_g, n_ctx // TOK_TILE)
            for st in groups]
    return jnp.concatenate(outs, axis=0)
```

```python
import functools
import math

import numpy as np
import jax
import jax.numpy as jnp
from jax import lax
from jax.experimental import pallas as pl
from jax.experimental.pallas import tpu as pltpu
from jax.experimental.pallas import tpu_sc as plsc

F32 = jnp.float32
MXU_DTYPE = jnp.bfloat16
HI = lax.Precision.HIGHEST

GRID_W = 64
HEAD_DIM = 64
ROPE_THETA = 10000.0
NORM_EPS = 1e-6

A_HEADS = 4
A_QK = HEAD_DIM
A_V = 2 * HEAD_DIM
B_HEADS = 4
B_DK = HEAD_DIM // 2
B_DV = HEAD_DIM
B_GATE_RANK = 16
B_GATE_TAU = 16.0
B_CHUNK = 64
LOG2_E = math.log2(math.e)
GLA_SUB = 16
C_HEADS = 4
C_DH = HEAD_DIM
NA_ROWS = 8
NA_COLS = 16

A_WIDTH = A_HEADS * A_V
B_WIDTH = B_HEADS * B_DV
C_WIDTH = C_HEADS * C_DH
B_QK = B_HEADS * B_DK
IN_SIZES = (A_HEADS * 2 * A_QK, A_HEADS * 2 * A_QK, A_WIDTH,
            B_QK, B_QK, B_WIDTH, B_WIDTH, 2 * B_GATE_RANK,
            C_WIDTH, C_WIDTH, C_WIDTH)

N_GROUPS = 4
EXPERTS_PER_GROUP = 8
N_EXPERTS = N_GROUPS * EXPERTS_PER_GROUP
TOP_K = 2
EXPERT_HIDDEN = 512

LANES = 128
TOK_TILE = 256
NA_TILE_ROWS = 4
NA_KEY_TILES = 3
BATCH_GROUPS = 1
EXPERT_TILE = 256
GATE_PAD = LANES
BG_WIDTH = 2 * B_QK + 2 * B_WIDTH + GATE_PAD
IN_PAD_WIDTH = 3 * A_WIDTH + BG_WIDTH + 3 * C_WIDTH
ROUTER_PAD = LANES
ROUTE_EXPERT = 0
ROUTE_GATE = 2
SC_CORES = 2
SC_SUBCORES = 16
SC_ROW_BUFFER_BYTES = 128 * 1024
VMEM_LIMIT = 48 * 1024 * 1024


def _split_bf16(x, pieces):
    out = []
    for _ in range(pieces):
        p = x.astype(jnp.bfloat16)
        out.append(p)
        x = x - p.astype(F32)
    return out


def _pack_bf16_pairs(x):
    w = x.shape[1] // 2
    bits = lax.bitcast_convert_type(x.astype(jnp.bfloat16).astype(F32), jnp.uint32)
    return (bits[:, :w] >> 16) | (bits[:, w:] & jnp.uint32(0xFFFF0000))


def _unpack_bf16_pairs(u):
    lo = lax.bitcast_convert_type(u << 16, F32)
    hi = lax.bitcast_convert_type(u & jnp.uint32(0xFFFF0000), F32)
    return lo, hi


def _silu(x):
    return x * (1.0 / (1.0 + jnp.exp(-x)))


def _cparams(sem):
    return pltpu.CompilerParams(dimension_semantics=sem, vmem_limit_bytes=VMEM_LIMIT)


def _mod_body(c_ref, w_ref, b_ref, o_ref):
    a = _silu(c_ref[...])
    o_ref[...] = jnp.dot(a, w_ref[...], precision=HI, preferred_element_type=F32) + b_ref[...]


def _modulation(cvec, w_mod, b_mod):
    depth, d, d6 = w_mod.shape
    rows = cvec.shape[0]
    return pl.pallas_call(
        _mod_body,
        grid=(depth, d6 // d),
        in_specs=[pl.BlockSpec((rows, d), lambda l, j: (0, 0)),
                  pl.BlockSpec((None, d, d), lambda l, j: (l, 0, j)),
                  pl.BlockSpec((None, 1, d), lambda l, j: (l, 0, j))],
        out_specs=pl.BlockSpec((None, rows, d), lambda l, j: (l, 0, j)),
        out_shape=jax.ShapeDtypeStruct((depth, rows, d6), F32),
        compiler_params=_cparams(("arbitrary", "arbitrary")),
        name="modulation",
    )(cvec, w_mod, b_mod.reshape(depth, 1, d6))


def _rms(x, g):
    return x * lax.rsqrt(jnp.mean(x * x, axis=-1, keepdims=True) + NORM_EPS) * g


def _rope(x, cos, sa, sb):
    return x * cos + pltpu.roll(x, LANES - 16, 1) * sa + pltpu.roll(x, 16, 1) * sb


def _moe_residual(x_ref, y0_ref, y1_ref, rt_ref, mod_ref):
    rt = rt_ref[0]
    g0, g1 = rt[:, ROUTE_GATE:ROUTE_GATE + 1], rt[:, ROUTE_GATE + 1:ROUTE_GATE + 2]
    lo0, hi0 = _unpack_bf16_pairs(y0_ref[0])
    lo1, hi1 = _unpack_bf16_pairs(y1_ref[0])
    moe = jnp.concatenate([g0 * lo0 + g1 * lo1, g0 * hi0 + g1 * hi1], axis=1)
    return x_ref[0] + mod_ref[5:6, :] * moe


def _in_proj_body(combine, *refs):
    if combine:
        (x_ref, y0_ref, y1_ref, rt_ref, pmod_ref, mod_ref, g_ref, w_ref, cos_ref, sa_ref, sb_ref,
         qa_ref, ka_ref, va_ref, bg_ref, qn_ref, kn_ref, vn_ref, xo_ref) = refs
        x = _moe_residual(x_ref, y0_ref, y1_ref, rt_ref, pmod_ref)
        xo_ref[0] = x
    else:
        (x_ref, mod_ref, g_ref, w_ref, cos_ref, sa_ref, sb_ref,
         qa_ref, ka_ref, va_ref, bg_ref, qn_ref, kn_ref, vn_ref) = refs
        x = x_ref[0]
    h = _rms(x, g_ref[...]) * (1.0 + mod_ref[1:2, :]) + mod_ref[0:1, :]
    hb = h.astype(MXU_DTYPE)

    def proj(lo, hi):
        return jnp.dot(hb, w_ref[:, lo:hi], preferred_element_type=F32)

    cos, sa, sb = cos_ref[...], sa_ref[...], sb_ref[...]
    for hh in range(A_HEADS):
        lo = hh * LANES
        q = proj(lo, lo + LANES)
        qa_ref[0, :, lo:lo + LANES] = (_rope(q, cos, sa, sb) * (A_QK ** -0.5 * LOG2_E)).astype(qa_ref.dtype)
        k = proj(A_WIDTH + lo, A_WIDTH + lo + LANES)
        ka_ref[0, :, lo:lo + LANES] = _rope(k, cos, sa, sb).astype(ka_ref.dtype)
    o = 2 * A_WIDTH
    va = proj(o, o + A_WIDTH)
    ones = jnp.ones((va.shape[0], A_V), F32)
    va_ref[0] = jnp.concatenate([piece for hh in range(A_HEADS)
                                 for piece in (va[:, hh * A_V:(hh + 1) * A_V], ones)],
                                axis=1).astype(va_ref.dtype)
    o += A_WIDTH
    bg_ref[0] = proj(o, o + BG_WIDTH)
    o += BG_WIDTH
    qn_ref[0] = (proj(o, o + C_WIDTH) * (C_DH ** -0.5)).astype(qn_ref.dtype)
    o += C_WIDTH
    kn_ref[0] = proj(o, o + C_WIDTH).astype(kn_ref.dtype)
    o += C_WIDTH
    vn_ref[0] = proj(o, o + C_WIDTH).astype(vn_ref.dtype)


def _moe_specs(tm, d, row_block):
    return [pl.BlockSpec((None, 1, tm, d // 2), lambda b, i: (0, b, row_block(i), 0)),
            pl.BlockSpec((None, 1, tm, d // 2), lambda b, i: (1, b, row_block(i), 0)),
            pl.BlockSpec((1, tm, ROUTER_PAD), lambda b, i: (b, row_block(i), 0))]


def _in_proj(layer, x, ymoe, route, modsel, g1, w_in_p, rope_tabs):
    bsz, t, d = x.shape
    tm = TOK_TILE
    combine = ymoe is not None
    tok = lambda b, i: (b, i, 0)
    x_spec = pl.BlockSpec((1, tm, d), tok)

    def mod_spec(l):
        return pl.BlockSpec((None, None, None, 6, d), lambda b, i: (l, b, jnp.minimum(i, 1), 0, 0))

    tab_spec = pl.BlockSpec((tm, LANES), lambda b, i: (i, 0))
    in_specs = [x_spec]
    args = [x]
    if combine:
        in_specs += _moe_specs(tm, d, lambda i: i) + [mod_spec(layer - 1)]
        args += [ymoe, ymoe, route, modsel]
    in_specs += [mod_spec(layer), pl.BlockSpec((1, d), lambda b, i: (0, 0)),
                 pl.BlockSpec((d, IN_PAD_WIDTH), lambda b, i: (0, 0)), tab_spec, tab_spec, tab_spec]
    args += [modsel, g1.reshape(1, d), w_in_p, *rope_tabs]

    def o(width, dtype):
        return pl.BlockSpec((1, tm, width), tok), jax.ShapeDtypeStruct((bsz, t, width), dtype)

    outs = [o(A_WIDTH, MXU_DTYPE), o(A_WIDTH, MXU_DTYPE), o(2 * A_WIDTH, MXU_DTYPE), o(BG_WIDTH, F32),
            o(C_WIDTH, MXU_DTYPE), o(C_WIDTH, MXU_DTYPE), o(C_WIDTH, MXU_DTYPE)]
    if combine:
        outs.append(o(d, F32))
    return pl.pallas_call(
        functools.partial(_in_proj_body, combine),
        grid=(bsz, t // tm),
        in_specs=in_specs,
        out_specs=[s for s, _ in outs],
        out_shape=[s for _, s in outs],
        compiler_params=_cparams(("parallel", "arbitrary")),
        name="in_proj",
    )(*args)


def _diff_rows(lam_init, q, k_ref, v_ref, n_keys, lam, g):
    lane = lax.broadcasted_iota(jnp.int32, (1, LANES), 1)
    zero = jnp.zeros_like(q)
    nt = (((1,), (1,)), ((), ()))
    k = k_ref[0, :n_keys, :]
    s1 = lax.dot_general(jnp.where(lane < A_QK, q, zero), k, nt, preferred_element_type=F32)
    s2 = lax.dot_general(jnp.where(lane >= A_QK, q, zero), k, nt, preferred_element_type=F32)

    def finish():
        v1 = v_ref[0, :n_keys, :]
        outs = []
        for s in (s1, s2):
            p = jnp.exp2((s - jnp.max(s, axis=-1, keepdims=True)).astype(MXU_DTYPE))
            outs.append(jnp.dot(p, v1, preferred_element_type=F32))
        o = (outs[0][:, :A_V] * (1.0 / outs[0][:, A_V:A_V + 1])
             - outs[1][:, :A_V] * (lam / outs[1][:, A_V:A_V + 1]))
        return _rms(o, g) * (1.0 - lam_init)

    return finish


def _diff_attn_body(lam_init, ctx_len, qa_ref, qb_ref, k_ref, v_ref, lam_ref, g_ref, oc_ref, ol_ref):
    lm = lam_ref[...]
    lam = (jnp.exp(jnp.sum(lm[0:1] * lm[1:2], axis=1, keepdims=True))
           - jnp.exp(jnp.sum(lm[2:3] * lm[3:4], axis=1, keepdims=True)) + lam_init)
    g = g_ref[...]
    rows = qa_ref.shape[1]

    @pl.when(pl.program_id(2) == 0)
    def _():
        oc_ref[0] = _diff_rows(lam_init, qa_ref[0], k_ref, v_ref, ctx_len, lam, g)().astype(oc_ref.dtype)

    @pl.when(pl.program_id(2) > 0)
    def _():
        n_keys = k_ref.shape[1]
        fin_a = _diff_rows(lam_init, qa_ref[0], k_ref, v_ref, n_keys, lam, g)
        fin_b = _diff_rows(lam_init, qb_ref[0], k_ref, v_ref, n_keys, lam, g)
        ol_ref[0, :rows, :] = fin_a().astype(ol_ref.dtype)
        ol_ref[0, rows:, :] = fin_b().astype(ol_ref.dtype)


def _diff_attention(layer, qa, ka, va, lam, g_sub):
    bsz, t, _ = qa.shape
    tq = TOK_TILE
    n_lat = (t - tq) // (2 * tq)
    assert n_lat * 2 * tq == t - tq
    lam_init = 0.8 - 0.6 * math.exp(-0.3 * layer)

    def q_spec(off):
        return pl.BlockSpec((1, tq, LANES), lambda b, h, i: (b, jnp.maximum(2 * i + off, 0), h))

    return pl.pallas_call(
        functools.partial(_diff_attn_body, lam_init, TOK_TILE),
        grid=(bsz, A_HEADS, 1 + n_lat),
        in_specs=[q_spec(-1), q_spec(0),
                  pl.BlockSpec((1, t, LANES), lambda b, h, i: (b, 0, h)),
                  pl.BlockSpec((1, t, 2 * A_V), lambda b, h, i: (b, 0, h)),
                  pl.BlockSpec((4, A_QK), lambda b, h, i: (0, 0)),
                  pl.BlockSpec((1, A_V), lambda b, h, i: (0, 0))],
        out_specs=[pl.BlockSpec((1, tq, LANES), lambda b, h, i: (b, 0, h)),
                   pl.BlockSpec((1, 2 * tq, LANES), lambda b, h, i: (b, jnp.maximum(i - 1, 0), h))],
        out_shape=[jax.ShapeDtypeStruct((bsz, tq, A_WIDTH), MXU_DTYPE),
                   jax.ShapeDtypeStruct((bsz, t - tq, A_WIDTH), MXU_DTYPE)],
        compiler_params=_cparams(("parallel", "parallel", "arbitrary")),
        name="diff_attention",
    )(qa, qa, ka, va, lam, g_sub.reshape(1, A_V))


def _gla_body(f_ref, r_ref, wdec_ref, bdec_ref, of_ref, ob_ref, sf_ref, sb_ref, es_ref, bs_ref, qss_ref):
    c = B_CHUNK
    n_chunks = TOK_TILE // c

    @pl.when(pl.program_id(1) == 0)
    def _():
        sf_ref[...] = jnp.zeros_like(sf_ref)
        sb_ref[...] = jnp.zeros_like(sb_ref)

    sub = GLA_SUB
    n_sub = c // sub
    nt = (((1,), (1,)), ((), ()))
    t_row = lax.broadcasted_iota(jnp.int32, (TOK_TILE, TOK_TILE), 0)
    t_col = lax.broadcasted_iota(jnp.int32, (TOK_TILE, TOK_TILE), 1)
    same_chunk = (t_row // c) == (t_col // c)
    tri_f = (same_chunk & (t_col <= t_row)).astype(jnp.bfloat16)
    tri_b = (same_chunk & (t_col >= t_row)).astype(jnp.bfloat16)
    s_iota = lax.broadcasted_iota(jnp.int32, (sub, LANES), 0)
    idx = lax.broadcasted_iota(jnp.int32, (c, LANES), 0)
    head_of_k = lax.broadcasted_iota(jnp.int32, (B_QK, B_WIDTH), 0) // B_DK
    head_of_v = lax.broadcasted_iota(jnp.int32, (B_QK, B_WIDTH), 1) // B_DV
    expand = (head_of_k == head_of_v).astype(MXU_DTYPE)
    same_head_t = (lax.broadcasted_iota(jnp.int32, (B_WIDTH, B_QK), 0) // B_DV
                   == lax.broadcasted_iota(jnp.int32, (B_WIDTH, B_QK), 1) // B_DK)
    n_ref = n_sub - 1
    kt_keep = (lax.broadcasted_iota(jnp.int32, (B_HEADS * c, n_ref * B_QK), 0) // c
               == (lax.broadcasted_iota(jnp.int32, (B_HEADS * c, n_ref * B_QK), 1) % B_QK) // B_DK)
    vx_keep = (lax.broadcasted_iota(jnp.int32, (B_HEADS * c, B_WIDTH), 0) // c
               == lax.broadcasted_iota(jnp.int32, (B_HEADS * c, B_WIDTH), 1) // B_DV)
    pick = (lax.broadcasted_iota(jnp.int32, (c, c * sub), 1) // sub
            == lax.broadcasted_iota(jnp.int32, (c, c * sub), 0)).astype(MXU_DTYPE)

    def log_decay(src_ref, backward):
        gl = src_ref[0, :, 2 * B_QK + 2 * B_WIDTH:BG_WIDTH]
        d0 = B_QK if backward else 0
        z = jnp.dot(gl, wdec_ref[:, d0:d0 + B_QK], precision=HI, preferred_element_type=F32) \
            + bdec_ref[:, d0:d0 + B_QK]
        log_a = (jnp.minimum(z, 0.0) - jnp.log(1.0 + jnp.exp(-jnp.abs(z)))) / B_GATE_TAU
        tri = tri_b if backward else tri_f
        return sum(jnp.dot(tri, p, preferred_element_type=F32) for p in _split_bf16(log_a, 3))

    def chunk(slot, src_ref, b_all, lo, backward, st_ref, out_ref):
        q = src_ref[0, lo:lo + c, 0:B_QK] * (B_DK ** -0.5)
        k = src_ref[0, lo:lo + c, B_QK:2 * B_QK]
        v = src_ref[0, lo:lo + c, 2 * B_QK:2 * B_QK + B_WIDTH]
        b = b_all[lo:lo + c]
        b_ref, qs_ref, e_ref = bs_ref.at[slot], qss_ref.at[slot], es_ref.at[slot]
        b_ref[...] = b
        qs_ref[...] = q
        blk = ((c - 1 - idx) if backward else idx) // sub

        q_parts, k_parts = [], []
        for m, late, early in ((1, blk == 1, blk == 0), (2, blk >= 2, blk <= 1), (3, blk == 3, blk == 2)):
            r_row = (c - 1 - sub * m) if backward else sub * m
            r = b_ref[r_row:r_row + 1, :]
            q_parts.append(q * jnp.exp(jnp.where(late, b - r, -jnp.inf)))
            k_parts.append(k * jnp.exp(jnp.where(early, r - b, -jnp.inf)))
        q_cat = jnp.concatenate(q_parts, axis=1).astype(MXU_DTYPE)
        k_cat = jnp.concatenate(k_parts, axis=1)
        k_exp = jnp.where(kt_keep, jnp.concatenate([k_cat] * B_HEADS, axis=0), 0.0).astype(MXU_DTYPE)
        a_off = lax.dot_general(q_cat, k_exp, nt, preferred_element_type=F32)

        for tt in range(c):
            lo_s = tt // sub * sub
            keep = (s_iota >= tt - lo_s) if backward else (s_iota <= tt - lo_s)
            bt = b_ref[tt:tt + 1, :]
            qt = qs_ref[tt:tt + 1, :]
            e = jnp.exp(jnp.where(keep, bt - b[lo_s:lo_s + sub], -jnp.inf)) * (qt * k[lo_s:lo_s + sub])
            e_ref[tt * sub:(tt + 1) * sub, :] = e.astype(e_ref.dtype)
        a_exp = jnp.dot(e_ref[...], expand, preferred_element_type=F32)
        b_end = b[0:1, :] if backward else b[c - 1:c, :]
        kd = k * jnp.exp(b_end - b)
        upd = lax.dot_general(v, kd, (((0,), (0,)), ((), ())), preferred_element_type=F32)

        def intra():
            v_exp = jnp.where(vx_keep, jnp.concatenate([v] * B_HEADS, axis=0), 0.0).astype(MXU_DTYPE)
            o_off = jnp.dot(a_off.astype(MXU_DTYPE), v_exp, preferred_element_type=F32)
            prod = a_exp.reshape(n_sub, sub, sub, B_WIDTH) * v.reshape(n_sub, 1, sub, B_WIDTH)
            o_diag = jnp.dot(pick, prod.reshape(c * sub, B_WIDTH).astype(MXU_DTYPE),
                             preferred_element_type=F32)
            o_intra = o_off + o_diag

            def recur():
                st = st_ref[...]
                o_inter = lax.dot_general(q * jnp.exp(b), st, nt, preferred_element_type=F32)
                out_ref[0, lo:lo + c, :] = o_intra + o_inter
                st_ref[...] = jnp.exp(b_end) * st + jnp.where(same_head_t, upd, 0.0)

            return recur

        return intra

    b_fwd = log_decay(f_ref, False)
    b_bwd = log_decay(r_ref, True)
    stage = []
    for ci in range(n_chunks):
        stage.append(chunk(2 * ci, f_ref, b_fwd, ci * c, False, sf_ref, of_ref))
        stage.append(chunk(2 * ci + 1, r_ref, b_bwd, (n_chunks - 1 - ci) * c, True, sb_ref, ob_ref))
    stage = [intra() for intra in stage]
    for recur in stage:
        recur()


def _gla_scan(bg, w_dec, b_dec):
    bsz, t, _ = bg.shape
    n = t // TOK_TILE
    rev = lambda b, i: (b, jnp.where(i == 0, 0, n - i), 0)
    fwd = lambda b, i: (b, i, 0)
    wdec = jnp.zeros((GATE_PAD, 2 * B_QK), F32)
    wdec = wdec.at[:B_GATE_RANK, :B_QK].set(w_dec[0]).at[B_GATE_RANK:2 * B_GATE_RANK, B_QK:].set(w_dec[1])
    bdec = b_dec.reshape(1, 2 * B_QK)
    o_shape = jax.ShapeDtypeStruct((bsz, t, B_WIDTH), F32)
    n_slots = 2 * (TOK_TILE // B_CHUNK)
    return pl.pallas_call(
        _gla_body,
        grid=(bsz, n),
        in_specs=[pl.BlockSpec((1, TOK_TILE, BG_WIDTH), fwd),
                  pl.BlockSpec((1, TOK_TILE, BG_WIDTH), rev),
                  pl.BlockSpec((GATE_PAD, 2 * B_QK), lambda b, i: (0, 0)),
                  pl.BlockSpec((1, 2 * B_QK), lambda b, i: (0, 0))],
        out_specs=[pl.BlockSpec((1, TOK_TILE, B_WIDTH), fwd),
                   pl.BlockSpec((1, TOK_TILE, B_WIDTH), rev)],
        out_shape=[o_shape, o_shape],
        scratch_shapes=[pltpu.VMEM((B_WIDTH, B_QK), F32), pltpu.VMEM((B_WIDTH, B_QK), F32),
                        pltpu.VMEM((n_slots, B_CHUNK * GLA_SUB, LANES), MXU_DTYPE),
                        pltpu.VMEM((n_slots, B_CHUNK, LANES), F32),
                        pltpu.VMEM((n_slots, B_CHUNK, LANES), F32)],
        compiler_params=_cparams(("parallel", "arbitrary")),
        name="gla_scan",
    )(bg, bg, wdec, bdec)


def _na_bias_tables(rpb, rows):
    n_tiles = rows // NA_TILE_ROWS
    wr = min(NA_ROWS, rows)
    n_dr, n_dc = 2 * NA_ROWS - 1, 2 * NA_COLS - 1
    cq = np.arange(GRID_W)[:, None]
    ck = np.arange(GRID_W)[None, :]
    cs = np.clip(cq - NA_COLS // 2, 0, GRID_W - NA_COLS)
    col_ok = (ck >= cs) & (ck < cs + NA_COLS)
    dc = np.clip(ck - cq, -(NA_COLS - 1), NA_COLS - 1) + (NA_COLS - 1)
    onehot = (dc.reshape(1, -1) == np.arange(n_dc)[:, None]).astype(np.float32)
    by_col = jnp.dot(rpb.astype(F32).reshape(-1, n_dc), onehot, precision=HI)
    by_col = jnp.where(col_ok.reshape(1, 1, GRID_W, GRID_W),
                       by_col.reshape(C_HEADS, n_dr, GRID_W, GRID_W), -jnp.inf)
    masked = jnp.full((C_HEADS, GRID_W, GRID_W), -jnp.inf, F32)
    tabs = []
    for j in (0, 1, n_tiles - 1):
        kr0 = int(np.clip(j - 1, 0, n_tiles - NA_KEY_TILES)) * NA_TILE_ROWS
        q_rows = []
        for qr in range(NA_TILE_ROWS):
            r = j * NA_TILE_ROWS + qr
            start = int(np.clip(r - wr // 2, 0, rows - wr))
            blocks = []
            for kw in range(NA_KEY_TILES * NA_TILE_ROWS):
                kr = kr0 + kw
                blocks.append(by_col[:, kr - r + NA_ROWS - 1] if start <= kr < start + wr else masked)
            q_rows.append(jnp.concatenate(blocks, axis=-1))
        tabs.append(jnp.concatenate(q_rows, axis=1))
    return jnp.stack(tabs)


def _na_body(q_ref, k0_ref, k1_ref, k2_ref, kc_ref, v0_ref, v1_ref, v2_ref, vc_ref, m_ref, o_ref):
    q = q_ref[0]
    lane = lax.broadcasted_iota(jnp.int32, (1, LANES), 1)
    zero = jnp.zeros_like(q)
    nt = (((1,), (1,)), ((), ()))

    def scores(qm, k_ref):
        return lax.dot_general(qm, k_ref[0], nt, preferred_element_type=F32)

    def head_out(hh, windows):
        qm = jnp.where((lane >= hh * C_DH) & (lane < (hh + 1) * C_DH), q, zero)
        s = [scores(qm, kc_ref)]
        for w, k_ref in enumerate(windows):
            s.append(scores(qm, k_ref) + m_ref[0, hh, :, w * TOK_TILE:(w + 1) * TOK_TILE])

        def finish():
            m = functools.reduce(jnp.maximum, [jnp.max(x, axis=-1, keepdims=True) for x in s])
            p = [jnp.exp(x - m) for x in s]
            den = functools.reduce(jnp.add, [jnp.sum(x, axis=-1, keepdims=True) for x in p])
            vals = [vc_ref] + [v0_ref, v1_ref, v2_ref][:len(windows)]
            o = functools.reduce(jnp.add, [jnp.dot(x.astype(MXU_DTYPE), v_ref[0], preferred_element_type=F32)
                                           for x, v_ref in zip(p, vals)])
            return o * (1.0 / den)

        return finish

    def emit(windows):
        fin0 = head_out(0, windows)
        fin1 = head_out(1, windows)
        o_ref[0] = jnp.where(lane < C_DH, fin0(), fin1()).astype(o_ref.dtype)

    @pl.when(pl.program_id(1) == 0)
    def _():
        emit([])

    @pl.when(pl.program_id(1) > 0)
    def _():
        emit([k0_ref, k1_ref, k2_ref])


def _neighborhood_attention(qn, kn, vn, bias_tabs):
    bsz, t, _ = qn.shape
    n = t // TOK_TILE
    n_lat = n - 1

    def win(w):
        def index(hp, i, b):
            kb0 = jnp.clip(i - 2, 0, n_lat - NA_KEY_TILES)
            return (b, kb0 + 1 + w, hp)
        return pl.BlockSpec((1, TOK_TILE, LANES), index)

    def cls(hp, i, b):
        j = i - 1
        return (jnp.where(j <= 0, 0, jnp.where(j == n_lat - 1, 2, 1)), hp, 0, 0)

    own = pl.BlockSpec((1, TOK_TILE, LANES), lambda hp, i, b: (b, i, hp))
    ctx = pl.BlockSpec((1, TOK_TILE, LANES), lambda hp, i, b: (b, 0, hp))
    heads_per_step = LANES // C_DH
    return pl.pallas_call(
        _na_body,
        grid=(C_HEADS // heads_per_step, n, bsz),
        in_specs=[own, win(0), win(1), win(2), ctx, win(0), win(1), win(2), ctx,
                  pl.BlockSpec((1, heads_per_step, TOK_TILE, NA_KEY_TILES * TOK_TILE), cls)],
        out_specs=own,
        out_shape=jax.ShapeDtypeStruct((bsz, t, C_WIDTH), MXU_DTYPE),
        compiler_params=_cparams(("parallel", "parallel", "arbitrary")),
        name="neighborhood_attention",
    )(qn, kn, kn, kn, kn, vn, vn, vn, vn, bias_tabs)


def _out_proj_body(yac_ref, yal_ref, of_ref, ob_ref, r_ref, yn_ref, x_ref, mod_ref, gg_ref, g2_ref, w_ref,
                   wr_ref, br_ref, xo_ref, h_ref, rt_ref):
    ya = jnp.where(pl.program_id(1) == 0, yac_ref[0], yal_ref[0])
    o = of_ref[0] + ob_ref[0]
    hi = lax.broadcasted_iota(jnp.int32, (B_WIDTH, B_WIDTH), 0) // B_DV
    hj = lax.broadcasted_iota(jnp.int32, (B_WIDTH, B_WIDTH), 1) // B_DV
    head_mean = jnp.where(hi == hj, 1.0 / B_DV, 0.0).astype(jnp.bfloat16)
    ms = sum(jnp.dot(p, head_mean, preferred_element_type=F32) for p in _split_bf16(o * o, 2))
    yb = o * lax.rsqrt(ms + NORM_EPS) * gg_ref[...] * _silu(r_ref[0])
    mix = (jnp.dot(ya, w_ref[0:A_WIDTH, :], preferred_element_type=F32)
           + jnp.dot(yb.astype(MXU_DTYPE), w_ref[A_WIDTH:A_WIDTH + B_WIDTH, :], preferred_element_type=F32)
           + jnp.dot(yn_ref[0], w_ref[A_WIDTH + B_WIDTH:, :], preferred_element_type=F32))
    x = x_ref[0] + mod_ref[2:3, :] * mix
    xo_ref[0] = x
    h = _rms(x, g2_ref[...]) * (1.0 + mod_ref[4:5, :]) + mod_ref[3:4, :]
    h_ref[0] = _pack_bf16_pairs(h)
    h_hi, h_lo = _split_bf16(h, 2)
    both = jnp.dot(h_hi, wr_ref[...], preferred_element_type=F32)
    logits = (both[:, :ROUTER_PAD] + both[:, ROUTER_PAD:]
              + jnp.dot(h_lo, wr_ref[:, :ROUTER_PAD], preferred_element_type=F32) + br_ref[...])
    rt_ref[0] = _route(logits)


def _route(lg):
    lane = lax.broadcasted_iota(jnp.int32, lg.shape, 1)
    big = jnp.int32(ROUTER_PAD)

    def top(mask):
        v = jnp.max(jnp.where(mask, lg, -jnp.inf), axis=-1, keepdims=True)
        i = jnp.min(jnp.where(mask & (lg == v), lane, big), axis=-1, keepdims=True)
        return v, i

    g_mask = lane < N_GROUPS
    g_max, grp = top(g_mask)
    p_grp = 1.0 / jnp.sum(jnp.where(g_mask, jnp.exp(lg - g_max), 0.0), axis=-1, keepdims=True)
    e_lo = N_GROUPS + grp * EXPERTS_PER_GROUP
    e_mask = (lane >= e_lo) & (lane < e_lo + EXPERTS_PER_GROUP)
    v1, i1 = top(e_mask)
    v2, i2 = top(e_mask & (lane != i1))
    r = jnp.exp(v2 - v1)
    gate1 = p_grp / (1.0 + r)
    gate2 = p_grp * r / (1.0 + r)
    out = jnp.where(lane == ROUTE_EXPERT, (i1 - N_GROUPS).astype(F32),
                    jnp.where(lane == ROUTE_EXPERT + 1, (i2 - N_GROUPS).astype(F32),
                              jnp.where(lane == ROUTE_GATE, gate1,
                                        jnp.where(lane == ROUTE_GATE + 1, gate2, 0.0))))
    return out


def _out_proj(layer, ya_ctx, ya_lat, o_f, o_b, bg, yn, x, modsel, g_gla, g2, w_out_b, w_router, b_router):
    bsz, t, d = x.shape
    tm = TOK_TILE
    tok = lambda b, i: (b, i, 0)
    const = lambda b, i: (0, 0)
    r_block = (2 * B_QK + B_WIDTH) // B_WIDTH
    return pl.pallas_call(
        _out_proj_body,
        grid=(bsz, t // tm),
        in_specs=[pl.BlockSpec((1, tm, A_WIDTH), lambda b, i: (b, 0, 0)),
                  pl.BlockSpec((1, tm, A_WIDTH), lambda b, i: (b, jnp.maximum(i - 1, 0), 0)),
                  pl.BlockSpec((1, tm, B_WIDTH), tok), pl.BlockSpec((1, tm, B_WIDTH), tok),
                  pl.BlockSpec((1, tm, B_WIDTH), lambda b, i: (b, i, r_block)),
                  pl.BlockSpec((1, tm, C_WIDTH), tok),
                  pl.BlockSpec((1, tm, d), tok),
                  pl.BlockSpec((None, None, None, 6, d), lambda b, i: (layer, b, jnp.minimum(i, 1), 0, 0)),
                  pl.BlockSpec((1, B_WIDTH), const), pl.BlockSpec((1, d), const),
                  pl.BlockSpec(w_out_b.shape, const),
                  pl.BlockSpec((d, 2 * ROUTER_PAD), const), pl.BlockSpec((1, ROUTER_PAD), const)],
        out_specs=[pl.BlockSpec((1, tm, d), tok), pl.BlockSpec((1, tm, d // 2), tok),
                   pl.BlockSpec((1, tm, ROUTER_PAD), tok)],
        out_shape=[jax.ShapeDtypeStruct((bsz, t, d), F32), jax.ShapeDtypeStruct((bsz, t, d // 2), jnp.uint32),
                   jax.ShapeDtypeStruct((bsz, t, ROUTER_PAD), F32)],
        compiler_params=_cparams(("parallel", "arbitrary")),
        name="out_proj",
    )(ya_ctx, ya_lat, o_f, o_b, bg, yn, x, modsel, jnp.tile(g_gla, B_HEADS).reshape(1, B_WIDTH), g2.reshape(1, d),
      w_out_b, w_router, b_router)


def _dispatch_plan(expert, n_blocks):
    n_tok = expert.shape[0]
    flat = expert.reshape(-1)
    n_assign = flat.shape[0]
    order = jnp.argsort(flat).astype(jnp.int32)
    rank = jnp.argsort(order).astype(jnp.int32)
    counts = jnp.sum((flat[:, None] == jnp.arange(N_EXPERTS, dtype=jnp.int32)[None, :]).astype(jnp.int32), axis=0)
    padded = (counts + EXPERT_TILE - 1) // EXPERT_TILE * EXPERT_TILE
    start = jnp.cumsum(counts) - counts
    pad_end = jnp.cumsum(padded)
    pad_start = pad_end - padded
    shift = (pad_start - start).astype(jnp.int32)
    dest = rank + shift[flat]
    blk_start = jnp.arange(n_blocks, dtype=jnp.int32) * EXPERT_TILE
    blk_expert = jnp.sum((pad_end[None, :] <= blk_start[:, None]).astype(jnp.int32), axis=1)
    blk_expert = jnp.minimum(blk_expert, N_EXPERTS - 1)
    blk_used = (blk_start < pad_end[-1]).astype(jnp.int32)
    row = jnp.arange(n_blocks * EXPERT_TILE, dtype=jnp.int32)
    e_row = jnp.repeat(blk_expert, EXPERT_TILE)
    real = (row - pad_start[e_row].astype(jnp.int32)) < counts[e_row]
    src_assign = order[jnp.clip(row - shift[e_row], 0, n_assign - 1)]
    src_tok = jnp.where(real, src_assign // TOP_K, 0)
    dest = dest.reshape(n_tok, TOP_K).T.reshape(-1)
    return src_tok, dest, blk_expert, blk_used


def _sc_gather(table, idx):
    n_rows = idx.shape[0]
    width = table.shape[1]
    n_workers = SC_CORES * SC_SUBCORES
    per_worker = n_rows // n_workers
    chunk = SC_ROW_BUFFER_BYTES // (width * table.dtype.itemsize)
    n_chunks = per_worker // chunk
    assert per_worker * n_workers == n_rows and n_chunks * chunk == per_worker and n_chunks % 2 == 0
    assert n_chunks >= 4 and chunk <= LANES
    mesh = plsc.VectorSubcoreMesh(core_axis_name="core", subcore_axis_name="subcore")

    def body(table_hbm, idx_hbm, out_hbm, idx_v, rows_v, gather_sem, write_sem):
        worker = lax.axis_index("subcore") * SC_CORES + lax.axis_index("core")
        base = worker * per_worker
        pltpu.sync_copy(idx_hbm.at[worker], idx_v)

        def gather(j, slot):
            return pltpu.make_async_copy(table_hbm.at[idx_v.at[j]], rows_v.at[slot], gather_sem.at[slot])

        def write(j, slot):
            return pltpu.make_async_copy(rows_v.at[slot], out_hbm.at[pl.ds(base + j * chunk, chunk)],
                                         write_sem.at[slot])

        gather(0, 0).start()
        gather(0, 0).wait()
        gather(1, 1).start()
        write(0, 0).start()

        @pl.loop(1, n_chunks - 1, step=2)
        def _(j):
            for s in range(2):
                slot = (1 + s) % 2
                gather(j + s, slot).wait()
                write(j + s - 1, 1 - slot).wait()
                gather(j + s + 1, 1 - slot).start()
                write(j + s, slot).start()

        last = n_chunks - 1
        gather(last, 1).wait()
        write(last, 1).start()
        write(last - 1, 0).wait()
        write(last, 1).wait()

    return pl.kernel(
        body,
        out_type=jax.ShapeDtypeStruct((n_rows, width), table.dtype),
        mesh=mesh,
        scratch_types=[pltpu.VMEM((n_chunks, chunk), jnp.int32),
                       pltpu.VMEM((2, chunk, width), table.dtype),
                       pltpu.SemaphoreType.DMA((2,)), pltpu.SemaphoreType.DMA((2,))],
        name="sc_row_gather",
    )(table, idx.reshape(n_workers, n_chunks, chunk))


def _expert_body(be_ref, used_ref, x_ref, wu_ref, wd_ref, o_ref, wub_ref, wdb_ref):
    i = pl.program_id(0)
    prev = be_ref[jnp.maximum(i - 1, 0)]

    @pl.when((i == 0) | (be_ref[i] != prev))
    def _():
        wub_ref[...] = wu_ref[...].astype(wub_ref.dtype)
        wdb_ref[...] = wd_ref[...].astype(wdb_ref.dtype)

    @pl.when(used_ref[i] > 0)
    def _():
        lo, hi = _unpack_bf16_pairs(x_ref[...])
        half = lo.shape[1]
        gu = (jnp.dot(lo.astype(MXU_DTYPE), wub_ref[:half, :], preferred_element_type=F32)
              + jnp.dot(hi.astype(MXU_DTYPE), wub_ref[half:, :], preferred_element_type=F32))
        act = _silu(gu[:, :EXPERT_HIDDEN]) * gu[:, EXPERT_HIDDEN:]
        o_ref[...] = _pack_bf16_pairs(jnp.dot(act.astype(MXU_DTYPE), wdb_ref[...], preferred_element_type=F32))

    @pl.when(used_ref[i] == 0)
    def _():
        o_ref[...] = jnp.zeros_like(o_ref)


def _expert_ffn(layer, buf, blk_expert, blk_used, w_up, w_down):
    n_rows, packed = buf.shape
    d = 2 * packed
    n_blocks = n_rows // EXPERT_TILE
    h2 = w_up.shape[-1]
    grid_spec = pltpu.PrefetchScalarGridSpec(
        num_scalar_prefetch=2,
        grid=(n_blocks,),
        in_specs=[pl.BlockSpec((EXPERT_TILE, packed), lambda i, be, us: (i, 0)),
                  pl.BlockSpec((None, None, d, h2), lambda i, be, us: (layer, be[i], 0, 0)),
                  pl.BlockSpec((None, None, h2 // 2, d), lambda i, be, us: (layer, be[i], 0, 0))],
        out_specs=pl.BlockSpec((EXPERT_TILE, packed), lambda i, be, us: (i, 0)),
        scratch_shapes=[pltpu.VMEM((d, h2), MXU_DTYPE), pltpu.VMEM((h2 // 2, d), MXU_DTYPE)],
    )
    return pl.pallas_call(
        _expert_body,
        grid_spec=grid_spec,
        out_shape=jax.ShapeDtypeStruct((n_rows, packed), jnp.uint32),
        compiler_params=_cparams(("arbitrary",)),
        name="expert_ffn",
    )(blk_expert, blk_used, buf, w_up, w_down)


def _moe(layer, h2, route, w_up, w_down):
    bsz, t, packed = h2.shape
    n_tok = bsz * t
    n_assign = n_tok * TOP_K
    n_blocks = -(-(n_assign + N_EXPERTS * (EXPERT_TILE - 1)) // EXPERT_TILE)
    expert = route.reshape(n_tok, ROUTER_PAD)[:, ROUTE_EXPERT:ROUTE_EXPERT + TOP_K].astype(jnp.int32)
    src_tok, dest, blk_expert, blk_used = _dispatch_plan(expert, n_blocks)
    buf = _sc_gather(h2.reshape(n_tok, packed), src_tok)
    y = _expert_ffn(layer, buf, blk_expert, blk_used, w_up, w_down)
    return _sc_gather(y, dest).reshape(TOP_K, bsz, t, packed)


def _final_body(x_ref, y0_ref, y1_ref, rt_ref, mod_ref, g_ref, o_ref):
    o_ref[0] = _rms(_moe_residual(x_ref, y0_ref, y1_ref, rt_ref, mod_ref), g_ref[...])


def _final_norm(layer, x, ymoe, route, modsel, g, ctx_tiles):
    bsz, t, d = x.shape
    tm = TOK_TILE
    lat = lambda b, i: (b, i + ctx_tiles, 0)
    return pl.pallas_call(
        _final_body,
        grid=(bsz, t // tm - ctx_tiles),
        in_specs=[pl.BlockSpec((1, tm, d), lat)] + _moe_specs(tm, d, lambda i: i + ctx_tiles)
        + [pl.BlockSpec((None, None, None, 6, d), lambda b, i: (layer, b, 1, 0, 0)),
           pl.BlockSpec((1, d), lambda b, i: (0, 0))],
        out_specs=pl.BlockSpec((1, tm, d), lambda b, i: (b, i, 0)),
        out_shape=jax.ShapeDtypeStruct((bsz, t - ctx_tiles * tm, d), F32),
        compiler_params=_cparams(("parallel", "arbitrary")),
        name="final_norm",
    )(x, ymoe, ymoe, route, modsel, g.reshape(1, d))


def _rope_tables(n_ctx, n_lat):
    t = jnp.arange(n_lat)
    row = (t // GRID_W).astype(F32)
    col = (t % GRID_W).astype(F32)
    n_freq = HEAD_DIM // 4
    inv = ROPE_THETA ** (-jnp.arange(n_freq, dtype=F32) / n_freq)
    ang_r = row[:, None] * inv
    ang_c = col[:, None] * inv
    cr, sr, cc, sc = jnp.cos(ang_r), jnp.sin(ang_r), jnp.cos(ang_c), jnp.sin(ang_c)
    z = jnp.zeros_like(sr)
    cos = jnp.concatenate([cr, cr, cc, cc], axis=-1)
    above = jnp.concatenate([-sr, z, -sc, z], axis=-1)
    below = jnp.concatenate([z, sr, z, sc], axis=-1)
    reps = LANES // HEAD_DIM

    def full(tab, ctx_value):
        tab = jnp.tile(tab, (1, reps))
        return jnp.concatenate([jnp.full((n_ctx, LANES), ctx_value, F32), tab], axis=0)

    return full(cos, 1.0), full(above, 0.0), full(below, 0.0)


def _pack_w_in(w_in):
    parts = jnp.split(w_in, np.cumsum(IN_SIZES)[:-1].tolist(), axis=-1)
    qa, ka, va, qb, kb, vb, rb, gb, qn, kn, vn = parts
    gb = jnp.pad(gb, ((0, 0), (0, GATE_PAD - gb.shape[-1])))
    return jnp.concatenate([qa, ka, va, qb, kb, vb, rb, gb, qn, kn, vn], axis=-1).astype(MXU_DTYPE)


def kernel(x, c, ctx, c_ctx, w_mod, b_mod, norm1_g, norm2_g, w_in, w_out, diff_lambda, diff_sub_g,
           gla_w_decay, gla_b_decay, gla_norm_g, na_rel_bias, w_router_group, b_router_group,
           w_router_expert, b_router_expert, w_expert_up, w_expert_down, final_g):
    bsz, seq, d = x.shape
    n_ctx = ctx.shape[1]
    depth = w_mod.shape[0]
    assert n_ctx == TOK_TILE and seq % (NA_TILE_ROWS * GRID_W) == 0 and d % LANES == 0
    assert seq // (NA_TILE_ROWS * GRID_W) >= NA_KEY_TILES
    mod_rows = -(-(bsz + 1) // 8) * 8
    cvec = jnp.zeros((mod_rows, d), F32).at[:bsz].set(c).at[bsz].set(c_ctx)
    mod = _modulation(cvec, w_mod, b_mod).reshape(depth, mod_rows, 6, d)
    modsel = jnp.stack([jnp.broadcast_to(mod[:, bsz][:, None], (depth, bsz, 6, d)), mod[:, :bsz]], axis=2)
    rope_tabs = _rope_tables(n_ctx, seq)
    xt = jnp.concatenate([ctx, x], axis=1)
    n_groups = BATCH_GROUPS if bsz % BATCH_GROUPS == 0 else 1
    gb = bsz // n_groups
    groups = [dict(xt=xt[g * gb:(g + 1) * gb], mod=modsel[:, g * gb:(g + 1) * gb], ymoe=None, route=None)
              for g in range(n_groups)]
    for l in range(depth):
        w_in_p = _pack_w_in(w_in[l])
        bias_tabs = _na_bias_tables(na_rel_bias[l], seq // GRID_W)
        w_router = jnp.pad(jnp.concatenate([w_router_group[l], w_router_expert[l]], axis=-1),
                           ((0, 0), (0, ROUTER_PAD - N_GROUPS - N_EXPERTS)))
        w_router = jnp.concatenate(_split_bf16(w_router, 2), axis=-1)
        b_router = jnp.pad(jnp.concatenate([b_router_group[l], b_router_expert[l]]),
                           (0, ROUTER_PAD - N_GROUPS - N_EXPERTS)).reshape(1, ROUTER_PAD)
        w_out_b = w_out[l].astype(MXU_DTYPE)
        for st in groups:
            outs = _in_proj(l, st["xt"], st["ymoe"], st["route"], st["mod"], norm1_g[l], w_in_p, rope_tabs)
            qa, ka, va, bg, qn, kn, vn = outs[:7]
            if st["ymoe"] is not None:
                st["xt"] = outs[7]
            ya_ctx, ya_lat = _diff_attention(l, qa, ka, va, diff_lambda[l], diff_sub_g[l])
            o_f, o_b = _gla_scan(bg, gla_w_decay[l], gla_b_decay[l])
            yn = _neighborhood_attention(qn, kn, vn, bias_tabs)
            st["xt"], h2, st["route"] = _out_proj(l, ya_ctx, ya_lat, o_f, o_b, bg, yn, st["xt"], st["mod"], gla_norm_g[l],
                                                  norm2_g[l], w_out_b, w_router, b_router)
            st["ymoe"] = _moe(l, h2, st["route"], w_expert_up, w_expert_down)
    outs = [_final_norm(depth - 1, st["xt"], st["ymoe"], st["route"], st["mod"], final_g, n_ctx // TOK_TILE)
            for st in groups]
    return jnp.concatenate(outs, axis=0)
```

```python
import functools
import math

import numpy as np
import jax
import jax.numpy as jnp
from jax import lax
from jax.experimental import pallas as pl
from jax.experimental.pallas import tpu as pltpu
from jax.experimental.pallas import tpu_sc as plsc

F32 = jnp.float32
MXU_DTYPE = jnp.bfloat16
HI = lax.Precision.HIGHEST

GRID_W = 64
HEAD_DIM = 64
ROPE_THETA = 10000.0
NORM_EPS = 1e-6

A_HEADS = 4
A_QK = HEAD_DIM
A_V = 2 * HEAD_DIM
B_HEADS = 4
B_DK = HEAD_DIM // 2
B_DV = HEAD_DIM
B_GATE_RANK = 16
B_GATE_TAU = 16.0
B_CHUNK = 64
LOG2_E = math.log2(math.e)
GLA_SUB = 16
C_HEADS = 4
C_DH = HEAD_DIM
NA_ROWS = 8
NA_COLS = 16

A_WIDTH = A_HEADS * A_V
B_WIDTH = B_HEADS * B_DV
C_WIDTH = C_HEADS * C_DH
B_QK = B_HEADS * B_DK
IN_SIZES = (A_HEADS * 2 * A_QK, A_HEADS * 2 * A_QK, A_WIDTH,
            B_QK, B_QK, B_WIDTH, B_WIDTH, 2 * B_GATE_RANK,
            C_WIDTH, C_WIDTH, C_WIDTH)

N_GROUPS = 4
EXPERTS_PER_GROUP = 8
N_EXPERTS = N_GROUPS * EXPERTS_PER_GROUP
TOP_K = 2
EXPERT_HIDDEN = 512

LANES = 128
TOK_TILE = 256
NA_TILE_ROWS = 4
NA_KEY_TILES = 3
BATCH_GROUPS = 1
EXPERT_TILE = 256
GATE_PAD = LANES
BG_WIDTH = 2 * B_QK + 2 * B_WIDTH + GATE_PAD
IN_PAD_WIDTH = 3 * A_WIDTH + BG_WIDTH + 3 * C_WIDTH
ROUTER_PAD = LANES
ROUTE_EXPERT = 0
ROUTE_GATE = 2
ROUTE_RANK = 4
SC_CORES = 2
SC_SUBCORES = 16
SC_ROW_BUFFER_BYTES = 128 * 1024
VMEM_LIMIT = 48 * 1024 * 1024


def _split_bf16(x, pieces):
    out = []
    for _ in range(pieces):
        p = x.astype(jnp.bfloat16)
        out.append(p)
        x = x - p.astype(F32)
    return out


def _pack_bf16_pairs(x):
    w = x.shape[1] // 2
    bits = lax.bitcast_convert_type(x.astype(jnp.bfloat16).astype(F32), jnp.uint32)
    return (bits[:, :w] >> 16) | (bits[:, w:] & jnp.uint32(0xFFFF0000))


def _unpack_bf16_pairs(u):
    lo = lax.bitcast_convert_type(u << 16, F32)
    hi = lax.bitcast_convert_type(u & jnp.uint32(0xFFFF0000), F32)
    return lo, hi


def _silu(x):
    return x * (1.0 / (1.0 + jnp.exp(-x)))


def _cparams(sem):
    return pltpu.CompilerParams(dimension_semantics=sem, vmem_limit_bytes=VMEM_LIMIT)


def _mod_body(c_ref, w_ref, b_ref, o_ref):
    a = _silu(c_ref[...])
    o_ref[...] = jnp.dot(a, w_ref[...], precision=HI, preferred_element_type=F32) + b_ref[...]


def _modulation(cvec, w_mod, b_mod):
    depth, d, d6 = w_mod.shape
    rows = cvec.shape[0]
    return pl.pallas_call(
        _mod_body,
        grid=(depth, d6 // d),
        in_specs=[pl.BlockSpec((rows, d), lambda l, j: (0, 0)),
                  pl.BlockSpec((None, d, d), lambda l, j: (l, 0, j)),
                  pl.BlockSpec((None, 1, d), lambda l, j: (l, 0, j))],
        out_specs=pl.BlockSpec((None, rows, d), lambda l, j: (l, 0, j)),
        out_shape=jax.ShapeDtypeStruct((depth, rows, d6), F32),
        compiler_params=_cparams(("arbitrary", "arbitrary")),
        name="modulation",
    )(cvec, w_mod, b_mod.reshape(depth, 1, d6))


def _rms(x, g):
    return x * lax.rsqrt(jnp.mean(x * x, axis=-1, keepdims=True) + NORM_EPS) * g


def _rope(x, cos, sa, sb):
    return x * cos + pltpu.roll(x, LANES - 16, 1) * sa + pltpu.roll(x, 16, 1) * sb


def _moe_residual(x_ref, y0_ref, y1_ref, rt_ref, mod_ref):
    rt = rt_ref[0]
    g0, g1 = rt[:, ROUTE_GATE:ROUTE_GATE + 1], rt[:, ROUTE_GATE + 1:ROUTE_GATE + 2]
    lo0, hi0 = _unpack_bf16_pairs(y0_ref[0])
    lo1, hi1 = _unpack_bf16_pairs(y1_ref[0])
    moe = jnp.concatenate([g0 * lo0 + g1 * lo1, g0 * hi0 + g1 * hi1], axis=1)
    return x_ref[0] + mod_ref[5:6, :] * moe


def _in_proj_body(combine, *refs):
    if combine:
        (x_ref, y0_ref, y1_ref, rt_ref, pmod_ref, mod_ref, g_ref, w_ref, cos_ref, sa_ref, sb_ref,
         qa_ref, ka_ref, va_ref, bg_ref, qn_ref, kn_ref, vn_ref, xo_ref) = refs
        x = _moe_residual(x_ref, y0_ref, y1_ref, rt_ref, pmod_ref)
        xo_ref[0] = x
    else:
        (x_ref, mod_ref, g_ref, w_ref, cos_ref, sa_ref, sb_ref,
         qa_ref, ka_ref, va_ref, bg_ref, qn_ref, kn_ref, vn_ref) = refs
        x = x_ref[0]
    h = _rms(x, g_ref[...]) * (1.0 + mod_ref[1:2, :]) + mod_ref[0:1, :]
    hb = h.astype(MXU_DTYPE)

    def proj(lo, hi):
        return jnp.dot(hb, w_ref[:, lo:hi], preferred_element_type=F32)

    cos, sa, sb = cos_ref[...], sa_ref[...], sb_ref[...]
    for hh in range(A_HEADS):
        lo = hh * LANES
        q = proj(lo, lo + LANES)
        qa_ref[0, :, lo:lo + LANES] = (_rope(q, cos, sa, sb) * (A_QK ** -0.5 * LOG2_E)).astype(qa_ref.dtype)
        k = proj(A_WIDTH + lo, A_WIDTH + lo + LANES)
        ka_ref[0, :, lo:lo + LANES] = _rope(k, cos, sa, sb).astype(ka_ref.dtype)
    o = 2 * A_WIDTH
    va = proj(o, o + A_WIDTH)
    ones = jnp.ones((va.shape[0], A_V), F32)
    va_ref[0] = jnp.concatenate([piece for hh in range(A_HEADS)
                                 for piece in (va[:, hh * A_V:(hh + 1) * A_V], ones)],
                                axis=1).astype(va_ref.dtype)
    o += A_WIDTH
    bg_ref[0] = proj(o, o + BG_WIDTH)
    o += BG_WIDTH
    qn_ref[0] = (proj(o, o + C_WIDTH) * (C_DH ** -0.5)).astype(qn_ref.dtype)
    o += C_WIDTH
    kn_ref[0] = proj(o, o + C_WIDTH).astype(kn_ref.dtype)
    o += C_WIDTH
    vn_ref[0] = proj(o, o + C_WIDTH).astype(vn_ref.dtype)


def _moe_specs(tm, d, row_block):
    return [pl.BlockSpec((None, 1, tm, d // 2), lambda b, i: (0, b, row_block(i), 0)),
            pl.BlockSpec((None, 1, tm, d // 2), lambda b, i: (1, b, row_block(i), 0)),
            pl.BlockSpec((1, tm, ROUTER_PAD), lambda b, i: (b, row_block(i), 0))]


def _in_proj(layer, x, ymoe, route, modsel, g1, w_in_p, rope_tabs):
    bsz, t, d = x.shape
    tm = TOK_TILE
    combine = ymoe is not None
    tok = lambda b, i: (b, i, 0)
    x_spec = pl.BlockSpec((1, tm, d), tok)

    def mod_spec(l):
        return pl.BlockSpec((None, None, None, 6, d), lambda b, i: (l, b, jnp.minimum(i, 1), 0, 0))

    tab_spec = pl.BlockSpec((tm, LANES), lambda b, i: (i, 0))
    in_specs = [x_spec]
    args = [x]
    if combine:
        in_specs += _moe_specs(tm, d, lambda i: i) + [mod_spec(layer - 1)]
        args += [ymoe, ymoe, route, modsel]
    in_specs += [mod_spec(layer), pl.BlockSpec((1, d), lambda b, i: (0, 0)),
                 pl.BlockSpec((d, IN_PAD_WIDTH), lambda b, i: (0, 0)), tab_spec, tab_spec, tab_spec]
    args += [modsel, g1.reshape(1, d), w_in_p, *rope_tabs]

    def o(width, dtype):
        return pl.BlockSpec((1, tm, width), tok), jax.ShapeDtypeStruct((bsz, t, width), dtype)

    outs = [o(A_WIDTH, MXU_DTYPE), o(A_WIDTH, MXU_DTYPE), o(2 * A_WIDTH, MXU_DTYPE), o(BG_WIDTH, F32),
            o(C_WIDTH, MXU_DTYPE), o(C_WIDTH, MXU_DTYPE), o(C_WIDTH, MXU_DTYPE)]
    if combine:
        outs.append(o(d, F32))
    return pl.pallas_call(
        functools.partial(_in_proj_body, combine),
        grid=(bsz, t // tm),
        in_specs=in_specs,
        out_specs=[s for s, _ in outs],
        out_shape=[s for _, s in outs],
        compiler_params=_cparams(("parallel", "arbitrary")),
        name="in_proj",
    )(*args)


def _diff_rows(lam_init, q, k_ref, v_ref, n_keys, lam, g):
    lane = lax.broadcasted_iota(jnp.int32, (1, LANES), 1)
    zero = jnp.zeros_like(q)
    nt = (((1,), (1,)), ((), ()))
    k = k_ref[0, :n_keys, :]
    s1 = lax.dot_general(jnp.where(lane < A_QK, q, zero), k, nt, preferred_element_type=F32)
    s2 = lax.dot_general(jnp.where(lane >= A_QK, q, zero), k, nt, preferred_element_type=F32)

    def finish():
        v1 = v_ref[0, :n_keys, :]
        outs = []
        for s in (s1, s2):
            p = jnp.exp2((s - jnp.max(s, axis=-1, keepdims=True)).astype(MXU_DTYPE))
            outs.append(jnp.dot(p, v1, preferred_element_type=F32))
        o = (outs[0][:, :A_V] * (1.0 / outs[0][:, A_V:A_V + 1])
             - outs[1][:, :A_V] * (lam / outs[1][:, A_V:A_V + 1]))
        return _rms(o, g) * (1.0 - lam_init)

    return finish


def _diff_attn_body(lam_init, ctx_len, qa_ref, qb_ref, k_ref, v_ref, lam_ref, g_ref, oc_ref, ol_ref):
    lm = lam_ref[...]
    lam = (jnp.exp(jnp.sum(lm[0:1] * lm[1:2], axis=1, keepdims=True))
           - jnp.exp(jnp.sum(lm[2:3] * lm[3:4], axis=1, keepdims=True)) + lam_init)
    g = g_ref[...]
    rows = qa_ref.shape[1]

    @pl.when(pl.program_id(2) == 0)
    def _():
        oc_ref[0] = _diff_rows(lam_init, qa_ref[0], k_ref, v_ref, ctx_len, lam, g)().astype(oc_ref.dtype)

    @pl.when(pl.program_id(2) > 0)
    def _():
        n_keys = k_ref.shape[1]
        fin_a = _diff_rows(lam_init, qa_ref[0], k_ref, v_ref, n_keys, lam, g)
        fin_b = _diff_rows(lam_init, qb_ref[0], k_ref, v_ref, n_keys, lam, g)
        ol_ref[0, :rows, :] = fin_a().astype(ol_ref.dtype)
        ol_ref[0, rows:, :] = fin_b().astype(ol_ref.dtype)


def _diff_attention(layer, qa, ka, va, lam, g_sub):
    bsz, t, _ = qa.shape
    tq = TOK_TILE
    n_lat = (t - tq) // (2 * tq)
    assert n_lat * 2 * tq == t - tq
    lam_init = 0.8 - 0.6 * math.exp(-0.3 * layer)

    def q_spec(off):
        return pl.BlockSpec((1, tq, LANES), lambda b, h, i: (b, jnp.maximum(2 * i + off, 0), h))

    return pl.pallas_call(
        functools.partial(_diff_attn_body, lam_init, TOK_TILE),
        grid=(bsz, A_HEADS, 1 + n_lat),
        in_specs=[q_spec(-1), q_spec(0),
                  pl.BlockSpec((1, t, LANES), lambda b, h, i: (b, 0, h)),
                  pl.BlockSpec((1, t, 2 * A_V), lambda b, h, i: (b, 0, h)),
                  pl.BlockSpec((4, A_QK), lambda b, h, i: (0, 0)),
                  pl.BlockSpec((1, A_V), lambda b, h, i: (0, 0))],
        out_specs=[pl.BlockSpec((1, tq, LANES), lambda b, h, i: (b, 0, h)),
                   pl.BlockSpec((1, 2 * tq, LANES), lambda b, h, i: (b, jnp.maximum(i - 1, 0), h))],
        out_shape=[jax.ShapeDtypeStruct((bsz, tq, A_WIDTH), MXU_DTYPE),
                   jax.ShapeDtypeStruct((bsz, t - tq, A_WIDTH), MXU_DTYPE)],
        compiler_params=_cparams(("parallel", "parallel", "arbitrary")),
        name="diff_attention",
    )(qa, qa, ka, va, lam, g_sub.reshape(1, A_V))


def _gla_body(f_ref, r_ref, wdec_ref, bdec_ref, of_ref, ob_ref, sf_ref, sb_ref, es_ref, bs_ref, qss_ref):
    c = B_CHUNK
    n_chunks = TOK_TILE // c

    @pl.when(pl.program_id(1) == 0)
    def _():
        sf_ref[...] = jnp.zeros_like(sf_ref)
        sb_ref[...] = jnp.zeros_like(sb_ref)

    sub = GLA_SUB
    n_sub = c // sub
    nt = (((1,), (1,)), ((), ()))
    t_row = lax.broadcasted_iota(jnp.int32, (TOK_TILE, TOK_TILE), 0)
    t_col = lax.broadcasted_iota(jnp.int32, (TOK_TILE, TOK_TILE), 1)
    same_chunk = (t_row // c) == (t_col // c)
    tri_f = (same_chunk & (t_col <= t_row)).astype(jnp.bfloat16)
    tri_b = (same_chunk & (t_col >= t_row)).astype(jnp.bfloat16)
    s_iota = lax.broadcasted_iota(jnp.int32, (sub, LANES), 0)
    idx = lax.broadcasted_iota(jnp.int32, (c, LANES), 0)
    head_of_k = lax.broadcasted_iota(jnp.int32, (B_QK, B_WIDTH), 0) // B_DK
    head_of_v = lax.broadcasted_iota(jnp.int32, (B_QK, B_WIDTH), 1) // B_DV
    expand = (head_of_k == head_of_v).astype(MXU_DTYPE)
    same_head_t = (lax.broadcasted_iota(jnp.int32, (B_WIDTH, B_QK), 0) // B_DV
                   == lax.broadcasted_iota(jnp.int32, (B_WIDTH, B_QK), 1) // B_DK)
    n_ref = n_sub - 1
    kt_keep = (lax.broadcasted_iota(jnp.int32, (B_HEADS * c, n_ref * B_QK), 0) // c
               == (lax.broadcasted_iota(jnp.int32, (B_HEADS * c, n_ref * B_QK), 1) % B_QK) // B_DK)
    vx_keep = (lax.broadcasted_iota(jnp.int32, (B_HEADS * c, B_WIDTH), 0) // c
               == lax.broadcasted_iota(jnp.int32, (B_HEADS * c, B_WIDTH), 1) // B_DV)
    pick = (lax.broadcasted_iota(jnp.int32, (c, c * sub), 1) // sub
            == lax.broadcasted_iota(jnp.int32, (c, c * sub), 0)).astype(MXU_DTYPE)

    def log_decay(src_ref, backward):
        gl = src_ref[0, :, 2 * B_QK + 2 * B_WIDTH:BG_WIDTH]
        d0 = B_QK if backward else 0
        z = jnp.dot(gl, wdec_ref[:, d0:d0 + B_QK], precision=HI, preferred_element_type=F32) \
            + bdec_ref[:, d0:d0 + B_QK]
        log_a = (jnp.minimum(z, 0.0) - jnp.log(1.0 + jnp.exp(-jnp.abs(z)))) / B_GATE_TAU
        tri = tri_b if backward else tri_f
        return sum(jnp.dot(tri, p, preferred_element_type=F32) for p in _split_bf16(log_a, 3))

    def chunk(slot, src_ref, b_all, lo, backward, st_ref, out_ref):
        q = src_ref[0, lo:lo + c, 0:B_QK] * (B_DK ** -0.5)
        k = src_ref[0, lo:lo + c, B_QK:2 * B_QK]
        v = src_ref[0, lo:lo + c, 2 * B_QK:2 * B_QK + B_WIDTH]
        b = b_all[lo:lo + c]
        b_ref, qs_ref, e_ref = bs_ref.at[slot], qss_ref.at[slot], es_ref.at[slot]
        b_ref[...] = b
        qs_ref[...] = q
        blk = ((c - 1 - idx) if backward else idx) // sub

        q_parts, k_parts = [], []
        for m, late, early in ((1, blk == 1, blk == 0), (2, blk >= 2, blk <= 1), (3, blk == 3, blk == 2)):
            r_row = (c - 1 - sub * m) if backward else sub * m
            r = b_ref[r_row:r_row + 1, :]
            q_parts.append(q * jnp.exp(jnp.where(late, b - r, -jnp.inf)))
            k_parts.append(k * jnp.exp(jnp.where(early, r - b, -jnp.inf)))
        q_cat = jnp.concatenate(q_parts, axis=1).astype(MXU_DTYPE)
        k_cat = jnp.concatenate(k_parts, axis=1)
        k_exp = jnp.where(kt_keep, jnp.concatenate([k_cat] * B_HEADS, axis=0), 0.0).astype(MXU_DTYPE)
        a_off = lax.dot_general(q_cat, k_exp, nt, preferred_element_type=F32)

        for tt in range(c):
            lo_s = tt // sub * sub
            keep = (s_iota >= tt - lo_s) if backward else (s_iota <= tt - lo_s)
            bt = b_ref[tt:tt + 1, :]
            qt = qs_ref[tt:tt + 1, :]
            e = jnp.exp(jnp.where(keep, bt - b[lo_s:lo_s + sub], -jnp.inf)) * (qt * k[lo_s:lo_s + sub])
            e_ref[tt * sub:(tt + 1) * sub, :] = e.astype(e_ref.dtype)
        a_exp = jnp.dot(e_ref[...], expand, preferred_element_type=F32)
        b_end = b[0:1, :] if backward else b[c - 1:c, :]
        kd = k * jnp.exp(b_end - b)
        upd = lax.dot_general(v, kd, (((0,), (0,)), ((), ())), preferred_element_type=F32)

        def intra():
            v_exp = jnp.where(vx_keep, jnp.concatenate([v] * B_HEADS, axis=0), 0.0).astype(MXU_DTYPE)
            o_off = jnp.dot(a_off.astype(MXU_DTYPE), v_exp, preferred_element_type=F32)
            prod = a_exp.reshape(n_sub, sub, sub, B_WIDTH) * v.reshape(n_sub, 1, sub, B_WIDTH)
            o_diag = jnp.dot(pick, prod.reshape(c * sub, B_WIDTH).astype(MXU_DTYPE),
                             preferred_element_type=F32)
            o_intra = o_off + o_diag

            def recur():
                st = st_ref[...]
                o_inter = lax.dot_general(q * jnp.exp(b), st, nt, preferred_element_type=F32)
                out_ref[0, lo:lo + c, :] = o_intra + o_inter
                st_ref[...] = jnp.exp(b_end) * st + jnp.where(same_head_t, upd, 0.0)

            return recur

        return intra

    b_fwd = log_decay(f_ref, False)
    b_bwd = log_decay(r_ref, True)
    stage = []
    for ci in range(n_chunks):
        stage.append(chunk(2 * ci, f_ref, b_fwd, ci * c, False, sf_ref, of_ref))
        stage.append(chunk(2 * ci + 1, r_ref, b_bwd, (n_chunks - 1 - ci) * c, True, sb_ref, ob_ref))
    stage = [intra() for intra in stage]
    for recur in stage:
        recur()


def _gla_scan(bg, w_dec, b_dec):
    bsz, t, _ = bg.shape
    n = t // TOK_TILE
    rev = lambda b, i: (b, jnp.where(i == 0, 0, n - i), 0)
    fwd = lambda b, i: (b, i, 0)
    wdec = jnp.zeros((GATE_PAD, 2 * B_QK), F32)
    wdec = wdec.at[:B_GATE_RANK, :B_QK].set(w_dec[0]).at[B_GATE_RANK:2 * B_GATE_RANK, B_QK:].set(w_dec[1])
    bdec = b_dec.reshape(1, 2 * B_QK)
    o_shape = jax.ShapeDtypeStruct((bsz, t, B_WIDTH), F32)
    n_slots = 2 * (TOK_TILE // B_CHUNK)
    return pl.pallas_call(
        _gla_body,
        grid=(bsz, n),
        in_specs=[pl.BlockSpec((1, TOK_TILE, BG_WIDTH), fwd),
                  pl.BlockSpec((1, TOK_TILE, BG_WIDTH), rev),
                  pl.BlockSpec((GATE_PAD, 2 * B_QK), lambda b, i: (0, 0)),
                  pl.BlockSpec((1, 2 * B_QK), lambda b, i: (0, 0))],
        out_specs=[pl.BlockSpec((1, TOK_TILE, B_WIDTH), fwd),
                   pl.BlockSpec((1, TOK_TILE, B_WIDTH), rev)],
        out_shape=[o_shape, o_shape],
        scratch_shapes=[pltpu.VMEM((B_WIDTH, B_QK), F32), pltpu.VMEM((B_WIDTH, B_QK), F32),
                        pltpu.VMEM((n_slots, B_CHUNK * GLA_SUB, LANES), MXU_DTYPE),
                        pltpu.VMEM((n_slots, B_CHUNK, LANES), F32),
                        pltpu.VMEM((n_slots, B_CHUNK, LANES), F32)],
        compiler_params=_cparams(("parallel", "arbitrary")),
        name="gla_scan",
    )(bg, bg, wdec, bdec)


def _na_bias_tables(rpb, rows):
    n_tiles = rows // NA_TILE_ROWS
    wr = min(NA_ROWS, rows)
    n_dr, n_dc = 2 * NA_ROWS - 1, 2 * NA_COLS - 1
    cq = np.arange(GRID_W)[:, None]
    ck = np.arange(GRID_W)[None, :]
    cs = np.clip(cq - NA_COLS // 2, 0, GRID_W - NA_COLS)
    col_ok = (ck >= cs) & (ck < cs + NA_COLS)
    dc = np.clip(ck - cq, -(NA_COLS - 1), NA_COLS - 1) + (NA_COLS - 1)
    onehot = (dc.reshape(1, -1) == np.arange(n_dc)[:, None]).astype(np.float32)
    by_col = jnp.dot(rpb.astype(F32).reshape(-1, n_dc), onehot, precision=HI)
    by_col = jnp.where(col_ok.reshape(1, 1, GRID_W, GRID_W),
                       by_col.reshape(C_HEADS, n_dr, GRID_W, GRID_W), -jnp.inf)
    masked = jnp.full((C_HEADS, GRID_W, GRID_W), -jnp.inf, F32)
    tabs = []
    for j in (0, 1, n_tiles - 1):
        kr0 = int(np.clip(j - 1, 0, n_tiles - NA_KEY_TILES)) * NA_TILE_ROWS
        q_rows = []
        for qr in range(NA_TILE_ROWS):
            r = j * NA_TILE_ROWS + qr
            start = int(np.clip(r - wr // 2, 0, rows - wr))
            blocks = []
            for kw in range(NA_KEY_TILES * NA_TILE_ROWS):
                kr = kr0 + kw
                blocks.append(by_col[:, kr - r + NA_ROWS - 1] if start <= kr < start + wr else masked)
            q_rows.append(jnp.concatenate(blocks, axis=-1))
        tabs.append(jnp.concatenate(q_rows, axis=1))
    return jnp.stack(tabs)


def _na_body(q_ref, k0_ref, k1_ref, k2_ref, kc_ref, v0_ref, v1_ref, v2_ref, vc_ref, m_ref, o_ref):
    q = q_ref[0]
    lane = lax.broadcasted_iota(jnp.int32, (1, LANES), 1)
    zero = jnp.zeros_like(q)
    nt = (((1,), (1,)), ((), ()))

    def scores(qm, k_ref):
        return lax.dot_general(qm, k_ref[0], nt, preferred_element_type=F32)

    def head_out(hh, windows):
        qm = jnp.where((lane >= hh * C_DH) & (lane < (hh + 1) * C_DH), q, zero)
        s = [scores(qm, kc_ref)]
        for w, k_ref in enumerate(windows):
            s.append(scores(qm, k_ref) + m_ref[0, hh, :, w * TOK_TILE:(w + 1) * TOK_TILE])

        def finish():
            m = functools.reduce(jnp.maximum, [jnp.max(x, axis=-1, keepdims=True) for x in s])
            p = [jnp.exp(x - m) for x in s]
            den = functools.reduce(jnp.add, [jnp.sum(x, axis=-1, keepdims=True) for x in p])
            vals = [vc_ref] + [v0_ref, v1_ref, v2_ref][:len(windows)]
            o = functools.reduce(jnp.add, [jnp.dot(x.astype(MXU_DTYPE), v_ref[0], preferred_element_type=F32)
                                           for x, v_ref in zip(p, vals)])
            return o * (1.0 / den)

        return finish

    def emit(windows):
        fin0 = head_out(0, windows)
        fin1 = head_out(1, windows)
        o_ref[0] = jnp.where(lane < C_DH, fin0(), fin1()).astype(o_ref.dtype)

    @pl.when(pl.program_id(1) == 0)
    def _():
        emit([])

    @pl.when(pl.program_id(1) > 0)
    def _():
        emit([k0_ref, k1_ref, k2_ref])


def _neighborhood_attention(qn, kn, vn, bias_tabs):
    bsz, t, _ = qn.shape
    n = t // TOK_TILE
    n_lat = n - 1

    def win(w):
        def index(hp, i, b):
            kb0 = jnp.clip(i - 2, 0, n_lat - NA_KEY_TILES)
            return (b, kb0 + 1 + w, hp)
        return pl.BlockSpec((1, TOK_TILE, LANES), index)

    def cls(hp, i, b):
        j = i - 1
        return (jnp.where(j <= 0, 0, jnp.where(j == n_lat - 1, 2, 1)), hp, 0, 0)

    own = pl.BlockSpec((1, TOK_TILE, LANES), lambda hp, i, b: (b, i, hp))
    ctx = pl.BlockSpec((1, TOK_TILE, LANES), lambda hp, i, b: (b, 0, hp))
    heads_per_step = LANES // C_DH
    return pl.pallas_call(
        _na_body,
        grid=(C_HEADS // heads_per_step, n, bsz),
        in_specs=[own, win(0), win(1), win(2), ctx, win(0), win(1), win(2), ctx,
                  pl.BlockSpec((1, heads_per_step, TOK_TILE, NA_KEY_TILES * TOK_TILE), cls)],
        out_specs=own,
        out_shape=jax.ShapeDtypeStruct((bsz, t, C_WIDTH), MXU_DTYPE),
        compiler_params=_cparams(("parallel", "parallel", "arbitrary")),
        name="neighborhood_attention",
    )(qn, kn, kn, kn, kn, vn, vn, vn, vn, bias_tabs)


def _out_proj_body(yac_ref, yal_ref, of_ref, ob_ref, r_ref, yn_ref, x_ref, mod_ref, gg_ref, g2_ref, w_ref,
                   wr_ref, br_ref, xo_ref, h_ref, rt_ref, cnt_ref, count_ref):
    ya = jnp.where(pl.program_id(1) == 0, yac_ref[0], yal_ref[0])
    o = of_ref[0] + ob_ref[0]
    hi = lax.broadcasted_iota(jnp.int32, (B_WIDTH, B_WIDTH), 0) // B_DV
    hj = lax.broadcasted_iota(jnp.int32, (B_WIDTH, B_WIDTH), 1) // B_DV
    head_mean = jnp.where(hi == hj, 1.0 / B_DV, 0.0).astype(jnp.bfloat16)
    ms = sum(jnp.dot(p, head_mean, preferred_element_type=F32) for p in _split_bf16(o * o, 2))
    yb = o * lax.rsqrt(ms + NORM_EPS) * gg_ref[...] * _silu(r_ref[0])
    mix = (jnp.dot(ya, w_ref[0:A_WIDTH, :], preferred_element_type=F32)
           + jnp.dot(yb.astype(MXU_DTYPE), w_ref[A_WIDTH:A_WIDTH + B_WIDTH, :], preferred_element_type=F32)
           + jnp.dot(yn_ref[0], w_ref[A_WIDTH + B_WIDTH:, :], preferred_element_type=F32))
    x = x_ref[0] + mod_ref[2:3, :] * mix
    xo_ref[0] = x
    h = _rms(x, g2_ref[...]) * (1.0 + mod_ref[4:5, :]) + mod_ref[3:4, :]
    h_ref[0] = _pack_bf16_pairs(h)
    h_hi, h_lo = _split_bf16(h, 2)
    both = jnp.dot(h_hi, wr_ref[...], preferred_element_type=F32)
    logits = (both[:, :ROUTER_PAD] + both[:, ROUTER_PAD:]
              + jnp.dot(h_lo, wr_ref[:, :ROUTER_PAD], preferred_element_type=F32) + br_ref[...])
    @pl.when((pl.program_id(0) == 0) & (pl.program_id(1) == 0))
    def _():
        count_ref[...] = jnp.zeros_like(count_ref)

    rt_ref[0] = _route(logits, count_ref)
    cnt_ref[...] = count_ref[...]


def _route(lg, count_ref):
    lane = lax.broadcasted_iota(jnp.int32, lg.shape, 1)
    big = jnp.int32(ROUTER_PAD)

    def top(mask):
        v = jnp.max(jnp.where(mask, lg, -jnp.inf), axis=-1, keepdims=True)
        i = jnp.min(jnp.where(mask & (lg == v), lane, big), axis=-1, keepdims=True)
        return v, i

    g_mask = lane < N_GROUPS
    g_max, grp = top(g_mask)
    p_grp = 1.0 / jnp.sum(jnp.where(g_mask, jnp.exp(lg - g_max), 0.0), axis=-1, keepdims=True)
    e_lo = N_GROUPS + grp * EXPERTS_PER_GROUP
    e_mask = (lane >= e_lo) & (lane < e_lo + EXPERTS_PER_GROUP)
    v1, i1 = top(e_mask)
    v2, i2 = top(e_mask & (lane != i1))
    r = jnp.exp(v2 - v1)
    gate1 = p_grp / (1.0 + r)
    gate2 = p_grp * r / (1.0 + r)
    e1, e2 = i1 - N_GROUPS, i2 - N_GROUPS
    hot1 = (lane == e1).astype(F32)
    hot2 = (lane == e2).astype(F32)
    both = hot1 + hot2
    rows = lg.shape[0]
    earlier = (lax.broadcasted_iota(jnp.int32, (rows, rows), 1)
               < lax.broadcasted_iota(jnp.int32, (rows, rows), 0)).astype(jnp.bfloat16)
    before = jnp.dot(earlier, both.astype(jnp.bfloat16), preferred_element_type=F32) + count_ref[0:1, :]
    rank1 = jnp.sum(hot1 * before, axis=-1, keepdims=True)
    rank2 = jnp.sum(hot2 * before, axis=-1, keepdims=True)
    count_ref[0:1, :] = count_ref[0:1, :] + jnp.sum(both, axis=0, keepdims=True)
    out = jnp.where(lane == ROUTE_EXPERT, e1.astype(F32),
                    jnp.where(lane == ROUTE_EXPERT + 1, e2.astype(F32),
                              jnp.where(lane == ROUTE_GATE, gate1,
                                        jnp.where(lane == ROUTE_GATE + 1, gate2,
                                                  jnp.where(lane == ROUTE_RANK, rank1,
                                                            jnp.where(lane == ROUTE_RANK + 1, rank2, 0.0))))))
    return out


def _out_proj(layer, ya_ctx, ya_lat, o_f, o_b, bg, yn, x, modsel, g_gla, g2, w_out_b, w_router, b_router):
    bsz, t, d = x.shape
    tm = TOK_TILE
    tok = lambda b, i: (b, i, 0)
    const = lambda b, i: (0, 0)
    r_block = (2 * B_QK + B_WIDTH) // B_WIDTH
    return pl.pallas_call(
        _out_proj_body,
        grid=(bsz, t // tm),
        in_specs=[pl.BlockSpec((1, tm, A_WIDTH), lambda b, i: (b, 0, 0)),
                  pl.BlockSpec((1, tm, A_WIDTH), lambda b, i: (b, jnp.maximum(i - 1, 0), 0)),
                  pl.BlockSpec((1, tm, B_WIDTH), tok), pl.BlockSpec((1, tm, B_WIDTH), tok),
                  pl.BlockSpec((1, tm, B_WIDTH), lambda b, i: (b, i, r_block)),
                  pl.BlockSpec((1, tm, C_WIDTH), tok),
                  pl.BlockSpec((1, tm, d), tok),
                  pl.BlockSpec((None, None, None, 6, d), lambda b, i: (layer, b, jnp.minimum(i, 1), 0, 0)),
                  pl.BlockSpec((1, B_WIDTH), const), pl.BlockSpec((1, d), const),
                  pl.BlockSpec(w_out_b.shape, const),
                  pl.BlockSpec((d, 2 * ROUTER_PAD), const), pl.BlockSpec((1, ROUTER_PAD), const)],
        out_specs=[pl.BlockSpec((1, tm, d), tok), pl.BlockSpec((1, tm, d // 2), tok),
                   pl.BlockSpec((1, tm, ROUTER_PAD), tok), pl.BlockSpec((8, ROUTER_PAD), const)],
        out_shape=[jax.ShapeDtypeStruct((bsz, t, d), F32), jax.ShapeDtypeStruct((bsz, t, d // 2), jnp.uint32),
                   jax.ShapeDtypeStruct((bsz, t, ROUTER_PAD), F32), jax.ShapeDtypeStruct((8, ROUTER_PAD), F32)],
        scratch_shapes=[pltpu.VMEM((8, ROUTER_PAD), F32)],
        compiler_params=_cparams(("arbitrary", "arbitrary")),
        name="out_proj",
    )(ya_ctx, ya_lat, o_f, o_b, bg, yn, x, modsel, jnp.tile(g_gla, B_HEADS).reshape(1, B_WIDTH), g2.reshape(1, d),
      w_out_b, w_router, b_router)


def _dispatch_plan(expert, rank, counts, n_blocks):
    padded = (counts + EXPERT_TILE - 1) // EXPERT_TILE * EXPERT_TILE
    pad_end = jnp.cumsum(padded)
    pad_start = (pad_end - padded).astype(jnp.int32)
    lanes = jnp.arange(N_EXPERTS, dtype=jnp.int32)
    dest = rank + jnp.sum(jnp.where(expert[..., None] == lanes, pad_start, 0), axis=-1)
    blk_start = jnp.arange(n_blocks, dtype=jnp.int32) * EXPERT_TILE
    blk_expert = jnp.sum((pad_end[None, :] <= blk_start[:, None]).astype(jnp.int32), axis=1)
    blk_expert = jnp.minimum(blk_expert, N_EXPERTS - 1)
    blk_valid = jnp.clip(pad_start[blk_expert] + counts[blk_expert] - blk_start, 0, EXPERT_TILE)
    return dest.astype(jnp.int32), blk_expert.astype(jnp.int32), blk_valid.astype(jnp.int32)


def _sc_dispatch(table, dest, n_rows):
    n_tok, width = table.shape
    n_workers = SC_CORES * SC_SUBCORES
    per_worker = n_tok // n_workers
    chunk = SC_ROW_BUFFER_BYTES // (2 * width * table.dtype.itemsize)
    n_chunks = per_worker // chunk
    assert per_worker * n_workers == n_tok and n_chunks * chunk == per_worker and n_chunks % 2 == 0
    assert n_chunks >= 4 and chunk <= LANES and dest.shape == (n_tok, TOP_K)
    mesh = plsc.VectorSubcoreMesh(core_axis_name="core", subcore_axis_name="subcore")

    def body(table_hbm, d0_hbm, d1_hbm, out_hbm, i0_v, i1_v, rows_v, read_sem, put_sem):
        worker = lax.axis_index("subcore") * SC_CORES + lax.axis_index("core")
        base = worker * per_worker
        pltpu.sync_copy(d0_hbm.at[worker], i0_v)
        pltpu.sync_copy(d1_hbm.at[worker], i1_v)

        def read(j, slot):
            return pltpu.make_async_copy(table_hbm.at[pl.ds(base + j * chunk, chunk)], rows_v.at[slot],
                                         read_sem.at[slot])

        def put(j, slot, idx_v, k):
            return pltpu.make_async_copy(rows_v.at[slot], out_hbm.at[idx_v.at[j]], put_sem.at[slot, k])

        def drain(j, slot):
            read(j, slot).wait()
            put(j, slot, i0_v, 0).start()
            put(j, slot, i1_v, 1).start()
            put(j, slot, i0_v, 0).wait()
            put(j, slot, i1_v, 1).wait()

        read(0, 0).start()
        read(1, 1).start()

        @pl.loop(0, n_chunks - 2, step=2)
        def _(j):
            for slot in range(2):
                drain(j + slot, slot)
                read(j + slot + 2, slot).start()

        drain(n_chunks - 2, 0)
        drain(n_chunks - 1, 1)

    idx = dest.reshape(n_workers, n_chunks, chunk, TOP_K)
    return pl.kernel(
        body,
        out_type=jax.ShapeDtypeStruct((n_rows, width), table.dtype),
        mesh=mesh,
        scratch_types=[pltpu.VMEM((n_chunks, chunk), jnp.int32), pltpu.VMEM((n_chunks, chunk), jnp.int32),
                       pltpu.VMEM((2, chunk, width), table.dtype),
                       pltpu.SemaphoreType.DMA((2,)), pltpu.SemaphoreType.DMA((2, TOP_K))],
        name="sc_row_dispatch",
    )(table, idx[..., 0], idx[..., 1])


def _sc_gather(table, idx):
    n_rows = idx.shape[0]
    width = table.shape[1]
    n_workers = SC_CORES * SC_SUBCORES
    per_worker = n_rows // n_workers
    chunk = SC_ROW_BUFFER_BYTES // (width * table.dtype.itemsize)
    n_chunks = per_worker // chunk
    assert per_worker * n_workers == n_rows and n_chunks * chunk == per_worker and n_chunks % 2 == 0
    assert n_chunks >= 4 and chunk <= LANES
    mesh = plsc.VectorSubcoreMesh(core_axis_name="core", subcore_axis_name="subcore")

    def body(table_hbm, idx_hbm, out_hbm, idx_v, rows_v, gather_sem, write_sem):
        worker = lax.axis_index("subcore") * SC_CORES + lax.axis_index("core")
        base = worker * per_worker
        pltpu.sync_copy(idx_hbm.at[worker], idx_v)

        def gather(j, slot):
            return pltpu.make_async_copy(table_hbm.at[idx_v.at[j]], rows_v.at[slot], gather_sem.at[slot])

        def write(j, slot):
            return pltpu.make_async_copy(rows_v.at[slot], out_hbm.at[pl.ds(base + j * chunk, chunk)],
                                         write_sem.at[slot])

        gather(0, 0).start()
        gather(0, 0).wait()
        gather(1, 1).start()
        write(0, 0).start()

        @pl.loop(1, n_chunks - 1, step=2)
        def _(j):
            for s in range(2):
                slot = (1 + s) % 2
                gather(j + s, slot).wait()
                write(j + s - 1, 1 - slot).wait()
                gather(j + s + 1, 1 - slot).start()
                write(j + s, slot).start()

        last = n_chunks - 1
        gather(last, 1).wait()
        write(last, 1).start()
        write(last - 1, 0).wait()
        write(last, 1).wait()

    return pl.kernel(
        body,
        out_type=jax.ShapeDtypeStruct((n_rows, width), table.dtype),
        mesh=mesh,
        scratch_types=[pltpu.VMEM((n_chunks, chunk), jnp.int32),
                       pltpu.VMEM((2, chunk, width), table.dtype),
                       pltpu.SemaphoreType.DMA((2,)), pltpu.SemaphoreType.DMA((2,))],
        name="sc_row_gather",
    )(table, idx.reshape(n_workers, n_chunks, chunk))


def _expert_body(be_ref, used_ref, x_ref, wu_ref, wd_ref, o_ref, wub_ref, wdb_ref):
    i = pl.program_id(0)
    prev = be_ref[jnp.maximum(i - 1, 0)]

    @pl.when((i == 0) | (be_ref[i] != prev))
    def _():
        wub_ref[...] = wu_ref[...].astype(wub_ref.dtype)
        wdb_ref[...] = wd_ref[...].astype(wdb_ref.dtype)

    @pl.when(used_ref[i] > 0)
    def _():
        real = lax.broadcasted_iota(jnp.int32, x_ref.shape, 0) < used_ref[i]
        lo, hi = _unpack_bf16_pairs(jnp.where(real, x_ref[...], jnp.uint32(0)))
        half = lo.shape[1]
        gu = (jnp.dot(lo.astype(MXU_DTYPE), wub_ref[:half, :], preferred_element_type=F32)
              + jnp.dot(hi.astype(MXU_DTYPE), wub_ref[half:, :], preferred_element_type=F32))
        act = _silu(gu[:, :EXPERT_HIDDEN]) * gu[:, EXPERT_HIDDEN:]
        o_ref[...] = _pack_bf16_pairs(jnp.dot(act.astype(MXU_DTYPE), wdb_ref[...], preferred_element_type=F32))

    @pl.when(used_ref[i] == 0)
    def _():
        o_ref[...] = jnp.zeros_like(o_ref)


def _expert_ffn(layer, buf, blk_expert, blk_used, w_up, w_down):
    n_rows, packed = buf.shape
    d = 2 * packed
    n_blocks = n_rows // EXPERT_TILE
    h2 = w_up.shape[-1]
    grid_spec = pltpu.PrefetchScalarGridSpec(
        num_scalar_prefetch=2,
        grid=(n_blocks,),
        in_specs=[pl.BlockSpec((EXPERT_TILE, packed), lambda i, be, us: (i, 0)),
                  pl.BlockSpec((None, None, d, h2), lambda i, be, us: (layer, be[i], 0, 0)),
                  pl.BlockSpec((None, None, h2 // 2, d), lambda i, be, us: (layer, be[i], 0, 0))],
        out_specs=pl.BlockSpec((EXPERT_TILE, packed), lambda i, be, us: (i, 0)),
        scratch_shapes=[pltpu.VMEM((d, h2), MXU_DTYPE), pltpu.VMEM((h2 // 2, d), MXU_DTYPE)],
    )
    return pl.pallas_call(
        _expert_body,
        grid_spec=grid_spec,
        out_shape=jax.ShapeDtypeStruct((n_rows, packed), jnp.uint32),
        compiler_params=_cparams(("arbitrary",)),
        name="expert_ffn",
    )(blk_expert, blk_used, buf, w_up, w_down)


def _moe(layer, h2, route, counts, w_up, w_down):
    bsz, t, packed = h2.shape
    n_tok = bsz * t
    n_assign = n_tok * TOP_K
    n_blocks = -(-(n_assign + N_EXPERTS * (EXPERT_TILE - 1)) // EXPERT_TILE)
    record = route.reshape(n_tok, ROUTER_PAD)
    expert = record[:, ROUTE_EXPERT:ROUTE_EXPERT + TOP_K].astype(jnp.int32)
    rank = record[:, ROUTE_RANK:ROUTE_RANK + TOP_K].astype(jnp.int32)
    dest, blk_expert, blk_valid = _dispatch_plan(expert, rank, counts[0, :N_EXPERTS].astype(jnp.int32), n_blocks)
    buf = _sc_dispatch(h2.reshape(n_tok, packed), dest, n_blocks * EXPERT_TILE)
    y = _expert_ffn(layer, buf, blk_expert, blk_valid, w_up, w_down)
    return _sc_gather(y, dest.T.reshape(-1)).reshape(TOP_K, bsz, t, packed)


def _final_body(x_ref, y0_ref, y1_ref, rt_ref, mod_ref, g_ref, o_ref):
    o_ref[0] = _rms(_moe_residual(x_ref, y0_ref, y1_ref, rt_ref, mod_ref), g_ref[...])


def _final_norm(layer, x, ymoe, route, modsel, g, ctx_tiles):
    bsz, t, d = x.shape
    tm = TOK_TILE
    lat = lambda b, i: (b, i + ctx_tiles, 0)
    return pl.pallas_call(
        _final_body,
        grid=(bsz, t // tm - ctx_tiles),
        in_specs=[pl.BlockSpec((1, tm, d), lat)] + _moe_specs(tm, d, lambda i: i + ctx_tiles)
        + [pl.BlockSpec((None, None, None, 6, d), lambda b, i: (layer, b, 1, 0, 0)),
           pl.BlockSpec((1, d), lambda b, i: (0, 0))],
        out_specs=pl.BlockSpec((1, tm, d), lambda b, i: (b, i, 0)),
        out_shape=jax.ShapeDtypeStruct((bsz, t - ctx_tiles * tm, d), F32),
        compiler_params=_cparams(("parallel", "arbitrary")),
        name="final_norm",
    )(x, ymoe, ymoe, route, modsel, g.reshape(1, d))


def _rope_tables(n_ctx, n_lat):
    t = jnp.arange(n_lat)
    row = (t // GRID_W).astype(F32)
    col = (t % GRID_W).astype(F32)
    n_freq = HEAD_DIM // 4
    inv = ROPE_THETA ** (-jnp.arange(n_freq, dtype=F32) / n_freq)
    ang_r = row[:, None] * inv
    ang_c = col[:, None] * inv
    cr, sr, cc, sc = jnp.cos(ang_r), jnp.sin(ang_r), jnp.cos(ang_c), jnp.sin(ang_c)
    z = jnp.zeros_like(sr)
    cos = jnp.concatenate([cr, cr, cc, cc], axis=-1)
    above = jnp.concatenate([-sr, z, -sc, z], axis=-1)
    below = jnp.concatenate([z, sr, z, sc], axis=-1)
    reps = LANES // HEAD_DIM

    def full(tab, ctx_value):
        tab = jnp.tile(tab, (1, reps))
        return jnp.concatenate([jnp.full((n_ctx, LANES), ctx_value, F32), tab], axis=0)

    return full(cos, 1.0), full(above, 0.0), full(below, 0.0)


def _pack_w_in(w_in):
    parts = jnp.split(w_in, np.cumsum(IN_SIZES)[:-1].tolist(), axis=-1)
    qa, ka, va, qb, kb, vb, rb, gb, qn, kn, vn = parts
    gb = jnp.pad(gb, ((0, 0), (0, GATE_PAD - gb.shape[-1])))
    return jnp.concatenate([qa, ka, va, qb, kb, vb, rb, gb, qn, kn, vn], axis=-1).astype(MXU_DTYPE)


def kernel(x, c, ctx, c_ctx, w_mod, b_mod, norm1_g, norm2_g, w_in, w_out, diff_lambda, diff_sub_g,
           gla_w_decay, gla_b_decay, gla_norm_g, na_rel_bias, w_router_group, b_router_group,
           w_router_expert, b_router_expert, w_expert_up, w_expert_down, final_g):
    bsz, seq, d = x.shape
    n_ctx = ctx.shape[1]
    depth = w_mod.shape[0]
    assert n_ctx == TOK_TILE and seq % (NA_TILE_ROWS * GRID_W) == 0 and d % LANES == 0
    assert seq // (NA_TILE_ROWS * GRID_W) >= NA_KEY_TILES
    mod_rows = -(-(bsz + 1) // 8) * 8
    cvec = jnp.zeros((mod_rows, d), F32).at[:bsz].set(c).at[bsz].set(c_ctx)
    mod = _modulation(cvec, w_mod, b_mod).reshape(depth, mod_rows, 6, d)
    modsel = jnp.stack([jnp.broadcast_to(mod[:, bsz][:, None], (depth, bsz, 6, d)), mod[:, :bsz]], axis=2)
    rope_tabs = _rope_tables(n_ctx, seq)
    xt = jnp.concatenate([ctx, x], axis=1)
    n_groups = BATCH_GROUPS if bsz % BATCH_GROUPS == 0 else 1
    gb = bsz // n_groups
    groups = [dict(xt=xt[g * gb:(g + 1) * gb], mod=modsel[:, g * gb:(g + 1) * gb], ymoe=None, route=None)
              for g in range(n_groups)]
    for l in range(depth):
        w_in_p = _pack_w_in(w_in[l])
        bias_tabs = _na_bias_tables(na_rel_bias[l], seq // GRID_W)
        w_router = jnp.pad(jnp.concatenate([w_router_group[l], w_router_expert[l]], axis=-1),
                           ((0, 0), (0, ROUTER_PAD - N_GROUPS - N_EXPERTS)))
        w_router = jnp.concatenate(_split_bf16(w_router, 2), axis=-1)
        b_router = jnp.pad(jnp.concatenate([b_router_group[l], b_router_expert[l]]),
                           (0, ROUTER_PAD - N_GROUPS - N_EXPERTS)).reshape(1, ROUTER_PAD)
        w_out_b = w_out[l].astype(MXU_DTYPE)
        for st in groups:
            outs = _in_proj(l, st["xt"], st["ymoe"], st["route"], st["mod"], norm1_g[l], w_in_p, rope_tabs)
            qa, ka, va, bg, qn, kn, vn = outs[:7]
            if st["ymoe"] is not None:
                st["xt"] = outs[7]
            ya_ctx, ya_lat = _diff_attention(l, qa, ka, va, diff_lambda[l], diff_sub_g[l])
            o_f, o_b = _gla_scan(bg, gla_w_decay[l], gla_b_decay[l])
            yn = _neighborhood_attention(qn, kn, vn, bias_tabs)
            st["xt"], h2, st["route"], counts = _out_proj(l, ya_ctx, ya_lat, o_f, o_b, bg, yn, st["xt"], st["mod"],
                                                          gla_norm_g[l], norm2_g[l], w_out_b, w_router, b_router)
            st["ymoe"] = _moe(l, h2, st["route"], counts, w_expert_up, w_expert_down)
    outs = [_final_norm(depth - 1, st["xt"], st["ymoe"], st["route"], st["mod"], final_g, n_ctx // TOK_TILE)
            for st in groups]
    return jnp.concatenate(outs, axis=0)
```

```python
import functools
import math

import numpy as np
import jax
import jax.numpy as jnp
from jax import lax
from jax.experimental import pallas as pl
from jax.experimental.pallas import tpu as pltpu
from jax.experimental.pallas import tpu_sc as plsc

F32 = jnp.float32
MXU_DTYPE = jnp.bfloat16
HI = lax.Precision.HIGHEST

GRID_W = 64
HEAD_DIM = 64
ROPE_THETA = 10000.0
NORM_EPS = 1e-6

A_HEADS = 4
A_QK = HEAD_DIM
A_V = 2 * HEAD_DIM
B_HEADS = 4
B_DK = HEAD_DIM // 2
B_DV = HEAD_DIM
B_GATE_RANK = 16
B_GATE_TAU = 16.0
B_CHUNK = 64
LOG2_E = math.log2(math.e)
GLA_SUB = 16
C_HEADS = 4
C_DH = HEAD_DIM
NA_ROWS = 8
NA_COLS = 16

A_WIDTH = A_HEADS * A_V
B_WIDTH = B_HEADS * B_DV
C_WIDTH = C_HEADS * C_DH
B_QK = B_HEADS * B_DK
IN_SIZES = (A_HEADS * 2 * A_QK, A_HEADS * 2 * A_QK, A_WIDTH,
            B_QK, B_QK, B_WIDTH, B_WIDTH, 2 * B_GATE_RANK,
            C_WIDTH, C_WIDTH, C_WIDTH)

N_GROUPS = 4
EXPERTS_PER_GROUP = 8
N_EXPERTS = N_GROUPS * EXPERTS_PER_GROUP
TOP_K = 2
EXPERT_HIDDEN = 512

LANES = 128
TOK_TILE = 256
NA_TILE_ROWS = 4
NA_KEY_TILES = 3
BATCH_GROUPS = 1
IN_PROJ_TILE = 256
EXPERT_TILE = 256
GATE_PAD = LANES
BG_WIDTH = 2 * B_QK + 2 * B_WIDTH + GATE_PAD
IN_PAD_WIDTH = 3 * A_WIDTH + BG_WIDTH + 3 * C_WIDTH
ROUTER_PAD = LANES
ROUTE_EXPERT = 0
ROUTE_GATE = 2
ROUTE_RANK = 4
SC_CORES = 2
SC_SUBCORES = 16
SC_ROW_BUFFER_BYTES = 128 * 1024
VMEM_LIMIT = 48 * 1024 * 1024


def _split_bf16(x, pieces):
    out = []
    for _ in range(pieces):
        p = x.astype(jnp.bfloat16)
        out.append(p)
        x = x - p.astype(F32)
    return out


def _pack_bf16_pairs(x):
    w = x.shape[1] // 2
    bits = lax.bitcast_convert_type(x.astype(jnp.bfloat16).astype(F32), jnp.uint32)
    return (bits[:, :w] >> 16) | (bits[:, w:] & jnp.uint32(0xFFFF0000))


def _unpack_bf16_pairs(u):
    lo = lax.bitcast_convert_type(u << 16, F32)
    hi = lax.bitcast_convert_type(u & jnp.uint32(0xFFFF0000), F32)
    return lo, hi


def _silu(x):
    return x * (1.0 / (1.0 + jnp.exp(-x)))


def _cparams(sem):
    return pltpu.CompilerParams(dimension_semantics=sem, vmem_limit_bytes=VMEM_LIMIT)


def _mod_body(c_ref, w_ref, b_ref, o_ref):
    a = _silu(c_ref[...])
    o_ref[...] = jnp.dot(a, w_ref[...], precision=HI, preferred_element_type=F32) + b_ref[...]


def _modulation(cvec, w_mod, b_mod):
    depth, d, d6 = w_mod.shape
    rows = cvec.shape[0]
    return pl.pallas_call(
        _mod_body,
        grid=(depth, d6 // d),
        in_specs=[pl.BlockSpec((rows, d), lambda l, j: (0, 0)),
                  pl.BlockSpec((None, d, d), lambda l, j: (l, 0, j)),
                  pl.BlockSpec((None, 1, d), lambda l, j: (l, 0, j))],
        out_specs=pl.BlockSpec((None, rows, d), lambda l, j: (l, 0, j)),
        out_shape=jax.ShapeDtypeStruct((depth, rows, d6), F32),
        compiler_params=_cparams(("arbitrary", "arbitrary")),
        name="modulation",
    )(cvec, w_mod, b_mod.reshape(depth, 1, d6))


def _rms(x, g):
    return x * lax.rsqrt(jnp.mean(x * x, axis=-1, keepdims=True) + NORM_EPS) * g


def _rope(x, cos, sa, sb):
    return x * cos + pltpu.roll(x, LANES - 16, 1) * sa + pltpu.roll(x, 16, 1) * sb


def _moe_residual(x_ref, y0_ref, y1_ref, rt_ref, gate2):
    rt = rt_ref[0]
    g0, g1 = rt[:, ROUTE_GATE:ROUTE_GATE + 1], rt[:, ROUTE_GATE + 1:ROUTE_GATE + 2]
    lo0, hi0 = _unpack_bf16_pairs(y0_ref[0])
    lo1, hi1 = _unpack_bf16_pairs(y1_ref[0])
    moe = jnp.concatenate([g0 * lo0 + g1 * lo1, g0 * hi0 + g1 * hi1], axis=1)
    return x_ref[0] + gate2 * moe


def _in_proj_body(combine, ctx_len, *refs):
    if combine:
        (x_ref, y0_ref, y1_ref, rt_ref, pmod_ref, mod_ref, g_ref, w_ref, cos_ref, sa_ref, sb_ref,
         qa_ref, ka_ref, va_ref, bg_ref, qn_ref, kn_ref, vn_ref, xo_ref) = refs
    else:
        (x_ref, mod_ref, g_ref, w_ref, cos_ref, sa_ref, sb_ref,
         qa_ref, ka_ref, va_ref, bg_ref, qn_ref, kn_ref, vn_ref) = refs
    tm = x_ref.shape[1]
    row = lax.broadcasted_iota(jnp.int32, (tm, 1), 0) + pl.program_id(1) * tm
    is_ctx = row < ctx_len

    def mod_row(ref, j):
        return jnp.where(is_ctx, ref[0, j:j + 1, :], ref[1, j:j + 1, :])

    if combine:
        x = _moe_residual(x_ref, y0_ref, y1_ref, rt_ref, mod_row(pmod_ref, 5))
        xo_ref[0] = x
    else:
        x = x_ref[0]
    h = _rms(x, g_ref[...]) * (1.0 + mod_row(mod_ref, 1)) + mod_row(mod_ref, 0)
    hb = h.astype(MXU_DTYPE)

    def proj(lo, hi):
        return jnp.dot(hb, w_ref[:, lo:hi], preferred_element_type=F32)

    cos, sa, sb = cos_ref[...], sa_ref[...], sb_ref[...]
    qk = proj(0, 2 * A_WIDTH)
    for hh in range(A_HEADS):
        lo = hh * LANES
        q = qk[:, lo:lo + LANES]
        qa_ref[0, :, lo:lo + LANES] = (_rope(q, cos, sa, sb) * (A_QK ** -0.5 * LOG2_E)).astype(qa_ref.dtype)
        k = qk[:, A_WIDTH + lo:A_WIDTH + lo + LANES]
        ka_ref[0, :, lo:lo + LANES] = _rope(k, cos, sa, sb).astype(ka_ref.dtype)
    o = 2 * A_WIDTH
    va = proj(o, o + A_WIDTH)
    ones = jnp.ones((va.shape[0], A_V), F32)
    va_ref[0] = jnp.concatenate([piece for hh in range(A_HEADS)
                                 for piece in (va[:, hh * A_V:(hh + 1) * A_V], ones)],
                                axis=1).astype(va_ref.dtype)
    o += A_WIDTH
    bg_ref[0] = proj(o, o + BG_WIDTH)
    o += BG_WIDTH
    qn_ref[0] = (proj(o, o + C_WIDTH) * (C_DH ** -0.5)).astype(qn_ref.dtype)
    o += C_WIDTH
    kn_ref[0] = proj(o, o + C_WIDTH).astype(kn_ref.dtype)
    o += C_WIDTH
    vn_ref[0] = proj(o, o + C_WIDTH).astype(vn_ref.dtype)


def _moe_specs(tm, d, row_block):
    return [pl.BlockSpec((None, 1, tm, d // 2), lambda b, i: (0, b, row_block(i), 0)),
            pl.BlockSpec((None, 1, tm, d // 2), lambda b, i: (1, b, row_block(i), 0)),
            pl.BlockSpec((1, tm, ROUTER_PAD), lambda b, i: (b, row_block(i), 0))]


def _in_proj(layer, x, ymoe, route, modsel, g1, w_in_p, rope_tabs):
    bsz, t, d = x.shape
    tm = IN_PROJ_TILE
    combine = ymoe is not None
    tok = lambda b, i: (b, i, 0)
    x_spec = pl.BlockSpec((1, tm, d), tok)

    def mod_spec(l):
        return pl.BlockSpec((None, None, 2, 6, d), lambda b, i: (l, b, 0, 0, 0))

    tab_spec = pl.BlockSpec((tm, LANES), lambda b, i: (i, 0))
    in_specs = [x_spec]
    args = [x]
    if combine:
        in_specs += _moe_specs(tm, d, lambda i: i) + [mod_spec(layer - 1)]
        args += [ymoe, ymoe, route, modsel]
    in_specs += [mod_spec(layer), pl.BlockSpec((1, d), lambda b, i: (0, 0)),
                 pl.BlockSpec((d, IN_PAD_WIDTH), lambda b, i: (0, 0)), tab_spec, tab_spec, tab_spec]
    args += [modsel, g1.reshape(1, d), w_in_p, *rope_tabs]

    def o(width, dtype):
        return pl.BlockSpec((1, tm, width), tok), jax.ShapeDtypeStruct((bsz, t, width), dtype)

    outs = [o(A_WIDTH, MXU_DTYPE), o(A_WIDTH, MXU_DTYPE), o(2 * A_WIDTH, MXU_DTYPE), o(BG_WIDTH, F32),
            o(C_WIDTH, MXU_DTYPE), o(C_WIDTH, MXU_DTYPE), o(C_WIDTH, MXU_DTYPE)]
    if combine:
        outs.append(o(d, F32))
    return pl.pallas_call(
        functools.partial(_in_proj_body, combine, TOK_TILE),
        grid=(bsz, pl.cdiv(t, tm)),
        in_specs=in_specs,
        out_specs=[s for s, _ in outs],
        out_shape=[s for _, s in outs],
        compiler_params=_cparams(("parallel", "arbitrary")),
        name="in_proj",
    )(*args)


def _diff_rows(lam_init, q, k_ref, v_ref, n_keys, lam, g):
    lane = lax.broadcasted_iota(jnp.int32, (1, LANES), 1)
    zero = jnp.zeros_like(q)
    nt = (((1,), (1,)), ((), ()))
    k = k_ref[0, :n_keys, :]
    s1 = lax.dot_general(jnp.where(lane < A_QK, q, zero), k, nt, preferred_element_type=F32)
    s2 = lax.dot_general(jnp.where(lane >= A_QK, q, zero), k, nt, preferred_element_type=F32)

    def finish():
        v1 = v_ref[0, :n_keys, :]
        outs = []
        for s in (s1, s2):
            p = jnp.exp2((s - jnp.max(s, axis=-1, keepdims=True)).astype(MXU_DTYPE))
            outs.append(jnp.dot(p, v1, preferred_element_type=F32))
        o = (outs[0][:, :A_V] * (1.0 / outs[0][:, A_V:A_V + 1])
             - outs[1][:, :A_V] * (lam / outs[1][:, A_V:A_V + 1]))
        return _rms(o, g) * (1.0 - lam_init)

    return finish


def _diff_attn_body(lam_init, ctx_len, qa_ref, qb_ref, k_ref, v_ref, lam_ref, g_ref, oc_ref, ol_ref):
    lm = lam_ref[...]
    lam = (jnp.exp(jnp.sum(lm[0:1] * lm[1:2], axis=1, keepdims=True))
           - jnp.exp(jnp.sum(lm[2:3] * lm[3:4], axis=1, keepdims=True)) + lam_init)
    g = g_ref[...]
    rows = qa_ref.shape[1]

    @pl.when(pl.program_id(2) == 0)
    def _():
        oc_ref[0] = _diff_rows(lam_init, qa_ref[0], k_ref, v_ref, ctx_len, lam, g)().astype(oc_ref.dtype)

    @pl.when(pl.program_id(2) > 0)
    def _():
        n_keys = k_ref.shape[1]
        fin_a = _diff_rows(lam_init, qa_ref[0], k_ref, v_ref, n_keys, lam, g)
        fin_b = _diff_rows(lam_init, qb_ref[0], k_ref, v_ref, n_keys, lam, g)
        ol_ref[0, :rows, :] = fin_a().astype(ol_ref.dtype)
        ol_ref[0, rows:, :] = fin_b().astype(ol_ref.dtype)


def _diff_attention(layer, qa, ka, va, lam, g_sub):
    bsz, t, _ = qa.shape
    tq = TOK_TILE
    n_lat = (t - tq) // (2 * tq)
    assert n_lat * 2 * tq == t - tq
    lam_init = 0.8 - 0.6 * math.exp(-0.3 * layer)

    def q_spec(off):
        return pl.BlockSpec((1, tq, LANES), lambda b, h, i: (b, jnp.maximum(2 * i + off, 0), h))

    return pl.pallas_call(
        functools.partial(_diff_attn_body, lam_init, TOK_TILE),
        grid=(bsz, A_HEADS, 1 + n_lat),
        in_specs=[q_spec(-1), q_spec(0),
                  pl.BlockSpec((1, t, LANES), lambda b, h, i: (b, 0, h)),
                  pl.BlockSpec((1, t, 2 * A_V), lambda b, h, i: (b, 0, h)),
                  pl.BlockSpec((4, A_QK), lambda b, h, i: (0, 0)),
                  pl.BlockSpec((1, A_V), lambda b, h, i: (0, 0))],
        out_specs=[pl.BlockSpec((1, tq, LANES), lambda b, h, i: (b, 0, h)),
                   pl.BlockSpec((1, 2 * tq, LANES), lambda b, h, i: (b, jnp.maximum(i - 1, 0), h))],
        out_shape=[jax.ShapeDtypeStruct((bsz, tq, A_WIDTH), MXU_DTYPE),
                   jax.ShapeDtypeStruct((bsz, t - tq, A_WIDTH), MXU_DTYPE)],
        compiler_params=_cparams(("parallel", "parallel", "arbitrary")),
        name="diff_attention",
    )(qa, qa, ka, va, lam, g_sub.reshape(1, A_V))


def _gla_body(f_ref, r_ref, wdec_ref, bdec_ref, of_ref, ob_ref, sf_ref, sb_ref, es_ref, bs_ref, qss_ref):
    c = B_CHUNK
    n_chunks = TOK_TILE // c

    @pl.when(pl.program_id(1) == 0)
    def _():
        sf_ref[...] = jnp.zeros_like(sf_ref)
        sb_ref[...] = jnp.zeros_like(sb_ref)

    sub = GLA_SUB
    n_sub = c // sub
    nt = (((1,), (1,)), ((), ()))
    t_row = lax.broadcasted_iota(jnp.int32, (TOK_TILE, TOK_TILE), 0)
    t_col = lax.broadcasted_iota(jnp.int32, (TOK_TILE, TOK_TILE), 1)
    same_chunk = (t_row // c) == (t_col // c)
    tri_f = (same_chunk & (t_col <= t_row)).astype(jnp.bfloat16)
    tri_b = (same_chunk & (t_col >= t_row)).astype(jnp.bfloat16)
    s_iota = lax.broadcasted_iota(jnp.int32, (sub, LANES), 0)
    idx = lax.broadcasted_iota(jnp.int32, (c, LANES), 0)
    head_of_k = lax.broadcasted_iota(jnp.int32, (B_QK, B_WIDTH), 0) // B_DK
    head_of_v = lax.broadcasted_iota(jnp.int32, (B_QK, B_WIDTH), 1) // B_DV
    expand = (head_of_k == head_of_v).astype(MXU_DTYPE)
    same_head_t = (lax.broadcasted_iota(jnp.int32, (B_WIDTH, B_QK), 0) // B_DV
                   == lax.broadcasted_iota(jnp.int32, (B_WIDTH, B_QK), 1) // B_DK)
    n_ref = n_sub - 1
    kt_keep = (lax.broadcasted_iota(jnp.int32, (B_HEADS * c, n_ref * B_QK), 0) // c
               == (lax.broadcasted_iota(jnp.int32, (B_HEADS * c, n_ref * B_QK), 1) % B_QK) // B_DK)
    vx_keep = (lax.broadcasted_iota(jnp.int32, (B_HEADS * c, B_WIDTH), 0) // c
               == lax.broadcasted_iota(jnp.int32, (B_HEADS * c, B_WIDTH), 1) // B_DV)
    pick = (lax.broadcasted_iota(jnp.int32, (c, c * sub), 1) // sub
            == lax.broadcasted_iota(jnp.int32, (c, c * sub), 0)).astype(MXU_DTYPE)

    def log_decay(src_ref, backward):
        gl = src_ref[0, :, 2 * B_QK + 2 * B_WIDTH:BG_WIDTH]
        d0 = B_QK if backward else 0
        z = jnp.dot(gl, wdec_ref[:, d0:d0 + B_QK], precision=HI, preferred_element_type=F32) \
            + bdec_ref[:, d0:d0 + B_QK]
        log_a = (jnp.minimum(z, 0.0) - jnp.log(1.0 + jnp.exp(-jnp.abs(z)))) / B_GATE_TAU
        tri = tri_b if backward else tri_f
        return sum(jnp.dot(tri, p, preferred_element_type=F32) for p in _split_bf16(log_a, 3))

    def chunk(slot, src_ref, b_all, lo, backward, st_ref, out_ref):
        q = src_ref[0, lo:lo + c, 0:B_QK] * (B_DK ** -0.5)
        k = src_ref[0, lo:lo + c, B_QK:2 * B_QK]
        v = src_ref[0, lo:lo + c, 2 * B_QK:2 * B_QK + B_WIDTH]
        b = b_all[lo:lo + c]
        b_ref, qs_ref, e_ref = bs_ref.at[slot], qss_ref.at[slot], es_ref.at[slot]
        b_ref[...] = b
        qs_ref[...] = q
        blk = ((c - 1 - idx) if backward else idx) // sub

        q_parts, k_parts = [], []
        for m, late, early in ((1, blk == 1, blk == 0), (2, blk >= 2, blk <= 1), (3, blk == 3, blk == 2)):
            r_row = (c - 1 - sub * m) if backward else sub * m
            r = b_ref[r_row:r_row + 1, :]
            q_parts.append(q * jnp.exp(jnp.where(late, b - r, -jnp.inf)))
            k_parts.append(k * jnp.exp(jnp.where(early, r - b, -jnp.inf)))
        q_cat = jnp.concatenate(q_parts, axis=1).astype(MXU_DTYPE)
        k_cat = jnp.concatenate(k_parts, axis=1)
        k_exp = jnp.where(kt_keep, jnp.concatenate([k_cat] * B_HEADS, axis=0), 0.0).astype(MXU_DTYPE)
        a_off = lax.dot_general(q_cat, k_exp, nt, preferred_element_type=F32)

        for tt in range(c):
            lo_s = tt // sub * sub
            keep = (s_iota >= tt - lo_s) if backward else (s_iota <= tt - lo_s)
            bt = b_ref[tt:tt + 1, :]
            qt = qs_ref[tt:tt + 1, :]
            e = jnp.exp(jnp.where(keep, bt - b[lo_s:lo_s + sub], -jnp.inf)) * (qt * k[lo_s:lo_s + sub])
            e_ref[tt * sub:(tt + 1) * sub, :] = e.astype(e_ref.dtype)
        a_exp = jnp.dot(e_ref[...], expand, preferred_element_type=F32)
        b_end = b[0:1, :] if backward else b[c - 1:c, :]
        kd = k * jnp.exp(b_end - b)
        upd = lax.dot_general(v, kd, (((0,), (0,)), ((), ())), preferred_element_type=F32)

        def intra():
            v_exp = jnp.where(vx_keep, jnp.concatenate([v] * B_HEADS, axis=0), 0.0).astype(MXU_DTYPE)
            o_off = jnp.dot(a_off.astype(MXU_DTYPE), v_exp, preferred_element_type=F32)
            prod = a_exp.reshape(n_sub, sub, sub, B_WIDTH) * v.reshape(n_sub, 1, sub, B_WIDTH)
            o_diag = jnp.dot(pick, prod.reshape(c * sub, B_WIDTH).astype(MXU_DTYPE),
                             preferred_element_type=F32)
            o_intra = o_off + o_diag

            def recur():
                st = st_ref[...]
                o_inter = lax.dot_general(q * jnp.exp(b), st, nt, preferred_element_type=F32)
                out_ref[0, lo:lo + c, :] = o_intra + o_inter
                st_ref[...] = jnp.exp(b_end) * st + jnp.where(same_head_t, upd, 0.0)

            return recur

        return intra

    b_fwd = log_decay(f_ref, False)
    b_bwd = log_decay(r_ref, True)
    stage = []
    for ci in range(n_chunks):
        stage.append(chunk(2 * ci, f_ref, b_fwd, ci * c, False, sf_ref, of_ref))
        stage.append(chunk(2 * ci + 1, r_ref, b_bwd, (n_chunks - 1 - ci) * c, True, sb_ref, ob_ref))
    stage = [intra() for intra in stage]
    for recur in stage:
        recur()


def _gla_scan(bg, w_dec, b_dec):
    bsz, t, _ = bg.shape
    n = t // TOK_TILE
    rev = lambda b, i: (b, jnp.where(i == 0, 0, n - i), 0)
    fwd = lambda b, i: (b, i, 0)
    wdec = jnp.zeros((GATE_PAD, 2 * B_QK), F32)
    wdec = wdec.at[:B_GATE_RANK, :B_QK].set(w_dec[0]).at[B_GATE_RANK:2 * B_GATE_RANK, B_QK:].set(w_dec[1])
    bdec = b_dec.reshape(1, 2 * B_QK)
    o_shape = jax.ShapeDtypeStruct((bsz, t, B_WIDTH), F32)
    n_slots = 2 * (TOK_TILE // B_CHUNK)
    return pl.pallas_call(
        _gla_body,
        grid=(bsz, n),
        in_specs=[pl.BlockSpec((1, TOK_TILE, BG_WIDTH), fwd),
                  pl.BlockSpec((1, TOK_TILE, BG_WIDTH), rev),
                  pl.BlockSpec((GATE_PAD, 2 * B_QK), lambda b, i: (0, 0)),
                  pl.BlockSpec((1, 2 * B_QK), lambda b, i: (0, 0))],
        out_specs=[pl.BlockSpec((1, TOK_TILE, B_WIDTH), fwd),
                   pl.BlockSpec((1, TOK_TILE, B_WIDTH), rev)],
        out_shape=[o_shape, o_shape],
        scratch_shapes=[pltpu.VMEM((B_WIDTH, B_QK), F32), pltpu.VMEM((B_WIDTH, B_QK), F32),
                        pltpu.VMEM((n_slots, B_CHUNK * GLA_SUB, LANES), MXU_DTYPE),
                        pltpu.VMEM((n_slots, B_CHUNK, LANES), F32),
                        pltpu.VMEM((n_slots, B_CHUNK, LANES), F32)],
        compiler_params=_cparams(("parallel", "arbitrary")),
        name="gla_scan",
    )(bg, bg, wdec, bdec)


def _na_bias_tables(rpb, rows):
    n_tiles = rows // NA_TILE_ROWS
    wr = min(NA_ROWS, rows)
    n_dr, n_dc = 2 * NA_ROWS - 1, 2 * NA_COLS - 1
    cq = np.arange(GRID_W)[:, None]
    ck = np.arange(GRID_W)[None, :]
    cs = np.clip(cq - NA_COLS // 2, 0, GRID_W - NA_COLS)
    col_ok = (ck >= cs) & (ck < cs + NA_COLS)
    dc = np.clip(ck - cq, -(NA_COLS - 1), NA_COLS - 1) + (NA_COLS - 1)
    onehot = (dc.reshape(1, -1) == np.arange(n_dc)[:, None]).astype(np.float32)
    by_col = jnp.dot(rpb.astype(F32).reshape(-1, n_dc), onehot, precision=HI)
    by_col = jnp.where(col_ok.reshape(1, 1, GRID_W, GRID_W),
                       by_col.reshape(C_HEADS, n_dr, GRID_W, GRID_W), -jnp.inf)
    masked = jnp.full((C_HEADS, GRID_W, GRID_W), -jnp.inf, F32)
    tabs = []
    for j in (0, 1, n_tiles - 1):
        kr0 = int(np.clip(j - 1, 0, n_tiles - NA_KEY_TILES)) * NA_TILE_ROWS
        q_rows = []
        for qr in range(NA_TILE_ROWS):
            r = j * NA_TILE_ROWS + qr
            start = int(np.clip(r - wr // 2, 0, rows - wr))
            blocks = []
            for kw in range(NA_KEY_TILES * NA_TILE_ROWS):
                kr = kr0 + kw
                blocks.append(by_col[:, kr - r + NA_ROWS - 1] if start <= kr < start + wr else masked)
            q_rows.append(jnp.concatenate(blocks, axis=-1))
        tabs.append(jnp.concatenate(q_rows, axis=1))
    return jnp.stack(tabs)


def _na_body(q_ref, k0_ref, k1_ref, k2_ref, kc_ref, v0_ref, v1_ref, v2_ref, vc_ref, m_ref, o_ref):
    q = q_ref[0]
    lane = lax.broadcasted_iota(jnp.int32, (1, LANES), 1)
    zero = jnp.zeros_like(q)
    nt = (((1,), (1,)), ((), ()))

    def scores(qm, k_ref):
        return lax.dot_general(qm, k_ref[0], nt, preferred_element_type=F32)

    def head_out(hh, windows):
        qm = jnp.where((lane >= hh * C_DH) & (lane < (hh + 1) * C_DH), q, zero)
        s = [scores(qm, kc_ref)]
        for w, k_ref in enumerate(windows):
            s.append(scores(qm, k_ref) + m_ref[0, hh, :, w * TOK_TILE:(w + 1) * TOK_TILE])

        def finish():
            m = functools.reduce(jnp.maximum, [jnp.max(x, axis=-1, keepdims=True) for x in s])
            p = [jnp.exp(x - m) for x in s]
            den = functools.reduce(jnp.add, [jnp.sum(x, axis=-1, keepdims=True) for x in p])
            vals = [vc_ref] + [v0_ref, v1_ref, v2_ref][:len(windows)]
            o = functools.reduce(jnp.add, [jnp.dot(x.astype(MXU_DTYPE), v_ref[0], preferred_element_type=F32)
                                           for x, v_ref in zip(p, vals)])
            return o * (1.0 / den)

        return finish

    def emit(windows):
        fin0 = head_out(0, windows)
        fin1 = head_out(1, windows)
        o_ref[0] = jnp.where(lane < C_DH, fin0(), fin1()).astype(o_ref.dtype)

    @pl.when(pl.program_id(1) == 0)
    def _():
        emit([])

    @pl.when(pl.program_id(1) > 0)
    def _():
        emit([k0_ref, k1_ref, k2_ref])


def _neighborhood_attention(qn, kn, vn, bias_tabs):
    bsz, t, _ = qn.shape
    n = t // TOK_TILE
    n_lat = n - 1

    def win(w):
        def index(hp, i, b):
            kb0 = jnp.clip(i - 2, 0, n_lat - NA_KEY_TILES)
            return (b, kb0 + 1 + w, hp)
        return pl.BlockSpec((1, TOK_TILE, LANES), index)

    def cls(hp, i, b):
        j = i - 1
        return (jnp.where(j <= 0, 0, jnp.where(j == n_lat - 1, 2, 1)), hp, 0, 0)

    own = pl.BlockSpec((1, TOK_TILE, LANES), lambda hp, i, b: (b, i, hp))
    ctx = pl.BlockSpec((1, TOK_TILE, LANES), lambda hp, i, b: (b, 0, hp))
    heads_per_step = LANES // C_DH
    return pl.pallas_call(
        _na_body,
        grid=(C_HEADS // heads_per_step, n, bsz),
        in_specs=[own, win(0), win(1), win(2), ctx, win(0), win(1), win(2), ctx,
                  pl.BlockSpec((1, heads_per_step, TOK_TILE, NA_KEY_TILES * TOK_TILE), cls)],
        out_specs=own,
        out_shape=jax.ShapeDtypeStruct((bsz, t, C_WIDTH), MXU_DTYPE),
        compiler_params=_cparams(("parallel", "parallel", "arbitrary")),
        name="neighborhood_attention",
    )(qn, kn, kn, kn, kn, vn, vn, vn, vn, bias_tabs)


def _out_proj_body(yac_ref, yal_ref, of_ref, ob_ref, r_ref, yn_ref, x_ref, mod_ref, gg_ref, g2_ref, w_ref,
                   wr_ref, br_ref, xo_ref, h_ref, rt_ref, cnt_ref, count_ref):
    ya = jnp.where(pl.program_id(1) == 0, yac_ref[0], yal_ref[0])
    o = of_ref[0] + ob_ref[0]
    hi = lax.broadcasted_iota(jnp.int32, (B_WIDTH, B_WIDTH), 0) // B_DV
    hj = lax.broadcasted_iota(jnp.int32, (B_WIDTH, B_WIDTH), 1) // B_DV
    head_mean = jnp.where(hi == hj, 1.0 / B_DV, 0.0).astype(jnp.bfloat16)
    ms = sum(jnp.dot(p, head_mean, preferred_element_type=F32) for p in _split_bf16(o * o, 2))
    yb = o * lax.rsqrt(ms + NORM_EPS) * gg_ref[...] * _silu(r_ref[0])
    mix = (jnp.dot(ya, w_ref[0:A_WIDTH, :], preferred_element_type=F32)
           + jnp.dot(yb.astype(MXU_DTYPE), w_ref[A_WIDTH:A_WIDTH + B_WIDTH, :], preferred_element_type=F32)
           + jnp.dot(yn_ref[0], w_ref[A_WIDTH + B_WIDTH:, :], preferred_element_type=F32))
    x = x_ref[0] + mod_ref[2:3, :] * mix
    xo_ref[0] = x
    h = _rms(x, g2_ref[...]) * (1.0 + mod_ref[4:5, :]) + mod_ref[3:4, :]
    h_ref[0] = _pack_bf16_pairs(h)
    h_hi, h_lo = _split_bf16(h, 2)
    both = jnp.dot(h_hi, wr_ref[...], preferred_element_type=F32)
    logits = (both[:, :ROUTER_PAD] + both[:, ROUTER_PAD:]
              + jnp.dot(h_lo, wr_ref[:, :ROUTER_PAD], preferred_element_type=F32) + br_ref[...])
    @pl.when((pl.program_id(0) == 0) & (pl.program_id(1) == 0))
    def _():
        count_ref[...] = jnp.zeros_like(count_ref)

    rt_ref[0] = _route(logits, count_ref)
    cnt_ref[...] = count_ref[...]


def _route(lg, count_ref):
    lane = lax.broadcasted_iota(jnp.int32, lg.shape, 1)
    big = jnp.int32(ROUTER_PAD)

    def top(mask):
        v = jnp.max(jnp.where(mask, lg, -jnp.inf), axis=-1, keepdims=True)
        i = jnp.min(jnp.where(mask & (lg == v), lane, big), axis=-1, keepdims=True)
        return v, i

    g_mask = lane < N_GROUPS
    g_max, grp = top(g_mask)
    p_grp = 1.0 / jnp.sum(jnp.where(g_mask, jnp.exp(lg - g_max), 0.0), axis=-1, keepdims=True)
    e_lo = N_GROUPS + grp * EXPERTS_PER_GROUP
    e_mask = (lane >= e_lo) & (lane < e_lo + EXPERTS_PER_GROUP)
    v1, i1 = top(e_mask)
    v2, i2 = top(e_mask & (lane != i1))
    r = jnp.exp(v2 - v1)
    gate1 = p_grp / (1.0 + r)
    gate2 = p_grp * r / (1.0 + r)
    e1, e2 = i1 - N_GROUPS, i2 - N_GROUPS
    hot1 = (lane == e1).astype(F32)
    hot2 = (lane == e2).astype(F32)
    both = hot1 + hot2
    rows = lg.shape[0]
    earlier = (lax.broadcasted_iota(jnp.int32, (rows, rows), 1)
               < lax.broadcasted_iota(jnp.int32, (rows, rows), 0)).astype(jnp.bfloat16)
    before = jnp.dot(earlier, both.astype(jnp.bfloat16), preferred_element_type=F32) + count_ref[0:1, :]
    rank1 = jnp.sum(hot1 * before, axis=-1, keepdims=True)
    rank2 = jnp.sum(hot2 * before, axis=-1, keepdims=True)
    count_ref[0:1, :] = count_ref[0:1, :] + jnp.sum(both, axis=0, keepdims=True)
    out = jnp.where(lane == ROUTE_EXPERT, e1.astype(F32),
                    jnp.where(lane == ROUTE_EXPERT + 1, e2.astype(F32),
                              jnp.where(lane == ROUTE_GATE, gate1,
                                        jnp.where(lane == ROUTE_GATE + 1, gate2,
                                                  jnp.where(lane == ROUTE_RANK, rank1,
                                                            jnp.where(lane == ROUTE_RANK + 1, rank2, 0.0))))))
    return out


def _out_proj(layer, ya_ctx, ya_lat, o_f, o_b, bg, yn, x, modsel, g_gla, g2, w_out_b, w_router, b_router):
    bsz, t, d = x.shape
    tm = TOK_TILE
    tok = lambda b, i: (b, i, 0)
    const = lambda b, i: (0, 0)
    r_block = (2 * B_QK + B_WIDTH) // B_WIDTH
    return pl.pallas_call(
        _out_proj_body,
        grid=(bsz, t // tm),
        in_specs=[pl.BlockSpec((1, tm, A_WIDTH), lambda b, i: (b, 0, 0)),
                  pl.BlockSpec((1, tm, A_WIDTH), lambda b, i: (b, jnp.maximum(i - 1, 0), 0)),
                  pl.BlockSpec((1, tm, B_WIDTH), tok), pl.BlockSpec((1, tm, B_WIDTH), tok),
                  pl.BlockSpec((1, tm, B_WIDTH), lambda b, i: (b, i, r_block)),
                  pl.BlockSpec((1, tm, C_WIDTH), tok),
                  pl.BlockSpec((1, tm, d), tok),
                  pl.BlockSpec((None, None, None, 6, d), lambda b, i: (layer, b, jnp.minimum(i, 1), 0, 0)),
                  pl.BlockSpec((1, B_WIDTH), const), pl.BlockSpec((1, d), const),
                  pl.BlockSpec(w_out_b.shape, const),
                  pl.BlockSpec((d, 2 * ROUTER_PAD), const), pl.BlockSpec((1, ROUTER_PAD), const)],
        out_specs=[pl.BlockSpec((1, tm, d), tok), pl.BlockSpec((1, tm, d // 2), tok),
                   pl.BlockSpec((1, tm, ROUTER_PAD), tok), pl.BlockSpec((8, ROUTER_PAD), const)],
        out_shape=[jax.ShapeDtypeStruct((bsz, t, d), F32), jax.ShapeDtypeStruct((bsz, t, d // 2), jnp.uint32),
                   jax.ShapeDtypeStruct((bsz, t, ROUTER_PAD), F32), jax.ShapeDtypeStruct((8, ROUTER_PAD), F32)],
        scratch_shapes=[pltpu.VMEM((8, ROUTER_PAD), F32)],
        compiler_params=_cparams(("arbitrary", "arbitrary")),
        name="out_proj",
    )(ya_ctx, ya_lat, o_f, o_b, bg, yn, x, modsel, jnp.tile(g_gla, B_HEADS).reshape(1, B_WIDTH), g2.reshape(1, d),
      w_out_b, w_router, b_router)


def _dispatch_plan(expert, rank, counts, n_blocks):
    padded = (counts + EXPERT_TILE - 1) // EXPERT_TILE * EXPERT_TILE
    pad_end = jnp.cumsum(padded)
    pad_start = (pad_end - padded).astype(jnp.int32)
    lanes = jnp.arange(N_EXPERTS, dtype=jnp.int32)
    dest = rank + jnp.sum(jnp.where(expert[..., None] == lanes, pad_start, 0), axis=-1)
    blk_start = jnp.arange(n_blocks, dtype=jnp.int32) * EXPERT_TILE
    blk_expert = jnp.sum((pad_end[None, :] <= blk_start[:, None]).astype(jnp.int32), axis=1)
    blk_expert = jnp.minimum(blk_expert, N_EXPERTS - 1)
    blk_valid = jnp.clip(pad_start[blk_expert] + counts[blk_expert] - blk_start, 0, EXPERT_TILE)
    return dest.astype(jnp.int32), blk_expert.astype(jnp.int32), blk_valid.astype(jnp.int32)


def _sc_dispatch(table, dest, n_rows):
    n_tok, width = table.shape
    n_workers = SC_CORES * SC_SUBCORES
    per_worker = n_tok // n_workers
    chunk = SC_ROW_BUFFER_BYTES // (2 * width * table.dtype.itemsize)
    n_chunks = per_worker // chunk
    assert per_worker * n_workers == n_tok and n_chunks * chunk == per_worker and n_chunks % 2 == 0
    assert n_chunks >= 4 and chunk <= LANES and dest.shape == (n_tok, TOP_K)
    mesh = plsc.VectorSubcoreMesh(core_axis_name="core", subcore_axis_name="subcore")

    def body(table_hbm, d0_hbm, d1_hbm, out_hbm, i0_v, i1_v, rows_v, read_sem, put_sem):
        worker = lax.axis_index("subcore") * SC_CORES + lax.axis_index("core")
        base = worker * per_worker
        pltpu.sync_copy(d0_hbm.at[worker], i0_v)
        pltpu.sync_copy(d1_hbm.at[worker], i1_v)

        def read(j, slot):
            return pltpu.make_async_copy(table_hbm.at[pl.ds(base + j * chunk, chunk)], rows_v.at[slot],
                                         read_sem.at[slot])

        def put(j, slot, idx_v, k):
            return pltpu.make_async_copy(rows_v.at[slot], out_hbm.at[idx_v.at[j]], put_sem.at[slot, k])

        def drain(j, slot):
            read(j, slot).wait()
            put(j, slot, i0_v, 0).start()
            put(j, slot, i1_v, 1).start()
            put(j, slot, i0_v, 0).wait()
            put(j, slot, i1_v, 1).wait()

        read(0, 0).start()
        read(1, 1).start()

        @pl.loop(0, n_chunks - 2, step=2)
        def _(j):
            for slot in range(2):
                drain(j + slot, slot)
                read(j + slot + 2, slot).start()

        drain(n_chunks - 2, 0)
        drain(n_chunks - 1, 1)

    idx = dest.reshape(n_workers, n_chunks, chunk, TOP_K)
    return pl.kernel(
        body,
        out_type=jax.ShapeDtypeStruct((n_rows, width), table.dtype),
        mesh=mesh,
        scratch_types=[pltpu.VMEM((n_chunks, chunk), jnp.int32), pltpu.VMEM((n_chunks, chunk), jnp.int32),
                       pltpu.VMEM((2, chunk, width), table.dtype),
                       pltpu.SemaphoreType.DMA((2,)), pltpu.SemaphoreType.DMA((2, TOP_K))],
        name="sc_row_dispatch",
    )(table, idx[..., 0], idx[..., 1])


def _sc_gather(table, idx):
    n_rows = idx.shape[0]
    width = table.shape[1]
    n_workers = SC_CORES * SC_SUBCORES
    per_worker = n_rows // n_workers
    chunk = SC_ROW_BUFFER_BYTES // (width * table.dtype.itemsize)
    n_chunks = per_worker // chunk
    assert per_worker * n_workers == n_rows and n_chunks * chunk == per_worker and n_chunks % 2 == 0
    assert n_chunks >= 4 and chunk <= LANES
    mesh = plsc.VectorSubcoreMesh(core_axis_name="core", subcore_axis_name="subcore")

    def body(table_hbm, idx_hbm, out_hbm, idx_v, rows_v, gather_sem, write_sem):
        worker = lax.axis_index("subcore") * SC_CORES + lax.axis_index("core")
        base = worker * per_worker
        pltpu.sync_copy(idx_hbm.at[worker], idx_v)

        def gather(j, slot):
            return pltpu.make_async_copy(table_hbm.at[idx_v.at[j]], rows_v.at[slot], gather_sem.at[slot])

        def write(j, slot):
            return pltpu.make_async_copy(rows_v.at[slot], out_hbm.at[pl.ds(base + j * chunk, chunk)],
                                         write_sem.at[slot])

        gather(0, 0).start()
        gather(0, 0).wait()
        gather(1, 1).start()
        write(0, 0).start()

        @pl.loop(1, n_chunks - 1, step=2)
        def _(j):
            for s in range(2):
                slot = (1 + s) % 2
                gather(j + s, slot).wait()
                write(j + s - 1, 1 - slot).wait()
                gather(j + s + 1, 1 - slot).start()
                write(j + s, slot).start()

        last = n_chunks - 1
        gather(last, 1).wait()
        write(last, 1).start()
        write(last - 1, 0).wait()
        write(last, 1).wait()

    return pl.kernel(
        body,
        out_type=jax.ShapeDtypeStruct((n_rows, width), table.dtype),
        mesh=mesh,
        scratch_types=[pltpu.VMEM((n_chunks, chunk), jnp.int32),
                       pltpu.VMEM((2, chunk, width), table.dtype),
                       pltpu.SemaphoreType.DMA((2,)), pltpu.SemaphoreType.DMA((2,))],
        name="sc_row_gather",
    )(table, idx.reshape(n_workers, n_chunks, chunk))


def _expert_body(be_ref, used_ref, x_ref, wu_ref, wd_ref, o_ref, wub_ref, wdb_ref):
    i = pl.program_id(0)
    prev = be_ref[jnp.maximum(i - 1, 0)]

    @pl.when((i == 0) | (be_ref[i] != prev))
    def _():
        wub_ref[...] = wu_ref[...].astype(wub_ref.dtype)
        wdb_ref[...] = wd_ref[...].astype(wdb_ref.dtype)

    @pl.when(used_ref[i] > 0)
    def _():
        real = lax.broadcasted_iota(jnp.int32, x_ref.shape, 0) < used_ref[i]
        lo, hi = _unpack_bf16_pairs(jnp.where(real, x_ref[...], jnp.uint32(0)))
        half = lo.shape[1]
        gu = (jnp.dot(lo.astype(MXU_DTYPE), wub_ref[:half, :], preferred_element_type=F32)
              + jnp.dot(hi.astype(MXU_DTYPE), wub_ref[half:, :], preferred_element_type=F32))
        act = _silu(gu[:, :EXPERT_HIDDEN]) * gu[:, EXPERT_HIDDEN:]
        o_ref[...] = _pack_bf16_pairs(jnp.dot(act.astype(MXU_DTYPE), wdb_ref[...], preferred_element_type=F32))

    @pl.when(used_ref[i] == 0)
    def _():
        o_ref[...] = jnp.zeros_like(o_ref)


def _expert_ffn(layer, buf, blk_expert, blk_used, w_up, w_down):
    n_rows, packed = buf.shape
    d = 2 * packed
    n_blocks = n_rows // EXPERT_TILE
    h2 = w_up.shape[-1]
    grid_spec = pltpu.PrefetchScalarGridSpec(
        num_scalar_prefetch=2,
        grid=(n_blocks,),
        in_specs=[pl.BlockSpec((EXPERT_TILE, packed), lambda i, be, us: (i, 0)),
                  pl.BlockSpec((None, None, d, h2), lambda i, be, us: (layer, be[i], 0, 0)),
                  pl.BlockSpec((None, None, h2 // 2, d), lambda i, be, us: (layer, be[i], 0, 0))],
        out_specs=pl.BlockSpec((EXPERT_TILE, packed), lambda i, be, us: (i, 0)),
        scratch_shapes=[pltpu.VMEM((d, h2), MXU_DTYPE), pltpu.VMEM((h2 // 2, d), MXU_DTYPE)],
    )
    return pl.pallas_call(
        _expert_body,
        grid_spec=grid_spec,
        out_shape=jax.ShapeDtypeStruct((n_rows, packed), jnp.uint32),
        compiler_params=_cparams(("arbitrary",)),
        name="expert_ffn",
    )(blk_expert, blk_used, buf, w_up, w_down)


def _moe(layer, h2, route, counts, w_up, w_down):
    bsz, t, packed = h2.shape
    n_tok = bsz * t
    n_assign = n_tok * TOP_K
    n_blocks = -(-(n_assign + N_EXPERTS * (EXPERT_TILE - 1)) // EXPERT_TILE)
    record = route.reshape(n_tok, ROUTER_PAD)
    expert = record[:, ROUTE_EXPERT:ROUTE_EXPERT + TOP_K].astype(jnp.int32)
    rank = record[:, ROUTE_RANK:ROUTE_RANK + TOP_K].astype(jnp.int32)
    dest, blk_expert, blk_valid = _dispatch_plan(expert, rank, counts[0, :N_EXPERTS].astype(jnp.int32), n_blocks)
    buf = _sc_dispatch(h2.reshape(n_tok, packed), dest, n_blocks * EXPERT_TILE)
    y = _expert_ffn(layer, buf, blk_expert, blk_valid, w_up, w_down)
    return _sc_gather(y, dest.T.reshape(-1)).reshape(TOP_K, bsz, t, packed)


def _final_body(x_ref, y0_ref, y1_ref, rt_ref, mod_ref, g_ref, o_ref):
    o_ref[0] = _rms(_moe_residual(x_ref, y0_ref, y1_ref, rt_ref, mod_ref[5:6, :]), g_ref[...])


def _final_norm(layer, x, ymoe, route, modsel, g, ctx_tiles):
    bsz, t, d = x.shape
    tm = TOK_TILE
    lat = lambda b, i: (b, i + ctx_tiles, 0)
    return pl.pallas_call(
        _final_body,
        grid=(bsz, t // tm - ctx_tiles),
        in_specs=[pl.BlockSpec((1, tm, d), lat)] + _moe_specs(tm, d, lambda i: i + ctx_tiles)
        + [pl.BlockSpec((None, None, None, 6, d), lambda b, i: (layer, b, 1, 0, 0)),
           pl.BlockSpec((1, d), lambda b, i: (0, 0))],
        out_specs=pl.BlockSpec((1, tm, d), lambda b, i: (b, i, 0)),
        out_shape=jax.ShapeDtypeStruct((bsz, t - ctx_tiles * tm, d), F32),
        compiler_params=_cparams(("parallel", "arbitrary")),
        name="final_norm",
    )(x, ymoe, ymoe, route, modsel, g.reshape(1, d))


def _rope_tables(n_ctx, n_lat):
    t = jnp.arange(n_lat)
    row = (t // GRID_W).astype(F32)
    col = (t % GRID_W).astype(F32)
    n_freq = HEAD_DIM // 4
    inv = ROPE_THETA ** (-jnp.arange(n_freq, dtype=F32) / n_freq)
    ang_r = row[:, None] * inv
    ang_c = col[:, None] * inv
    cr, sr, cc, sc = jnp.cos(ang_r), jnp.sin(ang_r), jnp.cos(ang_c), jnp.sin(ang_c)
    z = jnp.zeros_like(sr)
    cos = jnp.concatenate([cr, cr, cc, cc], axis=-1)
    above = jnp.concatenate([-sr, z, -sc, z], axis=-1)
    below = jnp.concatenate([z, sr, z, sc], axis=-1)
    reps = LANES // HEAD_DIM

    def full(tab, ctx_value):
        tab = jnp.tile(tab, (1, reps))
        return jnp.concatenate([jnp.full((n_ctx, LANES), ctx_value, F32), tab], axis=0)

    return full(cos, 1.0), full(above, 0.0), full(below, 0.0)


def _pack_w_in(w_in):
    parts = jnp.split(w_in, np.cumsum(IN_SIZES)[:-1].tolist(), axis=-1)
    qa, ka, va, qb, kb, vb, rb, gb, qn, kn, vn = parts
    gb = jnp.pad(gb, ((0, 0), (0, GATE_PAD - gb.shape[-1])))
    return jnp.concatenate([qa, ka, va, qb, kb, vb, rb, gb, qn, kn, vn], axis=-1).astype(MXU_DTYPE)


def kernel(x, c, ctx, c_ctx, w_mod, b_mod, norm1_g, norm2_g, w_in, w_out, diff_lambda, diff_sub_g,
           gla_w_decay, gla_b_decay, gla_norm_g, na_rel_bias, w_router_group, b_router_group,
           w_router_expert, b_router_expert, w_expert_up, w_expert_down, final_g):
    bsz, seq, d = x.shape
    n_ctx = ctx.shape[1]
    depth = w_mod.shape[0]
    assert n_ctx == TOK_TILE and seq % (NA_TILE_ROWS * GRID_W) == 0 and d % LANES == 0
    assert seq // (NA_TILE_ROWS * GRID_W) >= NA_KEY_TILES
    mod_rows = -(-(bsz + 1) // 8) * 8
    cvec = jnp.zeros((mod_rows, d), F32).at[:bsz].set(c).at[bsz].set(c_ctx)
    mod = _modulation(cvec, w_mod, b_mod).reshape(depth, mod_rows, 6, d)
    modsel = jnp.stack([jnp.broadcast_to(mod[:, bsz][:, None], (depth, bsz, 6, d)), mod[:, :bsz]], axis=2)
    rope_tabs = _rope_tables(n_ctx, seq)
    xt = jnp.concatenate([ctx, x], axis=1)
    n_groups = BATCH_GROUPS if bsz % BATCH_GROUPS == 0 else 1
    gb = bsz // n_groups
    groups = [dict(xt=xt[g * gb:(g + 1) * gb], mod=modsel[:, g * gb:(g + 1) * gb], ymoe=None, route=None)
              for g in range(n_groups)]
    for l in range(depth):
        w_in_p = _pack_w_in(w_in[l])
        bias_tabs = _na_bias_tables(na_rel_bias[l], seq // GRID_W)
        w_router = jnp.pad(jnp.concatenate([w_router_group[l], w_router_expert[l]], axis=-1),
                           ((0, 0), (0, ROUTER_PAD - N_GROUPS - N_EXPERTS)))
        w_router = jnp.concatenate(_split_bf16(w_router, 2), axis=-1)
        b_router = jnp.pad(jnp.concatenate([b_router_group[l], b_router_expert[l]]),
                           (0, ROUTER_PAD - N_GROUPS - N_EXPERTS)).reshape(1, ROUTER_PAD)
        w_out_b = w_out[l].astype(MXU_DTYPE)
        for st in groups:
            outs = _in_proj(l, st["xt"], st["ymoe"], st["route"], st["mod"], norm1_g[l], w_in_p, rope_tabs)
            qa, ka, va, bg, qn, kn, vn = outs[:7]
            if st["ymoe"] is not None:
                st["xt"] = outs[7]
            ya_ctx, ya_lat = _diff_attention(l, qa, ka, va, diff_lambda[l], diff_sub_g[l])
            o_f, o_b = _gla_scan(bg, gla_w_decay[l], gla_b_decay[l])
            yn = _neighborhood_attention(qn, kn, vn, bias_tabs)
            st["xt"], h2, st["route"], counts = _out_proj(l, ya_ctx, ya_lat, o_f, o_b, bg, yn, st["xt"], st["mod"],
                                                          gla_norm_g[l], norm2_g[l], w_out_b, w_router, b_router)
            st["ymoe"] = _moe(l, h2, st["route"], counts, w_expert_up, w_expert_down)
    outs = [_final_norm(depth - 1, st["xt"], st["ymoe"], st["route"], st["mod"], final_g, n_ctx // TOK_TILE)
            for st in groups]
    return jnp.concatenate(outs, axis=0)
```

```python
import functools
import math

import numpy as np
import jax
import jax.numpy as jnp
from jax import lax
from jax.experimental import pallas as pl
from jax.experimental.pallas import tpu as pltpu
from jax.experimental.pallas import tpu_sc as plsc

F32 = jnp.float32
MXU_DTYPE = jnp.bfloat16
HI = lax.Precision.HIGHEST

GRID_W = 64
HEAD_DIM = 64
ROPE_THETA = 10000.0
NORM_EPS = 1e-6

A_HEADS = 4
A_QK = HEAD_DIM
A_V = 2 * HEAD_DIM
B_HEADS = 4
B_DK = HEAD_DIM // 2
B_DV = HEAD_DIM
B_GATE_RANK = 16
B_GATE_TAU = 16.0
B_CHUNK = 64
LOG2_E = math.log2(math.e)
GLA_SUB = 16
C_HEADS = 4
C_DH = HEAD_DIM
NA_ROWS = 8
NA_COLS = 16

A_WIDTH = A_HEADS * A_V
B_WIDTH = B_HEADS * B_DV
C_WIDTH = C_HEADS * C_DH
B_QK = B_HEADS * B_DK
IN_SIZES = (A_HEADS * 2 * A_QK, A_HEADS * 2 * A_QK, A_WIDTH,
            B_QK, B_QK, B_WIDTH, B_WIDTH, 2 * B_GATE_RANK,
            C_WIDTH, C_WIDTH, C_WIDTH)

N_GROUPS = 4
EXPERTS_PER_GROUP = 8
N_EXPERTS = N_GROUPS * EXPERTS_PER_GROUP
TOP_K = 2
EXPERT_HIDDEN = 512

LANES = 128
TOK_TILE = 256
NA_TILE_ROWS = 4
NA_KEY_TILES = 3
BATCH_GROUPS = 1
IN_PROJ_TILE = 256
EXPERT_TILE = 256
GATE_PAD = LANES
BG_WIDTH = 2 * B_QK + 2 * B_WIDTH + GATE_PAD
IN_PAD_WIDTH = 3 * A_WIDTH + BG_WIDTH + 3 * C_WIDTH
ROUTER_PAD = LANES
ROUTE_EXPERT = 0
ROUTE_GATE = 2
ROUTE_RANK = 4
SC_CORES = 2
SC_SUBCORES = 16
SC_ROW_BUFFER_BYTES = 128 * 1024
VMEM_LIMIT = 48 * 1024 * 1024


def _split_bf16(x, pieces):
    out = []
    for _ in range(pieces):
        p = x.astype(jnp.bfloat16)
        out.append(p)
        x = x - p.astype(F32)
    return out


def _pack_bf16_pairs(x):
    w = x.shape[1] // 2
    bits = lax.bitcast_convert_type(x.astype(jnp.bfloat16).astype(F32), jnp.uint32)
    return (bits[:, :w] >> 16) | (bits[:, w:] & jnp.uint32(0xFFFF0000))


def _unpack_bf16_pairs(u):
    lo = lax.bitcast_convert_type(u << 16, F32)
    hi = lax.bitcast_convert_type(u & jnp.uint32(0xFFFF0000), F32)
    return lo, hi


def _silu(x):
    return x * (1.0 / (1.0 + jnp.exp(-x)))


def _cparams(sem):
    return pltpu.CompilerParams(dimension_semantics=sem, vmem_limit_bytes=VMEM_LIMIT)


def _mod_body(c_ref, w_ref, b_ref, o_ref):
    a = _silu(c_ref[...])
    o_ref[...] = jnp.dot(a, w_ref[...], precision=HI, preferred_element_type=F32) + b_ref[...]


def _modulation(cvec, w_mod, b_mod):
    depth, d, d6 = w_mod.shape
    rows = cvec.shape[0]
    return pl.pallas_call(
        _mod_body,
        grid=(depth, d6 // d),
        in_specs=[pl.BlockSpec((rows, d), lambda l, j: (0, 0)),
                  pl.BlockSpec((None, d, d), lambda l, j: (l, 0, j)),
                  pl.BlockSpec((None, 1, d), lambda l, j: (l, 0, j))],
        out_specs=pl.BlockSpec((None, rows, d), lambda l, j: (l, 0, j)),
        out_shape=jax.ShapeDtypeStruct((depth, rows, d6), F32),
        compiler_params=_cparams(("arbitrary", "arbitrary")),
        name="modulation",
    )(cvec, w_mod, b_mod.reshape(depth, 1, d6))


def _rms(x, g):
    return x * lax.rsqrt(jnp.mean(x * x, axis=-1, keepdims=True) + NORM_EPS) * g


def _rope(x, cos, sa, sb):
    return x * cos + pltpu.roll(x, LANES - 16, 1) * sa + pltpu.roll(x, 16, 1) * sb


def _moe_residual(x_ref, y0_ref, y1_ref, rt_ref, gate2):
    rt = rt_ref[0]
    g0, g1 = rt[:, ROUTE_GATE:ROUTE_GATE + 1], rt[:, ROUTE_GATE + 1:ROUTE_GATE + 2]
    lo0, hi0 = _unpack_bf16_pairs(y0_ref[0])
    lo1, hi1 = _unpack_bf16_pairs(y1_ref[0])
    moe = jnp.concatenate([g0 * lo0 + g1 * lo1, g0 * hi0 + g1 * hi1], axis=1)
    return x_ref[0] + gate2 * moe


def _in_proj_body(combine, ctx_len, *refs):
    if combine:
        (x_ref, y0_ref, y1_ref, rt_ref, pmod_ref, mod_ref, g_ref, w_ref, cos_ref, sa_ref, sb_ref,
         qa_ref, ka_ref, va_ref, bg_ref, qn_ref, kn_ref, vn_ref, xo_ref) = refs
    else:
        (x_ref, mod_ref, g_ref, w_ref, cos_ref, sa_ref, sb_ref,
         qa_ref, ka_ref, va_ref, bg_ref, qn_ref, kn_ref, vn_ref) = refs
    tm = x_ref.shape[1]
    row = lax.broadcasted_iota(jnp.int32, (tm, 1), 0) + pl.program_id(1) * tm
    is_ctx = row < ctx_len

    def mod_row(ref, j):
        return jnp.where(is_ctx, ref[0, j:j + 1, :], ref[1, j:j + 1, :])

    if combine:
        x = _moe_residual(x_ref, y0_ref, y1_ref, rt_ref, mod_row(pmod_ref, 5))
        xo_ref[0] = x
    else:
        x = x_ref[0]
    h = _rms(x, g_ref[...]) * (1.0 + mod_row(mod_ref, 1)) + mod_row(mod_ref, 0)
    hb = h.astype(MXU_DTYPE)

    def proj(lo, hi):
        return jnp.dot(hb, w_ref[:, lo:hi], preferred_element_type=F32)

    cos, sa, sb = cos_ref[...], sa_ref[...], sb_ref[...]
    qk = proj(0, 2 * A_WIDTH)
    for hh in range(A_HEADS):
        lo = hh * LANES
        q = qk[:, lo:lo + LANES]
        qa_ref[0, :, lo:lo + LANES] = (_rope(q, cos, sa, sb) * (A_QK ** -0.5 * LOG2_E)).astype(qa_ref.dtype)
        k = qk[:, A_WIDTH + lo:A_WIDTH + lo + LANES]
        ka_ref[0, :, lo:lo + LANES] = _rope(k, cos, sa, sb).astype(ka_ref.dtype)
    o = 2 * A_WIDTH
    va = proj(o, o + A_WIDTH)
    ones = jnp.ones((va.shape[0], A_V), F32)
    va_ref[0] = jnp.concatenate([piece for hh in range(A_HEADS)
                                 for piece in (va[:, hh * A_V:(hh + 1) * A_V], ones)],
                                axis=1).astype(va_ref.dtype)
    o += A_WIDTH
    bg_ref[0] = proj(o, o + BG_WIDTH)
    o += BG_WIDTH
    qn_ref[0] = (proj(o, o + C_WIDTH) * (C_DH ** -0.5)).astype(qn_ref.dtype)
    o += C_WIDTH
    kn_ref[0] = proj(o, o + C_WIDTH).astype(kn_ref.dtype)
    o += C_WIDTH
    vn_ref[0] = proj(o, o + C_WIDTH).astype(vn_ref.dtype)


def _moe_specs(tm, d, row_block):
    return [pl.BlockSpec((None, 1, tm, d // 2), lambda b, i: (0, b, row_block(i), 0)),
            pl.BlockSpec((None, 1, tm, d // 2), lambda b, i: (1, b, row_block(i), 0)),
            pl.BlockSpec((1, tm, ROUTER_PAD), lambda b, i: (b, row_block(i), 0))]


def _in_proj(layer, x, ymoe, route, modsel, g1, w_in_p, rope_tabs):
    bsz, t, d = x.shape
    tm = IN_PROJ_TILE
    combine = ymoe is not None
    tok = lambda b, i: (b, i, 0)
    x_spec = pl.BlockSpec((1, tm, d), tok)

    def mod_spec(l):
        return pl.BlockSpec((None, None, 2, 6, d), lambda b, i: (l, b, 0, 0, 0))

    tab_spec = pl.BlockSpec((tm, LANES), lambda b, i: (i, 0))
    in_specs = [x_spec]
    args = [x]
    if combine:
        in_specs += _moe_specs(tm, d, lambda i: i) + [mod_spec(layer - 1)]
        args += [ymoe, ymoe, route, modsel]
    in_specs += [mod_spec(layer), pl.BlockSpec((1, d), lambda b, i: (0, 0)),
                 pl.BlockSpec((d, IN_PAD_WIDTH), lambda b, i: (0, 0)), tab_spec, tab_spec, tab_spec]
    args += [modsel, g1.reshape(1, d), w_in_p, *rope_tabs]

    def o(width, dtype):
        return pl.BlockSpec((1, tm, width), tok), jax.ShapeDtypeStruct((bsz, t, width), dtype)

    outs = [o(A_WIDTH, MXU_DTYPE), o(A_WIDTH, MXU_DTYPE), o(2 * A_WIDTH, MXU_DTYPE), o(BG_WIDTH, F32),
            o(C_WIDTH, MXU_DTYPE), o(C_WIDTH, MXU_DTYPE), o(C_WIDTH, MXU_DTYPE)]
    if combine:
        outs.append(o(d, F32))
    return pl.pallas_call(
        functools.partial(_in_proj_body, combine, TOK_TILE),
        grid=(bsz, pl.cdiv(t, tm)),
        in_specs=in_specs,
        out_specs=[s for s, _ in outs],
        out_shape=[s for _, s in outs],
        compiler_params=_cparams(("parallel", "arbitrary")),
        name="in_proj",
    )(*args)


def _diff_rows(lam_init, q, k_ref, v_ref, n_keys, lam, g):
    lane = lax.broadcasted_iota(jnp.int32, (1, LANES), 1)
    zero = jnp.zeros_like(q)
    nt = (((1,), (1,)), ((), ()))
    k = k_ref[0, :n_keys, :]
    s1 = lax.dot_general(jnp.where(lane < A_QK, q, zero), k, nt, preferred_element_type=F32)
    s2 = lax.dot_general(jnp.where(lane >= A_QK, q, zero), k, nt, preferred_element_type=F32)

    def finish():
        v1 = v_ref[0, :n_keys, :]
        outs = []
        for s in (s1, s2):
            p = jnp.exp2((s - jnp.max(s, axis=-1, keepdims=True)).astype(MXU_DTYPE))
            outs.append(jnp.dot(p, v1, preferred_element_type=F32))
        o = (outs[0][:, :A_V] * (1.0 / outs[0][:, A_V:A_V + 1])
             - outs[1][:, :A_V] * (lam / outs[1][:, A_V:A_V + 1]))
        return _rms(o, g) * (1.0 - lam_init)

    return finish


def _diff_attn_body(lam_init, ctx_len, qa_ref, qb_ref, k_ref, v_ref, lam_ref, g_ref, oc_ref, ol_ref):
    lm = lam_ref[...]
    lam = (jnp.exp(jnp.sum(lm[0:1] * lm[1:2], axis=1, keepdims=True))
           - jnp.exp(jnp.sum(lm[2:3] * lm[3:4], axis=1, keepdims=True)) + lam_init)
    g = g_ref[...]
    rows = qa_ref.shape[1]

    @pl.when(pl.program_id(2) == 0)
    def _():
        oc_ref[0] = _diff_rows(lam_init, qa_ref[0], k_ref, v_ref, ctx_len, lam, g)().astype(oc_ref.dtype)

    @pl.when(pl.program_id(2) > 0)
    def _():
        n_keys = k_ref.shape[1]
        fin_a = _diff_rows(lam_init, qa_ref[0], k_ref, v_ref, n_keys, lam, g)
        fin_b = _diff_rows(lam_init, qb_ref[0], k_ref, v_ref, n_keys, lam, g)
        ol_ref[0, :rows, :] = fin_a().astype(ol_ref.dtype)
        ol_ref[0, rows:, :] = fin_b().astype(ol_ref.dtype)


def _diff_attention(layer, qa, ka, va, lam, g_sub):
    bsz, t, _ = qa.shape
    tq = TOK_TILE
    n_lat = (t - tq) // (2 * tq)
    assert n_lat * 2 * tq == t - tq
    lam_init = 0.8 - 0.6 * math.exp(-0.3 * layer)

    def q_spec(off):
        return pl.BlockSpec((1, tq, LANES), lambda b, h, i: (b, jnp.maximum(2 * i + off, 0), h))

    return pl.pallas_call(
        functools.partial(_diff_attn_body, lam_init, TOK_TILE),
        grid=(bsz, A_HEADS, 1 + n_lat),
        in_specs=[q_spec(-1), q_spec(0),
                  pl.BlockSpec((1, t, LANES), lambda b, h, i: (b, 0, h)),
                  pl.BlockSpec((1, t, 2 * A_V), lambda b, h, i: (b, 0, h)),
                  pl.BlockSpec((4, A_QK), lambda b, h, i: (0, 0)),
                  pl.BlockSpec((1, A_V), lambda b, h, i: (0, 0))],
        out_specs=[pl.BlockSpec((1, tq, LANES), lambda b, h, i: (b, 0, h)),
                   pl.BlockSpec((1, 2 * tq, LANES), lambda b, h, i: (b, jnp.maximum(i - 1, 0), h))],
        out_shape=[jax.ShapeDtypeStruct((bsz, tq, A_WIDTH), MXU_DTYPE),
                   jax.ShapeDtypeStruct((bsz, t - tq, A_WIDTH), MXU_DTYPE)],
        compiler_params=_cparams(("parallel", "parallel", "arbitrary")),
        name="diff_attention",
    )(qa, qa, ka, va, lam, g_sub.reshape(1, A_V))


def _gla_body(f_ref, r_ref, wdec_ref, bdec_ref, of_ref, ob_ref, sf_ref, sb_ref, es_ref, bs_ref, qss_ref):
    c = B_CHUNK
    n_chunks = TOK_TILE // c

    @pl.when(pl.program_id(1) == 0)
    def _():
        sf_ref[...] = jnp.zeros_like(sf_ref)
        sb_ref[...] = jnp.zeros_like(sb_ref)

    sub = GLA_SUB
    n_sub = c // sub
    nt = (((1,), (1,)), ((), ()))
    t_row = lax.broadcasted_iota(jnp.int32, (TOK_TILE, TOK_TILE), 0)
    t_col = lax.broadcasted_iota(jnp.int32, (TOK_TILE, TOK_TILE), 1)
    same_chunk = (t_row // c) == (t_col // c)
    tri_f = (same_chunk & (t_col <= t_row)).astype(jnp.bfloat16)
    tri_b = (same_chunk & (t_col >= t_row)).astype(jnp.bfloat16)
    s_iota = lax.broadcasted_iota(jnp.int32, (sub, LANES), 0)
    idx = lax.broadcasted_iota(jnp.int32, (c, LANES), 0)
    head_of_k = lax.broadcasted_iota(jnp.int32, (B_QK, B_WIDTH), 0) // B_DK
    head_of_v = lax.broadcasted_iota(jnp.int32, (B_QK, B_WIDTH), 1) // B_DV
    expand = (head_of_k == head_of_v).astype(MXU_DTYPE)
    same_head_t = (lax.broadcasted_iota(jnp.int32, (B_WIDTH, B_QK), 0) // B_DV
                   == lax.broadcasted_iota(jnp.int32, (B_WIDTH, B_QK), 1) // B_DK)
    n_ref = n_sub - 1
    kt_keep = (lax.broadcasted_iota(jnp.int32, (B_HEADS * c, n_ref * B_QK), 0) // c
               == (lax.broadcasted_iota(jnp.int32, (B_HEADS * c, n_ref * B_QK), 1) % B_QK) // B_DK)
    vx_keep = (lax.broadcasted_iota(jnp.int32, (B_HEADS * c, B_WIDTH), 0) // c
               == lax.broadcasted_iota(jnp.int32, (B_HEADS * c, B_WIDTH), 1) // B_DV)
    pick = (lax.broadcasted_iota(jnp.int32, (c, c * sub), 1) // sub
            == lax.broadcasted_iota(jnp.int32, (c, c * sub), 0)).astype(MXU_DTYPE)

    def log_decay(src_ref, backward):
        gl = src_ref[0, :, 2 * B_QK + 2 * B_WIDTH:BG_WIDTH]
        d0 = B_QK if backward else 0
        z = jnp.dot(gl, wdec_ref[:, d0:d0 + B_QK], precision=HI, preferred_element_type=F32) \
            + bdec_ref[:, d0:d0 + B_QK]
        log_a = (jnp.minimum(z, 0.0) - jnp.log(1.0 + jnp.exp(-jnp.abs(z)))) / B_GATE_TAU
        tri = tri_b if backward else tri_f
        return sum(jnp.dot(tri, p, preferred_element_type=F32) for p in _split_bf16(log_a, 3))

    def chunk(slot, src_ref, b_all, lo, backward, st_ref, out_ref):
        q = src_ref[0, lo:lo + c, 0:B_QK] * (B_DK ** -0.5)
        k = src_ref[0, lo:lo + c, B_QK:2 * B_QK]
        v = src_ref[0, lo:lo + c, 2 * B_QK:2 * B_QK + B_WIDTH]
        b = b_all[lo:lo + c]
        b_ref, qs_ref, e_ref = bs_ref.at[slot], qss_ref.at[slot], es_ref.at[slot]
        b_ref[...] = b
        qs_ref[...] = q
        blk = ((c - 1 - idx) if backward else idx) // sub

        q_parts, k_parts = [], []
        for m, late, early in ((1, blk == 1, blk == 0), (2, blk >= 2, blk <= 1), (3, blk == 3, blk == 2)):
            r_row = (c - 1 - sub * m) if backward else sub * m
            r = b_ref[r_row:r_row + 1, :]
            q_parts.append(q * jnp.exp(jnp.where(late, b - r, -jnp.inf)))
            k_parts.append(k * jnp.exp(jnp.where(early, r - b, -jnp.inf)))
        q_cat = jnp.concatenate(q_parts, axis=1).astype(MXU_DTYPE)
        k_cat = jnp.concatenate(k_parts, axis=1)
        k_exp = jnp.where(kt_keep, jnp.concatenate([k_cat] * B_HEADS, axis=0), 0.0).astype(MXU_DTYPE)
        a_off = lax.dot_general(q_cat, k_exp, nt, preferred_element_type=F32)

        for tt in range(c):
            lo_s = tt // sub * sub
            keep = (s_iota >= tt - lo_s) if backward else (s_iota <= tt - lo_s)
            bt = b_ref[tt:tt + 1, :]
            qt = qs_ref[tt:tt + 1, :]
            e = jnp.exp(jnp.where(keep, bt - b[lo_s:lo_s + sub], -jnp.inf)) * (qt * k[lo_s:lo_s + sub])
            e_ref[tt * sub:(tt + 1) * sub, :] = e.astype(e_ref.dtype)
        a_exp = jnp.dot(e_ref[...], expand, preferred_element_type=F32)
        b_end = b[0:1, :] if backward else b[c - 1:c, :]
        kd = k * jnp.exp(b_end - b)
        upd = lax.dot_general(v, kd, (((0,), (0,)), ((), ())), preferred_element_type=F32)

        def intra():
            v_exp = jnp.where(vx_keep, jnp.concatenate([v] * B_HEADS, axis=0), 0.0).astype(MXU_DTYPE)
            o_off = jnp.dot(a_off.astype(MXU_DTYPE), v_exp, preferred_element_type=F32)
            prod = a_exp.reshape(n_sub, sub, sub, B_WIDTH) * v.reshape(n_sub, 1, sub, B_WIDTH)
            o_diag = jnp.dot(pick, prod.reshape(c * sub, B_WIDTH).astype(MXU_DTYPE),
                             preferred_element_type=F32)
            o_intra = o_off + o_diag

            def recur():
                st = st_ref[...]
                o_inter = lax.dot_general(q * jnp.exp(b), st, nt, preferred_element_type=F32)
                out_ref[0, lo:lo + c, :] = o_intra + o_inter
                st_ref[...] = jnp.exp(b_end) * st + jnp.where(same_head_t, upd, 0.0)

            return recur

        return intra

    b_fwd = log_decay(f_ref, False)
    b_bwd = log_decay(r_ref, True)
    stage = []
    for ci in range(n_chunks):
        stage.append(chunk(2 * ci, f_ref, b_fwd, ci * c, False, sf_ref, of_ref))
        stage.append(chunk(2 * ci + 1, r_ref, b_bwd, (n_chunks - 1 - ci) * c, True, sb_ref, ob_ref))
    stage = [intra() for intra in stage]
    for recur in stage:
        recur()


def _gla_scan(bg, w_dec, b_dec):
    bsz, t, _ = bg.shape
    n = t // TOK_TILE
    rev = lambda b, i: (b, jnp.where(i == 0, 0, n - i), 0)
    fwd = lambda b, i: (b, i, 0)
    wdec = jnp.zeros((GATE_PAD, 2 * B_QK), F32)
    wdec = wdec.at[:B_GATE_RANK, :B_QK].set(w_dec[0]).at[B_GATE_RANK:2 * B_GATE_RANK, B_QK:].set(w_dec[1])
    bdec = b_dec.reshape(1, 2 * B_QK)
    o_shape = jax.ShapeDtypeStruct((bsz, t, B_WIDTH), F32)
    n_slots = 2 * (TOK_TILE // B_CHUNK)
    return pl.pallas_call(
        _gla_body,
        grid=(bsz, n),
        in_specs=[pl.BlockSpec((1, TOK_TILE, BG_WIDTH), fwd),
                  pl.BlockSpec((1, TOK_TILE, BG_WIDTH), rev),
                  pl.BlockSpec((GATE_PAD, 2 * B_QK), lambda b, i: (0, 0)),
                  pl.BlockSpec((1, 2 * B_QK), lambda b, i: (0, 0))],
        out_specs=[pl.BlockSpec((1, TOK_TILE, B_WIDTH), fwd),
                   pl.BlockSpec((1, TOK_TILE, B_WIDTH), rev)],
        out_shape=[o_shape, o_shape],
        scratch_shapes=[pltpu.VMEM((B_WIDTH, B_QK), F32), pltpu.VMEM((B_WIDTH, B_QK), F32),
                        pltpu.VMEM((n_slots, B_CHUNK * GLA_SUB, LANES), MXU_DTYPE),
                        pltpu.VMEM((n_slots, B_CHUNK, LANES), F32),
                        pltpu.VMEM((n_slots, B_CHUNK, LANES), F32)],
        compiler_params=_cparams(("parallel", "arbitrary")),
        name="gla_scan",
    )(bg, bg, wdec, bdec)


def _na_bias_tables(rpb, rows):
    n_tiles = rows // NA_TILE_ROWS
    wr = min(NA_ROWS, rows)
    n_dr, n_dc = 2 * NA_ROWS - 1, 2 * NA_COLS - 1
    cq = np.arange(GRID_W)[:, None]
    ck = np.arange(GRID_W)[None, :]
    cs = np.clip(cq - NA_COLS // 2, 0, GRID_W - NA_COLS)
    col_ok = (ck >= cs) & (ck < cs + NA_COLS)
    dc = np.clip(ck - cq, -(NA_COLS - 1), NA_COLS - 1) + (NA_COLS - 1)
    onehot = (dc.reshape(1, -1) == np.arange(n_dc)[:, None]).astype(np.float32)
    by_col = jnp.dot(rpb.astype(F32).reshape(-1, n_dc), onehot, precision=HI)
    by_col = jnp.where(col_ok.reshape(1, 1, GRID_W, GRID_W),
                       by_col.reshape(C_HEADS, n_dr, GRID_W, GRID_W), -jnp.inf)
    masked = jnp.full((C_HEADS, GRID_W, GRID_W), -jnp.inf, F32)
    tabs = []
    for j in (0, 1, n_tiles - 1):
        kr0 = int(np.clip(j - 1, 0, n_tiles - NA_KEY_TILES)) * NA_TILE_ROWS
        q_rows = []
        for qr in range(NA_TILE_ROWS):
            r = j * NA_TILE_ROWS + qr
            start = int(np.clip(r - wr // 2, 0, rows - wr))
            blocks = []
            for kw in range(NA_KEY_TILES * NA_TILE_ROWS):
                kr = kr0 + kw
                blocks.append(by_col[:, kr - r + NA_ROWS - 1] if start <= kr < start + wr else masked)
            q_rows.append(jnp.concatenate(blocks, axis=-1))
        tabs.append(jnp.concatenate(q_rows, axis=1))
    return jnp.stack(tabs)


def _na_body(q_ref, k0_ref, k1_ref, k2_ref, kc_ref, v0_ref, v1_ref, v2_ref, vc_ref, m_ref, o_ref):
    q = q_ref[0]
    lane = lax.broadcasted_iota(jnp.int32, (1, C_WIDTH), 1)
    zero = jnp.zeros_like(q)
    nt = (((1,), (1,)), ((), ()))

    def scores(qm, k_ref):
        return lax.dot_general(qm, k_ref[0], nt, preferred_element_type=F32)

    def head_out(hh, windows):
        qm = jnp.where((lane >= hh * C_DH) & (lane < (hh + 1) * C_DH), q, zero)
        s = [scores(qm, kc_ref)]
        for w, k_ref in enumerate(windows):
            s.append(scores(qm, k_ref) + m_ref[0, hh, :, w * TOK_TILE:(w + 1) * TOK_TILE])

        def finish():
            m = functools.reduce(jnp.maximum, [jnp.max(x, axis=-1, keepdims=True) for x in s])
            p = [jnp.exp(x - m) for x in s]
            den = functools.reduce(jnp.add, [jnp.sum(x, axis=-1, keepdims=True) for x in p])
            vals = [vc_ref] + [v0_ref, v1_ref, v2_ref][:len(windows)]
            o = functools.reduce(jnp.add, [jnp.dot(x.astype(MXU_DTYPE), v_ref[0], preferred_element_type=F32)
                                           for x, v_ref in zip(p, vals)])
            return o * (1.0 / den)

        return finish

    def emit(windows):
        finish = [head_out(hh, windows) for hh in range(C_HEADS)]
        o = finish[C_HEADS - 1]()
        for hh in range(C_HEADS - 2, -1, -1):
            o = jnp.where(lane < (hh + 1) * C_DH, finish[hh](), o)
        o_ref[0] = o.astype(o_ref.dtype)

    @pl.when(pl.program_id(0) == 0)
    def _():
        emit([])

    @pl.when(pl.program_id(0) > 0)
    def _():
        emit([k0_ref, k1_ref, k2_ref])


def _neighborhood_attention(qn, kn, vn, bias_tabs):
    bsz, t, _ = qn.shape
    n = t // TOK_TILE
    n_lat = n - 1

    def win(w):
        def index(i, b):
            kb0 = jnp.clip(i - 2, 0, n_lat - NA_KEY_TILES)
            return (b, kb0 + 1 + w, 0)
        return pl.BlockSpec((1, TOK_TILE, C_WIDTH), index)

    def cls(i, b):
        j = i - 1
        return (jnp.where(j <= 0, 0, jnp.where(j == n_lat - 1, 2, 1)), 0, 0, 0)

    own = pl.BlockSpec((1, TOK_TILE, C_WIDTH), lambda i, b: (b, i, 0))
    ctx = pl.BlockSpec((1, TOK_TILE, C_WIDTH), lambda i, b: (b, 0, 0))
    return pl.pallas_call(
        _na_body,
        grid=(n, bsz),
        in_specs=[own, win(0), win(1), win(2), ctx, win(0), win(1), win(2), ctx,
                  pl.BlockSpec((1, C_HEADS, TOK_TILE, NA_KEY_TILES * TOK_TILE), cls)],
        out_specs=own,
        out_shape=jax.ShapeDtypeStruct((bsz, t, C_WIDTH), MXU_DTYPE),
        compiler_params=_cparams(("parallel", "arbitrary")),
        name="neighborhood_attention",
    )(qn, kn, kn, kn, kn, vn, vn, vn, vn, bias_tabs)


def _out_proj_body(yac_ref, yal_ref, of_ref, ob_ref, r_ref, yn_ref, x_ref, mod_ref, gg_ref, g2_ref, w_ref,
                   wr_ref, br_ref, xo_ref, h_ref, rt_ref, cnt_ref, count_ref):
    is_ctx_tile = pl.program_id(1) == 0
    hi = lax.broadcasted_iota(jnp.int32, (B_WIDTH, B_WIDTH), 0) // B_DV
    hj = lax.broadcasted_iota(jnp.int32, (B_WIDTH, B_WIDTH), 1) // B_DV
    head_mean = jnp.where(hi == hj, 1.0 / B_DV, 0.0).astype(jnp.bfloat16)

    @pl.when((pl.program_id(0) == 0) & (pl.program_id(1) == 0))
    def _():
        count_ref[...] = jnp.zeros_like(count_ref)

    def rows(r0, r1):
        ya = jnp.where(is_ctx_tile, yac_ref[0, r0:r1, :], yal_ref[0, r0:r1, :])
        o = of_ref[0, r0:r1, :] + ob_ref[0, r0:r1, :]
        ms = sum(jnp.dot(p, head_mean, preferred_element_type=F32) for p in _split_bf16(o * o, 2))
        yb = o * lax.rsqrt(ms + NORM_EPS) * gg_ref[...] * _silu(r_ref[0, r0:r1, :])

        def project():
            mix = (jnp.dot(ya, w_ref[0:A_WIDTH, :], preferred_element_type=F32)
                   + jnp.dot(yb.astype(MXU_DTYPE), w_ref[A_WIDTH:A_WIDTH + B_WIDTH, :],
                             preferred_element_type=F32)
                   + jnp.dot(yn_ref[0, r0:r1, :], w_ref[A_WIDTH + B_WIDTH:, :], preferred_element_type=F32))
            x = x_ref[0, r0:r1, :] + mod_ref[2:3, :] * mix
            xo_ref[0, r0:r1, :] = x
            h = _rms(x, g2_ref[...]) * (1.0 + mod_ref[4:5, :]) + mod_ref[3:4, :]
            h_ref[0, r0:r1, :] = _pack_bf16_pairs(h)
            h_hi, h_lo = _split_bf16(h, 2)
            both = jnp.dot(h_hi, wr_ref[...], preferred_element_type=F32)
            logits = (both[:, :ROUTER_PAD] + both[:, ROUTER_PAD:]
                      + jnp.dot(h_lo, wr_ref[:, :ROUTER_PAD], preferred_element_type=F32) + br_ref[...])

            def route():
                rt_ref[0, r0:r1, :] = _route(logits, count_ref)

            return route

        return project

    half = x_ref.shape[1] // 2
    stages = [rows(0, half), rows(half, 2 * half)]
    stages = [project() for project in stages]
    for route in stages:
        route()
    cnt_ref[...] = count_ref[...]


def _route(lg, count_ref):
    lane = lax.broadcasted_iota(jnp.int32, lg.shape, 1)
    big = jnp.int32(ROUTER_PAD)

    def top(mask):
        v = jnp.max(jnp.where(mask, lg, -jnp.inf), axis=-1, keepdims=True)
        i = jnp.min(jnp.where(mask & (lg == v), lane, big), axis=-1, keepdims=True)
        return v, i

    g_mask = lane < N_GROUPS
    g_max, grp = top(g_mask)
    p_grp = 1.0 / jnp.sum(jnp.where(g_mask, jnp.exp(lg - g_max), 0.0), axis=-1, keepdims=True)
    e_lo = N_GROUPS + grp * EXPERTS_PER_GROUP
    e_mask = (lane >= e_lo) & (lane < e_lo + EXPERTS_PER_GROUP)
    v1, i1 = top(e_mask)
    v2, i2 = top(e_mask & (lane != i1))
    r = jnp.exp(v2 - v1)
    gate1 = p_grp / (1.0 + r)
    gate2 = p_grp * r / (1.0 + r)
    e1, e2 = i1 - N_GROUPS, i2 - N_GROUPS
    hot1 = (lane == e1).astype(F32)
    hot2 = (lane == e2).astype(F32)
    both = hot1 + hot2
    rows = lg.shape[0]
    earlier = (lax.broadcasted_iota(jnp.int32, (rows, rows), 1)
               < lax.broadcasted_iota(jnp.int32, (rows, rows), 0)).astype(jnp.bfloat16)
    before = jnp.dot(earlier, both.astype(jnp.bfloat16), preferred_element_type=F32) + count_ref[0:1, :]
    rank1 = jnp.sum(hot1 * before, axis=-1, keepdims=True)
    rank2 = jnp.sum(hot2 * before, axis=-1, keepdims=True)
    count_ref[0:1, :] = count_ref[0:1, :] + jnp.sum(both, axis=0, keepdims=True)
    out = jnp.where(lane == ROUTE_EXPERT, e1.astype(F32),
                    jnp.where(lane == ROUTE_EXPERT + 1, e2.astype(F32),
                              jnp.where(lane == ROUTE_GATE, gate1,
                                        jnp.where(lane == ROUTE_GATE + 1, gate2,
                                                  jnp.where(lane == ROUTE_RANK, rank1,
                                                            jnp.where(lane == ROUTE_RANK + 1, rank2, 0.0))))))
    return out


def _out_proj(layer, ya_ctx, ya_lat, o_f, o_b, bg, yn, x, modsel, g_gla, g2, w_out_b, w_router, b_router):
    bsz, t, d = x.shape
    tm = TOK_TILE
    tok = lambda b, i: (b, i, 0)
    const = lambda b, i: (0, 0)
    r_block = (2 * B_QK + B_WIDTH) // B_WIDTH
    return pl.pallas_call(
        _out_proj_body,
        grid=(bsz, t // tm),
        in_specs=[pl.BlockSpec((1, tm, A_WIDTH), lambda b, i: (b, 0, 0)),
                  pl.BlockSpec((1, tm, A_WIDTH), lambda b, i: (b, jnp.maximum(i - 1, 0), 0)),
                  pl.BlockSpec((1, tm, B_WIDTH), tok), pl.BlockSpec((1, tm, B_WIDTH), tok),
                  pl.BlockSpec((1, tm, B_WIDTH), lambda b, i: (b, i, r_block)),
                  pl.BlockSpec((1, tm, C_WIDTH), tok),
                  pl.BlockSpec((1, tm, d), tok),
                  pl.BlockSpec((None, None, None, 6, d), lambda b, i: (layer, b, jnp.minimum(i, 1), 0, 0)),
                  pl.BlockSpec((1, B_WIDTH), const), pl.BlockSpec((1, d), const),
                  pl.BlockSpec(w_out_b.shape, const),
                  pl.BlockSpec((d, 2 * ROUTER_PAD), const), pl.BlockSpec((1, ROUTER_PAD), const)],
        out_specs=[pl.BlockSpec((1, tm, d), tok), pl.BlockSpec((1, tm, d // 2), tok),
                   pl.BlockSpec((1, tm, ROUTER_PAD), tok), pl.BlockSpec((8, ROUTER_PAD), const)],
        out_shape=[jax.ShapeDtypeStruct((bsz, t, d), F32), jax.ShapeDtypeStruct((bsz, t, d // 2), jnp.uint32),
                   jax.ShapeDtypeStruct((bsz, t, ROUTER_PAD), F32), jax.ShapeDtypeStruct((8, ROUTER_PAD), F32)],
        scratch_shapes=[pltpu.VMEM((8, ROUTER_PAD), F32)],
        compiler_params=_cparams(("arbitrary", "arbitrary")),
        name="out_proj",
    )(ya_ctx, ya_lat, o_f, o_b, bg, yn, x, modsel, jnp.tile(g_gla, B_HEADS).reshape(1, B_WIDTH), g2.reshape(1, d),
      w_out_b, w_router, b_router)


def _dispatch_plan(expert, rank, counts, n_blocks):
    padded = (counts + EXPERT_TILE - 1) // EXPERT_TILE * EXPERT_TILE
    pad_end = jnp.cumsum(padded)
    pad_start = (pad_end - padded).astype(jnp.int32)
    lanes = jnp.arange(N_EXPERTS, dtype=jnp.int32)
    dest = rank + jnp.sum(jnp.where(expert[..., None] == lanes, pad_start, 0), axis=-1)
    blk_start = jnp.arange(n_blocks, dtype=jnp.int32) * EXPERT_TILE
    blk_expert = jnp.sum((pad_end[None, :] <= blk_start[:, None]).astype(jnp.int32), axis=1)
    blk_expert = jnp.minimum(blk_expert, N_EXPERTS - 1)
    blk_valid = jnp.clip(pad_start[blk_expert] + counts[blk_expert] - blk_start, 0, EXPERT_TILE)
    return dest.astype(jnp.int32), blk_expert.astype(jnp.int32), blk_valid.astype(jnp.int32)


def _sc_dispatch(table, dest, n_rows):
    n_tok, width = table.shape
    n_workers = SC_CORES * SC_SUBCORES
    per_worker = n_tok // n_workers
    chunk = SC_ROW_BUFFER_BYTES // (2 * width * table.dtype.itemsize)
    n_chunks = per_worker // chunk
    assert per_worker * n_workers == n_tok and n_chunks * chunk == per_worker and n_chunks % 2 == 0
    assert n_chunks >= 4 and chunk <= LANES and dest.shape == (n_tok, TOP_K)
    mesh = plsc.VectorSubcoreMesh(core_axis_name="core", subcore_axis_name="subcore")

    def body(table_hbm, d0_hbm, d1_hbm, out_hbm, i0_v, i1_v, rows_v, read_sem, put_sem):
        worker = lax.axis_index("subcore") * SC_CORES + lax.axis_index("core")
        base = worker * per_worker
        pltpu.sync_copy(d0_hbm.at[worker], i0_v)
        pltpu.sync_copy(d1_hbm.at[worker], i1_v)

        def read(j, slot):
            return pltpu.make_async_copy(table_hbm.at[pl.ds(base + j * chunk, chunk)], rows_v.at[slot],
                                         read_sem.at[slot])

        def put(j, slot, idx_v, k):
            return pltpu.make_async_copy(rows_v.at[slot], out_hbm.at[idx_v.at[j]], put_sem.at[slot, k])

        def drain(j, slot):
            read(j, slot).wait()
            put(j, slot, i0_v, 0).start()
            put(j, slot, i1_v, 1).start()
            put(j, slot, i0_v, 0).wait()
            put(j, slot, i1_v, 1).wait()

        read(0, 0).start()
        read(1, 1).start()

        @pl.loop(0, n_chunks - 2, step=2)
        def _(j):
            for slot in range(2):
                drain(j + slot, slot)
                read(j + slot + 2, slot).start()

        drain(n_chunks - 2, 0)
        drain(n_chunks - 1, 1)

    idx = dest.reshape(n_workers, n_chunks, chunk, TOP_K)
    return pl.kernel(
        body,
        out_type=jax.ShapeDtypeStruct((n_rows, width), table.dtype),
        mesh=mesh,
        scratch_types=[pltpu.VMEM((n_chunks, chunk), jnp.int32), pltpu.VMEM((n_chunks, chunk), jnp.int32),
                       pltpu.VMEM((2, chunk, width), table.dtype),
                       pltpu.SemaphoreType.DMA((2,)), pltpu.SemaphoreType.DMA((2, TOP_K))],
        name="sc_row_dispatch",
    )(table, idx[..., 0], idx[..., 1])


def _sc_gather(table, idx):
    n_rows = idx.shape[0]
    width = table.shape[1]
    n_workers = SC_CORES * SC_SUBCORES
    per_worker = n_rows // n_workers
    chunk = SC_ROW_BUFFER_BYTES // (width * table.dtype.itemsize)
    n_chunks = per_worker // chunk
    assert per_worker * n_workers == n_rows and n_chunks * chunk == per_worker and n_chunks % 2 == 0
    assert n_chunks >= 4 and chunk <= LANES
    mesh = plsc.VectorSubcoreMesh(core_axis_name="core", subcore_axis_name="subcore")

    def body(table_hbm, idx_hbm, out_hbm, idx_v, rows_v, gather_sem, write_sem):
        worker = lax.axis_index("subcore") * SC_CORES + lax.axis_index("core")
        base = worker * per_worker
        pltpu.sync_copy(idx_hbm.at[worker], idx_v)

        def gather(j, slot):
            return pltpu.make_async_copy(table_hbm.at[idx_v.at[j]], rows_v.at[slot], gather_sem.at[slot])

        def write(j, slot):
            return pltpu.make_async_copy(rows_v.at[slot], out_hbm.at[pl.ds(base + j * chunk, chunk)],
                                         write_sem.at[slot])

        gather(0, 0).start()
        gather(0, 0).wait()
        gather(1, 1).start()
        write(0, 0).start()

        @pl.loop(1, n_chunks - 1, step=2)
        def _(j):
            for s in range(2):
                slot = (1 + s) % 2
                gather(j + s, slot).wait()
                write(j + s - 1, 1 - slot).wait()
                gather(j + s + 1, 1 - slot).start()
                write(j + s, slot).start()

        last = n_chunks - 1
        gather(last, 1).wait()
        write(last, 1).start()
        write(last - 1, 0).wait()
        write(last, 1).wait()

    return pl.kernel(
        body,
        out_type=jax.ShapeDtypeStruct((n_rows, width), table.dtype),
        mesh=mesh,
        scratch_types=[pltpu.VMEM((n_chunks, chunk), jnp.int32),
                       pltpu.VMEM((2, chunk, width), table.dtype),
                       pltpu.SemaphoreType.DMA((2,)), pltpu.SemaphoreType.DMA((2,))],
        name="sc_row_gather",
    )(table, idx.reshape(n_workers, n_chunks, chunk))


def _expert_body(be_ref, used_ref, x_ref, wu_ref, wd_ref, o_ref, wub_ref, wdb_ref):
    i = pl.program_id(0)
    prev = be_ref[jnp.maximum(i - 1, 0)]

    @pl.when((i == 0) | (be_ref[i] != prev))
    def _():
        wub_ref[...] = wu_ref[...].astype(wub_ref.dtype)
        wdb_ref[...] = wd_ref[...].astype(wdb_ref.dtype)

    @pl.when(used_ref[i] > 0)
    def _():
        real = lax.broadcasted_iota(jnp.int32, x_ref.shape, 0) < used_ref[i]
        lo, hi = _unpack_bf16_pairs(jnp.where(real, x_ref[...], jnp.uint32(0)))
        half = lo.shape[1]
        gu = (jnp.dot(lo.astype(MXU_DTYPE), wub_ref[:half, :], preferred_element_type=F32)
              + jnp.dot(hi.astype(MXU_DTYPE), wub_ref[half:, :], preferred_element_type=F32))
        act = _silu(gu[:, :EXPERT_HIDDEN]) * gu[:, EXPERT_HIDDEN:]
        o_ref[...] = _pack_bf16_pairs(jnp.dot(act.astype(MXU_DTYPE), wdb_ref[...], preferred_element_type=F32))

    @pl.when(used_ref[i] == 0)
    def _():
        o_ref[...] = jnp.zeros_like(o_ref)


def _expert_ffn(layer, buf, blk_expert, blk_used, w_up, w_down):
    n_rows, packed = buf.shape
    d = 2 * packed
    n_blocks = n_rows // EXPERT_TILE
    h2 = w_up.shape[-1]
    grid_spec = pltpu.PrefetchScalarGridSpec(
        num_scalar_prefetch=2,
        grid=(n_blocks,),
        in_specs=[pl.BlockSpec((EXPERT_TILE, packed), lambda i, be, us: (i, 0)),
                  pl.BlockSpec((None, None, d, h2), lambda i, be, us: (layer, be[i], 0, 0)),
                  pl.BlockSpec((None, None, h2 // 2, d), lambda i, be, us: (layer, be[i], 0, 0))],
        out_specs=pl.BlockSpec((EXPERT_TILE, packed), lambda i, be, us: (i, 0)),
        scratch_shapes=[pltpu.VMEM((d, h2), MXU_DTYPE), pltpu.VMEM((h2 // 2, d), MXU_DTYPE)],
    )
    return pl.pallas_call(
        _expert_body,
        grid_spec=grid_spec,
        out_shape=jax.ShapeDtypeStruct((n_rows, packed), jnp.uint32),
        compiler_params=_cparams(("arbitrary",)),
        name="expert_ffn",
    )(blk_expert, blk_used, buf, w_up, w_down)


def _moe(layer, h2, route, counts, w_up, w_down):
    bsz, t, packed = h2.shape
    n_tok = bsz * t
    n_assign = n_tok * TOP_K
    n_blocks = -(-(n_assign + N_EXPERTS * (EXPERT_TILE - 1)) // EXPERT_TILE)
    record = route.reshape(n_tok, ROUTER_PAD)
    expert = record[:, ROUTE_EXPERT:ROUTE_EXPERT + TOP_K].astype(jnp.int32)
    rank = record[:, ROUTE_RANK:ROUTE_RANK + TOP_K].astype(jnp.int32)
    dest, blk_expert, blk_valid = _dispatch_plan(expert, rank, counts[0, :N_EXPERTS].astype(jnp.int32), n_blocks)
    buf = _sc_dispatch(h2.reshape(n_tok, packed), dest, n_blocks * EXPERT_TILE)
    y = _expert_ffn(layer, buf, blk_expert, blk_valid, w_up, w_down)
    return _sc_gather(y, dest.T.reshape(-1)).reshape(TOP_K, bsz, t, packed)


def _final_body(x_ref, y0_ref, y1_ref, rt_ref, mod_ref, g_ref, o_ref):
    o_ref[0] = _rms(_moe_residual(x_ref, y0_ref, y1_ref, rt_ref, mod_ref[5:6, :]), g_ref[...])


def _final_norm(layer, x, ymoe, route, modsel, g, ctx_tiles):
    bsz, t, d = x.shape
    tm = TOK_TILE
    lat = lambda b, i: (b, i + ctx_tiles, 0)
    return pl.pallas_call(
        _final_body,
        grid=(bsz, t // tm - ctx_tiles),
        in_specs=[pl.BlockSpec((1, tm, d), lat)] + _moe_specs(tm, d, lambda i: i + ctx_tiles)
        + [pl.BlockSpec((None, None, None, 6, d), lambda b, i: (layer, b, 1, 0, 0)),
           pl.BlockSpec((1, d), lambda b, i: (0, 0))],
        out_specs=pl.BlockSpec((1, tm, d), lambda b, i: (b, i, 0)),
        out_shape=jax.ShapeDtypeStruct((bsz, t - ctx_tiles * tm, d), F32),
        compiler_params=_cparams(("parallel", "arbitrary")),
        name="final_norm",
    )(x, ymoe, ymoe, route, modsel, g.reshape(1, d))


def _rope_tables(n_ctx, n_lat):
    t = jnp.arange(n_lat)
    row = (t // GRID_W).astype(F32)
    col = (t % GRID_W).astype(F32)
    n_freq = HEAD_DIM // 4
    inv = ROPE_THETA ** (-jnp.arange(n_freq, dtype=F32) / n_freq)
    ang_r = row[:, None] * inv
    ang_c = col[:, None] * inv
    cr, sr, cc, sc = jnp.cos(ang_r), jnp.sin(ang_r), jnp.cos(ang_c), jnp.sin(ang_c)
    z = jnp.zeros_like(sr)
    cos = jnp.concatenate([cr, cr, cc, cc], axis=-1)
    above = jnp.concatenate([-sr, z, -sc, z], axis=-1)
    below = jnp.concatenate([z, sr, z, sc], axis=-1)
    reps = LANES // HEAD_DIM

    def full(tab, ctx_value):
        tab = jnp.tile(tab, (1, reps))
        return jnp.concatenate([jnp.full((n_ctx, LANES), ctx_value, F32), tab], axis=0)

    return full(cos, 1.0), full(above, 0.0), full(below, 0.0)


def _pack_w_in(w_in):
    parts = jnp.split(w_in, np.cumsum(IN_SIZES)[:-1].tolist(), axis=-1)
    qa, ka, va, qb, kb, vb, rb, gb, qn, kn, vn = parts
    gb = jnp.pad(gb, ((0, 0), (0, GATE_PAD - gb.shape[-1])))
    return jnp.concatenate([qa, ka, va, qb, kb, vb, rb, gb, qn, kn, vn], axis=-1).astype(MXU_DTYPE)


def kernel(x, c, ctx, c_ctx, w_mod, b_mod, norm1_g, norm2_g, w_in, w_out, diff_lambda, diff_sub_g,
           gla_w_decay, gla_b_decay, gla_norm_g, na_rel_bias, w_router_group, b_router_group,
           w_router_expert, b_router_expert, w_expert_up, w_expert_down, final_g):
    bsz, seq, d = x.shape
    n_ctx = ctx.shape[1]
    depth = w_mod.shape[0]
    assert n_ctx == TOK_TILE and seq % (NA_TILE_ROWS * GRID_W) == 0 and d % LANES == 0
    assert seq // (NA_TILE_ROWS * GRID_W) >= NA_KEY_TILES
    mod_rows = -(-(bsz + 1) // 8) * 8
    cvec = jnp.zeros((mod_rows, d), F32).at[:bsz].set(c).at[bsz].set(c_ctx)
    mod = _modulation(cvec, w_mod, b_mod).reshape(depth, mod_rows, 6, d)
    modsel = jnp.stack([jnp.broadcast_to(mod[:, bsz][:, None], (depth, bsz, 6, d)), mod[:, :bsz]], axis=2)
    rope_tabs = _rope_tables(n_ctx, seq)
    xt = jnp.concatenate([ctx, x], axis=1)
    n_groups = BATCH_GROUPS if bsz % BATCH_GROUPS == 0 else 1
    gb = bsz // n_groups
    groups = [dict(xt=xt[g * gb:(g + 1) * gb], mod=modsel[:, g * gb:(g + 1) * gb], ymoe=None, route=None)
              for g in range(n_groups)]
    for l in range(depth):
        w_in_p = _pack_w_in(w_in[l])
        bias_tabs = _na_bias_tables(na_rel_bias[l], seq // GRID_W)
        w_router = jnp.pad(jnp.concatenate([w_router_group[l], w_router_expert[l]], axis=-1),
                           ((0, 0), (0, ROUTER_PAD - N_GROUPS - N_EXPERTS)))
        w_router = jnp.concatenate(_split_bf16(w_router, 2), axis=-1)
        b_router = jnp.pad(jnp.concatenate([b_router_group[l], b_router_expert[l]]),
                           (0, ROUTER_PAD - N_GROUPS - N_EXPERTS)).reshape(1, ROUTER_PAD)
        w_out_b = w_out[l].astype(MXU_DTYPE)
        for st in groups:
            outs = _in_proj(l, st["xt"], st["ymoe"], st["route"], st["mod"], norm1_g[l], w_in_p, rope_tabs)
            qa, ka, va, bg, qn, kn, vn = outs[:7]
            if st["ymoe"] is not None:
                st["xt"] = outs[7]
            ya_ctx, ya_lat = _diff_attention(l, qa, ka, va, diff_lambda[l], diff_sub_g[l])
            o_f, o_b = _gla_scan(bg, gla_w_decay[l], gla_b_decay[l])
            yn = _neighborhood_attention(qn, kn, vn, bias_tabs)
            st["xt"], h2, st["route"], counts = _out_proj(l, ya_ctx, ya_lat, o_f, o_b, bg, yn, st["xt"], st["mod"],
                                                          gla_norm_g[l], norm2_g[l], w_out_b, w_router, b_router)
            st["ymoe"] = _moe(l, h2, st["route"], counts, w_expert_up, w_expert_down)
    outs = [_final_norm(depth - 1, st["xt"], st["ymoe"], st["route"], st["mod"], final_g, n_ctx // TOK_TILE)
            for st in groups]
    return jnp.concatenate(outs, axis=0)
```

```python
import functools
import math

import numpy as np
import jax
import jax.numpy as jnp
from jax import lax
from jax.experimental import pallas as pl
from jax.experimental.pallas import tpu as pltpu
from jax.experimental.pallas import tpu_sc as plsc

F32 = jnp.float32
MXU_DTYPE = jnp.bfloat16
HI = lax.Precision.HIGHEST

GRID_W = 64
HEAD_DIM = 64
ROPE_THETA = 10000.0
NORM_EPS = 1e-6

A_HEADS = 4
A_QK = HEAD_DIM
A_V = 2 * HEAD_DIM
B_HEADS = 4
B_DK = HEAD_DIM // 2
B_DV = HEAD_DIM
B_GATE_RANK = 16
B_GATE_TAU = 16.0
B_CHUNK = 64
LOG2_E = math.log2(math.e)
GLA_SUB = 16
C_HEADS = 4
C_DH = HEAD_DIM
NA_ROWS = 8
NA_COLS = 16

A_WIDTH = A_HEADS * A_V
B_WIDTH = B_HEADS * B_DV
C_WIDTH = C_HEADS * C_DH
B_QK = B_HEADS * B_DK
IN_SIZES = (A_HEADS * 2 * A_QK, A_HEADS * 2 * A_QK, A_WIDTH,
            B_QK, B_QK, B_WIDTH, B_WIDTH, 2 * B_GATE_RANK,
            C_WIDTH, C_WIDTH, C_WIDTH)

N_GROUPS = 4
EXPERTS_PER_GROUP = 8
N_EXPERTS = N_GROUPS * EXPERTS_PER_GROUP
TOP_K = 2
EXPERT_HIDDEN = 512

LANES = 128
TOK_TILE = 256
NA_TILE_ROWS = 4
NA_KEY_TILES = 3
BATCH_GROUPS = 1
IN_PROJ_TILE = 256
EXPERT_TILE = 512
GATE_PAD = LANES
BG_WIDTH = 2 * B_QK + 2 * B_WIDTH + GATE_PAD
IN_PAD_WIDTH = 3 * A_WIDTH + BG_WIDTH + 3 * C_WIDTH
ROUTER_PAD = LANES
ROUTE_EXPERT = 0
ROUTE_GATE = 2
ROUTE_RANK = 4
SC_CORES = 2
SC_SUBCORES = 16
SC_ROW_BUFFER_BYTES = 128 * 1024
VMEM_LIMIT = 48 * 1024 * 1024


def _split_bf16(x, pieces):
    out = []
    for _ in range(pieces):
        p = x.astype(jnp.bfloat16)
        out.append(p)
        x = x - p.astype(F32)
    return out


def _pack_bf16_pairs(x):
    w = x.shape[1] // 2
    bits = lax.bitcast_convert_type(x.astype(jnp.bfloat16).astype(F32), jnp.uint32)
    return (bits[:, :w] >> 16) | (bits[:, w:] & jnp.uint32(0xFFFF0000))


def _unpack_bf16_pairs(u):
    lo = lax.bitcast_convert_type(u << 16, F32)
    hi = lax.bitcast_convert_type(u & jnp.uint32(0xFFFF0000), F32)
    return lo, hi


def _silu(x):
    return x * (1.0 / (1.0 + jnp.exp(-x)))


def _cparams(sem):
    return pltpu.CompilerParams(dimension_semantics=sem, vmem_limit_bytes=VMEM_LIMIT)


def _mod_body(c_ref, w_ref, b_ref, o_ref):
    a = _silu(c_ref[...])
    o_ref[...] = jnp.dot(a, w_ref[...], precision=HI, preferred_element_type=F32) + b_ref[...]


def _modulation(cvec, w_mod, b_mod):
    depth, d, d6 = w_mod.shape
    rows = cvec.shape[0]
    return pl.pallas_call(
        _mod_body,
        grid=(depth, d6 // d),
        in_specs=[pl.BlockSpec((rows, d), lambda l, j: (0, 0)),
                  pl.BlockSpec((None, d, d), lambda l, j: (l, 0, j)),
                  pl.BlockSpec((None, 1, d), lambda l, j: (l, 0, j))],
        out_specs=pl.BlockSpec((None, rows, d), lambda l, j: (l, 0, j)),
        out_shape=jax.ShapeDtypeStruct((depth, rows, d6), F32),
        compiler_params=_cparams(("arbitrary", "arbitrary")),
        name="modulation",
    )(cvec, w_mod, b_mod.reshape(depth, 1, d6))


def _rms(x, g):
    return x * lax.rsqrt(jnp.mean(x * x, axis=-1, keepdims=True) + NORM_EPS) * g


def _rope(x, cos, sa, sb):
    return x * cos + pltpu.roll(x, LANES - 16, 1) * sa + pltpu.roll(x, 16, 1) * sb


def _moe_residual(x_ref, y0_ref, y1_ref, rt_ref, gate2):
    rt = rt_ref[0]
    g0, g1 = rt[:, ROUTE_GATE:ROUTE_GATE + 1], rt[:, ROUTE_GATE + 1:ROUTE_GATE + 2]
    lo0, hi0 = _unpack_bf16_pairs(y0_ref[0])
    lo1, hi1 = _unpack_bf16_pairs(y1_ref[0])
    moe = jnp.concatenate([g0 * lo0 + g1 * lo1, g0 * hi0 + g1 * hi1], axis=1)
    return x_ref[0] + gate2 * moe


def _in_proj_body(combine, ctx_len, *refs):
    if combine:
        (x_ref, y0_ref, y1_ref, rt_ref, pmod_ref, mod_ref, g_ref, w_ref, cos_ref, sa_ref, sb_ref,
         qa_ref, ka_ref, va_ref, bg_ref, qn_ref, kn_ref, vn_ref, xo_ref) = refs
    else:
        (x_ref, mod_ref, g_ref, w_ref, cos_ref, sa_ref, sb_ref,
         qa_ref, ka_ref, va_ref, bg_ref, qn_ref, kn_ref, vn_ref) = refs
    tm = x_ref.shape[1]
    row = lax.broadcasted_iota(jnp.int32, (tm, 1), 0) + pl.program_id(1) * tm
    is_ctx = row < ctx_len

    def mod_row(ref, j):
        return jnp.where(is_ctx, ref[0, j:j + 1, :], ref[1, j:j + 1, :])

    if combine:
        x = _moe_residual(x_ref, y0_ref, y1_ref, rt_ref, mod_row(pmod_ref, 5))
        xo_ref[0] = x
    else:
        x = x_ref[0]
    h = _rms(x, g_ref[...]) * (1.0 + mod_row(mod_ref, 1)) + mod_row(mod_ref, 0)
    hb = h.astype(MXU_DTYPE)

    def proj(lo, hi):
        return jnp.dot(hb, w_ref[:, lo:hi], preferred_element_type=F32)

    cos, sa, sb = cos_ref[...], sa_ref[...], sb_ref[...]
    qk = proj(0, 2 * A_WIDTH)
    for hh in range(A_HEADS):
        lo = hh * LANES
        q = qk[:, lo:lo + LANES]
        qa_ref[0, :, lo:lo + LANES] = (_rope(q, cos, sa, sb) * (A_QK ** -0.5 * LOG2_E)).astype(qa_ref.dtype)
        k = qk[:, A_WIDTH + lo:A_WIDTH + lo + LANES]
        ka_ref[0, :, lo:lo + LANES] = _rope(k, cos, sa, sb).astype(ka_ref.dtype)
    o = 2 * A_WIDTH
    va = proj(o, o + A_WIDTH)
    ones = jnp.ones((va.shape[0], A_V), F32)
    va_ref[0] = jnp.concatenate([piece for hh in range(A_HEADS)
                                 for piece in (va[:, hh * A_V:(hh + 1) * A_V], ones)],
                                axis=1).astype(va_ref.dtype)
    o += A_WIDTH
    bg_ref[0] = proj(o, o + BG_WIDTH)
    o += BG_WIDTH
    qn_ref[0] = (proj(o, o + C_WIDTH) * (C_DH ** -0.5)).astype(qn_ref.dtype)
    o += C_WIDTH
    kn_ref[0] = proj(o, o + C_WIDTH).astype(kn_ref.dtype)
    o += C_WIDTH
    vn_ref[0] = proj(o, o + C_WIDTH).astype(vn_ref.dtype)


def _moe_specs(tm, d, row_block):
    return [pl.BlockSpec((None, 1, tm, d // 2), lambda b, i: (0, b, row_block(i), 0)),
            pl.BlockSpec((None, 1, tm, d // 2), lambda b, i: (1, b, row_block(i), 0)),
            pl.BlockSpec((1, tm, ROUTER_PAD), lambda b, i: (b, row_block(i), 0))]


def _in_proj(layer, x, ymoe, route, modsel, g1, w_in_p, rope_tabs):
    bsz, t, d = x.shape
    tm = IN_PROJ_TILE
    combine = ymoe is not None
    tok = lambda b, i: (b, i, 0)
    x_spec = pl.BlockSpec((1, tm, d), tok)

    def mod_spec(l):
        return pl.BlockSpec((None, None, 2, 6, d), lambda b, i: (l, b, 0, 0, 0))

    tab_spec = pl.BlockSpec((tm, LANES), lambda b, i: (i, 0))
    in_specs = [x_spec]
    args = [x]
    if combine:
        in_specs += _moe_specs(tm, d, lambda i: i) + [mod_spec(layer - 1)]
        args += [ymoe, ymoe, route, modsel]
    in_specs += [mod_spec(layer), pl.BlockSpec((1, d), lambda b, i: (0, 0)),
                 pl.BlockSpec((d, IN_PAD_WIDTH), lambda b, i: (0, 0)), tab_spec, tab_spec, tab_spec]
    args += [modsel, g1.reshape(1, d), w_in_p, *rope_tabs]

    def o(width, dtype):
        return pl.BlockSpec((1, tm, width), tok), jax.ShapeDtypeStruct((bsz, t, width), dtype)

    outs = [o(A_WIDTH, MXU_DTYPE), o(A_WIDTH, MXU_DTYPE), o(2 * A_WIDTH, MXU_DTYPE), o(BG_WIDTH, F32),
            o(C_WIDTH, MXU_DTYPE), o(C_WIDTH, MXU_DTYPE), o(C_WIDTH, MXU_DTYPE)]
    if combine:
        outs.append(o(d, F32))
    return pl.pallas_call(
        functools.partial(_in_proj_body, combine, TOK_TILE),
        grid=(bsz, pl.cdiv(t, tm)),
        in_specs=in_specs,
        out_specs=[s for s, _ in outs],
        out_shape=[s for _, s in outs],
        compiler_params=_cparams(("parallel", "arbitrary")),
        name="in_proj",
    )(*args)


def _diff_rows(lam_init, q, k_ref, v_ref, n_keys, lam, g):
    lane = lax.broadcasted_iota(jnp.int32, (1, LANES), 1)
    zero = jnp.zeros_like(q)
    nt = (((1,), (1,)), ((), ()))
    k = k_ref[0, :n_keys, :]
    s1 = lax.dot_general(jnp.where(lane < A_QK, q, zero), k, nt, preferred_element_type=F32)
    s2 = lax.dot_general(jnp.where(lane >= A_QK, q, zero), k, nt, preferred_element_type=F32)

    def finish():
        v1 = v_ref[0, :n_keys, :]
        outs = []
        for s in (s1, s2):
            p = jnp.exp2((s - jnp.max(s, axis=-1, keepdims=True)).astype(MXU_DTYPE))
            outs.append(jnp.dot(p, v1, preferred_element_type=F32))
        o = (outs[0][:, :A_V] * (1.0 / outs[0][:, A_V:A_V + 1])
             - outs[1][:, :A_V] * (lam / outs[1][:, A_V:A_V + 1]))
        return _rms(o, g) * (1.0 - lam_init)

    return finish


def _diff_attn_body(lam_init, ctx_len, qa_ref, qb_ref, k_ref, v_ref, lam_ref, g_ref, oc_ref, ol_ref):
    lm = lam_ref[...]
    lam = (jnp.exp(jnp.sum(lm[0:1] * lm[1:2], axis=1, keepdims=True))
           - jnp.exp(jnp.sum(lm[2:3] * lm[3:4], axis=1, keepdims=True)) + lam_init)
    g = g_ref[...]
    rows = qa_ref.shape[1]

    @pl.when(pl.program_id(2) == 0)
    def _():
        oc_ref[0] = _diff_rows(lam_init, qa_ref[0], k_ref, v_ref, ctx_len, lam, g)().astype(oc_ref.dtype)

    @pl.when(pl.program_id(2) > 0)
    def _():
        n_keys = k_ref.shape[1]
        fin_a = _diff_rows(lam_init, qa_ref[0], k_ref, v_ref, n_keys, lam, g)
        fin_b = _diff_rows(lam_init, qb_ref[0], k_ref, v_ref, n_keys, lam, g)
        ol_ref[0, :rows, :] = fin_a().astype(ol_ref.dtype)
        ol_ref[0, rows:, :] = fin_b().astype(ol_ref.dtype)


def _diff_attention(layer, qa, ka, va, lam, g_sub):
    bsz, t, _ = qa.shape
    tq = TOK_TILE
    n_lat = (t - tq) // (2 * tq)
    assert n_lat * 2 * tq == t - tq
    lam_init = 0.8 - 0.6 * math.exp(-0.3 * layer)

    def q_spec(off):
        return pl.BlockSpec((1, tq, LANES), lambda b, h, i: (b, jnp.maximum(2 * i + off, 0), h))

    return pl.pallas_call(
        functools.partial(_diff_attn_body, lam_init, TOK_TILE),
        grid=(bsz, A_HEADS, 1 + n_lat),
        in_specs=[q_spec(-1), q_spec(0),
                  pl.BlockSpec((1, t, LANES), lambda b, h, i: (b, 0, h)),
                  pl.BlockSpec((1, t, 2 * A_V), lambda b, h, i: (b, 0, h)),
                  pl.BlockSpec((4, A_QK), lambda b, h, i: (0, 0)),
                  pl.BlockSpec((1, A_V), lambda b, h, i: (0, 0))],
        out_specs=[pl.BlockSpec((1, tq, LANES), lambda b, h, i: (b, 0, h)),
                   pl.BlockSpec((1, 2 * tq, LANES), lambda b, h, i: (b, jnp.maximum(i - 1, 0), h))],
        out_shape=[jax.ShapeDtypeStruct((bsz, tq, A_WIDTH), MXU_DTYPE),
                   jax.ShapeDtypeStruct((bsz, t - tq, A_WIDTH), MXU_DTYPE)],
        compiler_params=_cparams(("parallel", "parallel", "arbitrary")),
        name="diff_attention",
    )(qa, qa, ka, va, lam, g_sub.reshape(1, A_V))


def _gla_body(f_ref, r_ref, wdec_ref, bdec_ref, of_ref, ob_ref, sf_ref, sb_ref, es_ref, bs_ref, qss_ref):
    c = B_CHUNK
    n_chunks = TOK_TILE // c

    @pl.when(pl.program_id(1) == 0)
    def _():
        sf_ref[...] = jnp.zeros_like(sf_ref)
        sb_ref[...] = jnp.zeros_like(sb_ref)

    sub = GLA_SUB
    n_sub = c // sub
    nt = (((1,), (1,)), ((), ()))
    t_row = lax.broadcasted_iota(jnp.int32, (TOK_TILE, TOK_TILE), 0)
    t_col = lax.broadcasted_iota(jnp.int32, (TOK_TILE, TOK_TILE), 1)
    same_chunk = (t_row // c) == (t_col // c)
    tri_f = (same_chunk & (t_col <= t_row)).astype(jnp.bfloat16)
    tri_b = (same_chunk & (t_col >= t_row)).astype(jnp.bfloat16)
    s_iota = lax.broadcasted_iota(jnp.int32, (sub, LANES), 0)
    idx = lax.broadcasted_iota(jnp.int32, (c, LANES), 0)
    head_of_k = lax.broadcasted_iota(jnp.int32, (B_QK, B_WIDTH), 0) // B_DK
    head_of_v = lax.broadcasted_iota(jnp.int32, (B_QK, B_WIDTH), 1) // B_DV
    expand = (head_of_k == head_of_v).astype(MXU_DTYPE)
    same_head_t = (lax.broadcasted_iota(jnp.int32, (B_WIDTH, B_QK), 0) // B_DV
                   == lax.broadcasted_iota(jnp.int32, (B_WIDTH, B_QK), 1) // B_DK)
    n_ref = n_sub - 1
    kt_keep = (lax.broadcasted_iota(jnp.int32, (B_HEADS * c, n_ref * B_QK), 0) // c
               == (lax.broadcasted_iota(jnp.int32, (B_HEADS * c, n_ref * B_QK), 1) % B_QK) // B_DK)
    vx_keep = (lax.broadcasted_iota(jnp.int32, (B_HEADS * c, B_WIDTH), 0) // c
               == lax.broadcasted_iota(jnp.int32, (B_HEADS * c, B_WIDTH), 1) // B_DV)
    pick = (lax.broadcasted_iota(jnp.int32, (c, c * sub), 1) // sub
            == lax.broadcasted_iota(jnp.int32, (c, c * sub), 0)).astype(MXU_DTYPE)

    def log_decay(src_ref, backward):
        gl = src_ref[0, :, 2 * B_QK + 2 * B_WIDTH:BG_WIDTH]
        d0 = B_QK if backward else 0
        z = jnp.dot(gl, wdec_ref[:, d0:d0 + B_QK], precision=HI, preferred_element_type=F32) \
            + bdec_ref[:, d0:d0 + B_QK]
        log_a = (jnp.minimum(z, 0.0) - jnp.log(1.0 + jnp.exp(-jnp.abs(z)))) / B_GATE_TAU
        tri = tri_b if backward else tri_f
        return sum(jnp.dot(tri, p, preferred_element_type=F32) for p in _split_bf16(log_a, 3))

    def chunk(slot, src_ref, b_all, lo, backward, st_ref, out_ref):
        q = src_ref[0, lo:lo + c, 0:B_QK] * (B_DK ** -0.5)
        k = src_ref[0, lo:lo + c, B_QK:2 * B_QK]
        v = src_ref[0, lo:lo + c, 2 * B_QK:2 * B_QK + B_WIDTH]
        b = b_all[lo:lo + c]
        b_ref, qs_ref, e_ref = bs_ref.at[slot], qss_ref.at[slot], es_ref.at[slot]
        b_ref[...] = b
        qs_ref[...] = q
        blk = ((c - 1 - idx) if backward else idx) // sub

        q_parts, k_parts = [], []
        for m, late, early in ((1, blk == 1, blk == 0), (2, blk >= 2, blk <= 1), (3, blk == 3, blk == 2)):
            r_row = (c - 1 - sub * m) if backward else sub * m
            r = b_ref[r_row:r_row + 1, :]
            q_parts.append(q * jnp.exp(jnp.where(late, b - r, -jnp.inf)))
            k_parts.append(k * jnp.exp(jnp.where(early, r - b, -jnp.inf)))
        q_cat = jnp.concatenate(q_parts, axis=1).astype(MXU_DTYPE)
        k_cat = jnp.concatenate(k_parts, axis=1)
        k_exp = jnp.where(kt_keep, jnp.concatenate([k_cat] * B_HEADS, axis=0), 0.0).astype(MXU_DTYPE)
        a_off = lax.dot_general(q_cat, k_exp, nt, preferred_element_type=F32)

        for tt in range(c):
            lo_s = tt // sub * sub
            keep = (s_iota >= tt - lo_s) if backward else (s_iota <= tt - lo_s)
            bt = b_ref[tt:tt + 1, :]
            qt = qs_ref[tt:tt + 1, :]
            e = jnp.exp(jnp.where(keep, bt - b[lo_s:lo_s + sub], -jnp.inf)) * (qt * k[lo_s:lo_s + sub])
            e_ref[tt * sub:(tt + 1) * sub, :] = e.astype(e_ref.dtype)
        a_exp = jnp.dot(e_ref[...], expand, preferred_element_type=F32)
        b_end = b[0:1, :] if backward else b[c - 1:c, :]
        kd = k * jnp.exp(b_end - b)
        upd = lax.dot_general(v, kd, (((0,), (0,)), ((), ())), preferred_element_type=F32)

        def intra():
            v_exp = jnp.where(vx_keep, jnp.concatenate([v] * B_HEADS, axis=0), 0.0).astype(MXU_DTYPE)
            o_off = jnp.dot(a_off.astype(MXU_DTYPE), v_exp, preferred_element_type=F32)
            prod = a_exp.reshape(n_sub, sub, sub, B_WIDTH) * v.reshape(n_sub, 1, sub, B_WIDTH)
            o_diag = jnp.dot(pick, prod.reshape(c * sub, B_WIDTH).astype(MXU_DTYPE),
                             preferred_element_type=F32)
            o_intra = o_off + o_diag

            def recur():
                st = st_ref[...]
                o_inter = lax.dot_general(q * jnp.exp(b), st, nt, preferred_element_type=F32)
                out_ref[0, lo:lo + c, :] = o_intra + o_inter
                st_ref[...] = jnp.exp(b_end) * st + jnp.where(same_head_t, upd, 0.0)

            return recur

        return intra

    b_fwd = log_decay(f_ref, False)
    b_bwd = log_decay(r_ref, True)
    stage = []
    for ci in range(n_chunks):
        stage.append(chunk(2 * ci, f_ref, b_fwd, ci * c, False, sf_ref, of_ref))
        stage.append(chunk(2 * ci + 1, r_ref, b_bwd, (n_chunks - 1 - ci) * c, True, sb_ref, ob_ref))
    stage = [intra() for intra in stage]
    for recur in stage:
        recur()


def _gla_scan(bg, w_dec, b_dec):
    bsz, t, _ = bg.shape
    n = t // TOK_TILE
    rev = lambda b, i: (b, jnp.where(i == 0, 0, n - i), 0)
    fwd = lambda b, i: (b, i, 0)
    wdec = jnp.zeros((GATE_PAD, 2 * B_QK), F32)
    wdec = wdec.at[:B_GATE_RANK, :B_QK].set(w_dec[0]).at[B_GATE_RANK:2 * B_GATE_RANK, B_QK:].set(w_dec[1])
    bdec = b_dec.reshape(1, 2 * B_QK)
    o_shape = jax.ShapeDtypeStruct((bsz, t, B_WIDTH), F32)
    n_slots = 2 * (TOK_TILE // B_CHUNK)
    return pl.pallas_call(
        _gla_body,
        grid=(bsz, n),
        in_specs=[pl.BlockSpec((1, TOK_TILE, BG_WIDTH), fwd),
                  pl.BlockSpec((1, TOK_TILE, BG_WIDTH), rev),
                  pl.BlockSpec((GATE_PAD, 2 * B_QK), lambda b, i: (0, 0)),
                  pl.BlockSpec((1, 2 * B_QK), lambda b, i: (0, 0))],
        out_specs=[pl.BlockSpec((1, TOK_TILE, B_WIDTH), fwd),
                   pl.BlockSpec((1, TOK_TILE, B_WIDTH), rev)],
        out_shape=[o_shape, o_shape],
        scratch_shapes=[pltpu.VMEM((B_WIDTH, B_QK), F32), pltpu.VMEM((B_WIDTH, B_QK), F32),
                        pltpu.VMEM((n_slots, B_CHUNK * GLA_SUB, LANES), MXU_DTYPE),
                        pltpu.VMEM((n_slots, B_CHUNK, LANES), F32),
                        pltpu.VMEM((n_slots, B_CHUNK, LANES), F32)],
        compiler_params=_cparams(("parallel", "arbitrary")),
        name="gla_scan",
    )(bg, bg, wdec, bdec)


def _na_bias_tables(rpb, rows):
    n_tiles = rows // NA_TILE_ROWS
    wr = min(NA_ROWS, rows)
    n_dr, n_dc = 2 * NA_ROWS - 1, 2 * NA_COLS - 1
    cq = np.arange(GRID_W)[:, None]
    ck = np.arange(GRID_W)[None, :]
    cs = np.clip(cq - NA_COLS // 2, 0, GRID_W - NA_COLS)
    col_ok = (ck >= cs) & (ck < cs + NA_COLS)
    dc = np.clip(ck - cq, -(NA_COLS - 1), NA_COLS - 1) + (NA_COLS - 1)
    onehot = (dc.reshape(1, -1) == np.arange(n_dc)[:, None]).astype(np.float32)
    by_col = jnp.dot(rpb.astype(F32).reshape(-1, n_dc), onehot, precision=HI)
    by_col = jnp.where(col_ok.reshape(1, 1, GRID_W, GRID_W),
                       by_col.reshape(C_HEADS, n_dr, GRID_W, GRID_W), -jnp.inf)
    masked = jnp.full((C_HEADS, GRID_W, GRID_W), -jnp.inf, F32)
    tabs = []
    for j in (0, 1, n_tiles - 1):
        kr0 = int(np.clip(j - 1, 0, n_tiles - NA_KEY_TILES)) * NA_TILE_ROWS
        q_rows = []
        for qr in range(NA_TILE_ROWS):
            r = j * NA_TILE_ROWS + qr
            start = int(np.clip(r - wr // 2, 0, rows - wr))
            blocks = []
            for kw in range(NA_KEY_TILES * NA_TILE_ROWS):
                kr = kr0 + kw
                blocks.append(by_col[:, kr - r + NA_ROWS - 1] if start <= kr < start + wr else masked)
            q_rows.append(jnp.concatenate(blocks, axis=-1))
        tabs.append(jnp.concatenate(q_rows, axis=1))
    return jnp.stack(tabs)


def _na_body(q_ref, k0_ref, k1_ref, k2_ref, kc_ref, v0_ref, v1_ref, v2_ref, vc_ref, m_ref, o_ref):
    q = q_ref[0]
    lane = lax.broadcasted_iota(jnp.int32, (1, C_WIDTH), 1)
    zero = jnp.zeros_like(q)
    nt = (((1,), (1,)), ((), ()))

    def scores(qm, k_ref):
        return lax.dot_general(qm, k_ref[0], nt, preferred_element_type=F32)

    def head_out(hh, windows):
        qm = jnp.where((lane >= hh * C_DH) & (lane < (hh + 1) * C_DH), q, zero)
        s = [scores(qm, kc_ref)]
        for w, k_ref in enumerate(windows):
            s.append(scores(qm, k_ref) + m_ref[0, hh, :, w * TOK_TILE:(w + 1) * TOK_TILE])

        def finish():
            m = functools.reduce(jnp.maximum, [jnp.max(x, axis=-1, keepdims=True) for x in s])
            p = [jnp.exp(x - m) for x in s]
            den = functools.reduce(jnp.add, [jnp.sum(x, axis=-1, keepdims=True) for x in p])
            vals = [vc_ref] + [v0_ref, v1_ref, v2_ref][:len(windows)]
            o = functools.reduce(jnp.add, [jnp.dot(x.astype(MXU_DTYPE), v_ref[0], preferred_element_type=F32)
                                           for x, v_ref in zip(p, vals)])
            return o * (1.0 / den)

        return finish

    def emit(windows):
        finish = [head_out(hh, windows) for hh in range(C_HEADS)]
        o = finish[C_HEADS - 1]()
        for hh in range(C_HEADS - 2, -1, -1):
            o = jnp.where(lane < (hh + 1) * C_DH, finish[hh](), o)
        o_ref[0] = o.astype(o_ref.dtype)

    @pl.when(pl.program_id(0) == 0)
    def _():
        emit([])

    @pl.when(pl.program_id(0) > 0)
    def _():
        emit([k0_ref, k1_ref, k2_ref])


def _neighborhood_attention(qn, kn, vn, bias_tabs):
    bsz, t, _ = qn.shape
    n = t // TOK_TILE
    n_lat = n - 1

    def win(w):
        def index(i, b):
            kb0 = jnp.clip(i - 2, 0, n_lat - NA_KEY_TILES)
            return (b, kb0 + 1 + w, 0)
        return pl.BlockSpec((1, TOK_TILE, C_WIDTH), index)

    def cls(i, b):
        j = i - 1
        return (jnp.where(j <= 0, 0, jnp.where(j == n_lat - 1, 2, 1)), 0, 0, 0)

    own = pl.BlockSpec((1, TOK_TILE, C_WIDTH), lambda i, b: (b, i, 0))
    ctx = pl.BlockSpec((1, TOK_TILE, C_WIDTH), lambda i, b: (b, 0, 0))
    return pl.pallas_call(
        _na_body,
        grid=(n, bsz),
        in_specs=[own, win(0), win(1), win(2), ctx, win(0), win(1), win(2), ctx,
                  pl.BlockSpec((1, C_HEADS, TOK_TILE, NA_KEY_TILES * TOK_TILE), cls)],
        out_specs=own,
        out_shape=jax.ShapeDtypeStruct((bsz, t, C_WIDTH), MXU_DTYPE),
        compiler_params=_cparams(("parallel", "arbitrary")),
        name="neighborhood_attention",
    )(qn, kn, kn, kn, kn, vn, vn, vn, vn, bias_tabs)


def _out_proj_body(yac_ref, yal_ref, of_ref, ob_ref, r_ref, yn_ref, x_ref, mod_ref, gg_ref, g2_ref, w_ref,
                   wr_ref, br_ref, xo_ref, h_ref, rt_ref, cnt_ref, count_ref):
    is_ctx_tile = pl.program_id(1) == 0
    hi = lax.broadcasted_iota(jnp.int32, (B_WIDTH, B_WIDTH), 0) // B_DV
    hj = lax.broadcasted_iota(jnp.int32, (B_WIDTH, B_WIDTH), 1) // B_DV
    head_mean = jnp.where(hi == hj, 1.0 / B_DV, 0.0).astype(jnp.bfloat16)

    @pl.when((pl.program_id(0) == 0) & (pl.program_id(1) == 0))
    def _():
        count_ref[...] = jnp.zeros_like(count_ref)

    def rows(r0, r1):
        ya = jnp.where(is_ctx_tile, yac_ref[0, r0:r1, :], yal_ref[0, r0:r1, :])
        o = of_ref[0, r0:r1, :] + ob_ref[0, r0:r1, :]
        ms = sum(jnp.dot(p, head_mean, preferred_element_type=F32) for p in _split_bf16(o * o, 2))
        yb = o * lax.rsqrt(ms + NORM_EPS) * gg_ref[...] * _silu(r_ref[0, r0:r1, :])

        def project():
            mix = (jnp.dot(ya, w_ref[0:A_WIDTH, :], preferred_element_type=F32)
                   + jnp.dot(yb.astype(MXU_DTYPE), w_ref[A_WIDTH:A_WIDTH + B_WIDTH, :],
                             preferred_element_type=F32)
                   + jnp.dot(yn_ref[0, r0:r1, :], w_ref[A_WIDTH + B_WIDTH:, :], preferred_element_type=F32))
            x = x_ref[0, r0:r1, :] + mod_ref[2:3, :] * mix
            xo_ref[0, r0:r1, :] = x
            h = _rms(x, g2_ref[...]) * (1.0 + mod_ref[4:5, :]) + mod_ref[3:4, :]
            h_ref[0, r0:r1, :] = _pack_bf16_pairs(h)
            h_hi, h_lo = _split_bf16(h, 2)
            both = jnp.dot(h_hi, wr_ref[...], preferred_element_type=F32)
            logits = (both[:, :ROUTER_PAD] + both[:, ROUTER_PAD:]
                      + jnp.dot(h_lo, wr_ref[:, :ROUTER_PAD], preferred_element_type=F32) + br_ref[...])

            def route():
                rt_ref[0, r0:r1, :] = _route(logits, count_ref)

            return route

        return project

    half = x_ref.shape[1] // 2
    stages = [rows(0, half), rows(half, 2 * half)]
    stages = [project() for project in stages]
    for route in stages:
        route()
    cnt_ref[...] = count_ref[...]


def _route(lg, count_ref):
    lane = lax.broadcasted_iota(jnp.int32, lg.shape, 1)
    big = jnp.int32(ROUTER_PAD)

    def top(mask):
        v = jnp.max(jnp.where(mask, lg, -jnp.inf), axis=-1, keepdims=True)
        i = jnp.min(jnp.where(mask & (lg == v), lane, big), axis=-1, keepdims=True)
        return v, i

    g_mask = lane < N_GROUPS
    g_max, grp = top(g_mask)
    p_grp = 1.0 / jnp.sum(jnp.where(g_mask, jnp.exp(lg - g_max), 0.0), axis=-1, keepdims=True)
    e_lo = N_GROUPS + grp * EXPERTS_PER_GROUP
    e_mask = (lane >= e_lo) & (lane < e_lo + EXPERTS_PER_GROUP)
    v1, i1 = top(e_mask)
    v2, i2 = top(e_mask & (lane != i1))
    r = jnp.exp(v2 - v1)
    gate1 = p_grp / (1.0 + r)
    gate2 = p_grp * r / (1.0 + r)
    e1, e2 = i1 - N_GROUPS, i2 - N_GROUPS
    hot1 = (lane == e1).astype(F32)
    hot2 = (lane == e2).astype(F32)
    both = hot1 + hot2
    rows = lg.shape[0]
    earlier = (lax.broadcasted_iota(jnp.int32, (rows, rows), 1)
               < lax.broadcasted_iota(jnp.int32, (rows, rows), 0)).astype(jnp.bfloat16)
    before = jnp.dot(earlier, both.astype(jnp.bfloat16), preferred_element_type=F32) + count_ref[0:1, :]
    rank1 = jnp.sum(hot1 * before, axis=-1, keepdims=True)
    rank2 = jnp.sum(hot2 * before, axis=-1, keepdims=True)
    count_ref[0:1, :] = count_ref[0:1, :] + jnp.sum(both, axis=0, keepdims=True)
    out = jnp.where(lane == ROUTE_EXPERT, e1.astype(F32),
                    jnp.where(lane == ROUTE_EXPERT + 1, e2.astype(F32),
                              jnp.where(lane == ROUTE_GATE, gate1,
                                        jnp.where(lane == ROUTE_GATE + 1, gate2,
                                                  jnp.where(lane == ROUTE_RANK, rank1,
                                                            jnp.where(lane == ROUTE_RANK + 1, rank2, 0.0))))))
    return out


def _out_proj(layer, ya_ctx, ya_lat, o_f, o_b, bg, yn, x, modsel, g_gla, g2, w_out_b, w_router, b_router):
    bsz, t, d = x.shape
    tm = TOK_TILE
    tok = lambda b, i: (b, i, 0)
    const = lambda b, i: (0, 0)
    r_block = (2 * B_QK + B_WIDTH) // B_WIDTH
    return pl.pallas_call(
        _out_proj_body,
        grid=(bsz, t // tm),
        in_specs=[pl.BlockSpec((1, tm, A_WIDTH), lambda b, i: (b, 0, 0)),
                  pl.BlockSpec((1, tm, A_WIDTH), lambda b, i: (b, jnp.maximum(i - 1, 0), 0)),
                  pl.BlockSpec((1, tm, B_WIDTH), tok), pl.BlockSpec((1, tm, B_WIDTH), tok),
                  pl.BlockSpec((1, tm, B_WIDTH), lambda b, i: (b, i, r_block)),
                  pl.BlockSpec((1, tm, C_WIDTH), tok),
                  pl.BlockSpec((1, tm, d), tok),
                  pl.BlockSpec((None, None, None, 6, d), lambda b, i: (layer, b, jnp.minimum(i, 1), 0, 0)),
                  pl.BlockSpec((1, B_WIDTH), const), pl.BlockSpec((1, d), const),
                  pl.BlockSpec(w_out_b.shape, const),
                  pl.BlockSpec((d, 2 * ROUTER_PAD), const), pl.BlockSpec((1, ROUTER_PAD), const)],
        out_specs=[pl.BlockSpec((1, tm, d), tok), pl.BlockSpec((1, tm, d // 2), tok),
                   pl.BlockSpec((1, tm, ROUTER_PAD), tok), pl.BlockSpec((8, ROUTER_PAD), const)],
        out_shape=[jax.ShapeDtypeStruct((bsz, t, d), F32), jax.ShapeDtypeStruct((bsz, t, d // 2), jnp.uint32),
                   jax.ShapeDtypeStruct((bsz, t, ROUTER_PAD), F32), jax.ShapeDtypeStruct((8, ROUTER_PAD), F32)],
        scratch_shapes=[pltpu.VMEM((8, ROUTER_PAD), F32)],
        compiler_params=_cparams(("arbitrary", "arbitrary")),
        name="out_proj",
    )(ya_ctx, ya_lat, o_f, o_b, bg, yn, x, modsel, jnp.tile(g_gla, B_HEADS).reshape(1, B_WIDTH), g2.reshape(1, d),
      w_out_b, w_router, b_router)


def _dispatch_plan(expert, rank, counts, n_blocks):
    padded = (counts + EXPERT_TILE - 1) // EXPERT_TILE * EXPERT_TILE
    pad_end = jnp.cumsum(padded)
    pad_start = (pad_end - padded).astype(jnp.int32)
    lanes = jnp.arange(N_EXPERTS, dtype=jnp.int32)
    dest = rank + jnp.sum(jnp.where(expert[..., None] == lanes, pad_start, 0), axis=-1)
    blk_start = jnp.arange(n_blocks, dtype=jnp.int32) * EXPERT_TILE
    blk_expert = jnp.sum((pad_end[None, :] <= blk_start[:, None]).astype(jnp.int32), axis=1)
    blk_expert = jnp.minimum(blk_expert, N_EXPERTS - 1)
    blk_valid = jnp.clip(pad_start[blk_expert] + counts[blk_expert] - blk_start, 0, EXPERT_TILE)
    return dest.astype(jnp.int32), blk_expert.astype(jnp.int32), blk_valid.astype(jnp.int32)


def _sc_dispatch(table, dest, n_rows):
    n_tok, width = table.shape
    n_workers = SC_CORES * SC_SUBCORES
    per_worker = n_tok // n_workers
    chunk = SC_ROW_BUFFER_BYTES // (2 * width * table.dtype.itemsize)
    n_chunks = per_worker // chunk
    assert per_worker * n_workers == n_tok and n_chunks * chunk == per_worker and n_chunks % 2 == 0
    assert n_chunks >= 4 and chunk <= LANES and dest.shape == (n_tok, TOP_K)
    mesh = plsc.VectorSubcoreMesh(core_axis_name="core", subcore_axis_name="subcore")

    def body(table_hbm, d0_hbm, d1_hbm, out_hbm, i0_v, i1_v, rows_v, read_sem, put_sem):
        worker = lax.axis_index("subcore") * SC_CORES + lax.axis_index("core")
        base = worker * per_worker
        pltpu.sync_copy(d0_hbm.at[worker], i0_v)
        pltpu.sync_copy(d1_hbm.at[worker], i1_v)

        def read(j, slot):
            return pltpu.make_async_copy(table_hbm.at[pl.ds(base + j * chunk, chunk)], rows_v.at[slot],
                                         read_sem.at[slot])

        def put(j, slot, idx_v, k):
            return pltpu.make_async_copy(rows_v.at[slot], out_hbm.at[idx_v.at[j]], put_sem.at[slot, k])

        def drain(j, slot):
            read(j, slot).wait()
            put(j, slot, i0_v, 0).start()
            put(j, slot, i1_v, 1).start()
            put(j, slot, i0_v, 0).wait()
            put(j, slot, i1_v, 1).wait()

        read(0, 0).start()
        read(1, 1).start()

        @pl.loop(0, n_chunks - 2, step=2)
        def _(j):
            for slot in range(2):
                drain(j + slot, slot)
                read(j + slot + 2, slot).start()

        drain(n_chunks - 2, 0)
        drain(n_chunks - 1, 1)

    idx = dest.reshape(n_workers, n_chunks, chunk, TOP_K)
    return pl.kernel(
        body,
        out_type=jax.ShapeDtypeStruct((n_rows, width), table.dtype),
        mesh=mesh,
        scratch_types=[pltpu.VMEM((n_chunks, chunk), jnp.int32), pltpu.VMEM((n_chunks, chunk), jnp.int32),
                       pltpu.VMEM((2, chunk, width), table.dtype),
                       pltpu.SemaphoreType.DMA((2,)), pltpu.SemaphoreType.DMA((2, TOP_K))],
        name="sc_row_dispatch",
    )(table, idx[..., 0], idx[..., 1])


def _sc_gather(table, idx):
    n_rows = idx.shape[0]
    width = table.shape[1]
    n_workers = SC_CORES * SC_SUBCORES
    per_worker = n_rows // n_workers
    chunk = SC_ROW_BUFFER_BYTES // (width * table.dtype.itemsize)
    n_chunks = per_worker // chunk
    assert per_worker * n_workers == n_rows and n_chunks * chunk == per_worker and n_chunks % 2 == 0
    assert n_chunks >= 4 and chunk <= LANES
    mesh = plsc.VectorSubcoreMesh(core_axis_name="core", subcore_axis_name="subcore")

    def body(table_hbm, idx_hbm, out_hbm, idx_v, rows_v, gather_sem, write_sem):
        worker = lax.axis_index("subcore") * SC_CORES + lax.axis_index("core")
        base = worker * per_worker
        pltpu.sync_copy(idx_hbm.at[worker], idx_v)

        def gather(j, slot):
            return pltpu.make_async_copy(table_hbm.at[idx_v.at[j]], rows_v.at[slot], gather_sem.at[slot])

        def write(j, slot):
            return pltpu.make_async_copy(rows_v.at[slot], out_hbm.at[pl.ds(base + j * chunk, chunk)],
                                         write_sem.at[slot])

        gather(0, 0).start()
        gather(0, 0).wait()
        gather(1, 1).start()
        write(0, 0).start()

        @pl.loop(1, n_chunks - 1, step=2)
        def _(j):
            for s in range(2):
                slot = (1 + s) % 2
                gather(j + s, slot).wait()
                write(j + s - 1, 1 - slot).wait()
                gather(j + s + 1, 1 - slot).start()
                write(j + s, slot).start()

        last = n_chunks - 1
        gather(last, 1).wait()
        write(last, 1).start()
        write(last - 1, 0).wait()
        write(last, 1).wait()

    return pl.kernel(
        body,
        out_type=jax.ShapeDtypeStruct((n_rows, width), table.dtype),
        mesh=mesh,
        scratch_types=[pltpu.VMEM((n_chunks, chunk), jnp.int32),
                       pltpu.VMEM((2, chunk, width), table.dtype),
                       pltpu.SemaphoreType.DMA((2,)), pltpu.SemaphoreType.DMA((2,))],
        name="sc_row_gather",
    )(table, idx.reshape(n_workers, n_chunks, chunk))


def _expert_body(be_ref, used_ref, x_ref, wu_ref, wd_ref, o_ref, wub_ref, wdb_ref):
    i = pl.program_id(0)
    prev = be_ref[jnp.maximum(i - 1, 0)]

    @pl.when((i == 0) | (be_ref[i] != prev))
    def _():
        wub_ref[...] = wu_ref[...].astype(wub_ref.dtype)
        wdb_ref[...] = wd_ref[...].astype(wdb_ref.dtype)

    @pl.when(used_ref[i] > 0)
    def _():
        def rows(r0, r1):
            real = (lax.broadcasted_iota(jnp.int32, (r1 - r0, x_ref.shape[1]), 0) + r0) < used_ref[i]
            lo, hi = _unpack_bf16_pairs(jnp.where(real, x_ref[r0:r1, :], jnp.uint32(0)))
            half = lo.shape[1]
            gu = (jnp.dot(lo.astype(MXU_DTYPE), wub_ref[:half, :], preferred_element_type=F32)
                  + jnp.dot(hi.astype(MXU_DTYPE), wub_ref[half:, :], preferred_element_type=F32))

            def down():
                act = _silu(gu[:, :EXPERT_HIDDEN]) * gu[:, EXPERT_HIDDEN:]
                o_ref[r0:r1, :] = _pack_bf16_pairs(
                    jnp.dot(act.astype(MXU_DTYPE), wdb_ref[...], preferred_element_type=F32))

            return down

        group = EXPERT_TILE // 2
        stages = [rows(0, group), rows(group, EXPERT_TILE)]
        for down in stages:
            down()

    @pl.when(used_ref[i] == 0)
    def _():
        o_ref[...] = jnp.zeros_like(o_ref)


def _expert_ffn(layer, buf, blk_expert, blk_used, w_up, w_down):
    n_rows, packed = buf.shape
    d = 2 * packed
    n_blocks = n_rows // EXPERT_TILE
    h2 = w_up.shape[-1]
    grid_spec = pltpu.PrefetchScalarGridSpec(
        num_scalar_prefetch=2,
        grid=(n_blocks,),
        in_specs=[pl.BlockSpec((EXPERT_TILE, packed), lambda i, be, us: (i, 0)),
                  pl.BlockSpec((None, None, d, h2), lambda i, be, us: (layer, be[i], 0, 0)),
                  pl.BlockSpec((None, None, h2 // 2, d), lambda i, be, us: (layer, be[i], 0, 0))],
        out_specs=pl.BlockSpec((EXPERT_TILE, packed), lambda i, be, us: (i, 0)),
        scratch_shapes=[pltpu.VMEM((d, h2), MXU_DTYPE), pltpu.VMEM((h2 // 2, d), MXU_DTYPE)],
    )
    return pl.pallas_call(
        _expert_body,
        grid_spec=grid_spec,
        out_shape=jax.ShapeDtypeStruct((n_rows, packed), jnp.uint32),
        compiler_params=_cparams(("arbitrary",)),
        name="expert_ffn",
    )(blk_expert, blk_used, buf, w_up, w_down)


def _moe(layer, h2, route, counts, w_up, w_down):
    bsz, t, packed = h2.shape
    n_tok = bsz * t
    n_assign = n_tok * TOP_K
    n_blocks = -(-(n_assign + N_EXPERTS * (EXPERT_TILE - 1)) // EXPERT_TILE)
    record = route.reshape(n_tok, ROUTER_PAD)
    expert = record[:, ROUTE_EXPERT:ROUTE_EXPERT + TOP_K].astype(jnp.int32)
    rank = record[:, ROUTE_RANK:ROUTE_RANK + TOP_K].astype(jnp.int32)
    dest, blk_expert, blk_valid = _dispatch_plan(expert, rank, counts[0, :N_EXPERTS].astype(jnp.int32), n_blocks)
    buf = _sc_dispatch(h2.reshape(n_tok, packed), dest, n_blocks * EXPERT_TILE)
    y = _expert_ffn(layer, buf, blk_expert, blk_valid, w_up, w_down)
    return _sc_gather(y, dest.T.reshape(-1)).reshape(TOP_K, bsz, t, packed)


def _final_body(x_ref, y0_ref, y1_ref, rt_ref, mod_ref, g_ref, o_ref):
    o_ref[0] = _rms(_moe_residual(x_ref, y0_ref, y1_ref, rt_ref, mod_ref[5:6, :]), g_ref[...])


def _final_norm(layer, x, ymoe, route, modsel, g, ctx_tiles):
    bsz, t, d = x.shape
    tm = TOK_TILE
    lat = lambda b, i: (b, i + ctx_tiles, 0)
    return pl.pallas_call(
        _final_body,
        grid=(bsz, t // tm - ctx_tiles),
        in_specs=[pl.BlockSpec((1, tm, d), lat)] + _moe_specs(tm, d, lambda i: i + ctx_tiles)
        + [pl.BlockSpec((None, None, None, 6, d), lambda b, i: (layer, b, 1, 0, 0)),
           pl.BlockSpec((1, d), lambda b, i: (0, 0))],
        out_specs=pl.BlockSpec((1, tm, d), lambda b, i: (b, i, 0)),
        out_shape=jax.ShapeDtypeStruct((bsz, t - ctx_tiles * tm, d), F32),
        compiler_params=_cparams(("parallel", "arbitrary")),
        name="final_norm",
    )(x, ymoe, ymoe, route, modsel, g.reshape(1, d))


def _rope_tables(n_ctx, n_lat):
    t = jnp.arange(n_lat)
    row = (t // GRID_W).astype(F32)
    col = (t % GRID_W).astype(F32)
    n_freq = HEAD_DIM // 4
    inv = ROPE_THETA ** (-jnp.arange(n_freq, dtype=F32) / n_freq)
    ang_r = row[:, None] * inv
    ang_c = col[:, None] * inv
    cr, sr, cc, sc = jnp.cos(ang_r), jnp.sin(ang_r), jnp.cos(ang_c), jnp.sin(ang_c)
    z = jnp.zeros_like(sr)
    cos = jnp.concatenate([cr, cr, cc, cc], axis=-1)
    above = jnp.concatenate([-sr, z, -sc, z], axis=-1)
    below = jnp.concatenate([z, sr, z, sc], axis=-1)
    reps = LANES // HEAD_DIM

    def full(tab, ctx_value):
        tab = jnp.tile(tab, (1, reps))
        return jnp.concatenate([jnp.full((n_ctx, LANES), ctx_value, F32), tab], axis=0)

    return full(cos, 1.0), full(above, 0.0), full(below, 0.0)


def _pack_w_in(w_in):
    parts = jnp.split(w_in, np.cumsum(IN_SIZES)[:-1].tolist(), axis=-1)
    qa, ka, va, qb, kb, vb, rb, gb, qn, kn, vn = parts
    gb = jnp.pad(gb, ((0, 0), (0, GATE_PAD - gb.shape[-1])))
    return jnp.concatenate([qa, ka, va, qb, kb, vb, rb, gb, qn, kn, vn], axis=-1).astype(MXU_DTYPE)


def kernel(x, c, ctx, c_ctx, w_mod, b_mod, norm1_g, norm2_g, w_in, w_out, diff_lambda, diff_sub_g,
           gla_w_decay, gla_b_decay, gla_norm_g, na_rel_bias, w_router_group, b_router_group,
           w_router_expert, b_router_expert, w_expert_up, w_expert_down, final_g):
    bsz, seq, d = x.shape
    n_ctx = ctx.shape[1]
    depth = w_mod.shape[0]
    assert n_ctx == TOK_TILE and seq % (NA_TILE_ROWS * GRID_W) == 0 and d % LANES == 0
    assert seq // (NA_TILE_ROWS * GRID_W) >= NA_KEY_TILES
    mod_rows = -(-(bsz + 1) // 8) * 8
    cvec = jnp.zeros((mod_rows, d), F32).at[:bsz].set(c).at[bsz].set(c_ctx)
    mod = _modulation(cvec, w_mod, b_mod).reshape(depth, mod_rows, 6, d)
    modsel = jnp.stack([jnp.broadcast_to(mod[:, bsz][:, None], (depth, bsz, 6, d)), mod[:, :bsz]], axis=2)
    rope_tabs = _rope_tables(n_ctx, seq)
    xt = jnp.concatenate([ctx, x], axis=1)
    n_groups = BATCH_GROUPS if bsz % BATCH_GROUPS == 0 else 1
    gb = bsz // n_groups
    groups = [dict(xt=xt[g * gb:(g + 1) * gb], mod=modsel[:, g * gb:(g + 1) * gb], ymoe=None, route=None)
              for g in range(n_groups)]
    for l in range(depth):
        w_in_p = _pack_w_in(w_in[l])
        bias_tabs = _na_bias_tables(na_rel_bias[l], seq // GRID_W)
        w_router = jnp.pad(jnp.concatenate([w_router_group[l], w_router_expert[l]], axis=-1),
                           ((0, 0), (0, ROUTER_PAD - N_GROUPS - N_EXPERTS)))
        w_router = jnp.concatenate(_split_bf16(w_router, 2), axis=-1)
        b_router = jnp.pad(jnp.concatenate([b_router_group[l], b_router_expert[l]]),
                           (0, ROUTER_PAD - N_GROUPS - N_EXPERTS)).reshape(1, ROUTER_PAD)
        w_out_b = w_out[l].astype(MXU_DTYPE)
        for st in groups:
            outs = _in_proj(l, st["xt"], st["ymoe"], st["route"], st["mod"], norm1_g[l], w_in_p, rope_tabs)
            qa, ka, va, bg, qn, kn, vn = outs[:7]
            if st["ymoe"] is not None:
                st["xt"] = outs[7]
            ya_ctx, ya_lat = _diff_attention(l, qa, ka, va, diff_lambda[l], diff_sub_g[l])
            o_f, o_b = _gla_scan(bg, gla_w_decay[l], gla_b_decay[l])
            yn = _neighborhood_attention(qn, kn, vn, bias_tabs)
            st["xt"], h2, st["route"], counts = _out_proj(l, ya_ctx, ya_lat, o_f, o_b, bg, yn, st["xt"], st["mod"],
                                                          gla_norm_g[l], norm2_g[l], w_out_b, w_router, b_router)
            st["ymoe"] = _moe(l, h2, st["route"], counts, w_expert_up, w_expert_down)
    outs = [_final_norm(depth - 1, st["xt"], st["ymoe"], st["route"], st["mod"], final_g, n_ctx // TOK_TILE)
            for st in groups]
    return jnp.concatenate(outs, axis=0)
```

```python
import functools
import math

import numpy as np
import jax
import jax.numpy as jnp
from jax import lax
from jax.experimental import pallas as pl
from jax.experimental.pallas import tpu as pltpu
from jax.experimental.pallas import tpu_sc as plsc

F32 = jnp.float32
MXU_DTYPE = jnp.bfloat16
HI = lax.Precision.HIGHEST

GRID_W = 64
HEAD_DIM = 64
ROPE_THETA = 10000.0
NORM_EPS = 1e-6

A_HEADS = 4
A_QK = HEAD_DIM
A_V = 2 * HEAD_DIM
B_HEADS = 4
B_DK = HEAD_DIM // 2
B_DV = HEAD_DIM
B_GATE_RANK = 16
B_GATE_TAU = 16.0
B_CHUNK = 64
LOG2_E = math.log2(math.e)
GLA_SUB = 16
C_HEADS = 4
C_DH = HEAD_DIM
NA_ROWS = 8
NA_COLS = 16

A_WIDTH = A_HEADS * A_V
B_WIDTH = B_HEADS * B_DV
C_WIDTH = C_HEADS * C_DH
B_QK = B_HEADS * B_DK
IN_SIZES = (A_HEADS * 2 * A_QK, A_HEADS * 2 * A_QK, A_WIDTH,
            B_QK, B_QK, B_WIDTH, B_WIDTH, 2 * B_GATE_RANK,
            C_WIDTH, C_WIDTH, C_WIDTH)

N_GROUPS = 4
EXPERTS_PER_GROUP = 8
N_EXPERTS = N_GROUPS * EXPERTS_PER_GROUP
TOP_K = 2
EXPERT_HIDDEN = 512

LANES = 128
TOK_TILE = 256
NA_TILE_ROWS = 4
NA_KEY_TILES = 3
IN_PROJ_TILE = 256
EXPERT_TILE = 512
GATE_PAD = LANES
BG_WIDTH = 2 * B_QK + 2 * B_WIDTH + GATE_PAD
IN_PAD_WIDTH = 3 * A_WIDTH + BG_WIDTH + 3 * C_WIDTH
ROUTER_PAD = LANES
ROUTE_EXPERT = 0
ROUTE_GATE = 2
ROUTE_RANK = 4
SC_CORES = 2
SC_SUBCORES = 16
SC_ROW_BUFFER_BYTES = 128 * 1024
VMEM_LIMIT = 48 * 1024 * 1024


def _split_bf16(x, pieces):
    out = []
    for _ in range(pieces):
        p = x.astype(jnp.bfloat16)
        out.append(p)
        x = x - p.astype(F32)
    return out


def _pack_bf16_pairs(x):
    w = x.shape[1] // 2
    bits = lax.bitcast_convert_type(x.astype(jnp.bfloat16).astype(F32), jnp.uint32)
    return (bits[:, :w] >> 16) | (bits[:, w:] & jnp.uint32(0xFFFF0000))


def _unpack_bf16_pairs(u):
    lo = lax.bitcast_convert_type(u << 16, F32)
    hi = lax.bitcast_convert_type(u & jnp.uint32(0xFFFF0000), F32)
    return lo, hi


def _silu(x):
    return x * (1.0 / (1.0 + jnp.exp(-x)))


def _cparams(sem):
    return pltpu.CompilerParams(dimension_semantics=sem, vmem_limit_bytes=VMEM_LIMIT)


def _mod_body(c_ref, w_ref, b_ref, o_ref):
    a = _silu(c_ref[...])
    o_ref[...] = jnp.dot(a, w_ref[...], precision=HI, preferred_element_type=F32) + b_ref[...]


def _modulation(cvec, w_mod, b_mod):
    depth, d, d6 = w_mod.shape
    rows = cvec.shape[0]
    return pl.pallas_call(
        _mod_body,
        grid=(depth, d6 // d),
        in_specs=[pl.BlockSpec((rows, d), lambda l, j: (0, 0)),
                  pl.BlockSpec((None, d, d), lambda l, j: (l, 0, j)),
                  pl.BlockSpec((None, 1, d), lambda l, j: (l, 0, j))],
        out_specs=pl.BlockSpec((None, rows, d), lambda l, j: (l, 0, j)),
        out_shape=jax.ShapeDtypeStruct((depth, rows, d6), F32),
        compiler_params=_cparams(("arbitrary", "arbitrary")),
        name="modulation",
    )(cvec, w_mod, b_mod.reshape(depth, 1, d6))


def _rms(x, g):
    return x * lax.rsqrt(jnp.mean(x * x, axis=-1, keepdims=True) + NORM_EPS) * g


def _rope(x, cos, sa, sb):
    return x * cos + pltpu.roll(x, LANES - 16, 1) * sa + pltpu.roll(x, 16, 1) * sb


def _moe_residual(x_ref, y0_ref, y1_ref, rt_ref, gate2):
    rt = rt_ref[0]
    g0, g1 = rt[:, ROUTE_GATE:ROUTE_GATE + 1], rt[:, ROUTE_GATE + 1:ROUTE_GATE + 2]
    lo0, hi0 = _unpack_bf16_pairs(y0_ref[0])
    lo1, hi1 = _unpack_bf16_pairs(y1_ref[0])
    moe = jnp.concatenate([g0 * lo0 + g1 * lo1, g0 * hi0 + g1 * hi1], axis=1)
    return x_ref[0] + gate2 * moe


def _in_proj_body(combine, ctx_len, *refs):
    if combine:
        (x_ref, y0_ref, y1_ref, rt_ref, pmod_ref, mod_ref, g_ref, w_ref, cos_ref, sa_ref, sb_ref,
         qa_ref, ka_ref, va_ref, bg_ref, qn_ref, kn_ref, vn_ref, xo_ref) = refs
    else:
        (x_ref, mod_ref, g_ref, w_ref, cos_ref, sa_ref, sb_ref,
         qa_ref, ka_ref, va_ref, bg_ref, qn_ref, kn_ref, vn_ref) = refs
    tm = x_ref.shape[1]
    row = lax.broadcasted_iota(jnp.int32, (tm, 1), 0) + pl.program_id(1) * tm
    is_ctx = row < ctx_len

    def mod_row(ref, j):
        return jnp.where(is_ctx, ref[0, j:j + 1, :], ref[1, j:j + 1, :])

    if combine:
        x = _moe_residual(x_ref, y0_ref, y1_ref, rt_ref, mod_row(pmod_ref, 5))
        xo_ref[0] = x
    else:
        x = x_ref[0]
    h = _rms(x, g_ref[...]) * (1.0 + mod_row(mod_ref, 1)) + mod_row(mod_ref, 0)
    hb = h.astype(MXU_DTYPE)

    def proj(lo, hi):
        return jnp.dot(hb, w_ref[:, lo:hi], preferred_element_type=F32)

    cos, sa, sb = cos_ref[...], sa_ref[...], sb_ref[...]
    qk = proj(0, 2 * A_WIDTH)
    for hh in range(A_HEADS):
        lo = hh * LANES
        q = qk[:, lo:lo + LANES]
        qa_ref[0, :, lo:lo + LANES] = (_rope(q, cos, sa, sb) * (A_QK ** -0.5 * LOG2_E)).astype(qa_ref.dtype)
        k = qk[:, A_WIDTH + lo:A_WIDTH + lo + LANES]
        ka_ref[0, :, lo:lo + LANES] = _rope(k, cos, sa, sb).astype(ka_ref.dtype)
    o = 2 * A_WIDTH
    va = proj(o, o + A_WIDTH)
    ones = jnp.ones((va.shape[0], A_V), F32)
    va_ref[0] = jnp.concatenate([piece for hh in range(A_HEADS)
                                 for piece in (va[:, hh * A_V:(hh + 1) * A_V], ones)],
                                axis=1).astype(va_ref.dtype)
    o += A_WIDTH
    bg_ref[0] = proj(o, o + BG_WIDTH)
    o += BG_WIDTH
    qn_ref[0] = (proj(o, o + C_WIDTH) * (C_DH ** -0.5)).astype(qn_ref.dtype)
    o += C_WIDTH
    kn_ref[0] = proj(o, o + C_WIDTH).astype(kn_ref.dtype)
    o += C_WIDTH
    vn_ref[0] = proj(o, o + C_WIDTH).astype(vn_ref.dtype)


def _moe_specs(tm, d, row_block):
    return [pl.BlockSpec((None, 1, tm, d // 2), lambda b, i: (0, b, row_block(i), 0)),
            pl.BlockSpec((None, 1, tm, d // 2), lambda b, i: (1, b, row_block(i), 0)),
            pl.BlockSpec((1, tm, ROUTER_PAD), lambda b, i: (b, row_block(i), 0))]


def _in_proj(layer, x, ymoe, route, modsel, g1, w_in_p, rope_tabs):
    bsz, t, d = x.shape
    tm = IN_PROJ_TILE
    combine = ymoe is not None
    tok = lambda b, i: (b, i, 0)
    x_spec = pl.BlockSpec((1, tm, d), tok)

    def mod_spec(l):
        return pl.BlockSpec((None, None, 2, 6, d), lambda b, i: (l, b, 0, 0, 0))

    tab_spec = pl.BlockSpec((tm, LANES), lambda b, i: (i, 0))
    in_specs = [x_spec]
    args = [x]
    if combine:
        in_specs += _moe_specs(tm, d, lambda i: i) + [mod_spec(layer - 1)]
        args += [ymoe, ymoe, route, modsel]
    in_specs += [mod_spec(layer), pl.BlockSpec((1, d), lambda b, i: (0, 0)),
                 pl.BlockSpec((d, IN_PAD_WIDTH), lambda b, i: (0, 0)), tab_spec, tab_spec, tab_spec]
    args += [modsel, g1.reshape(1, d), w_in_p, *rope_tabs]

    def o(width, dtype):
        return pl.BlockSpec((1, tm, width), tok), jax.ShapeDtypeStruct((bsz, t, width), dtype)

    outs = [o(A_WIDTH, MXU_DTYPE), o(A_WIDTH, MXU_DTYPE), o(2 * A_WIDTH, MXU_DTYPE), o(BG_WIDTH, F32),
            o(C_WIDTH, MXU_DTYPE), o(C_WIDTH, MXU_DTYPE), o(C_WIDTH, MXU_DTYPE)]
    if combine:
        outs.append(o(d, F32))
    return pl.pallas_call(
        functools.partial(_in_proj_body, combine, TOK_TILE),
        grid=(bsz, pl.cdiv(t, tm)),
        in_specs=in_specs,
        out_specs=[s for s, _ in outs],
        out_shape=[s for _, s in outs],
        compiler_params=_cparams(("parallel", "arbitrary")),
        name="in_proj",
    )(*args)


def _diff_rows(lam_init, q, k_ref, v_ref, n_keys, lam, g):
    lane = lax.broadcasted_iota(jnp.int32, (1, LANES), 1)
    zero = jnp.zeros_like(q)
    nt = (((1,), (1,)), ((), ()))
    k = k_ref[0, :n_keys, :]
    s1 = lax.dot_general(jnp.where(lane < A_QK, q, zero), k, nt, preferred_element_type=F32)
    s2 = lax.dot_general(jnp.where(lane >= A_QK, q, zero), k, nt, preferred_element_type=F32)

    def finish():
        v1 = v_ref[0, :n_keys, :]
        outs = []
        for s in (s1, s2):
            p = jnp.exp2((s - jnp.max(s, axis=-1, keepdims=True)).astype(MXU_DTYPE))
            outs.append(jnp.dot(p, v1, preferred_element_type=F32))
        o = (outs[0][:, :A_V] * (1.0 / outs[0][:, A_V:A_V + 1])
             - outs[1][:, :A_V] * (lam / outs[1][:, A_V:A_V + 1]))
        return _rms(o, g) * (1.0 - lam_init)

    return finish


def _diff_attn_body(lam_init, ctx_len, qa_ref, qb_ref, k_ref, v_ref, lam_ref, g_ref, oc_ref, ol_ref):
    lm = lam_ref[...]
    lam = (jnp.exp(jnp.sum(lm[0:1] * lm[1:2], axis=1, keepdims=True))
           - jnp.exp(jnp.sum(lm[2:3] * lm[3:4], axis=1, keepdims=True)) + lam_init)
    g = g_ref[...]
    rows = qa_ref.shape[1]

    @pl.when(pl.program_id(2) == 0)
    def _():
        oc_ref[0] = _diff_rows(lam_init, qa_ref[0], k_ref, v_ref, ctx_len, lam, g)().astype(oc_ref.dtype)

    @pl.when(pl.program_id(2) > 0)
    def _():
        n_keys = k_ref.shape[1]
        fin_a = _diff_rows(lam_init, qa_ref[0], k_ref, v_ref, n_keys, lam, g)
        fin_b = _diff_rows(lam_init, qb_ref[0], k_ref, v_ref, n_keys, lam, g)
        ol_ref[0, :rows, :] = fin_a().astype(ol_ref.dtype)
        ol_ref[0, rows:, :] = fin_b().astype(ol_ref.dtype)


def _diff_attention(layer, qa, ka, va, lam, g_sub):
    bsz, t, _ = qa.shape
    tq = TOK_TILE
    n_lat = (t - tq) // (2 * tq)
    assert n_lat * 2 * tq == t - tq
    lam_init = 0.8 - 0.6 * math.exp(-0.3 * layer)

    def q_spec(off):
        return pl.BlockSpec((1, tq, LANES), lambda b, h, i: (b, jnp.maximum(2 * i + off, 0), h))

    return pl.pallas_call(
        functools.partial(_diff_attn_body, lam_init, TOK_TILE),
        grid=(bsz, A_HEADS, 1 + n_lat),
        in_specs=[q_spec(-1), q_spec(0),
                  pl.BlockSpec((1, t, LANES), lambda b, h, i: (b, 0, h)),
                  pl.BlockSpec((1, t, 2 * A_V), lambda b, h, i: (b, 0, h)),
                  pl.BlockSpec((4, A_QK), lambda b, h, i: (0, 0)),
                  pl.BlockSpec((1, A_V), lambda b, h, i: (0, 0))],
        out_specs=[pl.BlockSpec((1, tq, LANES), lambda b, h, i: (b, 0, h)),
                   pl.BlockSpec((1, 2 * tq, LANES), lambda b, h, i: (b, jnp.maximum(i - 1, 0), h))],
        out_shape=[jax.ShapeDtypeStruct((bsz, tq, A_WIDTH), MXU_DTYPE),
                   jax.ShapeDtypeStruct((bsz, t - tq, A_WIDTH), MXU_DTYPE)],
        compiler_params=_cparams(("parallel", "parallel", "arbitrary")),
        name="diff_attention",
    )(qa, qa, ka, va, lam, g_sub.reshape(1, A_V))


def _gla_body(f_ref, r_ref, wdec_ref, bdec_ref, of_ref, ob_ref, sf_ref, sb_ref, es_ref, bs_ref, qss_ref):
    c = B_CHUNK
    n_chunks = TOK_TILE // c

    @pl.when(pl.program_id(1) == 0)
    def _():
        sf_ref[...] = jnp.zeros_like(sf_ref)
        sb_ref[...] = jnp.zeros_like(sb_ref)

    sub = GLA_SUB
    n_sub = c // sub
    nt = (((1,), (1,)), ((), ()))
    t_row = lax.broadcasted_iota(jnp.int32, (TOK_TILE, TOK_TILE), 0)
    t_col = lax.broadcasted_iota(jnp.int32, (TOK_TILE, TOK_TILE), 1)
    same_chunk = (t_row // c) == (t_col // c)
    tri_f = (same_chunk & (t_col <= t_row)).astype(jnp.bfloat16)
    tri_b = (same_chunk & (t_col >= t_row)).astype(jnp.bfloat16)
    s_iota = lax.broadcasted_iota(jnp.int32, (sub, LANES), 0)
    idx = lax.broadcasted_iota(jnp.int32, (c, LANES), 0)
    head_of_k = lax.broadcasted_iota(jnp.int32, (B_QK, B_WIDTH), 0) // B_DK
    head_of_v = lax.broadcasted_iota(jnp.int32, (B_QK, B_WIDTH), 1) // B_DV
    expand = (head_of_k == head_of_v).astype(MXU_DTYPE)
    same_head_t = (lax.broadcasted_iota(jnp.int32, (B_WIDTH, B_QK), 0) // B_DV
                   == lax.broadcasted_iota(jnp.int32, (B_WIDTH, B_QK), 1) // B_DK)
    n_ref = n_sub - 1
    kt_keep = (lax.broadcasted_iota(jnp.int32, (B_HEADS * c, n_ref * B_QK), 0) // c
               == (lax.broadcasted_iota(jnp.int32, (B_HEADS * c, n_ref * B_QK), 1) % B_QK) // B_DK)
    vx_keep = (lax.broadcasted_iota(jnp.int32, (B_HEADS * c, B_WIDTH), 0) // c
               == lax.broadcasted_iota(jnp.int32, (B_HEADS * c, B_WIDTH), 1) // B_DV)
    pick = (lax.broadcasted_iota(jnp.int32, (c, c * sub), 1) // sub
            == lax.broadcasted_iota(jnp.int32, (c, c * sub), 0)).astype(MXU_DTYPE)

    def log_decay(src_ref, backward):
        gl = src_ref[0, :, 2 * B_QK + 2 * B_WIDTH:BG_WIDTH]
        d0 = B_QK if backward else 0
        g_hi, g_lo = _split_bf16(gl, 2)
        w_hi, w_lo = _split_bf16(wdec_ref[:, d0:d0 + B_QK], 2)
        z = (jnp.dot(g_hi, w_hi, preferred_element_type=F32) + jnp.dot(g_hi, w_lo, preferred_element_type=F32)
             + jnp.dot(g_lo, w_hi, preferred_element_type=F32) + bdec_ref[:, d0:d0 + B_QK])
        log_a = (jnp.minimum(z, 0.0) - jnp.log(1.0 + jnp.exp(-jnp.abs(z)))) / B_GATE_TAU
        tri = tri_b if backward else tri_f
        return sum(jnp.dot(tri, p, preferred_element_type=F32) for p in _split_bf16(log_a, 3))

    def chunk(slot, src_ref, b_all, lo, backward, st_ref, out_ref):
        q = src_ref[0, lo:lo + c, 0:B_QK] * (B_DK ** -0.5)
        k = src_ref[0, lo:lo + c, B_QK:2 * B_QK]
        v = src_ref[0, lo:lo + c, 2 * B_QK:2 * B_QK + B_WIDTH]
        b = b_all[lo:lo + c]
        b_ref, qs_ref, e_ref = bs_ref.at[slot], qss_ref.at[slot], es_ref.at[slot]
        b_ref[...] = b
        qs_ref[...] = q
        blk = ((c - 1 - idx) if backward else idx) // sub

        q_parts, k_parts = [], []
        for m, late, early in ((1, blk == 1, blk == 0), (2, blk >= 2, blk <= 1), (3, blk == 3, blk == 2)):
            r_row = (c - 1 - sub * m) if backward else sub * m
            r = b_ref[r_row:r_row + 1, :]
            q_parts.append(q * jnp.exp(jnp.where(late, b - r, -jnp.inf)))
            k_parts.append(k * jnp.exp(jnp.where(early, r - b, -jnp.inf)))
        q_cat = jnp.concatenate(q_parts, axis=1).astype(MXU_DTYPE)
        k_cat = jnp.concatenate(k_parts, axis=1)
        k_exp = jnp.where(kt_keep, jnp.concatenate([k_cat] * B_HEADS, axis=0), 0.0).astype(MXU_DTYPE)
        a_off = lax.dot_general(q_cat, k_exp, nt, preferred_element_type=F32)

        for tt in range(c):
            lo_s = tt // sub * sub
            keep = (s_iota >= tt - lo_s) if backward else (s_iota <= tt - lo_s)
            bt = b_ref[tt:tt + 1, :]
            qt = qs_ref[tt:tt + 1, :]
            e = jnp.exp(jnp.where(keep, bt - b[lo_s:lo_s + sub], -jnp.inf)) * (qt * k[lo_s:lo_s + sub])
            e_ref[tt * sub:(tt + 1) * sub, :] = e.astype(e_ref.dtype)
        a_exp = jnp.dot(e_ref[...], expand, preferred_element_type=F32)
        b_end = b[0:1, :] if backward else b[c - 1:c, :]
        kd = k * jnp.exp(b_end - b)
        upd = lax.dot_general(v, kd, (((0,), (0,)), ((), ())), preferred_element_type=F32)

        def intra():
            v_exp = jnp.where(vx_keep, jnp.concatenate([v] * B_HEADS, axis=0), 0.0).astype(MXU_DTYPE)
            o_off = jnp.dot(a_off.astype(MXU_DTYPE), v_exp, preferred_element_type=F32)
            prod = a_exp.reshape(n_sub, sub, sub, B_WIDTH) * v.reshape(n_sub, 1, sub, B_WIDTH)
            o_diag = jnp.dot(pick, prod.reshape(c * sub, B_WIDTH).astype(MXU_DTYPE),
                             preferred_element_type=F32)
            o_intra = o_off + o_diag

            def recur():
                st = st_ref[...]
                o_inter = lax.dot_general(q * jnp.exp(b), st, nt, preferred_element_type=F32)
                out_ref[0, lo:lo + c, :] = o_intra + o_inter
                st_ref[...] = jnp.exp(b_end) * st + jnp.where(same_head_t, upd, 0.0)

            return recur

        return intra

    b_fwd = log_decay(f_ref, False)
    b_bwd = log_decay(r_ref, True)
    stage = []
    for ci in range(n_chunks):
        stage.append(chunk(2 * ci, f_ref, b_fwd, ci * c, False, sf_ref, of_ref))
        stage.append(chunk(2 * ci + 1, r_ref, b_bwd, (n_chunks - 1 - ci) * c, True, sb_ref, ob_ref))
    stage = [intra() for intra in stage]
    for recur in stage:
        recur()


def _gla_scan(bg, w_dec, b_dec):
    bsz, t, _ = bg.shape
    n = t // TOK_TILE
    rev = lambda b, i: (b, jnp.where(i == 0, 0, n - i), 0)
    fwd = lambda b, i: (b, i, 0)
    wdec = jnp.zeros((GATE_PAD, 2 * B_QK), F32)
    wdec = wdec.at[:B_GATE_RANK, :B_QK].set(w_dec[0]).at[B_GATE_RANK:2 * B_GATE_RANK, B_QK:].set(w_dec[1])
    bdec = b_dec.reshape(1, 2 * B_QK)
    o_shape = jax.ShapeDtypeStruct((bsz, t, B_WIDTH), F32)
    n_slots = 2 * (TOK_TILE // B_CHUNK)
    return pl.pallas_call(
        _gla_body,
        grid=(bsz, n),
        in_specs=[pl.BlockSpec((1, TOK_TILE, BG_WIDTH), fwd),
                  pl.BlockSpec((1, TOK_TILE, BG_WIDTH), rev),
                  pl.BlockSpec((GATE_PAD, 2 * B_QK), lambda b, i: (0, 0)),
                  pl.BlockSpec((1, 2 * B_QK), lambda b, i: (0, 0))],
        out_specs=[pl.BlockSpec((1, TOK_TILE, B_WIDTH), fwd),
                   pl.BlockSpec((1, TOK_TILE, B_WIDTH), rev)],
        out_shape=[o_shape, o_shape],
        scratch_shapes=[pltpu.VMEM((B_WIDTH, B_QK), F32), pltpu.VMEM((B_WIDTH, B_QK), F32),
                        pltpu.VMEM((n_slots, B_CHUNK * GLA_SUB, LANES), MXU_DTYPE),
                        pltpu.VMEM((n_slots, B_CHUNK, LANES), F32),
                        pltpu.VMEM((n_slots, B_CHUNK, LANES), F32)],
        compiler_params=_cparams(("parallel", "arbitrary")),
        name="gla_scan",
    )(bg, bg, wdec, bdec)


def _na_bias_tables(rpb, rows):
    n_tiles = rows // NA_TILE_ROWS
    wr = min(NA_ROWS, rows)
    n_dr, n_dc = 2 * NA_ROWS - 1, 2 * NA_COLS - 1
    cq = np.arange(GRID_W)[:, None]
    ck = np.arange(GRID_W)[None, :]
    cs = np.clip(cq - NA_COLS // 2, 0, GRID_W - NA_COLS)
    col_ok = (ck >= cs) & (ck < cs + NA_COLS)
    dc = np.clip(ck - cq, -(NA_COLS - 1), NA_COLS - 1) + (NA_COLS - 1)
    onehot = (dc.reshape(1, -1) == np.arange(n_dc)[:, None]).astype(np.float32)
    by_col = jnp.dot(rpb.astype(F32).reshape(-1, n_dc), onehot, precision=HI)
    by_col = jnp.where(col_ok.reshape(1, 1, GRID_W, GRID_W),
                       by_col.reshape(C_HEADS, n_dr, GRID_W, GRID_W), -jnp.inf)
    masked = jnp.full((C_HEADS, GRID_W, GRID_W), -jnp.inf, F32)
    tabs = []
    for j in (0, 1, n_tiles - 1):
        kr0 = int(np.clip(j - 1, 0, n_tiles - NA_KEY_TILES)) * NA_TILE_ROWS
        q_rows = []
        for qr in range(NA_TILE_ROWS):
            r = j * NA_TILE_ROWS + qr
            start = int(np.clip(r - wr // 2, 0, rows - wr))
            blocks = []
            for kw in range(NA_KEY_TILES * NA_TILE_ROWS):
                kr = kr0 + kw
                blocks.append(by_col[:, kr - r + NA_ROWS - 1] if start <= kr < start + wr else masked)
            q_rows.append(jnp.concatenate(blocks, axis=-1))
        tabs.append(jnp.concatenate(q_rows, axis=1))
    return jnp.stack(tabs)


def _na_body(q_ref, k0_ref, k1_ref, k2_ref, kc_ref, v0_ref, v1_ref, v2_ref, vc_ref, m_ref, o_ref):
    q = q_ref[0]
    lane = lax.broadcasted_iota(jnp.int32, (1, C_WIDTH), 1)
    zero = jnp.zeros_like(q)
    nt = (((1,), (1,)), ((), ()))

    def scores(qm, k_ref):
        return lax.dot_general(qm, k_ref[0], nt, preferred_element_type=F32)

    def head_out(hh, windows):
        qm = jnp.where((lane >= hh * C_DH) & (lane < (hh + 1) * C_DH), q, zero)
        s = [scores(qm, kc_ref)]
        for w, k_ref in enumerate(windows):
            s.append(scores(qm, k_ref) + m_ref[0, hh, :, w * TOK_TILE:(w + 1) * TOK_TILE])

        def finish():
            m = functools.reduce(jnp.maximum, [jnp.max(x, axis=-1, keepdims=True) for x in s])
            p = [jnp.exp(x - m) for x in s]
            den = functools.reduce(jnp.add, [jnp.sum(x, axis=-1, keepdims=True) for x in p])
            vals = [vc_ref] + [v0_ref, v1_ref, v2_ref][:len(windows)]
            o = functools.reduce(jnp.add, [jnp.dot(x.astype(MXU_DTYPE), v_ref[0], preferred_element_type=F32)
                                           for x, v_ref in zip(p, vals)])
            return o * (1.0 / den)

        return finish

    def emit(windows):
        finish = [head_out(hh, windows) for hh in range(C_HEADS)]
        o = finish[C_HEADS - 1]()
        for hh in range(C_HEADS - 2, -1, -1):
            o = jnp.where(lane < (hh + 1) * C_DH, finish[hh](), o)
        o_ref[0] = o.astype(o_ref.dtype)

    @pl.when(pl.program_id(0) == 0)
    def _():
        emit([])

    @pl.when(pl.program_id(0) > 0)
    def _():
        emit([k0_ref, k1_ref, k2_ref])


def _neighborhood_attention(qn, kn, vn, bias_tabs):
    bsz, t, _ = qn.shape
    n = t // TOK_TILE
    n_lat = n - 1

    def win(w):
        def index(i, b):
            kb0 = jnp.clip(i - 2, 0, n_lat - NA_KEY_TILES)
            return (b, kb0 + 1 + w, 0)
        return pl.BlockSpec((1, TOK_TILE, C_WIDTH), index)

    def cls(i, b):
        j = i - 1
        return (jnp.where(j <= 0, 0, jnp.where(j == n_lat - 1, 2, 1)), 0, 0, 0)

    own = pl.BlockSpec((1, TOK_TILE, C_WIDTH), lambda i, b: (b, i, 0))
    ctx = pl.BlockSpec((1, TOK_TILE, C_WIDTH), lambda i, b: (b, 0, 0))
    return pl.pallas_call(
        _na_body,
        grid=(n, bsz),
        in_specs=[own, win(0), win(1), win(2), ctx, win(0), win(1), win(2), ctx,
                  pl.BlockSpec((1, C_HEADS, TOK_TILE, NA_KEY_TILES * TOK_TILE), cls)],
        out_specs=own,
        out_shape=jax.ShapeDtypeStruct((bsz, t, C_WIDTH), MXU_DTYPE),
        compiler_params=_cparams(("parallel", "arbitrary")),
        name="neighborhood_attention",
    )(qn, kn, kn, kn, kn, vn, vn, vn, vn, bias_tabs)


def _out_proj_body(yac_ref, yal_ref, of_ref, ob_ref, r_ref, yn_ref, x_ref, mod_ref, gg_ref, g2_ref, w_ref,
                   wr_ref, br_ref, xo_ref, h_ref, rt_ref, cnt_ref, count_ref):
    is_ctx_tile = pl.program_id(1) == 0
    hi = lax.broadcasted_iota(jnp.int32, (B_WIDTH, B_WIDTH), 0) // B_DV
    hj = lax.broadcasted_iota(jnp.int32, (B_WIDTH, B_WIDTH), 1) // B_DV
    head_mean = jnp.where(hi == hj, 1.0 / B_DV, 0.0).astype(jnp.bfloat16)

    @pl.when((pl.program_id(0) == 0) & (pl.program_id(1) == 0))
    def _():
        count_ref[...] = jnp.zeros_like(count_ref)

    def rows(r0, r1):
        ya = jnp.where(is_ctx_tile, yac_ref[0, r0:r1, :], yal_ref[0, r0:r1, :])
        o = of_ref[0, r0:r1, :] + ob_ref[0, r0:r1, :]
        ms = sum(jnp.dot(p, head_mean, preferred_element_type=F32) for p in _split_bf16(o * o, 2))
        yb = o * lax.rsqrt(ms + NORM_EPS) * gg_ref[...] * _silu(r_ref[0, r0:r1, :])

        def project():
            mix = (jnp.dot(ya, w_ref[0:A_WIDTH, :], preferred_element_type=F32)
                   + jnp.dot(yb.astype(MXU_DTYPE), w_ref[A_WIDTH:A_WIDTH + B_WIDTH, :],
                             preferred_element_type=F32)
                   + jnp.dot(yn_ref[0, r0:r1, :], w_ref[A_WIDTH + B_WIDTH:, :], preferred_element_type=F32))
            x = x_ref[0, r0:r1, :] + mod_ref[2:3, :] * mix
            xo_ref[0, r0:r1, :] = x
            h = _rms(x, g2_ref[...]) * (1.0 + mod_ref[4:5, :]) + mod_ref[3:4, :]
            h_ref[0, r0:r1, :] = _pack_bf16_pairs(h)
            h_hi, h_lo = _split_bf16(h, 2)
            both = jnp.dot(h_hi, wr_ref[...], preferred_element_type=F32)
            logits = (both[:, :ROUTER_PAD] + both[:, ROUTER_PAD:]
                      + jnp.dot(h_lo, wr_ref[:, :ROUTER_PAD], preferred_element_type=F32) + br_ref[...])

            def route():
                rt_ref[0, r0:r1, :] = _route(logits, count_ref)

            return route

        return project

    half = x_ref.shape[1] // 2
    stages = [rows(0, half), rows(half, 2 * half)]
    stages = [project() for project in stages]
    for route in stages:
        route()
    cnt_ref[...] = count_ref[...]


def _route(lg, count_ref):
    lane = lax.broadcasted_iota(jnp.int32, lg.shape, 1)
    big = jnp.int32(ROUTER_PAD)

    def top(mask):
        v = jnp.max(jnp.where(mask, lg, -jnp.inf), axis=-1, keepdims=True)
        i = jnp.min(jnp.where(mask & (lg == v), lane, big), axis=-1, keepdims=True)
        return v, i

    g_mask = lane < N_GROUPS
    g_max, grp = top(g_mask)
    p_grp = 1.0 / jnp.sum(jnp.where(g_mask, jnp.exp(lg - g_max), 0.0), axis=-1, keepdims=True)
    e_lo = N_GROUPS + grp * EXPERTS_PER_GROUP
    e_mask = (lane >= e_lo) & (lane < e_lo + EXPERTS_PER_GROUP)
    v1, i1 = top(e_mask)
    v2, i2 = top(e_mask & (lane != i1))
    r = jnp.exp(v2 - v1)
    gate1 = p_grp / (1.0 + r)
    gate2 = p_grp * r / (1.0 + r)
    e1, e2 = i1 - N_GROUPS, i2 - N_GROUPS
    hot1 = (lane == e1).astype(F32)
    hot2 = (lane == e2).astype(F32)
    both = hot1 + hot2
    rows = lg.shape[0]
    earlier = (lax.broadcasted_iota(jnp.int32, (rows, rows), 1)
               < lax.broadcasted_iota(jnp.int32, (rows, rows), 0)).astype(jnp.bfloat16)
    before = jnp.dot(earlier, both.astype(jnp.bfloat16), preferred_element_type=F32) + count_ref[0:1, :]
    rank1 = jnp.sum(hot1 * before, axis=-1, keepdims=True)
    rank2 = jnp.sum(hot2 * before, axis=-1, keepdims=True)
    count_ref[0:1, :] = count_ref[0:1, :] + jnp.sum(both, axis=0, keepdims=True)
    out = jnp.where(lane == ROUTE_EXPERT, e1.astype(F32),
                    jnp.where(lane == ROUTE_EXPERT + 1, e2.astype(F32),
                              jnp.where(lane == ROUTE_GATE, gate1,
                                        jnp.where(lane == ROUTE_GATE + 1, gate2,
                                                  jnp.where(lane == ROUTE_RANK, rank1,
                                                            jnp.where(lane == ROUTE_RANK + 1, rank2, 0.0))))))
    return out


def _out_proj(layer, ya_ctx, ya_lat, o_f, o_b, bg, yn, x, modsel, g_gla, g2, w_out_b, w_router, b_router):
    bsz, t, d = x.shape
    tm = TOK_TILE
    tok = lambda b, i: (b, i, 0)
    const = lambda b, i: (0, 0)
    r_block = (2 * B_QK + B_WIDTH) // B_WIDTH
    return pl.pallas_call(
        _out_proj_body,
        grid=(bsz, t // tm),
        in_specs=[pl.BlockSpec((1, tm, A_WIDTH), lambda b, i: (b, 0, 0)),
                  pl.BlockSpec((1, tm, A_WIDTH), lambda b, i: (b, jnp.maximum(i - 1, 0), 0)),
                  pl.BlockSpec((1, tm, B_WIDTH), tok), pl.BlockSpec((1, tm, B_WIDTH), tok),
                  pl.BlockSpec((1, tm, B_WIDTH), lambda b, i: (b, i, r_block)),
                  pl.BlockSpec((1, tm, C_WIDTH), tok),
                  pl.BlockSpec((1, tm, d), tok),
                  pl.BlockSpec((None, None, None, 6, d), lambda b, i: (layer, b, jnp.minimum(i, 1), 0, 0)),
                  pl.BlockSpec((1, B_WIDTH), const), pl.BlockSpec((1, d), const),
                  pl.BlockSpec(w_out_b.shape, const),
                  pl.BlockSpec((d, 2 * ROUTER_PAD), const), pl.BlockSpec((1, ROUTER_PAD), const)],
        out_specs=[pl.BlockSpec((1, tm, d), tok), pl.BlockSpec((1, tm, d // 2), tok),
                   pl.BlockSpec((1, tm, ROUTER_PAD), tok), pl.BlockSpec((8, ROUTER_PAD), const)],
        out_shape=[jax.ShapeDtypeStruct((bsz, t, d), F32), jax.ShapeDtypeStruct((bsz, t, d // 2), jnp.uint32),
                   jax.ShapeDtypeStruct((bsz, t, ROUTER_PAD), F32), jax.ShapeDtypeStruct((8, ROUTER_PAD), F32)],
        scratch_shapes=[pltpu.VMEM((8, ROUTER_PAD), F32)],
        compiler_params=_cparams(("arbitrary", "arbitrary")),
        name="out_proj",
    )(ya_ctx, ya_lat, o_f, o_b, bg, yn, x, modsel, jnp.tile(g_gla, B_HEADS).reshape(1, B_WIDTH), g2.reshape(1, d),
      w_out_b, w_router, b_router)


def _dispatch_plan(expert, rank, counts, n_blocks):
    padded = (counts + EXPERT_TILE - 1) // EXPERT_TILE * EXPERT_TILE
    pad_end = jnp.cumsum(padded)
    pad_start = (pad_end - padded).astype(jnp.int32)
    lanes = jnp.arange(N_EXPERTS, dtype=jnp.int32)
    dest = rank + jnp.sum(jnp.where(expert[..., None] == lanes, pad_start, 0), axis=-1)
    blk_start = jnp.arange(n_blocks, dtype=jnp.int32) * EXPERT_TILE
    blk_expert = jnp.sum((pad_end[None, :] <= blk_start[:, None]).astype(jnp.int32), axis=1)
    blk_expert = jnp.minimum(blk_expert, N_EXPERTS - 1)
    blk_valid = jnp.clip(pad_start[blk_expert] + counts[blk_expert] - blk_start, 0, EXPERT_TILE)
    return dest.astype(jnp.int32), blk_expert.astype(jnp.int32), blk_valid.astype(jnp.int32)


def _sc_dispatch(table, dest, n_rows):
    n_tok, width = table.shape
    n_workers = SC_CORES * SC_SUBCORES
    per_worker = n_tok // n_workers
    chunk = SC_ROW_BUFFER_BYTES // (2 * width * table.dtype.itemsize)
    n_chunks = per_worker // chunk
    assert per_worker * n_workers == n_tok and n_chunks * chunk == per_worker and n_chunks % 2 == 0
    assert n_chunks >= 4 and chunk <= LANES and dest.shape == (n_tok, TOP_K)
    mesh = plsc.VectorSubcoreMesh(core_axis_name="core", subcore_axis_name="subcore")

    def body(table_hbm, d0_hbm, d1_hbm, out_hbm, i0_v, i1_v, rows_v, read_sem, put_sem):
        worker = lax.axis_index("subcore") * SC_CORES + lax.axis_index("core")
        base = worker * per_worker
        pltpu.sync_copy(d0_hbm.at[worker], i0_v)
        pltpu.sync_copy(d1_hbm.at[worker], i1_v)

        def read(j, slot):
            return pltpu.make_async_copy(table_hbm.at[pl.ds(base + j * chunk, chunk)], rows_v.at[slot],
                                         read_sem.at[slot])

        def put(j, slot, idx_v, k):
            return pltpu.make_async_copy(rows_v.at[slot], out_hbm.at[idx_v.at[j]], put_sem.at[slot, k])

        def drain(j, slot):
            read(j, slot).wait()
            put(j, slot, i0_v, 0).start()
            put(j, slot, i1_v, 1).start()
            put(j, slot, i0_v, 0).wait()
            put(j, slot, i1_v, 1).wait()

        read(0, 0).start()
        read(1, 1).start()

        @pl.loop(0, n_chunks - 2, step=2)
        def _(j):
            for slot in range(2):
                drain(j + slot, slot)
                read(j + slot + 2, slot).start()

        drain(n_chunks - 2, 0)
        drain(n_chunks - 1, 1)

    idx = dest.reshape(n_workers, n_chunks, chunk, TOP_K)
    return pl.kernel(
        body,
        out_type=jax.ShapeDtypeStruct((n_rows, width), table.dtype),
        mesh=mesh,
        scratch_types=[pltpu.VMEM((n_chunks, chunk), jnp.int32), pltpu.VMEM((n_chunks, chunk), jnp.int32),
                       pltpu.VMEM((2, chunk, width), table.dtype),
                       pltpu.SemaphoreType.DMA((2,)), pltpu.SemaphoreType.DMA((2, TOP_K))],
        name="sc_row_dispatch",
    )(table, idx[..., 0], idx[..., 1])


def _sc_gather(table, idx):
    n_rows = idx.shape[0]
    width = table.shape[1]
    n_workers = SC_CORES * SC_SUBCORES
    per_worker = n_rows // n_workers
    chunk = SC_ROW_BUFFER_BYTES // (width * table.dtype.itemsize)
    n_chunks = per_worker // chunk
    assert per_worker * n_workers == n_rows and n_chunks * chunk == per_worker and n_chunks % 2 == 0
    assert n_chunks >= 4 and chunk <= LANES
    mesh = plsc.VectorSubcoreMesh(core_axis_name="core", subcore_axis_name="subcore")

    def body(table_hbm, idx_hbm, out_hbm, idx_v, rows_v, gather_sem, write_sem):
        worker = lax.axis_index("subcore") * SC_CORES + lax.axis_index("core")
        base = worker * per_worker
        pltpu.sync_copy(idx_hbm.at[worker], idx_v)

        def gather(j, slot):
            return pltpu.make_async_copy(table_hbm.at[idx_v.at[j]], rows_v.at[slot], gather_sem.at[slot])

        def write(j, slot):
            return pltpu.make_async_copy(rows_v.at[slot], out_hbm.at[pl.ds(base + j * chunk, chunk)],
                                         write_sem.at[slot])

        gather(0, 0).start()
        gather(0, 0).wait()
        gather(1, 1).start()
        write(0, 0).start()

        @pl.loop(1, n_chunks - 1, step=2)
        def _(j):
            for s in range(2):
                slot = (1 + s) % 2
                gather(j + s, slot).wait()
                write(j + s - 1, 1 - slot).wait()
                gather(j + s + 1, 1 - slot).start()
                write(j + s, slot).start()

        last = n_chunks - 1
        gather(last, 1).wait()
        write(last, 1).start()
        write(last - 1, 0).wait()
        write(last, 1).wait()

    return pl.kernel(
        body,
        out_type=jax.ShapeDtypeStruct((n_rows, width), table.dtype),
        mesh=mesh,
        scratch_types=[pltpu.VMEM((n_chunks, chunk), jnp.int32),
                       pltpu.VMEM((2, chunk, width), table.dtype),
                       pltpu.SemaphoreType.DMA((2,)), pltpu.SemaphoreType.DMA((2,))],
        name="sc_row_gather",
    )(table, idx.reshape(n_workers, n_chunks, chunk))


def _expert_body(be_ref, used_ref, x_ref, wu_ref, wd_ref, o_ref, wub_ref, wdb_ref):
    i = pl.program_id(0)
    prev = be_ref[jnp.maximum(i - 1, 0)]

    @pl.when((i == 0) | (be_ref[i] != prev))
    def _():
        wub_ref[...] = wu_ref[...].astype(wub_ref.dtype)
        wdb_ref[...] = wd_ref[...].astype(wdb_ref.dtype)

    @pl.when(used_ref[i] > 0)
    def _():
        def rows(r0, r1):
            real = (lax.broadcasted_iota(jnp.int32, (r1 - r0, x_ref.shape[1]), 0) + r0) < used_ref[i]
            lo, hi = _unpack_bf16_pairs(jnp.where(real, x_ref[r0:r1, :], jnp.uint32(0)))
            half = lo.shape[1]
            gu = (jnp.dot(lo.astype(MXU_DTYPE), wub_ref[:half, :], preferred_element_type=F32)
                  + jnp.dot(hi.astype(MXU_DTYPE), wub_ref[half:, :], preferred_element_type=F32))

            def down():
                act = _silu(gu[:, :EXPERT_HIDDEN]) * gu[:, EXPERT_HIDDEN:]
                o_ref[r0:r1, :] = _pack_bf16_pairs(
                    jnp.dot(act.astype(MXU_DTYPE), wdb_ref[...], preferred_element_type=F32))

            return down

        group = EXPERT_TILE // 2
        stages = [rows(0, group), rows(group, EXPERT_TILE)]
        for down in stages:
            down()

    @pl.when(used_ref[i] == 0)
    def _():
        o_ref[...] = jnp.zeros_like(o_ref)


def _expert_ffn(layer, buf, blk_expert, blk_used, w_up, w_down):
    n_rows, packed = buf.shape
    d = 2 * packed
    n_blocks = n_rows // EXPERT_TILE
    h2 = w_up.shape[-1]
    grid_spec = pltpu.PrefetchScalarGridSpec(
        num_scalar_prefetch=2,
        grid=(n_blocks,),
        in_specs=[pl.BlockSpec((EXPERT_TILE, packed), lambda i, be, us: (i, 0)),
                  pl.BlockSpec((None, None, d, h2), lambda i, be, us: (layer, be[i], 0, 0)),
                  pl.BlockSpec((None, None, h2 // 2, d), lambda i, be, us: (layer, be[i], 0, 0))],
        out_specs=pl.BlockSpec((EXPERT_TILE, packed), lambda i, be, us: (i, 0)),
        scratch_shapes=[pltpu.VMEM((d, h2), MXU_DTYPE), pltpu.VMEM((h2 // 2, d), MXU_DTYPE)],
    )
    return pl.pallas_call(
        _expert_body,
        grid_spec=grid_spec,
        out_shape=jax.ShapeDtypeStruct((n_rows, packed), jnp.uint32),
        compiler_params=_cparams(("arbitrary",)),
        name="expert_ffn",
    )(blk_expert, blk_used, buf, w_up, w_down)


def _moe(layer, h2, route, counts, w_up, w_down):
    bsz, t, packed = h2.shape
    n_tok = bsz * t
    n_assign = n_tok * TOP_K
    n_blocks = -(-(n_assign + N_EXPERTS * (EXPERT_TILE - 1)) // EXPERT_TILE)
    record = route.reshape(n_tok, ROUTER_PAD)
    expert = record[:, ROUTE_EXPERT:ROUTE_EXPERT + TOP_K].astype(jnp.int32)
    rank = record[:, ROUTE_RANK:ROUTE_RANK + TOP_K].astype(jnp.int32)
    dest, blk_expert, blk_valid = _dispatch_plan(expert, rank, counts[0, :N_EXPERTS].astype(jnp.int32), n_blocks)
    buf = _sc_dispatch(h2.reshape(n_tok, packed), dest, n_blocks * EXPERT_TILE)
    y = _expert_ffn(layer, buf, blk_expert, blk_valid, w_up, w_down)
    return _sc_gather(y, dest.T.reshape(-1)).reshape(TOP_K, bsz, t, packed)


def _final_body(x_ref, y0_ref, y1_ref, rt_ref, mod_ref, g_ref, o_ref):
    o_ref[0] = _rms(_moe_residual(x_ref, y0_ref, y1_ref, rt_ref, mod_ref[5:6, :]), g_ref[...])


def _final_norm(layer, x, ymoe, route, modsel, g, ctx_tiles):
    bsz, t, d = x.shape
    tm = TOK_TILE
    lat = lambda b, i: (b, i + ctx_tiles, 0)
    return pl.pallas_call(
        _final_body,
        grid=(bsz, t // tm - ctx_tiles),
        in_specs=[pl.BlockSpec((1, tm, d), lat)] + _moe_specs(tm, d, lambda i: i + ctx_tiles)
        + [pl.BlockSpec((None, None, None, 6, d), lambda b, i: (layer, b, 1, 0, 0)),
           pl.BlockSpec((1, d), lambda b, i: (0, 0))],
        out_specs=pl.BlockSpec((1, tm, d), lambda b, i: (b, i, 0)),
        out_shape=jax.ShapeDtypeStruct((bsz, t - ctx_tiles * tm, d), F32),
        compiler_params=_cparams(("parallel", "arbitrary")),
        name="final_norm",
    )(x, ymoe, ymoe, route, modsel, g.reshape(1, d))


def _rope_tables(n_ctx, n_lat):
    t = jnp.arange(n_lat)
    row = (t // GRID_W).astype(F32)
    col = (t % GRID_W).astype(F32)
    n_freq = HEAD_DIM // 4
    inv = ROPE_THETA ** (-jnp.arange(n_freq, dtype=F32) / n_freq)
    ang_r = row[:, None] * inv
    ang_c = col[:, None] * inv
    cr, sr, cc, sc = jnp.cos(ang_r), jnp.sin(ang_r), jnp.cos(ang_c), jnp.sin(ang_c)
    z = jnp.zeros_like(sr)
    cos = jnp.concatenate([cr, cr, cc, cc], axis=-1)
    above = jnp.concatenate([-sr, z, -sc, z], axis=-1)
    below = jnp.concatenate([z, sr, z, sc], axis=-1)
    reps = LANES // HEAD_DIM

    def full(tab, ctx_value):
        tab = jnp.tile(tab, (1, reps))
        return jnp.concatenate([jnp.full((n_ctx, LANES), ctx_value, F32), tab], axis=0)

    return full(cos, 1.0), full(above, 0.0), full(below, 0.0)


def _pack_w_in(w_in):
    parts = jnp.split(w_in, np.cumsum(IN_SIZES)[:-1].tolist(), axis=-1)
    qa, ka, va, qb, kb, vb, rb, gb, qn, kn, vn = parts
    gb = jnp.pad(gb, ((0, 0), (0, GATE_PAD - gb.shape[-1])))
    return jnp.concatenate([qa, ka, va, qb, kb, vb, rb, gb, qn, kn, vn], axis=-1).astype(MXU_DTYPE)


def kernel(x, c, ctx, c_ctx, w_mod, b_mod, norm1_g, norm2_g, w_in, w_out, diff_lambda, diff_sub_g,
           gla_w_decay, gla_b_decay, gla_norm_g, na_rel_bias, w_router_group, b_router_group,
           w_router_expert, b_router_expert, w_expert_up, w_expert_down, final_g):
    bsz, seq, d = x.shape
    n_ctx = ctx.shape[1]
    depth = w_mod.shape[0]
    assert n_ctx == TOK_TILE and seq % (NA_TILE_ROWS * GRID_W) == 0 and d % LANES == 0
    assert seq // (NA_TILE_ROWS * GRID_W) >= NA_KEY_TILES
    mod_rows = -(-(bsz + 1) // 8) * 8
    cvec = jnp.zeros((mod_rows, d), F32).at[:bsz].set(c).at[bsz].set(c_ctx)
    mod = _modulation(cvec, w_mod, b_mod).reshape(depth, mod_rows, 6, d)
    modsel = jnp.stack([jnp.broadcast_to(mod[:, bsz][:, None], (depth, bsz, 6, d)), mod[:, :bsz]], axis=2)
    rope_tabs = _rope_tables(n_ctx, seq)
    xt = jnp.concatenate([ctx, x], axis=1)
    ymoe = route = None
    for l in range(depth):
        w_in_p = _pack_w_in(w_in[l])
        bias_tabs = _na_bias_tables(na_rel_bias[l], seq // GRID_W)
        w_router = jnp.pad(jnp.concatenate([w_router_group[l], w_router_expert[l]], axis=-1),
                           ((0, 0), (0, ROUTER_PAD - N_GROUPS - N_EXPERTS)))
        w_router = jnp.concatenate(_split_bf16(w_router, 2), axis=-1)
        b_router = jnp.pad(jnp.concatenate([b_router_group[l], b_router_expert[l]]),
                           (0, ROUTER_PAD - N_GROUPS - N_EXPERTS)).reshape(1, ROUTER_PAD)
        w_out_b = w_out[l].astype(MXU_DTYPE)
        outs = _in_proj(l, xt, ymoe, route, modsel, norm1_g[l], w_in_p, rope_tabs)
        qa, ka, va, bg, qn, kn, vn = outs[:7]
        if ymoe is not None:
            xt = outs[7]
        ya_ctx, ya_lat = _diff_attention(l, qa, ka, va, diff_lambda[l], diff_sub_g[l])
        o_f, o_b = _gla_scan(bg, gla_w_decay[l], gla_b_decay[l])
        yn = _neighborhood_attention(qn, kn, vn, bias_tabs)
        xt, h2, route, counts = _out_proj(l, ya_ctx, ya_lat, o_f, o_b, bg, yn, xt, modsel, gla_norm_g[l],
                                          norm2_g[l], w_out_b, w_router, b_router)
        ymoe = _moe(l, h2, route, counts, w_expert_up, w_expert_down)
    return _final_norm(depth - 1, xt, ymoe, route, modsel, final_g, n_ctx // TOK_TILE)
```

```python
import functools
import math

import numpy as np
import jax
import jax.numpy as jnp
from jax import lax
from jax.experimental import pallas as pl
from jax.experimental.pallas import tpu as pltpu
from jax.experimental.pallas import tpu_sc as plsc

F32 = jnp.float32
MXU_DTYPE = jnp.bfloat16
HI = lax.Precision.HIGHEST

GRID_W = 64
HEAD_DIM = 64
ROPE_THETA = 10000.0
NORM_EPS = 1e-6

A_HEADS = 4
A_QK = HEAD_DIM
A_V = 2 * HEAD_DIM
B_HEADS = 4
B_DK = HEAD_DIM // 2
B_DV = HEAD_DIM
B_GATE_RANK = 16
B_GATE_TAU = 16.0
B_CHUNK = 64
LOG2_E = math.log2(math.e)
GLA_SUB = 16
C_HEADS = 4
C_DH = HEAD_DIM
NA_ROWS = 8
NA_COLS = 16

A_WIDTH = A_HEADS * A_V
B_WIDTH = B_HEADS * B_DV
C_WIDTH = C_HEADS * C_DH
B_QK = B_HEADS * B_DK
IN_SIZES = (A_HEADS * 2 * A_QK, A_HEADS * 2 * A_QK, A_WIDTH,
            B_QK, B_QK, B_WIDTH, B_WIDTH, 2 * B_GATE_RANK,
            C_WIDTH, C_WIDTH, C_WIDTH)

N_GROUPS = 4
EXPERTS_PER_GROUP = 8
N_EXPERTS = N_GROUPS * EXPERTS_PER_GROUP
TOP_K = 2
EXPERT_HIDDEN = 512

LANES = 128
TOK_TILE = 256
NA_TILE_ROWS = 4
NA_KEY_TILES = 3
IN_PROJ_TILE = 256
EXPERT_TILE = 512
GATE_PAD = LANES
BG_WIDTH = 2 * B_QK + 2 * B_WIDTH + GATE_PAD
IN_PAD_WIDTH = 3 * A_WIDTH + BG_WIDTH + 3 * C_WIDTH
ROUTER_PAD = LANES
ROUTE_EXPERT = 0
ROUTE_GATE = 2
ROUTE_RANK = 4
SC_CORES = 2
SC_SUBCORES = 16
SC_ROW_BUFFER_BYTES = 128 * 1024
VMEM_LIMIT = 48 * 1024 * 1024


def _split_bf16(x, pieces):
    out = []
    for _ in range(pieces):
        p = x.astype(jnp.bfloat16)
        out.append(p)
        x = x - p.astype(F32)
    return out


def _pack_bf16_pairs(x):
    w = x.shape[1] // 2
    bits = lax.bitcast_convert_type(x.astype(jnp.bfloat16).astype(F32), jnp.uint32)
    return (bits[:, :w] >> 16) | (bits[:, w:] & jnp.uint32(0xFFFF0000))


def _unpack_bf16_pairs(u):
    lo = lax.bitcast_convert_type(u << 16, F32)
    hi = lax.bitcast_convert_type(u & jnp.uint32(0xFFFF0000), F32)
    return lo, hi


def _silu(x):
    return x * (1.0 / (1.0 + jnp.exp(-x)))


def _cparams(sem):
    return pltpu.CompilerParams(dimension_semantics=sem, vmem_limit_bytes=VMEM_LIMIT)


def _mod_body(c_ref, w_ref, b_ref, o_ref):
    a = _silu(c_ref[...])
    o_ref[...] = jnp.dot(a, w_ref[...], precision=HI, preferred_element_type=F32) + b_ref[...]


def _modulation(cvec, w_mod, b_mod):
    depth, d, d6 = w_mod.shape
    rows = cvec.shape[0]
    return pl.pallas_call(
        _mod_body,
        grid=(depth, d6 // d),
        in_specs=[pl.BlockSpec((rows, d), lambda l, j: (0, 0)),
                  pl.BlockSpec((None, d, d), lambda l, j: (l, 0, j)),
                  pl.BlockSpec((None, 1, d), lambda l, j: (l, 0, j))],
        out_specs=pl.BlockSpec((None, rows, d), lambda l, j: (l, 0, j)),
        out_shape=jax.ShapeDtypeStruct((depth, rows, d6), F32),
        compiler_params=_cparams(("arbitrary", "arbitrary")),
        name="modulation",
    )(cvec, w_mod, b_mod.reshape(depth, 1, d6))


def _rms(x, g):
    return x * lax.rsqrt(jnp.mean(x * x, axis=-1, keepdims=True) + NORM_EPS) * g


def _rope(x, cos, sa, sb):
    return x * cos + pltpu.roll(x, LANES - 16, 1) * sa + pltpu.roll(x, 16, 1) * sb


def _moe_residual(x_ref, y0_ref, y1_ref, rt_ref, gate2):
    rt = rt_ref[0]
    g0, g1 = rt[:, ROUTE_GATE:ROUTE_GATE + 1], rt[:, ROUTE_GATE + 1:ROUTE_GATE + 2]
    lo0, hi0 = _unpack_bf16_pairs(y0_ref[0])
    lo1, hi1 = _unpack_bf16_pairs(y1_ref[0])
    moe = jnp.concatenate([g0 * lo0 + g1 * lo1, g0 * hi0 + g1 * hi1], axis=1)
    return x_ref[0] + gate2 * moe


def _in_proj_body(combine, ctx_len, *refs):
    if combine:
        (x_ref, y0_ref, y1_ref, rt_ref, pmod_ref, mod_ref, g_ref, w_ref, cos_ref, sa_ref, sb_ref,
         qa_ref, ka_ref, va_ref, bg_ref, qn_ref, kn_ref, vn_ref, xo_ref) = refs
    else:
        (x_ref, mod_ref, g_ref, w_ref, cos_ref, sa_ref, sb_ref,
         qa_ref, ka_ref, va_ref, bg_ref, qn_ref, kn_ref, vn_ref) = refs
    tm = x_ref.shape[1]
    row = lax.broadcasted_iota(jnp.int32, (tm, 1), 0) + pl.program_id(1) * tm
    is_ctx = row < ctx_len

    def mod_row(ref, j):
        return jnp.where(is_ctx, ref[0, j:j + 1, :], ref[1, j:j + 1, :])

    if combine:
        x = _moe_residual(x_ref, y0_ref, y1_ref, rt_ref, mod_row(pmod_ref, 5))
        xo_ref[0] = x
    else:
        x = x_ref[0]
    h = _rms(x, g_ref[...]) * (1.0 + mod_row(mod_ref, 1)) + mod_row(mod_ref, 0)
    hb = h.astype(MXU_DTYPE)

    def proj(lo, hi):
        return jnp.dot(hb, w_ref[:, lo:hi], preferred_element_type=F32)

    cos, sa, sb = cos_ref[...], sa_ref[...], sb_ref[...]
    qk = proj(0, 2 * A_WIDTH)
    for hh in range(A_HEADS):
        lo = hh * LANES
        q = qk[:, lo:lo + LANES]
        qa_ref[0, :, lo:lo + LANES] = (_rope(q, cos, sa, sb) * (A_QK ** -0.5 * LOG2_E)).astype(qa_ref.dtype)
        k = qk[:, A_WIDTH + lo:A_WIDTH + lo + LANES]
        ka_ref[0, :, lo:lo + LANES] = _rope(k, cos, sa, sb).astype(ka_ref.dtype)
    o = 2 * A_WIDTH
    va = proj(o, o + A_WIDTH)
    ones = jnp.ones((va.shape[0], A_V), F32)
    va_ref[0] = jnp.concatenate([piece for hh in range(A_HEADS)
                                 for piece in (va[:, hh * A_V:(hh + 1) * A_V], ones)],
                                axis=1).astype(va_ref.dtype)
    o += A_WIDTH
    bg_ref[0] = proj(o, o + BG_WIDTH)
    o += BG_WIDTH
    qn_ref[0] = (proj(o, o + C_WIDTH) * (C_DH ** -0.5)).astype(qn_ref.dtype)
    o += C_WIDTH
    kn_ref[0] = proj(o, o + C_WIDTH).astype(kn_ref.dtype)
    o += C_WIDTH
    vn_ref[0] = proj(o, o + C_WIDTH).astype(vn_ref.dtype)


def _moe_specs(tm, d, row_block):
    return [pl.BlockSpec((None, 1, tm, d // 2), lambda b, i: (0, b, row_block(i), 0)),
            pl.BlockSpec((None, 1, tm, d // 2), lambda b, i: (1, b, row_block(i), 0)),
            pl.BlockSpec((1, tm, ROUTER_PAD), lambda b, i: (b, row_block(i), 0))]


def _in_proj(layer, x, ymoe, route, modsel, g1, w_in_p, rope_tabs):
    bsz, t, d = x.shape
    tm = IN_PROJ_TILE
    combine = ymoe is not None
    tok = lambda b, i: (b, i, 0)
    x_spec = pl.BlockSpec((1, tm, d), tok)

    def mod_spec(l):
        return pl.BlockSpec((None, None, 2, 6, d), lambda b, i: (l, b, 0, 0, 0))

    tab_spec = pl.BlockSpec((tm, LANES), lambda b, i: (i, 0))
    in_specs = [x_spec]
    args = [x]
    if combine:
        in_specs += _moe_specs(tm, d, lambda i: i) + [mod_spec(layer - 1)]
        args += [ymoe, ymoe, route, modsel]
    in_specs += [mod_spec(layer), pl.BlockSpec((1, d), lambda b, i: (0, 0)),
                 pl.BlockSpec((d, IN_PAD_WIDTH), lambda b, i: (0, 0)), tab_spec, tab_spec, tab_spec]
    args += [modsel, g1.reshape(1, d), w_in_p, *rope_tabs]

    def o(width, dtype):
        return pl.BlockSpec((1, tm, width), tok), jax.ShapeDtypeStruct((bsz, t, width), dtype)

    outs = [o(A_WIDTH, MXU_DTYPE), o(A_WIDTH, MXU_DTYPE), o(2 * A_WIDTH, MXU_DTYPE), o(BG_WIDTH, F32),
            o(C_WIDTH, MXU_DTYPE), o(C_WIDTH, MXU_DTYPE), o(C_WIDTH, MXU_DTYPE)]
    if combine:
        outs.append(o(d, F32))
    return pl.pallas_call(
        functools.partial(_in_proj_body, combine, TOK_TILE),
        grid=(bsz, pl.cdiv(t, tm)),
        in_specs=in_specs,
        out_specs=[s for s, _ in outs],
        out_shape=[s for _, s in outs],
        compiler_params=_cparams(("parallel", "arbitrary")),
        name="in_proj",
    )(*args)


def _diff_rows(lam_init, q, k_ref, v_ref, n_keys, lam, g):
    lane = lax.broadcasted_iota(jnp.int32, (1, LANES), 1)
    zero = jnp.zeros_like(q)
    nt = (((1,), (1,)), ((), ()))
    k = k_ref[0, :n_keys, :]
    s1 = lax.dot_general(jnp.where(lane < A_QK, q, zero), k, nt, preferred_element_type=F32)
    s2 = lax.dot_general(jnp.where(lane >= A_QK, q, zero), k, nt, preferred_element_type=F32)

    def finish():
        v1 = v_ref[0, :n_keys, :]
        outs = []
        for s in (s1, s2):
            p = jnp.exp2((s - jnp.max(s, axis=-1, keepdims=True)).astype(MXU_DTYPE))
            outs.append(jnp.dot(p, v1, preferred_element_type=F32))
        o = (outs[0][:, :A_V] * (1.0 / outs[0][:, A_V:A_V + 1])
             - outs[1][:, :A_V] * (lam / outs[1][:, A_V:A_V + 1]))
        return _rms(o, g) * (1.0 - lam_init)

    return finish


def _diff_attn_body(lam_init, ctx_len, qa_ref, qb_ref, k_ref, v_ref, lam_ref, g_ref, oc_ref, ol_ref):
    lm = lam_ref[...]
    lam = (jnp.exp(jnp.sum(lm[0:1] * lm[1:2], axis=1, keepdims=True))
           - jnp.exp(jnp.sum(lm[2:3] * lm[3:4], axis=1, keepdims=True)) + lam_init)
    g = g_ref[...]
    rows = qa_ref.shape[1]

    @pl.when(pl.program_id(2) == 0)
    def _():
        oc_ref[0] = _diff_rows(lam_init, qa_ref[0], k_ref, v_ref, ctx_len, lam, g)().astype(oc_ref.dtype)

    @pl.when(pl.program_id(2) > 0)
    def _():
        n_keys = k_ref.shape[1]
        fin_a = _diff_rows(lam_init, qa_ref[0], k_ref, v_ref, n_keys, lam, g)
        fin_b = _diff_rows(lam_init, qb_ref[0], k_ref, v_ref, n_keys, lam, g)
        ol_ref[0, :rows, :] = fin_a().astype(ol_ref.dtype)
        ol_ref[0, rows:, :] = fin_b().astype(ol_ref.dtype)


def _diff_attention(layer, qa, ka, va, lam, g_sub):
    bsz, t, _ = qa.shape
    tq = TOK_TILE
    n_lat = (t - tq) // (2 * tq)
    assert n_lat * 2 * tq == t - tq
    lam_init = 0.8 - 0.6 * math.exp(-0.3 * layer)

    def q_spec(off):
        return pl.BlockSpec((1, tq, LANES), lambda b, h, i: (b, jnp.maximum(2 * i + off, 0), h))

    return pl.pallas_call(
        functools.partial(_diff_attn_body, lam_init, TOK_TILE),
        grid=(bsz, A_HEADS, 1 + n_lat),
        in_specs=[q_spec(-1), q_spec(0),
                  pl.BlockSpec((1, t, LANES), lambda b, h, i: (b, 0, h)),
                  pl.BlockSpec((1, t, 2 * A_V), lambda b, h, i: (b, 0, h)),
                  pl.BlockSpec((4, A_QK), lambda b, h, i: (0, 0)),
                  pl.BlockSpec((1, A_V), lambda b, h, i: (0, 0))],
        out_specs=[pl.BlockSpec((1, tq, LANES), lambda b, h, i: (b, 0, h)),
                   pl.BlockSpec((1, 2 * tq, LANES), lambda b, h, i: (b, jnp.maximum(i - 1, 0), h))],
        out_shape=[jax.ShapeDtypeStruct((bsz, tq, A_WIDTH), MXU_DTYPE),
                   jax.ShapeDtypeStruct((bsz, t - tq, A_WIDTH), MXU_DTYPE)],
        compiler_params=_cparams(("parallel", "parallel", "arbitrary")),
        name="diff_attention",
    )(qa, qa, ka, va, lam, g_sub.reshape(1, A_V))


def _gla_body(f_ref, r_ref, wdec_ref, bdec_ref, of_ref, ob_ref, sf_ref, sb_ref, es_ref, bs_ref, qss_ref):
    c = B_CHUNK
    n_chunks = TOK_TILE // c

    @pl.when(pl.program_id(1) == 0)
    def _():
        sf_ref[...] = jnp.zeros_like(sf_ref)
        sb_ref[...] = jnp.zeros_like(sb_ref)

    sub = GLA_SUB
    n_sub = c // sub
    nt = (((1,), (1,)), ((), ()))
    t_row = lax.broadcasted_iota(jnp.int32, (TOK_TILE, TOK_TILE), 0)
    t_col = lax.broadcasted_iota(jnp.int32, (TOK_TILE, TOK_TILE), 1)
    same_chunk = (t_row // c) == (t_col // c)
    tri_f = (same_chunk & (t_col <= t_row)).astype(jnp.bfloat16)
    tri_b = (same_chunk & (t_col >= t_row)).astype(jnp.bfloat16)
    s_iota = lax.broadcasted_iota(jnp.int32, (sub, LANES), 0)
    idx = lax.broadcasted_iota(jnp.int32, (c, LANES), 0)
    head_of_k = lax.broadcasted_iota(jnp.int32, (B_QK, B_WIDTH), 0) // B_DK
    head_of_v = lax.broadcasted_iota(jnp.int32, (B_QK, B_WIDTH), 1) // B_DV
    expand = (head_of_k == head_of_v).astype(MXU_DTYPE)
    same_head_t = (lax.broadcasted_iota(jnp.int32, (B_WIDTH, B_QK), 0) // B_DV
                   == lax.broadcasted_iota(jnp.int32, (B_WIDTH, B_QK), 1) // B_DK)
    n_ref = n_sub - 1
    kt_keep = (lax.broadcasted_iota(jnp.int32, (B_HEADS * c, n_ref * B_QK), 0) // c
               == (lax.broadcasted_iota(jnp.int32, (B_HEADS * c, n_ref * B_QK), 1) % B_QK) // B_DK)
    vx_keep = (lax.broadcasted_iota(jnp.int32, (B_HEADS * c, B_WIDTH), 0) // c
               == lax.broadcasted_iota(jnp.int32, (B_HEADS * c, B_WIDTH), 1) // B_DV)
    pick = (lax.broadcasted_iota(jnp.int32, (c, c * sub), 1) // sub
            == lax.broadcasted_iota(jnp.int32, (c, c * sub), 0)).astype(MXU_DTYPE)

    def log_decay(src_ref, backward):
        gl = src_ref[0, :, 2 * B_QK + 2 * B_WIDTH:BG_WIDTH]
        d0 = B_QK if backward else 0
        z = jnp.dot(gl, wdec_ref[:, d0:d0 + B_QK], precision=HI, preferred_element_type=F32) \
            + bdec_ref[:, d0:d0 + B_QK]
        log_a = (jnp.minimum(z, 0.0) - jnp.log(1.0 + jnp.exp(-jnp.abs(z)))) / B_GATE_TAU
        tri = tri_b if backward else tri_f
        return sum(jnp.dot(tri, p, preferred_element_type=F32) for p in _split_bf16(log_a, 3))

    def chunk(slot, src_ref, b_all, lo, backward, st_ref, out_ref):
        q = src_ref[0, lo:lo + c, 0:B_QK] * (B_DK ** -0.5)
        k = src_ref[0, lo:lo + c, B_QK:2 * B_QK]
        v = src_ref[0, lo:lo + c, 2 * B_QK:2 * B_QK + B_WIDTH]
        b = b_all[lo:lo + c]
        b_ref, qs_ref, e_ref = bs_ref.at[slot], qss_ref.at[slot], es_ref.at[slot]
        b_ref[...] = b
        qs_ref[...] = q
        blk = ((c - 1 - idx) if backward else idx) // sub

        q_parts, k_parts = [], []
        for m, late, early in ((1, blk == 1, blk == 0), (2, blk >= 2, blk <= 1), (3, blk == 3, blk == 2)):
            r_row = (c - 1 - sub * m) if backward else sub * m
            r = b_ref[r_row:r_row + 1, :]
            q_parts.append(q * jnp.exp(jnp.where(late, b - r, -jnp.inf)))
            k_parts.append(k * jnp.exp(jnp.where(early, r - b, -jnp.inf)))
        q_cat = jnp.concatenate(q_parts, axis=1).astype(MXU_DTYPE)
        k_cat = jnp.concatenate(k_parts, axis=1)
        k_exp = jnp.where(kt_keep, jnp.concatenate([k_cat] * B_HEADS, axis=0), 0.0).astype(MXU_DTYPE)
        a_off = lax.dot_general(q_cat, k_exp, nt, preferred_element_type=F32)

        for tt in range(c):
            lo_s = tt // sub * sub
            keep = (s_iota >= tt - lo_s) if backward else (s_iota <= tt - lo_s)
            bt = b_ref[tt:tt + 1, :]
            qt = qs_ref[tt:tt + 1, :]
            e = jnp.exp(jnp.where(keep, bt - b[lo_s:lo_s + sub], -jnp.inf)) * (qt * k[lo_s:lo_s + sub])
            e_ref[tt * sub:(tt + 1) * sub, :] = e.astype(e_ref.dtype)
        a_exp = jnp.dot(e_ref[...], expand, preferred_element_type=F32)
        b_end = b[0:1, :] if backward else b[c - 1:c, :]
        kd = k * jnp.exp(b_end - b)
        upd = lax.dot_general(v, kd, (((0,), (0,)), ((), ())), preferred_element_type=F32)

        def intra():
            v_exp = jnp.where(vx_keep, jnp.concatenate([v] * B_HEADS, axis=0), 0.0).astype(MXU_DTYPE)
            o_off = jnp.dot(a_off.astype(MXU_DTYPE), v_exp, preferred_element_type=F32)
            prod = a_exp.reshape(n_sub, sub, sub, B_WIDTH) * v.reshape(n_sub, 1, sub, B_WIDTH)
            o_diag = jnp.dot(pick, prod.reshape(c * sub, B_WIDTH).astype(MXU_DTYPE),
                             preferred_element_type=F32)
            o_intra = o_off + o_diag

            def recur():
                st = st_ref[...]
                o_inter = lax.dot_general(q * jnp.exp(b), st, nt, preferred_element_type=F32)
                out_ref[0, lo:lo + c, :] = o_intra + o_inter
                st_ref[...] = jnp.exp(b_end) * st + jnp.where(same_head_t, upd, 0.0)

            return recur

        return intra

    b_fwd = log_decay(f_ref, False)
    b_bwd = log_decay(r_ref, True)
    stage = []
    for ci in range(n_chunks):
        stage.append(chunk(2 * ci, f_ref, b_fwd, ci * c, False, sf_ref, of_ref))
        stage.append(chunk(2 * ci + 1, r_ref, b_bwd, (n_chunks - 1 - ci) * c, True, sb_ref, ob_ref))
    stage = [intra() for intra in stage]
    for recur in stage:
        recur()


def _gla_scan(bg, w_dec, b_dec):
    bsz, t, _ = bg.shape
    n = t // TOK_TILE
    rev = lambda b, i: (b, jnp.where(i == 0, 0, n - i), 0)
    fwd = lambda b, i: (b, i, 0)
    wdec = jnp.zeros((GATE_PAD, 2 * B_QK), F32)
    wdec = wdec.at[:B_GATE_RANK, :B_QK].set(w_dec[0]).at[B_GATE_RANK:2 * B_GATE_RANK, B_QK:].set(w_dec[1])
    bdec = b_dec.reshape(1, 2 * B_QK)
    o_shape = jax.ShapeDtypeStruct((bsz, t, B_WIDTH), F32)
    n_slots = 2 * (TOK_TILE // B_CHUNK)
    return pl.pallas_call(
        _gla_body,
        grid=(bsz, n),
        in_specs=[pl.BlockSpec((1, TOK_TILE, BG_WIDTH), fwd),
                  pl.BlockSpec((1, TOK_TILE, BG_WIDTH), rev),
                  pl.BlockSpec((GATE_PAD, 2 * B_QK), lambda b, i: (0, 0)),
                  pl.BlockSpec((1, 2 * B_QK), lambda b, i: (0, 0))],
        out_specs=[pl.BlockSpec((1, TOK_TILE, B_WIDTH), fwd),
                   pl.BlockSpec((1, TOK_TILE, B_WIDTH), rev)],
        out_shape=[o_shape, o_shape],
        scratch_shapes=[pltpu.VMEM((B_WIDTH, B_QK), F32), pltpu.VMEM((B_WIDTH, B_QK), F32),
                        pltpu.VMEM((n_slots, B_CHUNK * GLA_SUB, LANES), MXU_DTYPE),
                        pltpu.VMEM((n_slots, B_CHUNK, LANES), F32),
                        pltpu.VMEM((n_slots, B_CHUNK, LANES), F32)],
        compiler_params=_cparams(("parallel", "arbitrary")),
        name="gla_scan",
    )(bg, bg, wdec, bdec)


def _na_bias_tables(rpb, rows):
    n_tiles = rows // NA_TILE_ROWS
    wr = min(NA_ROWS, rows)
    n_dr, n_dc = 2 * NA_ROWS - 1, 2 * NA_COLS - 1
    cq = np.arange(GRID_W)[:, None]
    ck = np.arange(GRID_W)[None, :]
    cs = np.clip(cq - NA_COLS // 2, 0, GRID_W - NA_COLS)
    col_ok = (ck >= cs) & (ck < cs + NA_COLS)
    dc = np.clip(ck - cq, -(NA_COLS - 1), NA_COLS - 1) + (NA_COLS - 1)
    onehot = (dc.reshape(1, -1) == np.arange(n_dc)[:, None]).astype(np.float32)
    by_col = jnp.dot(rpb.astype(F32).reshape(-1, n_dc), onehot, precision=HI)
    by_col = jnp.where(col_ok.reshape(1, 1, GRID_W, GRID_W),
                       by_col.reshape(C_HEADS, n_dr, GRID_W, GRID_W), -jnp.inf)
    masked = jnp.full((C_HEADS, GRID_W, GRID_W), -jnp.inf, F32)
    tabs = []
    for j in (0, 1, n_tiles - 1):
        kr0 = int(np.clip(j - 1, 0, n_tiles - NA_KEY_TILES)) * NA_TILE_ROWS
        q_rows = []
        for qr in range(NA_TILE_ROWS):
            r = j * NA_TILE_ROWS + qr
            start = int(np.clip(r - wr // 2, 0, rows - wr))
            blocks = []
            for kw in range(NA_KEY_TILES * NA_TILE_ROWS):
                kr = kr0 + kw
                blocks.append(by_col[:, kr - r + NA_ROWS - 1] if start <= kr < start + wr else masked)
            q_rows.append(jnp.concatenate(blocks, axis=-1))
        tabs.append(jnp.concatenate(q_rows, axis=1))
    return jnp.stack(tabs)


def _na_body(q_ref, k0_ref, k1_ref, k2_ref, kc_ref, v0_ref, v1_ref, v2_ref, vc_ref, m_ref, o_ref):
    q = q_ref[0]
    lane = lax.broadcasted_iota(jnp.int32, (1, C_WIDTH), 1)
    zero = jnp.zeros_like(q)
    nt = (((1,), (1,)), ((), ()))

    def scores(qm, k_ref):
        return lax.dot_general(qm, k_ref[0], nt, preferred_element_type=F32)

    def head_out(hh, windows):
        qm = jnp.where((lane >= hh * C_DH) & (lane < (hh + 1) * C_DH), q, zero)
        s = [scores(qm, kc_ref)]
        for w, k_ref in enumerate(windows):
            s.append(scores(qm, k_ref) + m_ref[0, hh, :, w * TOK_TILE:(w + 1) * TOK_TILE])

        def finish():
            m = functools.reduce(jnp.maximum, [jnp.max(x, axis=-1, keepdims=True) for x in s])
            p = [jnp.exp(x - m) for x in s]
            den = functools.reduce(jnp.add, [jnp.sum(x, axis=-1, keepdims=True) for x in p])
            vals = [vc_ref] + [v0_ref, v1_ref, v2_ref][:len(windows)]
            o = functools.reduce(jnp.add, [jnp.dot(x.astype(MXU_DTYPE), v_ref[0], preferred_element_type=F32)
                                           for x, v_ref in zip(p, vals)])
            return o * (1.0 / den)

        return finish

    def emit(windows):
        finish = [head_out(hh, windows) for hh in range(C_HEADS)]
        o = finish[C_HEADS - 1]()
        for hh in range(C_HEADS - 2, -1, -1):
            o = jnp.where(lane < (hh + 1) * C_DH, finish[hh](), o)
        o_ref[0] = o.astype(o_ref.dtype)

    @pl.when(pl.program_id(0) == 0)
    def _():
        emit([])

    @pl.when(pl.program_id(0) > 0)
    def _():
        emit([k0_ref, k1_ref, k2_ref])


def _neighborhood_attention(qn, kn, vn, bias_tabs):
    bsz, t, _ = qn.shape
    n = t // TOK_TILE
    n_lat = n - 1

    def win(w):
        def index(i, b):
            kb0 = jnp.clip(i - 2, 0, n_lat - NA_KEY_TILES)
            return (b, kb0 + 1 + w, 0)
        return pl.BlockSpec((1, TOK_TILE, C_WIDTH), index)

    def cls(i, b):
        j = i - 1
        return (jnp.where(j <= 0, 0, jnp.where(j == n_lat - 1, 2, 1)), 0, 0, 0)

    own = pl.BlockSpec((1, TOK_TILE, C_WIDTH), lambda i, b: (b, i, 0))
    ctx = pl.BlockSpec((1, TOK_TILE, C_WIDTH), lambda i, b: (b, 0, 0))
    return pl.pallas_call(
        _na_body,
        grid=(n, bsz),
        in_specs=[own, win(0), win(1), win(2), ctx, win(0), win(1), win(2), ctx,
                  pl.BlockSpec((1, C_HEADS, TOK_TILE, NA_KEY_TILES * TOK_TILE), cls)],
        out_specs=own,
        out_shape=jax.ShapeDtypeStruct((bsz, t, C_WIDTH), MXU_DTYPE),
        compiler_params=_cparams(("parallel", "arbitrary")),
        name="neighborhood_attention",
    )(qn, kn, kn, kn, kn, vn, vn, vn, vn, bias_tabs)


def _out_proj_body(yac_ref, yal_ref, of_ref, ob_ref, r_ref, yn_ref, x_ref, mod_ref, gg_ref, g2_ref, w_ref,
                   wr_ref, br_ref, xo_ref, h_ref, rt_ref, cnt_ref, count_ref):
    is_ctx_tile = pl.program_id(1) == 0
    hi = lax.broadcasted_iota(jnp.int32, (B_WIDTH, B_WIDTH), 0) // B_DV
    hj = lax.broadcasted_iota(jnp.int32, (B_WIDTH, B_WIDTH), 1) // B_DV
    head_mean = jnp.where(hi == hj, 1.0 / B_DV, 0.0).astype(jnp.bfloat16)

    @pl.when((pl.program_id(0) == 0) & (pl.program_id(1) == 0))
    def _():
        count_ref[...] = jnp.zeros_like(count_ref)

    def rows(r0, r1):
        ya = jnp.where(is_ctx_tile, yac_ref[0, r0:r1, :], yal_ref[0, r0:r1, :])
        o = of_ref[0, r0:r1, :] + ob_ref[0, r0:r1, :]
        ms = sum(jnp.dot(p, head_mean, preferred_element_type=F32) for p in _split_bf16(o * o, 2))
        yb = o * lax.rsqrt(ms + NORM_EPS) * gg_ref[...] * _silu(r_ref[0, r0:r1, :])

        def project():
            mix = (jnp.dot(ya, w_ref[0:A_WIDTH, :], preferred_element_type=F32)
                   + jnp.dot(yb.astype(MXU_DTYPE), w_ref[A_WIDTH:A_WIDTH + B_WIDTH, :],
                             preferred_element_type=F32)
                   + jnp.dot(yn_ref[0, r0:r1, :], w_ref[A_WIDTH + B_WIDTH:, :], preferred_element_type=F32))
            x = x_ref[0, r0:r1, :] + mod_ref[2:3, :] * mix
            xo_ref[0, r0:r1, :] = x
            h = _rms(x, g2_ref[...]) * (1.0 + mod_ref[4:5, :]) + mod_ref[3:4, :]
            h_ref[0, r0:r1, :] = _pack_bf16_pairs(h)
            h_hi, h_lo = _split_bf16(h, 2)
            both = jnp.dot(h_hi, wr_ref[...], preferred_element_type=F32)
            logits = (both[:, :ROUTER_PAD] + both[:, ROUTER_PAD:]
                      + jnp.dot(h_lo, wr_ref[:, :ROUTER_PAD], preferred_element_type=F32) + br_ref[...])

            def route():
                rt_ref[0, r0:r1, :] = _route(logits, count_ref)

            return route

        return project

    half = x_ref.shape[1] // 2
    stages = [rows(0, half), rows(half, 2 * half)]
    stages = [project() for project in stages]
    for route in stages:
        route()
    cnt_ref[...] = count_ref[...]


def _route(lg, count_ref):
    lane = lax.broadcasted_iota(jnp.int32, lg.shape, 1)
    big = jnp.int32(ROUTER_PAD)

    def top(mask):
        v = jnp.max(jnp.where(mask, lg, -jnp.inf), axis=-1, keepdims=True)
        i = jnp.min(jnp.where(mask & (lg == v), lane, big), axis=-1, keepdims=True)
        return v, i

    g_mask = lane < N_GROUPS
    g_max, grp = top(g_mask)
    p_grp = 1.0 / jnp.sum(jnp.where(g_mask, jnp.exp(lg - g_max), 0.0), axis=-1, keepdims=True)
    e_lo = N_GROUPS + grp * EXPERTS_PER_GROUP
    e_mask = (lane >= e_lo) & (lane < e_lo + EXPERTS_PER_GROUP)
    v1, i1 = top(e_mask)
    v2, i2 = top(e_mask & (lane != i1))
    r = jnp.exp(v2 - v1)
    gate1 = p_grp / (1.0 + r)
    gate2 = p_grp * r / (1.0 + r)
    e1, e2 = i1 - N_GROUPS, i2 - N_GROUPS
    hot1 = (lane == e1).astype(F32)
    hot2 = (lane == e2).astype(F32)
    both = hot1 + hot2
    rows = lg.shape[0]
    earlier = (lax.broadcasted_iota(jnp.int32, (rows, rows), 1)
               < lax.broadcasted_iota(jnp.int32, (rows, rows), 0)).astype(jnp.bfloat16)
    before = jnp.dot(earlier, both.astype(jnp.bfloat16), preferred_element_type=F32) + count_ref[0:1, :]
    rank1 = jnp.sum(hot1 * before, axis=-1, keepdims=True)
    rank2 = jnp.sum(hot2 * before, axis=-1, keepdims=True)
    count_ref[0:1, :] = count_ref[0:1, :] + jnp.sum(both, axis=0, keepdims=True)
    out = jnp.where(lane == ROUTE_EXPERT, e1.astype(F32),
                    jnp.where(lane == ROUTE_EXPERT + 1, e2.astype(F32),
                              jnp.where(lane == ROUTE_GATE, gate1,
                                        jnp.where(lane == ROUTE_GATE + 1, gate2,
                                                  jnp.where(lane == ROUTE_RANK, rank1,
                                                            jnp.where(lane == ROUTE_RANK + 1, rank2, 0.0))))))
    return out


def _out_proj(layer, ya_ctx, ya_lat, o_f, o_b, bg, yn, x, modsel, g_gla, g2, w_out_b, w_router, b_router):
    bsz, t, d = x.shape
    tm = TOK_TILE
    tok = lambda b, i: (b, i, 0)
    const = lambda b, i: (0, 0)
    r_block = (2 * B_QK + B_WIDTH) // B_WIDTH
    return pl.pallas_call(
        _out_proj_body,
        grid=(bsz, t // tm),
        in_specs=[pl.BlockSpec((1, tm, A_WIDTH), lambda b, i: (b, 0, 0)),
                  pl.BlockSpec((1, tm, A_WIDTH), lambda b, i: (b, jnp.maximum(i - 1, 0), 0)),
                  pl.BlockSpec((1, tm, B_WIDTH), tok), pl.BlockSpec((1, tm, B_WIDTH), tok),
                  pl.BlockSpec((1, tm, B_WIDTH), lambda b, i: (b, i, r_block)),
                  pl.BlockSpec((1, tm, C_WIDTH), tok),
                  pl.BlockSpec((1, tm, d), tok),
                  pl.BlockSpec((None, None, None, 6, d), lambda b, i: (layer, b, jnp.minimum(i, 1), 0, 0)),
                  pl.BlockSpec((1, B_WIDTH), const), pl.BlockSpec((1, d), const),
                  pl.BlockSpec(w_out_b.shape, const),
                  pl.BlockSpec((d, 2 * ROUTER_PAD), const), pl.BlockSpec((1, ROUTER_PAD), const)],
        out_specs=[pl.BlockSpec((1, tm, d), tok), pl.BlockSpec((1, tm, d // 2), tok),
                   pl.BlockSpec((1, tm, ROUTER_PAD), tok), pl.BlockSpec((8, ROUTER_PAD), const)],
        out_shape=[jax.ShapeDtypeStruct((bsz, t, d), F32), jax.ShapeDtypeStruct((bsz, t, d // 2), jnp.uint32),
                   jax.ShapeDtypeStruct((bsz, t, ROUTER_PAD), F32), jax.ShapeDtypeStruct((8, ROUTER_PAD), F32)],
        scratch_shapes=[pltpu.VMEM((8, ROUTER_PAD), F32)],
        compiler_params=_cparams(("arbitrary", "arbitrary")),
        name="out_proj",
    )(ya_ctx, ya_lat, o_f, o_b, bg, yn, x, modsel, jnp.tile(g_gla, B_HEADS).reshape(1, B_WIDTH), g2.reshape(1, d),
      w_out_b, w_router, b_router)


def _dispatch_plan(expert, rank, counts, n_blocks):
    padded = (counts + EXPERT_TILE - 1) // EXPERT_TILE * EXPERT_TILE
    pad_end = jnp.cumsum(padded)
    pad_start = (pad_end - padded).astype(jnp.int32)
    lanes = jnp.arange(N_EXPERTS, dtype=jnp.int32)
    dest = rank + jnp.sum(jnp.where(expert[..., None] == lanes, pad_start, 0), axis=-1)
    blk_start = jnp.arange(n_blocks, dtype=jnp.int32) * EXPERT_TILE
    blk_expert = jnp.sum((pad_end[None, :] <= blk_start[:, None]).astype(jnp.int32), axis=1)
    blk_expert = jnp.minimum(blk_expert, N_EXPERTS - 1)
    blk_valid = jnp.clip(pad_start[blk_expert] + counts[blk_expert] - blk_start, 0, EXPERT_TILE)
    return dest.astype(jnp.int32), blk_expert.astype(jnp.int32), blk_valid.astype(jnp.int32)


def _sc_dispatch(table, dest, n_rows):
    n_tok, width = table.shape
    n_workers = SC_CORES * SC_SUBCORES
    per_worker = n_tok // n_workers
    chunk = SC_ROW_BUFFER_BYTES // (2 * width * table.dtype.itemsize)
    n_chunks = per_worker // chunk
    assert per_worker * n_workers == n_tok and n_chunks * chunk == per_worker and n_chunks % 2 == 0
    assert n_chunks >= 4 and chunk <= LANES and dest.shape == (n_tok, TOP_K)
    mesh = plsc.VectorSubcoreMesh(core_axis_name="core", subcore_axis_name="subcore")

    def body(table_hbm, d0_hbm, d1_hbm, out_hbm, i0_v, i1_v, rows_v, read_sem, put_sem):
        worker = lax.axis_index("subcore") * SC_CORES + lax.axis_index("core")
        base = worker * per_worker
        pltpu.sync_copy(d0_hbm.at[worker], i0_v)
        pltpu.sync_copy(d1_hbm.at[worker], i1_v)

        def read(j, slot):
            return pltpu.make_async_copy(table_hbm.at[pl.ds(base + j * chunk, chunk)], rows_v.at[slot],
                                         read_sem.at[slot])

        def put(j, slot, idx_v, k):
            return pltpu.make_async_copy(rows_v.at[slot], out_hbm.at[idx_v.at[j]], put_sem.at[slot, k])

        def drain(j, slot):
            read(j, slot).wait()
            put(j, slot, i0_v, 0).start()
            put(j, slot, i1_v, 1).start()
            put(j, slot, i0_v, 0).wait()
            put(j, slot, i1_v, 1).wait()

        read(0, 0).start()
        read(1, 1).start()

        @pl.loop(0, n_chunks - 2, step=2)
        def _(j):
            for slot in range(2):
                drain(j + slot, slot)
                read(j + slot + 2, slot).start()

        drain(n_chunks - 2, 0)
        drain(n_chunks - 1, 1)

    idx = dest.reshape(n_workers, n_chunks, chunk, TOP_K)
    return pl.kernel(
        body,
        out_type=jax.ShapeDtypeStruct((n_rows, width), table.dtype),
        mesh=mesh,
        scratch_types=[pltpu.VMEM((n_chunks, chunk), jnp.int32), pltpu.VMEM((n_chunks, chunk), jnp.int32),
                       pltpu.VMEM((2, chunk, width), table.dtype),
                       pltpu.SemaphoreType.DMA((2,)), pltpu.SemaphoreType.DMA((2, TOP_K))],
        name="sc_row_dispatch",
    )(table, idx[..., 0], idx[..., 1])


def _sc_gather(table, idx):
    n_rows = idx.shape[0]
    width = table.shape[1]
    n_workers = SC_CORES * SC_SUBCORES
    per_worker = n_rows // n_workers
    chunk = SC_ROW_BUFFER_BYTES // (width * table.dtype.itemsize)
    n_chunks = per_worker // chunk
    assert per_worker * n_workers == n_rows and n_chunks * chunk == per_worker and n_chunks % 2 == 0
    assert n_chunks >= 4 and chunk <= LANES
    mesh = plsc.VectorSubcoreMesh(core_axis_name="core", subcore_axis_name="subcore")

    def body(table_hbm, idx_hbm, out_hbm, idx_v, rows_v, gather_sem, write_sem):
        worker = lax.axis_index("subcore") * SC_CORES + lax.axis_index("core")
        base = worker * per_worker
        pltpu.sync_copy(idx_hbm.at[worker], idx_v)

        def gather(j, slot):
            return pltpu.make_async_copy(table_hbm.at[idx_v.at[j]], rows_v.at[slot], gather_sem.at[slot])

        def write(j, slot):
            return pltpu.make_async_copy(rows_v.at[slot], out_hbm.at[pl.ds(base + j * chunk, chunk)],
                                         write_sem.at[slot])

        gather(0, 0).start()
        gather(0, 0).wait()
        gather(1, 1).start()
        write(0, 0).start()

        @pl.loop(1, n_chunks - 1, step=2)
        def _(j):
            for s in range(2):
                slot = (1 + s) % 2
                gather(j + s, slot).wait()
                write(j + s - 1, 1 - slot).wait()
                gather(j + s + 1, 1 - slot).start()
                write(j + s, slot).start()

        last = n_chunks - 1
        gather(last, 1).wait()
        write(last, 1).start()
        write(last - 1, 0).wait()
        write(last, 1).wait()

    return pl.kernel(
        body,
        out_type=jax.ShapeDtypeStruct((n_rows, width), table.dtype),
        mesh=mesh,
        scratch_types=[pltpu.VMEM((n_chunks, chunk), jnp.int32),
                       pltpu.VMEM((2, chunk, width), table.dtype),
                       pltpu.SemaphoreType.DMA((2,)), pltpu.SemaphoreType.DMA((2,))],
        name="sc_row_gather",
    )(table, idx.reshape(n_workers, n_chunks, chunk))


def _expert_body(be_ref, used_ref, x_ref, wu_ref, wd_ref, o_ref, wub_ref, wdb_ref):
    i = pl.program_id(0)
    prev = be_ref[jnp.maximum(i - 1, 0)]

    @pl.when((i == 0) | (be_ref[i] != prev))
    def _():
        wub_ref[...] = wu_ref[...].astype(wub_ref.dtype)
        wdb_ref[...] = wd_ref[...].astype(wdb_ref.dtype)

    @pl.when(used_ref[i] > 0)
    def _():
        def rows(r0, r1):
            real = (lax.broadcasted_iota(jnp.int32, (r1 - r0, x_ref.shape[1]), 0) + r0) < used_ref[i]
            lo, hi = _unpack_bf16_pairs(jnp.where(real, x_ref[r0:r1, :], jnp.uint32(0)))
            half = lo.shape[1]
            gu = (jnp.dot(lo.astype(MXU_DTYPE), wub_ref[:half, :], preferred_element_type=F32)
                  + jnp.dot(hi.astype(MXU_DTYPE), wub_ref[half:, :], preferred_element_type=F32))

            def down():
                act = _silu(gu[:, :EXPERT_HIDDEN]) * gu[:, EXPERT_HIDDEN:]
                o_ref[r0:r1, :] = _pack_bf16_pairs(
                    jnp.dot(act.astype(MXU_DTYPE), wdb_ref[...], preferred_element_type=F32))

            return down

        group = EXPERT_TILE // 2
        stages = [rows(0, group), rows(group, EXPERT_TILE)]
        for down in stages:
            down()

    @pl.when(used_ref[i] == 0)
    def _():
        o_ref[...] = jnp.zeros_like(o_ref)


def _expert_ffn(layer, buf, blk_expert, blk_used, w_up, w_down):
    n_rows, packed = buf.shape
    d = 2 * packed
    n_blocks = n_rows // EXPERT_TILE
    h2 = w_up.shape[-1]
    grid_spec = pltpu.PrefetchScalarGridSpec(
        num_scalar_prefetch=2,
        grid=(n_blocks,),
        in_specs=[pl.BlockSpec((EXPERT_TILE, packed), lambda i, be, us: (i, 0)),
                  pl.BlockSpec((None, None, d, h2), lambda i, be, us: (layer, be[i], 0, 0)),
                  pl.BlockSpec((None, None, h2 // 2, d), lambda i, be, us: (layer, be[i], 0, 0))],
        out_specs=pl.BlockSpec((EXPERT_TILE, packed), lambda i, be, us: (i, 0)),
        scratch_shapes=[pltpu.VMEM((d, h2), MXU_DTYPE), pltpu.VMEM((h2 // 2, d), MXU_DTYPE)],
    )
    return pl.pallas_call(
        _expert_body,
        grid_spec=grid_spec,
        out_shape=jax.ShapeDtypeStruct((n_rows, packed), jnp.uint32),
        compiler_params=_cparams(("arbitrary",)),
        name="expert_ffn",
    )(blk_expert, blk_used, buf, w_up, w_down)


def _moe(layer, h2, route, counts, w_up, w_down):
    bsz, t, packed = h2.shape
    n_tok = bsz * t
    n_assign = n_tok * TOP_K
    n_blocks = -(-(n_assign + N_EXPERTS * (EXPERT_TILE - 1)) // EXPERT_TILE)
    record = route.reshape(n_tok, ROUTER_PAD)
    expert = record[:, ROUTE_EXPERT:ROUTE_EXPERT + TOP_K].astype(jnp.int32)
    rank = record[:, ROUTE_RANK:ROUTE_RANK + TOP_K].astype(jnp.int32)
    dest, blk_expert, blk_valid = _dispatch_plan(expert, rank, counts[0, :N_EXPERTS].astype(jnp.int32), n_blocks)
    buf = _sc_dispatch(h2.reshape(n_tok, packed), dest, n_blocks * EXPERT_TILE)
    y = _expert_ffn(layer, buf, blk_expert, blk_valid, w_up, w_down)
    return _sc_gather(y, dest.T.reshape(-1)).reshape(TOP_K, bsz, t, packed)


def _final_body(x_ref, y0_ref, y1_ref, rt_ref, mod_ref, g_ref, o_ref):
    o_ref[0] = _rms(_moe_residual(x_ref, y0_ref, y1_ref, rt_ref, mod_ref[5:6, :]), g_ref[...])


def _final_norm(layer, x, ymoe, route, modsel, g, ctx_tiles):
    bsz, t, d = x.shape
    tm = TOK_TILE
    lat = lambda b, i: (b, i + ctx_tiles, 0)
    return pl.pallas_call(
        _final_body,
        grid=(bsz, t // tm - ctx_tiles),
        in_specs=[pl.BlockSpec((1, tm, d), lat)] + _moe_specs(tm, d, lambda i: i + ctx_tiles)
        + [pl.BlockSpec((None, None, None, 6, d), lambda b, i: (layer, b, 1, 0, 0)),
           pl.BlockSpec((1, d), lambda b, i: (0, 0))],
        out_specs=pl.BlockSpec((1, tm, d), lambda b, i: (b, i, 0)),
        out_shape=jax.ShapeDtypeStruct((bsz, t - ctx_tiles * tm, d), F32),
        compiler_params=_cparams(("parallel", "arbitrary")),
        name="final_norm",
    )(x, ymoe, ymoe, route, modsel, g.reshape(1, d))


def _rope_tables(n_ctx, n_lat):
    t = jnp.arange(n_lat)
    row = (t // GRID_W).astype(F32)
    col = (t % GRID_W).astype(F32)
    n_freq = HEAD_DIM // 4
    inv = ROPE_THETA ** (-jnp.arange(n_freq, dtype=F32) / n_freq)
    ang_r = row[:, None] * inv
    ang_c = col[:, None] * inv
    cr, sr, cc, sc = jnp.cos(ang_r), jnp.sin(ang_r), jnp.cos(ang_c), jnp.sin(ang_c)
    z = jnp.zeros_like(sr)
    cos = jnp.concatenate([cr, cr, cc, cc], axis=-1)
    above = jnp.concatenate([-sr, z, -sc, z], axis=-1)
    below = jnp.concatenate([z, sr, z, sc], axis=-1)
    reps = LANES // HEAD_DIM

    def full(tab, ctx_value):
        tab = jnp.tile(tab, (1, reps))
        return jnp.concatenate([jnp.full((n_ctx, LANES), ctx_value, F32), tab], axis=0)

    return full(cos, 1.0), full(above, 0.0), full(below, 0.0)


def _pack_w_in(w_in):
    parts = jnp.split(w_in, np.cumsum(IN_SIZES)[:-1].tolist(), axis=-1)
    qa, ka, va, qb, kb, vb, rb, gb, qn, kn, vn = parts
    gb = jnp.pad(gb, ((0, 0), (0, GATE_PAD - gb.shape[-1])))
    return jnp.concatenate([qa, ka, va, qb, kb, vb, rb, gb, qn, kn, vn], axis=-1).astype(MXU_DTYPE)


def kernel(x, c, ctx, c_ctx, w_mod, b_mod, norm1_g, norm2_g, w_in, w_out, diff_lambda, diff_sub_g,
           gla_w_decay, gla_b_decay, gla_norm_g, na_rel_bias, w_router_group, b_router_group,
           w_router_expert, b_router_expert, w_expert_up, w_expert_down, final_g):
    bsz, seq, d = x.shape
    n_ctx = ctx.shape[1]
    depth = w_mod.shape[0]
    assert n_ctx == TOK_TILE and seq % (NA_TILE_ROWS * GRID_W) == 0 and d % LANES == 0
    assert seq // (NA_TILE_ROWS * GRID_W) >= NA_KEY_TILES
    mod_rows = -(-(bsz + 1) // 8) * 8
    cvec = jnp.zeros((mod_rows, d), F32).at[:bsz].set(c).at[bsz].set(c_ctx)
    mod = _modulation(cvec, w_mod, b_mod).reshape(depth, mod_rows, 6, d)
    modsel = jnp.stack([jnp.broadcast_to(mod[:, bsz][:, None], (depth, bsz, 6, d)), mod[:, :bsz]], axis=2)
    rope_tabs = _rope_tables(n_ctx, seq)
    xt = jnp.concatenate([ctx, x], axis=1)
    ymoe = route = None
    for l in range(depth):
        w_in_p = _pack_w_in(w_in[l])
        bias_tabs = _na_bias_tables(na_rel_bias[l], seq // GRID_W)
        w_router = jnp.pad(jnp.concatenate([w_router_group[l], w_router_expert[l]], axis=-1),
                           ((0, 0), (0, ROUTER_PAD - N_GROUPS - N_EXPERTS)))
        w_router = jnp.concatenate(_split_bf16(w_router, 2), axis=-1)
        b_router = jnp.pad(jnp.concatenate([b_router_group[l], b_router_expert[l]]),
                           (0, ROUTER_PAD - N_GROUPS - N_EXPERTS)).reshape(1, ROUTER_PAD)
        w_out_b = w_out[l].astype(MXU_DTYPE)
        outs = _in_proj(l, xt, ymoe, route, modsel, norm1_g[l], w_in_p, rope_tabs)
        qa, ka, va, bg, qn, kn, vn = outs[:7]
        if ymoe is not None:
            xt = outs[7]
        ya_ctx, ya_lat = _diff_attention(l, qa, ka, va, diff_lambda[l], diff_sub_g[l])
        o_f, o_b = _gla_scan(bg, gla_w_decay[l], gla_b_decay[l])
        yn = _neighborhood_attention(qn, kn, vn, bias_tabs)
        xt, h2, route, counts = _out_proj(l, ya_ctx, ya_lat, o_f, o_b, bg, yn, xt, modsel, gla_norm_g[l],
                                          norm2_g[l], w_out_b, w_router, b_router)
        ymoe = _moe(l, h2, route, counts, w_expert_up, w_expert_down)
    return _final_norm(depth - 1, xt, ymoe, route, modsel, final_g, n_ctx // TOK_TILE)
```

```python
import functools
import math

import numpy as np
import jax
import jax.numpy as jnp
from jax import lax
from jax.experimental import pallas as pl
from jax.experimental.pallas import tpu as pltpu
from jax.experimental.pallas import tpu_sc as plsc

F32 = jnp.float32
MXU_DTYPE = jnp.bfloat16
HI = lax.Precision.HIGHEST

GRID_W = 64
HEAD_DIM = 64
ROPE_THETA = 10000.0
NORM_EPS = 1e-6

A_HEADS = 4
A_QK = HEAD_DIM
A_V = 2 * HEAD_DIM
B_HEADS = 4
B_DK = HEAD_DIM // 2
B_DV = HEAD_DIM
B_GATE_RANK = 16
B_GATE_TAU = 16.0
B_CHUNK = 64
LOG2_E = math.log2(math.e)
GLA_SUB = 16
C_HEADS = 4
C_DH = HEAD_DIM
NA_ROWS = 8
NA_COLS = 16

A_WIDTH = A_HEADS * A_V
B_WIDTH = B_HEADS * B_DV
C_WIDTH = C_HEADS * C_DH
B_QK = B_HEADS * B_DK
IN_SIZES = (A_HEADS * 2 * A_QK, A_HEADS * 2 * A_QK, A_WIDTH,
            B_QK, B_QK, B_WIDTH, B_WIDTH, 2 * B_GATE_RANK,
            C_WIDTH, C_WIDTH, C_WIDTH)

N_GROUPS = 4
EXPERTS_PER_GROUP = 8
N_EXPERTS = N_GROUPS * EXPERTS_PER_GROUP
TOP_K = 2
EXPERT_HIDDEN = 512

LANES = 128
TOK_TILE = 256
NA_TILE_ROWS = 4
NA_KEY_TILES = 3
IN_PROJ_TILE = 256
EXPERT_TILE = 512
GATE_PAD = LANES
BG_WIDTH = 2 * B_QK + 2 * B_WIDTH + GATE_PAD
IN_PAD_WIDTH = 3 * A_WIDTH + BG_WIDTH + 3 * C_WIDTH
ROUTER_PAD = LANES
ROUTE_EXPERT = 0
ROUTE_GATE = 2
ROUTE_RANK = 4
SC_CORES = 2
SC_SUBCORES = 16
SC_ROW_BUFFER_BYTES = 128 * 1024
VMEM_LIMIT = 48 * 1024 * 1024


def _split_bf16(x, pieces):
    out = []
    for _ in range(pieces):
        p = x.astype(jnp.bfloat16)
        out.append(p)
        x = x - p.astype(F32)
    return out


def _pack_bf16_pairs(x):
    w = x.shape[1] // 2
    bits = lax.bitcast_convert_type(x.astype(jnp.bfloat16).astype(F32), jnp.uint32)
    return (bits[:, :w] >> 16) | (bits[:, w:] & jnp.uint32(0xFFFF0000))


def _unpack_bf16_pairs(u):
    lo = lax.bitcast_convert_type(u << 16, F32)
    hi = lax.bitcast_convert_type(u & jnp.uint32(0xFFFF0000), F32)
    return lo, hi


def _silu(x):
    return x * (1.0 / (1.0 + jnp.exp(-x)))


def _cparams(sem):
    return pltpu.CompilerParams(dimension_semantics=sem, vmem_limit_bytes=VMEM_LIMIT)


def _mod_body(c_ref, w_ref, b_ref, o_ref):
    a = _silu(c_ref[...])
    o_ref[...] = jnp.dot(a, w_ref[...], precision=HI, preferred_element_type=F32) + b_ref[...]


def _modulation(cvec, w_mod, b_mod):
    depth, d, d6 = w_mod.shape
    rows = cvec.shape[0]
    return pl.pallas_call(
        _mod_body,
        grid=(depth, d6 // d),
        in_specs=[pl.BlockSpec((rows, d), lambda l, j: (0, 0)),
                  pl.BlockSpec((None, d, d), lambda l, j: (l, 0, j)),
                  pl.BlockSpec((None, 1, d), lambda l, j: (l, 0, j))],
        out_specs=pl.BlockSpec((None, rows, d), lambda l, j: (l, 0, j)),
        out_shape=jax.ShapeDtypeStruct((depth, rows, d6), F32),
        compiler_params=_cparams(("arbitrary", "arbitrary")),
        name="modulation",
    )(cvec, w_mod, b_mod.reshape(depth, 1, d6))


def _rms(x, g):
    return x * lax.rsqrt(jnp.mean(x * x, axis=-1, keepdims=True) + NORM_EPS) * g


def _rope(x, cos, sa, sb):
    return x * cos + pltpu.roll(x, LANES - 16, 1) * sa + pltpu.roll(x, 16, 1) * sb


def _moe_residual(x_ref, y0_ref, y1_ref, rt_ref, gate2):
    rt = rt_ref[0]
    g0, g1 = rt[:, ROUTE_GATE:ROUTE_GATE + 1], rt[:, ROUTE_GATE + 1:ROUTE_GATE + 2]
    lo0, hi0 = _unpack_bf16_pairs(y0_ref[0])
    lo1, hi1 = _unpack_bf16_pairs(y1_ref[0])
    moe = jnp.concatenate([g0 * lo0 + g1 * lo1, g0 * hi0 + g1 * hi1], axis=1)
    return x_ref[0] + gate2 * moe


def _in_proj_body(combine, ctx_len, *refs):
    if combine:
        (x_ref, y0_ref, y1_ref, rt_ref, pmod_ref, mod_ref, g_ref, w_ref, cos_ref, sa_ref, sb_ref,
         qa_ref, ka_ref, va_ref, bg_ref, qn_ref, kn_ref, vn_ref, xo_ref) = refs
    else:
        (x_ref, ctx_ref, mod_ref, g_ref, w_ref, cos_ref, sa_ref, sb_ref,
         qa_ref, ka_ref, va_ref, bg_ref, qn_ref, kn_ref, vn_ref, xo_ref) = refs
    tm = x_ref.shape[1]
    row = lax.broadcasted_iota(jnp.int32, (tm, 1), 0) + pl.program_id(1) * tm
    is_ctx = row < ctx_len

    def mod_row(ref, j):
        return jnp.where(is_ctx, ref[0, j:j + 1, :], ref[1, j:j + 1, :])

    if combine:
        x = _moe_residual(x_ref, y0_ref, y1_ref, rt_ref, mod_row(pmod_ref, 5))
        xo_ref[0] = x
    else:
        x = jnp.where(pl.program_id(1) == 0, ctx_ref[0], x_ref[0])
        xo_ref[0] = x
    h = _rms(x, g_ref[...]) * (1.0 + mod_row(mod_ref, 1)) + mod_row(mod_ref, 0)
    hb = h.astype(MXU_DTYPE)

    def proj(lo, hi):
        return jnp.dot(hb, w_ref[:, lo:hi], preferred_element_type=F32)

    cos, sa, sb = cos_ref[...], sa_ref[...], sb_ref[...]
    qk = proj(0, 2 * A_WIDTH)
    for hh in range(A_HEADS):
        lo = hh * LANES
        q = qk[:, lo:lo + LANES]
        qa_ref[0, :, lo:lo + LANES] = (_rope(q, cos, sa, sb) * (A_QK ** -0.5 * LOG2_E)).astype(qa_ref.dtype)
        k = qk[:, A_WIDTH + lo:A_WIDTH + lo + LANES]
        ka_ref[0, :, lo:lo + LANES] = _rope(k, cos, sa, sb).astype(ka_ref.dtype)
    o = 2 * A_WIDTH
    va = proj(o, o + A_WIDTH)
    ones = jnp.ones((va.shape[0], A_V), F32)
    va_ref[0] = jnp.concatenate([piece for hh in range(A_HEADS)
                                 for piece in (va[:, hh * A_V:(hh + 1) * A_V], ones)],
                                axis=1).astype(va_ref.dtype)
    o += A_WIDTH
    bg_ref[0] = proj(o, o + BG_WIDTH)
    o += BG_WIDTH
    qn_ref[0] = (proj(o, o + C_WIDTH) * (C_DH ** -0.5)).astype(qn_ref.dtype)
    o += C_WIDTH
    kn_ref[0] = proj(o, o + C_WIDTH).astype(kn_ref.dtype)
    o += C_WIDTH
    vn_ref[0] = proj(o, o + C_WIDTH).astype(vn_ref.dtype)


def _moe_specs(tm, d, row_block):
    return [pl.BlockSpec((None, 1, tm, d // 2), lambda b, i: (0, b, row_block(i), 0)),
            pl.BlockSpec((None, 1, tm, d // 2), lambda b, i: (1, b, row_block(i), 0)),
            pl.BlockSpec((1, tm, ROUTER_PAD), lambda b, i: (b, row_block(i), 0))]


def _in_proj(layer, x, ymoe, route, modsel, g1, w_in_p, rope_tabs, ctx=None):
    bsz, t, d = x.shape
    tm = IN_PROJ_TILE
    combine = ymoe is not None
    tok = lambda b, i: (b, i, 0)
    x_spec = pl.BlockSpec((1, tm, d), tok)
    if ctx is not None:
        assert not combine and ctx.shape[1] == tm
        t += ctx.shape[1]

    def mod_spec(l):
        return pl.BlockSpec((None, None, 2, 6, d), lambda b, i: (l, b, 0, 0, 0))

    tab_spec = pl.BlockSpec((tm, LANES), lambda b, i: (i, 0))
    in_specs = [x_spec]
    args = [x]
    if combine:
        in_specs += _moe_specs(tm, d, lambda i: i) + [mod_spec(layer - 1)]
        args += [ymoe, ymoe, route, modsel]
    else:
        in_specs = [pl.BlockSpec((1, tm, d), lambda b, i: (b, jnp.maximum(i - 1, 0), 0)),
                    pl.BlockSpec((1, tm, d), lambda b, i: (b, 0, 0))]
        args += [ctx]
    in_specs += [mod_spec(layer), pl.BlockSpec((1, d), lambda b, i: (0, 0)),
                 pl.BlockSpec((d, IN_PAD_WIDTH), lambda b, i: (0, 0)), tab_spec, tab_spec, tab_spec]
    args += [modsel, g1.reshape(1, d), w_in_p, *rope_tabs]

    def o(width, dtype):
        return pl.BlockSpec((1, tm, width), tok), jax.ShapeDtypeStruct((bsz, t, width), dtype)

    outs = [o(A_WIDTH, MXU_DTYPE), o(A_WIDTH, MXU_DTYPE), o(2 * A_WIDTH, MXU_DTYPE), o(BG_WIDTH, F32),
            o(C_WIDTH, MXU_DTYPE), o(C_WIDTH, MXU_DTYPE), o(C_WIDTH, MXU_DTYPE)]
    outs.append(o(d, F32))
    return pl.pallas_call(
        functools.partial(_in_proj_body, combine, TOK_TILE),
        grid=(bsz, pl.cdiv(t, tm)),
        in_specs=in_specs,
        out_specs=[s for s, _ in outs],
        out_shape=[s for _, s in outs],
        compiler_params=_cparams(("parallel", "arbitrary")),
        name="in_proj",
    )(*args)


def _diff_rows(lam_init, q, k_ref, v_ref, n_keys, lam, g):
    lane = lax.broadcasted_iota(jnp.int32, (1, LANES), 1)
    zero = jnp.zeros_like(q)
    nt = (((1,), (1,)), ((), ()))
    k = k_ref[0, :n_keys, :]
    s1 = lax.dot_general(jnp.where(lane < A_QK, q, zero), k, nt, preferred_element_type=F32)
    s2 = lax.dot_general(jnp.where(lane >= A_QK, q, zero), k, nt, preferred_element_type=F32)

    def finish():
        v1 = v_ref[0, :n_keys, :]
        outs = []
        for s in (s1, s2):
            p = jnp.exp2((s - jnp.max(s, axis=-1, keepdims=True)).astype(MXU_DTYPE))
            outs.append(jnp.dot(p, v1, preferred_element_type=F32))
        o = (outs[0][:, :A_V] * (1.0 / outs[0][:, A_V:A_V + 1])
             - outs[1][:, :A_V] * (lam / outs[1][:, A_V:A_V + 1]))
        return _rms(o, g) * (1.0 - lam_init)

    return finish


def _diff_attn_body(lam_init, ctx_len, qa_ref, qb_ref, k_ref, v_ref, lam_ref, g_ref, oc_ref, ol_ref):
    lm = lam_ref[...]
    lam = (jnp.exp(jnp.sum(lm[0:1] * lm[1:2], axis=1, keepdims=True))
           - jnp.exp(jnp.sum(lm[2:3] * lm[3:4], axis=1, keepdims=True)) + lam_init)
    g = g_ref[...]
    rows = qa_ref.shape[1]

    @pl.when(pl.program_id(2) == 0)
    def _():
        oc_ref[0] = _diff_rows(lam_init, qa_ref[0], k_ref, v_ref, ctx_len, lam, g)().astype(oc_ref.dtype)

    @pl.when(pl.program_id(2) > 0)
    def _():
        n_keys = k_ref.shape[1]
        fin_a = _diff_rows(lam_init, qa_ref[0], k_ref, v_ref, n_keys, lam, g)
        fin_b = _diff_rows(lam_init, qb_ref[0], k_ref, v_ref, n_keys, lam, g)
        ol_ref[0, :rows, :] = fin_a().astype(ol_ref.dtype)
        ol_ref[0, rows:, :] = fin_b().astype(ol_ref.dtype)


def _diff_attention(layer, qa, ka, va, lam, g_sub):
    bsz, t, _ = qa.shape
    tq = TOK_TILE
    n_lat = (t - tq) // (2 * tq)
    assert n_lat * 2 * tq == t - tq
    lam_init = 0.8 - 0.6 * math.exp(-0.3 * layer)

    def q_spec(off):
        return pl.BlockSpec((1, tq, LANES), lambda b, h, i: (b, jnp.maximum(2 * i + off, 0), h))

    return pl.pallas_call(
        functools.partial(_diff_attn_body, lam_init, TOK_TILE),
        grid=(bsz, A_HEADS, 1 + n_lat),
        in_specs=[q_spec(-1), q_spec(0),
                  pl.BlockSpec((1, t, LANES), lambda b, h, i: (b, 0, h)),
                  pl.BlockSpec((1, t, 2 * A_V), lambda b, h, i: (b, 0, h)),
                  pl.BlockSpec((4, A_QK), lambda b, h, i: (0, 0)),
                  pl.BlockSpec((1, A_V), lambda b, h, i: (0, 0))],
        out_specs=[pl.BlockSpec((1, tq, LANES), lambda b, h, i: (b, 0, h)),
                   pl.BlockSpec((1, 2 * tq, LANES), lambda b, h, i: (b, jnp.maximum(i - 1, 0), h))],
        out_shape=[jax.ShapeDtypeStruct((bsz, tq, A_WIDTH), MXU_DTYPE),
                   jax.ShapeDtypeStruct((bsz, t - tq, A_WIDTH), MXU_DTYPE)],
        compiler_params=_cparams(("parallel", "parallel", "arbitrary")),
        name="diff_attention",
    )(qa, qa, ka, va, lam, g_sub.reshape(1, A_V))


def _gla_body(f_ref, r_ref, wdec_ref, bdec_ref, of_ref, ob_ref, sf_ref, sb_ref, es_ref, bs_ref, qss_ref):
    c = B_CHUNK
    n_chunks = TOK_TILE // c

    @pl.when(pl.program_id(1) == 0)
    def _():
        sf_ref[...] = jnp.zeros_like(sf_ref)
        sb_ref[...] = jnp.zeros_like(sb_ref)

    sub = GLA_SUB
    n_sub = c // sub
    nt = (((1,), (1,)), ((), ()))
    t_row = lax.broadcasted_iota(jnp.int32, (TOK_TILE, TOK_TILE), 0)
    t_col = lax.broadcasted_iota(jnp.int32, (TOK_TILE, TOK_TILE), 1)
    same_chunk = (t_row // c) == (t_col // c)
    tri_f = (same_chunk & (t_col <= t_row)).astype(jnp.bfloat16)
    tri_b = (same_chunk & (t_col >= t_row)).astype(jnp.bfloat16)
    s_iota = lax.broadcasted_iota(jnp.int32, (sub, LANES), 0)
    idx = lax.broadcasted_iota(jnp.int32, (c, LANES), 0)
    head_of_k = lax.broadcasted_iota(jnp.int32, (B_QK, B_WIDTH), 0) // B_DK
    head_of_v = lax.broadcasted_iota(jnp.int32, (B_QK, B_WIDTH), 1) // B_DV
    expand = (head_of_k == head_of_v).astype(MXU_DTYPE)
    same_head_t = (lax.broadcasted_iota(jnp.int32, (B_WIDTH, B_QK), 0) // B_DV
                   == lax.broadcasted_iota(jnp.int32, (B_WIDTH, B_QK), 1) // B_DK)
    n_ref = n_sub - 1
    kt_keep = (lax.broadcasted_iota(jnp.int32, (B_HEADS * c, n_ref * B_QK), 0) // c
               == (lax.broadcasted_iota(jnp.int32, (B_HEADS * c, n_ref * B_QK), 1) % B_QK) // B_DK)
    vx_keep = (lax.broadcasted_iota(jnp.int32, (B_HEADS * c, B_WIDTH), 0) // c
               == lax.broadcasted_iota(jnp.int32, (B_HEADS * c, B_WIDTH), 1) // B_DV)
    pick = (lax.broadcasted_iota(jnp.int32, (c, c * sub), 1) // sub
            == lax.broadcasted_iota(jnp.int32, (c, c * sub), 0)).astype(MXU_DTYPE)

    def log_decay(src_ref, backward):
        gl = src_ref[0, :, 2 * B_QK + 2 * B_WIDTH:BG_WIDTH]
        d0 = B_QK if backward else 0
        z = jnp.dot(gl, wdec_ref[:, d0:d0 + B_QK], precision=HI, preferred_element_type=F32) \
            + bdec_ref[:, d0:d0 + B_QK]
        log_a = (jnp.minimum(z, 0.0) - jnp.log(1.0 + jnp.exp(-jnp.abs(z)))) / B_GATE_TAU
        tri = tri_b if backward else tri_f
        return sum(jnp.dot(tri, p, preferred_element_type=F32) for p in _split_bf16(log_a, 3))

    def chunk(slot, src_ref, b_all, lo, backward, st_ref, out_ref):
        q = src_ref[0, lo:lo + c, 0:B_QK] * (B_DK ** -0.5)
        k = src_ref[0, lo:lo + c, B_QK:2 * B_QK]
        v = src_ref[0, lo:lo + c, 2 * B_QK:2 * B_QK + B_WIDTH]
        b = b_all[lo:lo + c]
        b_ref, qs_ref, e_ref = bs_ref.at[slot], qss_ref.at[slot], es_ref.at[slot]
        b_ref[...] = b
        qs_ref[...] = q
        blk = ((c - 1 - idx) if backward else idx) // sub

        q_parts, k_parts = [], []
        for m, late, early in ((1, blk == 1, blk == 0), (2, blk >= 2, blk <= 1), (3, blk == 3, blk == 2)):
            r_row = (c - 1 - sub * m) if backward else sub * m
            r = b_ref[r_row:r_row + 1, :]
            q_parts.append(q * jnp.exp(jnp.where(late, b - r, -jnp.inf)))
            k_parts.append(k * jnp.exp(jnp.where(early, r - b, -jnp.inf)))
        q_cat = jnp.concatenate(q_parts, axis=1).astype(MXU_DTYPE)
        k_cat = jnp.concatenate(k_parts, axis=1)
        k_exp = jnp.where(kt_keep, jnp.concatenate([k_cat] * B_HEADS, axis=0), 0.0).astype(MXU_DTYPE)
        a_off = lax.dot_general(q_cat, k_exp, nt, preferred_element_type=F32)

        for tt in range(c):
            lo_s = tt // sub * sub
            keep = (s_iota >= tt - lo_s) if backward else (s_iota <= tt - lo_s)
            bt = b_ref[tt:tt + 1, :]
            qt = qs_ref[tt:tt + 1, :]
            e = jnp.exp(jnp.where(keep, bt - b[lo_s:lo_s + sub], -jnp.inf)) * (qt * k[lo_s:lo_s + sub])
            e_ref[tt * sub:(tt + 1) * sub, :] = e.astype(e_ref.dtype)
        a_exp = jnp.dot(e_ref[...], expand, preferred_element_type=F32)
        b_end = b[0:1, :] if backward else b[c - 1:c, :]
        kd = k * jnp.exp(b_end - b)
        upd = lax.dot_general(v, kd, (((0,), (0,)), ((), ())), preferred_element_type=F32)

        def intra():
            v_exp = jnp.where(vx_keep, jnp.concatenate([v] * B_HEADS, axis=0), 0.0).astype(MXU_DTYPE)
            o_off = jnp.dot(a_off.astype(MXU_DTYPE), v_exp, preferred_element_type=F32)
            prod = a_exp.reshape(n_sub, sub, sub, B_WIDTH) * v.reshape(n_sub, 1, sub, B_WIDTH)
            o_diag = jnp.dot(pick, prod.reshape(c * sub, B_WIDTH).astype(MXU_DTYPE),
                             preferred_element_type=F32)
            o_intra = o_off + o_diag

            def recur():
                st = st_ref[...]
                o_inter = lax.dot_general(q * jnp.exp(b), st, nt, preferred_element_type=F32)
                out_ref[0, lo:lo + c, :] = o_intra + o_inter
                st_ref[...] = jnp.exp(b_end) * st + jnp.where(same_head_t, upd, 0.0)

            return recur

        return intra

    b_fwd = log_decay(f_ref, False)
    b_bwd = log_decay(r_ref, True)
    stage = []
    for ci in range(n_chunks):
        stage.append(chunk(2 * ci, f_ref, b_fwd, ci * c, False, sf_ref, of_ref))
        stage.append(chunk(2 * ci + 1, r_ref, b_bwd, (n_chunks - 1 - ci) * c, True, sb_ref, ob_ref))
    stage = [intra() for intra in stage]
    for recur in stage:
        recur()


def _gla_scan(bg, w_dec, b_dec):
    bsz, t, _ = bg.shape
    n = t // TOK_TILE
    rev = lambda b, i: (b, jnp.where(i == 0, 0, n - i), 0)
    fwd = lambda b, i: (b, i, 0)
    wdec = jnp.zeros((GATE_PAD, 2 * B_QK), F32)
    wdec = wdec.at[:B_GATE_RANK, :B_QK].set(w_dec[0]).at[B_GATE_RANK:2 * B_GATE_RANK, B_QK:].set(w_dec[1])
    bdec = b_dec.reshape(1, 2 * B_QK)
    o_shape = jax.ShapeDtypeStruct((bsz, t, B_WIDTH), F32)
    n_slots = 2 * (TOK_TILE // B_CHUNK)
    return pl.pallas_call(
        _gla_body,
        grid=(bsz, n),
        in_specs=[pl.BlockSpec((1, TOK_TILE, BG_WIDTH), fwd),
                  pl.BlockSpec((1, TOK_TILE, BG_WIDTH), rev),
                  pl.BlockSpec((GATE_PAD, 2 * B_QK), lambda b, i: (0, 0)),
                  pl.BlockSpec((1, 2 * B_QK), lambda b, i: (0, 0))],
        out_specs=[pl.BlockSpec((1, TOK_TILE, B_WIDTH), fwd),
                   pl.BlockSpec((1, TOK_TILE, B_WIDTH), rev)],
        out_shape=[o_shape, o_shape],
        scratch_shapes=[pltpu.VMEM((B_WIDTH, B_QK), F32), pltpu.VMEM((B_WIDTH, B_QK), F32),
                        pltpu.VMEM((n_slots, B_CHUNK * GLA_SUB, LANES), MXU_DTYPE),
                        pltpu.VMEM((n_slots, B_CHUNK, LANES), F32),
                        pltpu.VMEM((n_slots, B_CHUNK, LANES), F32)],
        compiler_params=_cparams(("parallel", "arbitrary")),
        name="gla_scan",
    )(bg, bg, wdec, bdec)


def _na_bias_tables(rpb, rows):
    n_tiles = rows // NA_TILE_ROWS
    wr = min(NA_ROWS, rows)
    n_dr, n_dc = 2 * NA_ROWS - 1, 2 * NA_COLS - 1
    cq = np.arange(GRID_W)[:, None]
    ck = np.arange(GRID_W)[None, :]
    cs = np.clip(cq - NA_COLS // 2, 0, GRID_W - NA_COLS)
    col_ok = (ck >= cs) & (ck < cs + NA_COLS)
    dc = np.clip(ck - cq, -(NA_COLS - 1), NA_COLS - 1) + (NA_COLS - 1)
    onehot = (dc.reshape(1, -1) == np.arange(n_dc)[:, None]).astype(np.float32)
    by_col = jnp.dot(rpb.astype(F32).reshape(-1, n_dc), onehot, precision=HI)
    by_col = jnp.where(col_ok.reshape(1, 1, GRID_W, GRID_W),
                       by_col.reshape(C_HEADS, n_dr, GRID_W, GRID_W), -jnp.inf)
    masked = jnp.full((C_HEADS, GRID_W, GRID_W), -jnp.inf, F32)
    tabs = []
    for j in (0, 1, n_tiles - 1):
        kr0 = int(np.clip(j - 1, 0, n_tiles - NA_KEY_TILES)) * NA_TILE_ROWS
        q_rows = []
        for qr in range(NA_TILE_ROWS):
            r = j * NA_TILE_ROWS + qr
            start = int(np.clip(r - wr // 2, 0, rows - wr))
            blocks = []
            for kw in range(NA_KEY_TILES * NA_TILE_ROWS):
                kr = kr0 + kw
                blocks.append(by_col[:, kr - r + NA_ROWS - 1] if start <= kr < start + wr else masked)
            q_rows.append(jnp.concatenate(blocks, axis=-1))
        tabs.append(jnp.concatenate(q_rows, axis=1))
    return jnp.stack(tabs)


def _na_body(q_ref, k0_ref, k1_ref, k2_ref, kc_ref, v0_ref, v1_ref, v2_ref, vc_ref, m_ref, o_ref):
    q = q_ref[0]
    lane = lax.broadcasted_iota(jnp.int32, (1, C_WIDTH), 1)
    zero = jnp.zeros_like(q)
    nt = (((1,), (1,)), ((), ()))

    def scores(qm, k_ref):
        return lax.dot_general(qm, k_ref[0], nt, preferred_element_type=F32)

    def head_out(hh, windows):
        qm = jnp.where((lane >= hh * C_DH) & (lane < (hh + 1) * C_DH), q, zero)
        s = [scores(qm, kc_ref)]
        for w, k_ref in enumerate(windows):
            s.append(scores(qm, k_ref) + m_ref[0, hh, :, w * TOK_TILE:(w + 1) * TOK_TILE])

        def finish():
            m = functools.reduce(jnp.maximum, [jnp.max(x, axis=-1, keepdims=True) for x in s])
            p = [jnp.exp(x - m) for x in s]
            den = functools.reduce(jnp.add, [jnp.sum(x, axis=-1, keepdims=True) for x in p])
            vals = [vc_ref] + [v0_ref, v1_ref, v2_ref][:len(windows)]
            o = functools.reduce(jnp.add, [jnp.dot(x.astype(MXU_DTYPE), v_ref[0], preferred_element_type=F32)
                                           for x, v_ref in zip(p, vals)])
            return o * (1.0 / den)

        return finish

    def emit(windows):
        finish = [head_out(hh, windows) for hh in range(C_HEADS)]
        o = finish[C_HEADS - 1]()
        for hh in range(C_HEADS - 2, -1, -1):
            o = jnp.where(lane < (hh + 1) * C_DH, finish[hh](), o)
        o_ref[0] = o.astype(o_ref.dtype)

    @pl.when(pl.program_id(0) == 0)
    def _():
        emit([])

    @pl.when(pl.program_id(0) > 0)
    def _():
        emit([k0_ref, k1_ref, k2_ref])


def _neighborhood_attention(qn, kn, vn, bias_tabs):
    bsz, t, _ = qn.shape
    n = t // TOK_TILE
    n_lat = n - 1

    def win(w):
        def index(i, b):
            kb0 = jnp.clip(i - 2, 0, n_lat - NA_KEY_TILES)
            return (b, kb0 + 1 + w, 0)
        return pl.BlockSpec((1, TOK_TILE, C_WIDTH), index)

    def cls(i, b):
        j = i - 1
        return (jnp.where(j <= 0, 0, jnp.where(j == n_lat - 1, 2, 1)), 0, 0, 0)

    own = pl.BlockSpec((1, TOK_TILE, C_WIDTH), lambda i, b: (b, i, 0))
    ctx = pl.BlockSpec((1, TOK_TILE, C_WIDTH), lambda i, b: (b, 0, 0))
    return pl.pallas_call(
        _na_body,
        grid=(n, bsz),
        in_specs=[own, win(0), win(1), win(2), ctx, win(0), win(1), win(2), ctx,
                  pl.BlockSpec((1, C_HEADS, TOK_TILE, NA_KEY_TILES * TOK_TILE), cls)],
        out_specs=own,
        out_shape=jax.ShapeDtypeStruct((bsz, t, C_WIDTH), MXU_DTYPE),
        compiler_params=_cparams(("parallel", "arbitrary")),
        name="neighborhood_attention",
    )(qn, kn, kn, kn, kn, vn, vn, vn, vn, bias_tabs)


def _out_proj_body(yac_ref, yal_ref, of_ref, ob_ref, r_ref, yn_ref, x_ref, mod_ref, gg_ref, g2_ref, w_ref,
                   wr_ref, br_ref, xo_ref, h_ref, rt_ref, cnt_ref, count_ref):
    is_ctx_tile = pl.program_id(1) == 0
    hi = lax.broadcasted_iota(jnp.int32, (B_WIDTH, B_WIDTH), 0) // B_DV
    hj = lax.broadcasted_iota(jnp.int32, (B_WIDTH, B_WIDTH), 1) // B_DV
    head_mean = jnp.where(hi == hj, 1.0 / B_DV, 0.0).astype(jnp.bfloat16)

    @pl.when((pl.program_id(0) == 0) & (pl.program_id(1) == 0))
    def _():
        count_ref[...] = jnp.zeros_like(count_ref)

    def rows(r0, r1):
        ya = jnp.where(is_ctx_tile, yac_ref[0, r0:r1, :], yal_ref[0, r0:r1, :])
        o = of_ref[0, r0:r1, :] + ob_ref[0, r0:r1, :]
        ms = sum(jnp.dot(p, head_mean, preferred_element_type=F32) for p in _split_bf16(o * o, 2))
        yb = o * lax.rsqrt(ms + NORM_EPS) * gg_ref[...] * _silu(r_ref[0, r0:r1, :])

        def project():
            mix = (jnp.dot(ya, w_ref[0:A_WIDTH, :], preferred_element_type=F32)
                   + jnp.dot(yb.astype(MXU_DTYPE), w_ref[A_WIDTH:A_WIDTH + B_WIDTH, :],
                             preferred_element_type=F32)
                   + jnp.dot(yn_ref[0, r0:r1, :], w_ref[A_WIDTH + B_WIDTH:, :], preferred_element_type=F32))
            x = x_ref[0, r0:r1, :] + mod_ref[2:3, :] * mix
            xo_ref[0, r0:r1, :] = x
            h = _rms(x, g2_ref[...]) * (1.0 + mod_ref[4:5, :]) + mod_ref[3:4, :]
            h_ref[0, r0:r1, :] = _pack_bf16_pairs(h)
            h_hi, h_lo = _split_bf16(h, 2)
            both = jnp.dot(h_hi, wr_ref[...], preferred_element_type=F32)
            logits = (both[:, :ROUTER_PAD] + both[:, ROUTER_PAD:]
                      + jnp.dot(h_lo, wr_ref[:, :ROUTER_PAD], preferred_element_type=F32) + br_ref[...])

            def route():
                rt_ref[0, r0:r1, :] = _route(logits, count_ref)

            return route

        return project

    half = x_ref.shape[1] // 2
    stages = [rows(0, half), rows(half, 2 * half)]
    stages = [project() for project in stages]
    for route in stages:
        route()
    cnt_ref[...] = count_ref[...]


def _route(lg, count_ref):
    lane = lax.broadcasted_iota(jnp.int32, lg.shape, 1)
    big = jnp.int32(ROUTER_PAD)

    def top(mask):
        v = jnp.max(jnp.where(mask, lg, -jnp.inf), axis=-1, keepdims=True)
        i = jnp.min(jnp.where(mask & (lg == v), lane, big), axis=-1, keepdims=True)
        return v, i

    g_mask = lane < N_GROUPS
    g_max, grp = top(g_mask)
    p_grp = 1.0 / jnp.sum(jnp.where(g_mask, jnp.exp(lg - g_max), 0.0), axis=-1, keepdims=True)
    e_lo = N_GROUPS + grp * EXPERTS_PER_GROUP
    e_mask = (lane >= e_lo) & (lane < e_lo + EXPERTS_PER_GROUP)
    v1, i1 = top(e_mask)
    v2, i2 = top(e_mask & (lane != i1))
    r = jnp.exp(v2 - v1)
    gate1 = p_grp / (1.0 + r)
    gate2 = p_grp * r / (1.0 + r)
    e1, e2 = i1 - N_GROUPS, i2 - N_GROUPS
    hot1 = (lane == e1).astype(F32)
    hot2 = (lane == e2).astype(F32)
    both = hot1 + hot2
    rows = lg.shape[0]
    earlier = (lax.broadcasted_iota(jnp.int32, (rows, rows), 1)
               < lax.broadcasted_iota(jnp.int32, (rows, rows), 0)).astype(jnp.bfloat16)
    before = jnp.dot(earlier, both.astype(jnp.bfloat16), preferred_element_type=F32) + count_ref[0:1, :]
    rank1 = jnp.sum(hot1 * before, axis=-1, keepdims=True)
    rank2 = jnp.sum(hot2 * before, axis=-1, keepdims=True)
    count_ref[0:1, :] = count_ref[0:1, :] + jnp.sum(both, axis=0, keepdims=True)
    out = jnp.where(lane == ROUTE_EXPERT, e1.astype(F32),
                    jnp.where(lane == ROUTE_EXPERT + 1, e2.astype(F32),
                              jnp.where(lane == ROUTE_GATE, gate1,
                                        jnp.where(lane == ROUTE_GATE + 1, gate2,
                                                  jnp.where(lane == ROUTE_RANK, rank1,
                                                            jnp.where(lane == ROUTE_RANK + 1, rank2, 0.0))))))
    return out


def _out_proj(layer, ya_ctx, ya_lat, o_f, o_b, bg, yn, x, modsel, g_gla, g2, w_out_b, w_router, b_router):
    bsz, t, d = x.shape
    tm = TOK_TILE
    tok = lambda b, i: (b, i, 0)
    const = lambda b, i: (0, 0)
    r_block = (2 * B_QK + B_WIDTH) // B_WIDTH
    return pl.pallas_call(
        _out_proj_body,
        grid=(bsz, t // tm),
        in_specs=[pl.BlockSpec((1, tm, A_WIDTH), lambda b, i: (b, 0, 0)),
                  pl.BlockSpec((1, tm, A_WIDTH), lambda b, i: (b, jnp.maximum(i - 1, 0), 0)),
                  pl.BlockSpec((1, tm, B_WIDTH), tok), pl.BlockSpec((1, tm, B_WIDTH), tok),
                  pl.BlockSpec((1, tm, B_WIDTH), lambda b, i: (b, i, r_block)),
                  pl.BlockSpec((1, tm, C_WIDTH), tok),
                  pl.BlockSpec((1, tm, d), tok),
                  pl.BlockSpec((None, None, None, 6, d), lambda b, i: (layer, b, jnp.minimum(i, 1), 0, 0)),
                  pl.BlockSpec((1, B_WIDTH), const), pl.BlockSpec((1, d), const),
                  pl.BlockSpec(w_out_b.shape, const),
                  pl.BlockSpec((d, 2 * ROUTER_PAD), const), pl.BlockSpec((1, ROUTER_PAD), const)],
        out_specs=[pl.BlockSpec((1, tm, d), tok), pl.BlockSpec((1, tm, d // 2), tok),
                   pl.BlockSpec((1, tm, ROUTER_PAD), tok), pl.BlockSpec((8, ROUTER_PAD), const)],
        out_shape=[jax.ShapeDtypeStruct((bsz, t, d), F32), jax.ShapeDtypeStruct((bsz, t, d // 2), jnp.uint32),
                   jax.ShapeDtypeStruct((bsz, t, ROUTER_PAD), F32), jax.ShapeDtypeStruct((8, ROUTER_PAD), F32)],
        scratch_shapes=[pltpu.VMEM((8, ROUTER_PAD), F32)],
        compiler_params=_cparams(("arbitrary", "arbitrary")),
        name="out_proj",
    )(ya_ctx, ya_lat, o_f, o_b, bg, yn, x, modsel, jnp.tile(g_gla, B_HEADS).reshape(1, B_WIDTH), g2.reshape(1, d),
      w_out_b, w_router, b_router)


def _dispatch_plan(expert, rank, counts, n_blocks):
    padded = (counts + EXPERT_TILE - 1) // EXPERT_TILE * EXPERT_TILE
    pad_end = jnp.cumsum(padded)
    pad_start = (pad_end - padded).astype(jnp.int32)
    lanes = jnp.arange(N_EXPERTS, dtype=jnp.int32)
    dest = rank + jnp.sum(jnp.where(expert[..., None] == lanes, pad_start, 0), axis=-1)
    blk_start = jnp.arange(n_blocks, dtype=jnp.int32) * EXPERT_TILE
    blk_expert = jnp.sum((pad_end[None, :] <= blk_start[:, None]).astype(jnp.int32), axis=1)
    blk_expert = jnp.minimum(blk_expert, N_EXPERTS - 1)
    blk_valid = jnp.clip(pad_start[blk_expert] + counts[blk_expert] - blk_start, 0, EXPERT_TILE)
    return dest.astype(jnp.int32), blk_expert.astype(jnp.int32), blk_valid.astype(jnp.int32)


def _sc_dispatch(table, dest, n_rows):
    n_tok, width = table.shape
    n_workers = SC_CORES * SC_SUBCORES
    per_worker = n_tok // n_workers
    chunk = SC_ROW_BUFFER_BYTES // (2 * width * table.dtype.itemsize)
    n_chunks = per_worker // chunk
    assert per_worker * n_workers == n_tok and n_chunks * chunk == per_worker and n_chunks % 2 == 0
    assert n_chunks >= 4 and chunk <= LANES and dest.shape == (n_tok, TOP_K)
    mesh = plsc.VectorSubcoreMesh(core_axis_name="core", subcore_axis_name="subcore")

    def body(table_hbm, d0_hbm, d1_hbm, out_hbm, i0_v, i1_v, rows_v, read_sem, put_sem):
        worker = lax.axis_index("subcore") * SC_CORES + lax.axis_index("core")
        base = worker * per_worker
        pltpu.sync_copy(d0_hbm.at[worker], i0_v)
        pltpu.sync_copy(d1_hbm.at[worker], i1_v)

        def read(j, slot):
            return pltpu.make_async_copy(table_hbm.at[pl.ds(base + j * chunk, chunk)], rows_v.at[slot],
                                         read_sem.at[slot])

        def put(j, slot, idx_v, k):
            return pltpu.make_async_copy(rows_v.at[slot], out_hbm.at[idx_v.at[j]], put_sem.at[slot, k])

        def drain(j, slot):
            read(j, slot).wait()
            put(j, slot, i0_v, 0).start()
            put(j, slot, i1_v, 1).start()
            put(j, slot, i0_v, 0).wait()
            put(j, slot, i1_v, 1).wait()

        read(0, 0).start()
        read(1, 1).start()

        @pl.loop(0, n_chunks - 2, step=2)
        def _(j):
            for slot in range(2):
                drain(j + slot, slot)
                read(j + slot + 2, slot).start()

        drain(n_chunks - 2, 0)
        drain(n_chunks - 1, 1)

    idx = dest.reshape(n_workers, n_chunks, chunk, TOP_K)
    return pl.kernel(
        body,
        out_type=jax.ShapeDtypeStruct((n_rows, width), table.dtype),
        mesh=mesh,
        scratch_types=[pltpu.VMEM((n_chunks, chunk), jnp.int32), pltpu.VMEM((n_chunks, chunk), jnp.int32),
                       pltpu.VMEM((2, chunk, width), table.dtype),
                       pltpu.SemaphoreType.DMA((2,)), pltpu.SemaphoreType.DMA((2, TOP_K))],
        name="sc_row_dispatch",
    )(table, idx[..., 0], idx[..., 1])


def _sc_gather(table, idx):
    n_rows = idx.shape[0]
    width = table.shape[1]
    n_workers = SC_CORES * SC_SUBCORES
    per_worker = n_rows // n_workers
    chunk = SC_ROW_BUFFER_BYTES // (width * table.dtype.itemsize)
    n_chunks = per_worker // chunk
    assert per_worker * n_workers == n_rows and n_chunks * chunk == per_worker and n_chunks % 2 == 0
    assert n_chunks >= 4 and chunk <= LANES
    mesh = plsc.VectorSubcoreMesh(core_axis_name="core", subcore_axis_name="subcore")

    def body(table_hbm, idx_hbm, out_hbm, idx_v, rows_v, gather_sem, write_sem):
        worker = lax.axis_index("subcore") * SC_CORES + lax.axis_index("core")
        base = worker * per_worker
        pltpu.sync_copy(idx_hbm.at[worker], idx_v)

        def gather(j, slot):
            return pltpu.make_async_copy(table_hbm.at[idx_v.at[j]], rows_v.at[slot], gather_sem.at[slot])

        def write(j, slot):
            return pltpu.make_async_copy(rows_v.at[slot], out_hbm.at[pl.ds(base + j * chunk, chunk)],
                                         write_sem.at[slot])

        gather(0, 0).start()
        gather(0, 0).wait()
        gather(1, 1).start()
        write(0, 0).start()

        @pl.loop(1, n_chunks - 1, step=2)
        def _(j):
            for s in range(2):
                slot = (1 + s) % 2
                gather(j + s, slot).wait()
                write(j + s - 1, 1 - slot).wait()
                gather(j + s + 1, 1 - slot).start()
                write(j + s, slot).start()

        last = n_chunks - 1
        gather(last, 1).wait()
        write(last, 1).start()
        write(last - 1, 0).wait()
        write(last, 1).wait()

    return pl.kernel(
        body,
        out_type=jax.ShapeDtypeStruct((n_rows, width), table.dtype),
        mesh=mesh,
        scratch_types=[pltpu.VMEM((n_chunks, chunk), jnp.int32),
                       pltpu.VMEM((2, chunk, width), table.dtype),
                       pltpu.SemaphoreType.DMA((2,)), pltpu.SemaphoreType.DMA((2,))],
        name="sc_row_gather",
    )(table, idx.reshape(n_workers, n_chunks, chunk))


def _expert_body(be_ref, used_ref, x_ref, wu_ref, wd_ref, o_ref, wub_ref, wdb_ref):
    i = pl.program_id(0)
    prev = be_ref[jnp.maximum(i - 1, 0)]

    @pl.when((i == 0) | (be_ref[i] != prev))
    def _():
        wub_ref[...] = wu_ref[...].astype(wub_ref.dtype)
        wdb_ref[...] = wd_ref[...].astype(wdb_ref.dtype)

    @pl.when(used_ref[i] > 0)
    def _():
        def rows(r0, r1):
            real = (lax.broadcasted_iota(jnp.int32, (r1 - r0, x_ref.shape[1]), 0) + r0) < used_ref[i]
            lo, hi = _unpack_bf16_pairs(jnp.where(real, x_ref[r0:r1, :], jnp.uint32(0)))
            half = lo.shape[1]
            gu = (jnp.dot(lo.astype(MXU_DTYPE), wub_ref[:half, :], preferred_element_type=F32)
                  + jnp.dot(hi.astype(MXU_DTYPE), wub_ref[half:, :], preferred_element_type=F32))

            def down():
                act = _silu(gu[:, :EXPERT_HIDDEN]) * gu[:, EXPERT_HIDDEN:]
                o_ref[r0:r1, :] = _pack_bf16_pairs(
                    jnp.dot(act.astype(MXU_DTYPE), wdb_ref[...], preferred_element_type=F32))

            return down

        group = EXPERT_TILE // 2
        stages = [rows(0, group), rows(group, EXPERT_TILE)]
        for down in stages:
            down()

    @pl.when(used_ref[i] == 0)
    def _():
        o_ref[...] = jnp.zeros_like(o_ref)


def _expert_ffn(layer, buf, blk_expert, blk_used, w_up, w_down):
    n_rows, packed = buf.shape
    d = 2 * packed
    n_blocks = n_rows // EXPERT_TILE
    h2 = w_up.shape[-1]
    grid_spec = pltpu.PrefetchScalarGridSpec(
        num_scalar_prefetch=2,
        grid=(n_blocks,),
        in_specs=[pl.BlockSpec((EXPERT_TILE, packed), lambda i, be, us: (i, 0)),
                  pl.BlockSpec((None, None, d, h2), lambda i, be, us: (layer, be[i], 0, 0)),
                  pl.BlockSpec((None, None, h2 // 2, d), lambda i, be, us: (layer, be[i], 0, 0))],
        out_specs=pl.BlockSpec((EXPERT_TILE, packed), lambda i, be, us: (i, 0)),
        scratch_shapes=[pltpu.VMEM((d, h2), MXU_DTYPE), pltpu.VMEM((h2 // 2, d), MXU_DTYPE)],
    )
    return pl.pallas_call(
        _expert_body,
        grid_spec=grid_spec,
        out_shape=jax.ShapeDtypeStruct((n_rows, packed), jnp.uint32),
        compiler_params=_cparams(("arbitrary",)),
        name="expert_ffn",
    )(blk_expert, blk_used, buf, w_up, w_down)


def _moe(layer, h2, route, counts, w_up, w_down):
    bsz, t, packed = h2.shape
    n_tok = bsz * t
    n_assign = n_tok * TOP_K
    n_blocks = -(-(n_assign + N_EXPERTS * (EXPERT_TILE - 1)) // EXPERT_TILE)
    record = route.reshape(n_tok, ROUTER_PAD)
    expert = record[:, ROUTE_EXPERT:ROUTE_EXPERT + TOP_K].astype(jnp.int32)
    rank = record[:, ROUTE_RANK:ROUTE_RANK + TOP_K].astype(jnp.int32)
    dest, blk_expert, blk_valid = _dispatch_plan(expert, rank, counts[0, :N_EXPERTS].astype(jnp.int32), n_blocks)
    buf = _sc_dispatch(h2.reshape(n_tok, packed), dest, n_blocks * EXPERT_TILE)
    y = _expert_ffn(layer, buf, blk_expert, blk_valid, w_up, w_down)
    return _sc_gather(y, dest.T.reshape(-1)).reshape(TOP_K, bsz, t, packed)


def _final_body(x_ref, y0_ref, y1_ref, rt_ref, mod_ref, g_ref, o_ref):
    o_ref[0] = _rms(_moe_residual(x_ref, y0_ref, y1_ref, rt_ref, mod_ref[5:6, :]), g_ref[...])


def _final_norm(layer, x, ymoe, route, modsel, g, ctx_tiles):
    bsz, t, d = x.shape
    tm = TOK_TILE
    lat = lambda b, i: (b, i + ctx_tiles, 0)
    return pl.pallas_call(
        _final_body,
        grid=(bsz, t // tm - ctx_tiles),
        in_specs=[pl.BlockSpec((1, tm, d), lat)] + _moe_specs(tm, d, lambda i: i + ctx_tiles)
        + [pl.BlockSpec((None, None, None, 6, d), lambda b, i: (layer, b, 1, 0, 0)),
           pl.BlockSpec((1, d), lambda b, i: (0, 0))],
        out_specs=pl.BlockSpec((1, tm, d), lambda b, i: (b, i, 0)),
        out_shape=jax.ShapeDtypeStruct((bsz, t - ctx_tiles * tm, d), F32),
        compiler_params=_cparams(("parallel", "arbitrary")),
        name="final_norm",
    )(x, ymoe, ymoe, route, modsel, g.reshape(1, d))


def _rope_tables(n_ctx, n_lat):
    t = jnp.arange(n_lat)
    row = (t // GRID_W).astype(F32)
    col = (t % GRID_W).astype(F32)
    n_freq = HEAD_DIM // 4
    inv = ROPE_THETA ** (-jnp.arange(n_freq, dtype=F32) / n_freq)
    ang_r = row[:, None] * inv
    ang_c = col[:, None] * inv
    cr, sr, cc, sc = jnp.cos(ang_r), jnp.sin(ang_r), jnp.cos(ang_c), jnp.sin(ang_c)
    z = jnp.zeros_like(sr)
    cos = jnp.concatenate([cr, cr, cc, cc], axis=-1)
    above = jnp.concatenate([-sr, z, -sc, z], axis=-1)
    below = jnp.concatenate([z, sr, z, sc], axis=-1)
    reps = LANES // HEAD_DIM

    def full(tab, ctx_value):
        tab = jnp.tile(tab, (1, reps))
        return jnp.concatenate([jnp.full((n_ctx, LANES), ctx_value, F32), tab], axis=0)

    return full(cos, 1.0), full(above, 0.0), full(below, 0.0)


def _pack_w_in(w_in):
    parts = jnp.split(w_in, np.cumsum(IN_SIZES)[:-1].tolist(), axis=-1)
    qa, ka, va, qb, kb, vb, rb, gb, qn, kn, vn = parts
    gb = jnp.pad(gb, ((0, 0), (0, GATE_PAD - gb.shape[-1])))
    return jnp.concatenate([qa, ka, va, qb, kb, vb, rb, gb, qn, kn, vn], axis=-1).astype(MXU_DTYPE)


def kernel(x, c, ctx, c_ctx, w_mod, b_mod, norm1_g, norm2_g, w_in, w_out, diff_lambda, diff_sub_g,
           gla_w_decay, gla_b_decay, gla_norm_g, na_rel_bias, w_router_group, b_router_group,
           w_router_expert, b_router_expert, w_expert_up, w_expert_down, final_g):
    bsz, seq, d = x.shape
    n_ctx = ctx.shape[1]
    depth = w_mod.shape[0]
    assert n_ctx == TOK_TILE and seq % (NA_TILE_ROWS * GRID_W) == 0 and d % LANES == 0
    assert seq // (NA_TILE_ROWS * GRID_W) >= NA_KEY_TILES
    mod_rows = -(-(bsz + 1) // 8) * 8
    cvec = jnp.zeros((mod_rows, d), F32).at[:bsz].set(c).at[bsz].set(c_ctx)
    mod = _modulation(cvec, w_mod, b_mod).reshape(depth, mod_rows, 6, d)
    modsel = jnp.stack([jnp.broadcast_to(mod[:, bsz][:, None], (depth, bsz, 6, d)), mod[:, :bsz]], axis=2)
    rope_tabs = _rope_tables(n_ctx, seq)
    xt = ymoe = route = None
    for l in range(depth):
        w_in_p = _pack_w_in(w_in[l])
        bias_tabs = _na_bias_tables(na_rel_bias[l], seq // GRID_W)
        w_router = jnp.pad(jnp.concatenate([w_router_group[l], w_router_expert[l]], axis=-1),
                           ((0, 0), (0, ROUTER_PAD - N_GROUPS - N_EXPERTS)))
        w_router = jnp.concatenate(_split_bf16(w_router, 2), axis=-1)
        b_router = jnp.pad(jnp.concatenate([b_router_group[l], b_router_expert[l]]),
                           (0, ROUTER_PAD - N_GROUPS - N_EXPERTS)).reshape(1, ROUTER_PAD)
        w_out_b = w_out[l].astype(MXU_DTYPE)
        if l == 0:
            outs = _in_proj(l, x, None, None, modsel, norm1_g[l], w_in_p, rope_tabs, ctx=ctx)
        else:
            outs = _in_proj(l, xt, ymoe, route, modsel, norm1_g[l], w_in_p, rope_tabs)
        qa, ka, va, bg, qn, kn, vn, xt = outs
        ya_ctx, ya_lat = _diff_attention(l, qa, ka, va, diff_lambda[l], diff_sub_g[l])
        o_f, o_b = _gla_scan(bg, gla_w_decay[l], gla_b_decay[l])
        yn = _neighborhood_attention(qn, kn, vn, bias_tabs)
        xt, h2, route, counts = _out_proj(l, ya_ctx, ya_lat, o_f, o_b, bg, yn, xt, modsel, gla_norm_g[l],
                                          norm2_g[l], w_out_b, w_router, b_router)
        ymoe = _moe(l, h2, route, counts, w_expert_up, w_expert_down)
    return _final_norm(depth - 1, xt, ymoe, route, modsel, final_g, n_ctx // TOK_TILE)
```
